```python
import math
import jax, jax.numpy as jnp
from jax import lax
import numpy as np

D_MODEL = 1024
BATCH = 16
SEQ = 2048
DEPTH = 1

CHUNK = 64
Q_BLOCK = 128
N_HEADS_A = 4
HEAD_DIM_A = 128
D_QK_A = 2 * N_HEADS_A * HEAD_DIM_A
D_V_A = 2 * N_HEADS_A * HEAD_DIM_A
D_CONV = D_MODEL
CONV_WIDTH = 3
N_BUCKETS = 32
MAX_DISTANCE = 128
N_GROUPS = 4
EXPERTS_PER_GROUP = 8
N_EXPERTS = N_GROUPS * EXPERTS_PER_GROUP
TOP_K = 2
D_EXPERT = 512
MOE_BLOCK = 512
LN_EPS = 1e-5
RMS_EPS = 1e-6
DEEPNORM_ALPHA = (2.0 * DEPTH) ** 0.25
DEEPNORM_BETA = (8.0 * DEPTH) ** -0.25
N_IN = 2 * D_QK_A + D_V_A + 3 * D_CONV + 2 * D_MODEL
SPLITS = (D_QK_A, 2 * D_QK_A, 2 * D_QK_A + D_V_A,
          2 * D_QK_A + D_V_A + D_CONV, 2 * D_QK_A + D_V_A + 2 * D_CONV,
          2 * D_QK_A + D_V_A + 3 * D_CONV, 2 * D_QK_A + D_V_A + 3 * D_CONV + D_MODEL)

kernel_name = 'hybrid_diffattn_shortconv_hiermoe_deepnorm'


def layer_norm(x, g, b):
    xf = x.astype(jnp.float32)
    mu = jnp.mean(xf, axis=-1, keepdims=True)
    var = jnp.mean(jnp.square(xf - mu), axis=-1, keepdims=True)
    y = (xf - mu) * lax.rsqrt(var + LN_EPS) * g.astype(jnp.float32) + b.astype(jnp.float32)
    return y.astype(x.dtype)


def t5_bucket(rel):
    nb = N_BUCKETS // 2
    max_exact = nb // 2
    bucket = jnp.where(rel > 0, nb, 0)
    n = jnp.abs(rel)
    nf = jnp.maximum(n, 1).astype(jnp.float32)
    large = max_exact + (jnp.log(nf / max_exact) / math.log(MAX_DISTANCE / max_exact)
                         * (nb - max_exact)).astype(jnp.int32)
    large = jnp.minimum(large, nb - 1)
    return bucket + jnp.where(n < max_exact, n, large)


def diff_attention(q, k, v, rel_bias, lam, lam_init, subln_g):
    bsz, seq = q.shape[0], q.shape[1]
    scale = HEAD_DIM_A ** -0.5
    outs = []
    for qb in range(seq // Q_BLOCK):
        q0 = qb * Q_BLOCK
        kv_end = q0 + Q_BLOCK
        q_blk = q[:, q0:kv_end]
        k_blk = k[:, :kv_end]
        v_blk = v[:, :kv_end]
        qpos = jnp.arange(q0, kv_end, dtype=jnp.int32)
        kpos = jnp.arange(kv_end, dtype=jnp.int32)
        bias = rel_bias.astype(jnp.float32)[t5_bucket(kpos[None, :] - qpos[:, None])]
        bias = jnp.transpose(bias, (2, 0, 1))
        allowed = (kpos[None, :] // CHUNK) <= (qpos[:, None] // CHUNK)
        logits = jnp.einsum('bqhmd,bkhmd->bhmqk', q_blk, k_blk).astype(jnp.float32) * scale
        logits = logits + bias[None, :, None]
        logits = jnp.where(allowed, logits, -jnp.inf)
        p = jax.nn.softmax(logits, axis=-1)
        attn = p[:, :, 0] - lam * p[:, :, 1]
        outs.append(jnp.einsum('bhqk,bkhd->bqhd', attn.astype(v.dtype), v_blk))
    o = jnp.concatenate(outs, axis=1)
    of = o.astype(jnp.float32)
    of = of * lax.rsqrt(jnp.mean(jnp.square(of), axis=-1, keepdims=True) + RMS_EPS)
    of = of * subln_g.astype(jnp.float32) * (1.0 - lam_init)
    return of.astype(o.dtype).reshape(bsz, seq, N_HEADS_A * 2 * HEAD_DIM_A)


def mixer_block(u, w_in, b_gate, lambda_q, lambda_k, subln_g, rel_bias, conv_w,
                w_a_proj, w_b_proj, w_o, lam_init):
    bsz, seq, _ = u.shape
    proj = u @ w_in
    q, k, v, cb, cc, ch, ga, gb = jnp.split(proj, SPLITS, axis=-1)
    q = q.reshape(bsz, seq, N_HEADS_A, 2, HEAD_DIM_A)
    k = k.reshape(bsz, seq, N_HEADS_A, 2, HEAD_DIM_A)
    v = v.reshape(bsz, seq, N_HEADS_A, 2 * HEAD_DIM_A)
    lq = lambda_q.astype(jnp.float32)
    lk = lambda_k.astype(jnp.float32)
    lam = jnp.exp(jnp.sum(lq[0] * lk[0])) - jnp.exp(jnp.sum(lq[1] * lk[1])) + lam_init
    y_a = diff_attention(q, k, v, rel_bias, lam, lam_init, subln_g) @ w_a_proj
    z = cc * ch
    z = lax.conv_general_dilated(z, conv_w, window_strides=(1,),
                                 padding=[(CONV_WIDTH - 1, 0)],
                                 dimension_numbers=('NWC', 'WIO', 'NWC'),
                                 feature_group_count=D_CONV)
    y_b = (cb * z) @ w_b_proj
    g_a = jax.nn.sigmoid(ga + b_gate[0])
    g_b = jax.nn.sigmoid(gb + b_gate[1])
    return (g_a * y_a + g_b * y_b) @ w_o


def hier_moe(h, w_group, b_group, w_sub, b_sub, w_gate_e, w_up_e, w_down_e):
    bsz, seq, d = h.shape
    n_tok = bsz * seq
    xt = h.reshape(n_tok, d)
    xf = xt.astype(jnp.float32)
    g_prob = jax.nn.softmax(xf @ w_group.astype(jnp.float32) + b_group.astype(jnp.float32), axis=-1)
    g_p, g_idx = lax.top_k(g_prob, 1)
    sub_logits = jnp.einsum('td,gde->tge', xf, w_sub.astype(jnp.float32)) + b_sub.astype(jnp.float32)
    sel = jnp.take_along_axis(sub_logits, g_idx[:, :, None], axis=1)[:, 0]
    s_top, s_idx = lax.top_k(sel, TOP_K)
    weights = g_p * jax.nn.softmax(s_top, axis=-1)
    expert = g_idx * EXPERTS_PER_GROUP + s_idx
    n_assign = n_tok * TOP_K
    e_flat = expert.reshape(n_assign)
    tok_flat = jnp.repeat(jnp.arange(n_tok, dtype=jnp.int32), TOP_K)
    w_flat = weights.reshape(n_assign)
    order = jnp.argsort(e_flat)
    e_s, tok_s, w_s = e_flat[order], tok_flat[order], w_flat[order]
    counts = jnp.bincount(e_flat, length=N_EXPERTS)
    padded = ((counts + MOE_BLOCK - 1) // MOE_BLOCK) * MOE_BLOCK
    pad_end = jnp.cumsum(padded)
    pad_start = pad_end - padded
    cnt_start = jnp.cumsum(counts) - counts
    dest = pad_start[e_s] + (jnp.arange(n_assign, dtype=jnp.int32) - cnt_start[e_s])
    n_blocks = -(-n_assign // MOE_BLOCK) + N_EXPERTS
    n_rows = n_blocks * MOE_BLOCK
    tok_pad = jnp.full((n_rows,), n_tok, dtype=jnp.int32).at[dest].set(tok_s)
    w_pad = jnp.zeros((n_rows,), jnp.float32).at[dest].set(w_s)
    block_expert = jnp.minimum(
        jnp.searchsorted(pad_end, jnp.arange(n_blocks, dtype=jnp.int32) * MOE_BLOCK, side='right'),
        N_EXPERTS - 1).astype(jnp.int32)
    x_pad = jnp.concatenate([xt, jnp.zeros((1, d), xt.dtype)], axis=0)
    xb = x_pad[tok_pad].reshape(n_blocks, MOE_BLOCK, d)

    def expert_block(args):
        x_blk, e = args
        return (jax.nn.silu(x_blk @ w_gate_e[e]) * (x_blk @ w_up_e[e])) @ w_down_e[e]

    yb = lax.map(expert_block, (xb, block_expert))
    y = yb.reshape(n_rows, d).astype(jnp.float32) * w_pad[:, None]
    out = jax.ops.segment_sum(y, tok_pad, num_segments=n_tok + 1)[:n_tok]
    return out.astype(h.dtype).reshape(bsz, seq, d)


def setup_inputs(seed: int = 0) -> dict:
    key = jax.random.key(seed)
    ks = jax.random.split(key, 32)
    f32 = jnp.float32
    D = D_MODEL
    beta = DEEPNORM_BETA

    def nrm(k, shape, scale):
        return jax.random.normal(k, shape, f32) * scale

    w_q = nrm(ks[1], (DEPTH, D, D_QK_A), D ** -0.5)
    w_k = nrm(ks[2], (DEPTH, D, D_QK_A), D ** -0.5)
    w_v = nrm(ks[3], (DEPTH, D, D_V_A), beta * D ** -0.5)
    w_rest = nrm(ks[4], (DEPTH, D, 3 * D_CONV + 2 * D_MODEL), D ** -0.5)
    w_in = jnp.concatenate([w_q, w_k, w_v, w_rest], axis=-1)
    return {
        'x': jax.random.normal(ks[0], (BATCH, SEQ, D), f32),
        'ln_in_g': 1.0 + nrm(ks[5], (D,), 0.02),
        'ln_in_b': nrm(ks[6], (D,), 0.02),
        'w_in': w_in,
        'b_gate': nrm(ks[7], (DEPTH, 2, D), 0.02),
        'lambda_q': nrm(ks[8], (DEPTH, 2, HEAD_DIM_A), 0.1),
        'lambda_k': nrm(ks[9], (DEPTH, 2, HEAD_DIM_A), 0.1),
        'subln_g': 1.0 + nrm(ks[10], (DEPTH, 2 * HEAD_DIM_A), 0.02),
        'rel_bias': nrm(ks[11], (N_BUCKETS, N_HEADS_A), 0.5),
        'conv_w': nrm(ks[12], (DEPTH, CONV_WIDTH, 1, D_CONV), CONV_WIDTH ** -0.5),
        'w_a_proj': nrm(ks[13], (DEPTH, D_V_A, D), beta * D_V_A ** -0.5),
        'w_b_proj': nrm(ks[14], (DEPTH, D_CONV, D), beta * D_CONV ** -0.5),
        'w_o': nrm(ks[15], (DEPTH, D, D), beta * D ** -0.5),
        'ln1_g': 1.0 + nrm(ks[16], (DEPTH, D), 0.02),
        'ln1_b': nrm(ks[17], (DEPTH, D), 0.02),
        'w_group': nrm(ks[18], (DEPTH, D, N_GROUPS), D ** -0.5),
        'b_group': nrm(ks[19], (DEPTH, N_GROUPS), 0.01),
        'w_sub': nrm(ks[20], (DEPTH, N_GROUPS, D, EXPERTS_PER_GROUP), D ** -0.5),
        'b_sub': nrm(ks[21], (DEPTH, N_GROUPS, EXPERTS_PER_GROUP), 0.01),
        'w_gate_e': nrm(ks[22], (DEPTH, N_EXPERTS, D, D_EXPERT), beta * D ** -0.5),
        'w_up_e': nrm(ks[23], (DEPTH, N_EXPERTS, D, D_EXPERT), beta * D ** -0.5),
        'w_down_e': nrm(ks[24], (DEPTH, N_EXPERTS, D_EXPERT, D), beta * D_EXPERT ** -0.5),
        'ln2_g': 1.0 + nrm(ks[25], (DEPTH, D), 0.02),
        'ln2_b': nrm(ks[26], (DEPTH, D), 0.02),
    }


def reference(x, ln_in_g, ln_in_b, w_in, b_gate, lambda_q, lambda_k, subln_g, rel_bias,
              conv_w, w_a_proj, w_b_proj, w_o, ln1_g, ln1_b, w_group, b_group, w_sub, b_sub,
              w_gate_e, w_up_e, w_down_e, ln2_g, ln2_b):
    x = layer_norm(x, ln_in_g, ln_in_b)
    for l in range(DEPTH):
        lam_init = 0.8 - 0.6 * math.exp(-0.3 * l)
        mix = mixer_block(x, w_in[l], b_gate[l], lambda_q[l], lambda_k[l], subln_g[l], rel_bias,
                          conv_w[l], w_a_proj[l], w_b_proj[l], w_o[l], lam_init)
        x = layer_norm(DEEPNORM_ALPHA * x + mix, ln1_g[l], ln1_b[l])
        ffn = hier_moe(x, w_group[l], b_group[l], w_sub[l], b_sub[l],
                       w_gate_e[l], w_up_e[l], w_down_e[l])
        x = layer_norm(DEEPNORM_ALPHA * x + ffn, ln2_g[l], ln2_b[l])
    return x
```

```python
import functools
import math

import numpy as np
import jax
import jax.numpy as jnp
from jax import lax
from jax.experimental import pallas as pl
from jax.experimental.pallas import tpu as pltpu

F32 = jnp.float32
BF16 = jnp.bfloat16

D_MODEL = 1024
N_HEADS = 4
HEAD_DIM = 128
D_HEAD_V = 2 * HEAD_DIM
CHUNK = 64
N_BUCKETS = 32
MAX_DISTANCE = 128
N_GROUPS = 4
EXPERTS_PER_GROUP = 8
N_EXPERTS = N_GROUPS * EXPERTS_PER_GROUP
TOP_K = 2
D_EXPERT = 512
MOE_BLOCK = 512
LN_EPS = 1e-5
RMS_EPS = 1e-6
DEPTH = 1
DEEPNORM_ALPHA = (2.0 * DEPTH) ** 0.25
N_IN = 8 * D_MODEL
LANES = 128
NEG_BIG = -1e30

VMEM_LIMIT = 56 * 1024 * 1024


def _layer_norm(x, g, b):
    mu = jnp.mean(x, axis=-1, keepdims=True)
    xc = x - mu
    var = jnp.mean(xc * xc, axis=-1, keepdims=True)
    return xc * lax.rsqrt(var + LN_EPS) * g + b


def _ln_proj_kernel(x_ref, g_ref, b_ref, w_ref, xn_ref, p_ref, *, q_scale):
    xn = _layer_norm(x_ref[...], g_ref[...], b_ref[...])
    xn_ref[...] = xn
    xb = xn.astype(BF16)
    for c in range(N_IN // D_MODEL):
        cols = slice(c * D_MODEL, (c + 1) * D_MODEL)
        acc = jnp.dot(xb, w_ref[:, cols], preferred_element_type=F32)
        if c == 0:
            acc = acc * q_scale
        p_ref[:, cols] = acc.astype(BF16)


def _ln_proj(x2, g, b, w_bf, tm):
    n_tok = x2.shape[0]
    kern = functools.partial(_ln_proj_kernel, q_scale=HEAD_DIM ** -0.5)
    return pl.pallas_call(
        kern,
        grid=(n_tok // tm,),
        in_specs=[
            pl.BlockSpec((tm, D_MODEL), lambda r: (r, 0)),
            pl.BlockSpec((1, D_MODEL), lambda r: (0, 0)),
            pl.BlockSpec((1, D_MODEL), lambda r: (0, 0)),
            pl.BlockSpec((D_MODEL, N_IN), lambda r: (0, 0), pipeline_mode=pl.Buffered(1)),
        ],
        out_specs=[
            pl.BlockSpec((tm, D_MODEL), lambda r: (r, 0)),
            pl.BlockSpec((tm, N_IN), lambda r: (r, 0)),
        ],
        out_shape=[
            jax.ShapeDtypeStruct((n_tok, D_MODEL), F32),
            jax.ShapeDtypeStruct((n_tok, N_IN), BF16),
        ],
        compiler_params=pltpu.CompilerParams(
            dimension_semantics=("arbitrary",), vmem_limit_bytes=VMEM_LIMIT),
        name="ln_proj",
    )(x2, g, b, w_bf)


def _t5_bucket_np(rel):
    nb = N_BUCKETS // 2
    max_exact = nb // 2
    n = np.abs(rel)
    large = np.full(n.shape, max_exact, np.int64)
    for d in range(max_exact, MAX_DISTANCE + 1):
        val = max_exact + int(math.log(d / max_exact) / math.log(MAX_DISTANCE / max_exact)
                              * (nb - max_exact))
        large = np.where(n >= d, min(val, nb - 1), large)
    return np.where(rel > 0, nb, 0) + np.where(n < max_exact, n, large)


def _band_bias(rel_bias, tq):
    qi = np.arange(tq)[:, None]
    kj = np.arange(2 * tq)[None, :] - tq
    bucket = _t5_bucket_np(kj - qi)
    allowed = (kj // CHUNK) <= (qi // CHUNK)
    far_bucket = int(_t5_bucket_np(np.array([-(tq + 1)]))[0])
    rb = rel_bias.astype(F32)
    tile = jnp.transpose(rb[jnp.asarray(bucket)], (2, 0, 1)) - rb[far_bucket][:, None, None]
    return jnp.where(jnp.asarray(allowed)[None], tile, NEG_BIG)


def _diff_attn_kernel(q_ref, k_ref, v_ref, bias_ref, lq_ref, lk_ref, sg_ref, o_ref,
                      s_ref, mx_ref, ls_ref, acc_ref, *, tq, lam_init):
    i = pl.program_id(1)
    n_lane_tiles = tq // LANES

    lq = lq_ref[...]
    lk = lk_ref[...]
    dots = jnp.sum(lq * lk, axis=-1, keepdims=True)
    lam = jnp.exp(dots[0:1]) - jnp.exp(dots[1:2]) + lam_init

    def tile_max(s):
        m = s[:, 0:LANES]
        for c in range(1, n_lane_tiles):
            m = jnp.maximum(m, s[:, c * LANES:(c + 1) * LANES])
        return m

    def tile_sum(s):
        m = s[:, 0:LANES]
        for c in range(1, n_lane_tiles):
            m = m + s[:, c * LANES:(c + 1) * LANES]
        return m

    for h in range(N_HEADS):
        c0 = h * D_HEAD_V
        q_maps = (q_ref[:, c0:c0 + HEAD_DIM], q_ref[:, c0 + HEAD_DIM:c0 + D_HEAD_V])

        def logits(j, m, c0=c0, q_maps=q_maps):
            ks = pl.multiple_of(j * tq, tq)
            k_blk = k_ref[pl.ds(ks, tq), c0 + m * HEAD_DIM:c0 + (m + 1) * HEAD_DIM]
            return lax.dot_general(q_maps[m], k_blk, (((1,), (1,)), ((), ())),
                                   preferred_element_type=F32)

        for m in range(2):
            s = logits(i, m) + bias_ref[h, :, tq:2 * tq]
            s_ref[m, i] = s
            mx_ref[m] = tile_max(s)

        @pl.when(i >= 1)
        def _():
            for m in range(2):
                s = logits(i - 1, m) + bias_ref[h, :, 0:tq]
                s_ref[m, i - 1] = s
                mx_ref[m] = jnp.maximum(mx_ref[m], tile_max(s))

        def far_body(j, carry):
            for m in range(2):
                s = logits(j, m)
                s_ref[m, j] = s
                mx_ref[m] = jnp.maximum(mx_ref[m], tile_max(s))
            return carry

        lax.fori_loop(0, jnp.maximum(i - 1, 0), far_body, 0)

        row_max = [jnp.max(mx_ref[m], axis=-1, keepdims=True) for m in range(2)]

        def pv(j, first, c0=c0, row_max=row_max):
            ks = pl.multiple_of(j * tq, tq)
            v_blk = v_ref[pl.ds(ks, tq), c0:c0 + D_HEAD_V]
            e = [jnp.exp(s_ref[m, j] - row_max[m]) for m in range(2)]
            p = jnp.concatenate(e, axis=0).astype(BF16)
            contrib = jnp.dot(p, v_blk, preferred_element_type=F32)
            if first:
                for m in range(2):
                    ls_ref[m] = tile_sum(e[m])
                acc_ref[...] = contrib
            else:
                for m in range(2):
                    ls_ref[m] = ls_ref[m] + tile_sum(e[m])
                acc_ref[...] = acc_ref[...] + contrib

        pv(i, True)

        def pv_body(j, carry):
            pv(j, False)
            return carry

        lax.fori_loop(0, i, pv_body, 0)

        l1 = jnp.sum(ls_ref[0], axis=-1, keepdims=True)
        l2 = jnp.sum(ls_ref[1], axis=-1, keepdims=True)
        o = acc_ref[0:tq, :] / l1 - lam * (acc_ref[tq:2 * tq, :] / l2)
        o = o * lax.rsqrt(jnp.mean(o * o, axis=-1, keepdims=True) + RMS_EPS)
        o = o * sg_ref[...] * (1.0 - lam_init)
        o_ref[:, c0:c0 + D_HEAD_V] = o.astype(BF16)


def _diff_attn(p, band, lam_q, lam_k, subln_g, bsz, seq, tq, lam_init):
    n_tok = bsz * seq
    nq = seq // tq
    kern = functools.partial(_diff_attn_kernel, tq=tq, lam_init=lam_init)
    return pl.pallas_call(
        kern,
        grid=(bsz, nq),
        in_specs=[
            pl.BlockSpec((tq, D_MODEL), lambda b, i: (b * nq + i, 0)),
            pl.BlockSpec((seq, D_MODEL), lambda b, i: (b, 1)),
            pl.BlockSpec((seq, D_MODEL), lambda b, i: (b, 2)),
            pl.BlockSpec((N_HEADS, tq, 2 * tq), lambda b, i: (0, 0, 0),
                         pipeline_mode=pl.Buffered(1)),
            pl.BlockSpec((2, HEAD_DIM), lambda b, i: (0, 0)),
            pl.BlockSpec((2, HEAD_DIM), lambda b, i: (0, 0)),
            pl.BlockSpec((1, D_HEAD_V), lambda b, i: (0, 0)),
        ],
        out_specs=pl.BlockSpec((tq, D_MODEL), lambda b, i: (b * nq + i, 0)),
        out_shape=jax.ShapeDtypeStruct((n_tok, D_MODEL), BF16),
        scratch_shapes=[
            pltpu.VMEM((2, nq, tq, tq), F32),
            pltpu.VMEM((2, tq, LANES), F32),
            pltpu.VMEM((2, tq, LANES), F32),
            pltpu.VMEM((2 * tq, D_HEAD_V), F32),
        ],
        compiler_params=pltpu.CompilerParams(
            dimension_semantics=("arbitrary", "arbitrary"), vmem_limit_bytes=VMEM_LIMIT),
        name="diff_attn",
    )(p, p, p, band, lam_q, lam_k, subln_g)


HALO = 16


def _mix_route_kernel(on_ref, cb_ref, cc_ref, ch_ref, ga_ref, gb_ref, cch_ref, chh_ref, xn_ref,
                      wa_ref, wb_ref, wo_ref, cw_ref, bg_ref, g1_ref, b1_ref, wr_ref, br_ref,
                      tri_ref, x1_ref, route_ref, cnt_ref, zs_ref, run_ref, *, tm, tiles_per_seq):
    r = pl.program_id(0)

    @pl.when(r == 0)
    def _():
        run_ref[...] = jnp.zeros_like(run_ref)

    z = cc_ref[...].astype(F32) * ch_ref[...].astype(F32)
    zh = cch_ref[...].astype(F32) * chh_ref[...].astype(F32)
    seq_start = (r % tiles_per_seq) == 0
    zh = jnp.where(seq_start, 0.0, zh)
    zs_ref[0:HALO, :] = zh
    zs_ref[HALO:HALO + tm, :] = z
    z1 = zs_ref[HALO - 1:HALO - 1 + tm, :]
    z2 = zs_ref[HALO - 2:HALO - 2 + tm, :]
    cw = cw_ref[...]
    zc = cw[0:1] * z2 + cw[1:2] * z1 + cw[2:3] * z
    yb_in = (cb_ref[...].astype(F32) * zc).astype(BF16)
    y_b = jnp.dot(yb_in, wb_ref[...], preferred_element_type=F32)
    y_a = jnp.dot(on_ref[...], wa_ref[...], preferred_element_type=F32)

    bg = bg_ref[...]
    g_a = jax.nn.sigmoid(ga_ref[...].astype(F32) + bg[0:1])
    g_b = jax.nn.sigmoid(gb_ref[...].astype(F32) + bg[1:2])
    merged = (g_a * y_a + g_b * y_b).astype(BF16)
    mix = jnp.dot(merged, wo_ref[...], preferred_element_type=F32)
    x1 = _layer_norm(DEEPNORM_ALPHA * xn_ref[...] + mix, g1_ref[...], b1_ref[...])
    x1_ref[...] = x1

    logit = jnp.dot(x1, wr_ref[...], preferred_element_type=F32,
                    precision=lax.Precision.HIGHEST) + br_ref[...]
    lane = lax.broadcasted_iota(jnp.int32, (tm, LANES), 1).astype(F32)
    big = float(4 * LANES)

    def first_argmax(vals, vmax):
        return jnp.min(jnp.where(vals == vmax, lane, big), axis=-1, keepdims=True)

    gl = jnp.where(lane < N_GROUPS, logit, -jnp.inf)
    gmax = jnp.max(gl, axis=-1, keepdims=True)
    gsum = jnp.sum(jnp.exp(gl - gmax), axis=-1, keepdims=True)
    g_p = 1.0 / gsum
    g_idx = first_argmax(gl, gmax)
    lo = N_GROUPS + EXPERTS_PER_GROUP * g_idx
    in_group = (lane >= lo) & (lane < lo + EXPERTS_PER_GROUP)
    sl = jnp.where(in_group, logit, -jnp.inf)
    s1 = jnp.max(sl, axis=-1, keepdims=True)
    i1 = first_argmax(sl, s1)
    sl2 = jnp.where(lane == i1, -jnp.inf, sl)
    s2 = jnp.max(sl2, axis=-1, keepdims=True)
    i2 = first_argmax(sl2, s2)
    t = jnp.exp(s2 - s1)
    w1 = g_p / (1.0 + t)
    w2 = g_p * t / (1.0 + t)
    e1 = i1 - N_GROUPS
    e2 = i2 - N_GROUPS

    oh1 = lane == e1
    oh2 = lane == e2
    onehot = jnp.where(oh1 | oh2, 1.0, 0.0)
    before = jnp.dot(tri_ref[...], onehot.astype(BF16), preferred_element_type=F32) + run_ref[...]
    rank1 = jnp.sum(jnp.where(oh1, before, 0.0), axis=-1, keepdims=True)
    rank2 = jnp.sum(jnp.where(oh2, before, 0.0), axis=-1, keepdims=True)
    run_ref[...] = run_ref[...] + jnp.sum(onehot, axis=0, keepdims=True)
    cnt_ref[...] = jnp.broadcast_to(run_ref[...], cnt_ref.shape)

    route = jnp.where(lane == 0, e1, 0.0)
    for idx, val in enumerate((e2, w1, w2, rank1, rank2), start=1):
        route = jnp.where(lane == idx, val, route)
    route_ref[...] = route[:, 0:route_ref.shape[1]]


ROUTE_COLS = 8


def _mix_route(o_n, p, xn, wa, wb, wo, conv_w, b_gate, g1, b1, wr, br, seq, tm):
    n_tok = xn.shape[0]
    halo_blocks = tm // HALO
    tri = jnp.asarray(np.tril(np.ones((tm, tm), np.float32), k=-1), BF16)
    kern = functools.partial(_mix_route_kernel, tm=tm, tiles_per_seq=seq // tm)

    def col(c):
        return pl.BlockSpec((tm, D_MODEL), lambda r, c=c: (r, c))

    def halo(c):
        return pl.BlockSpec((HALO, D_MODEL),
                            lambda r, c=c: (jnp.maximum(r * halo_blocks - 1, 0), c))

    def const(shape):
        return pl.BlockSpec(shape, lambda r: (0,) * len(shape), pipeline_mode=pl.Buffered(1))

    return pl.pallas_call(
        kern,
        grid=(n_tok // tm,),
        in_specs=[
            pl.BlockSpec((tm, D_MODEL), lambda r: (r, 0)),
            col(3), col(4), col(5), col(6), col(7),
            halo(4), halo(5),
            pl.BlockSpec((tm, D_MODEL), lambda r: (r, 0)),
            const((D_MODEL, D_MODEL)), const((D_MODEL, D_MODEL)), const((D_MODEL, D_MODEL)),
            const((3, D_MODEL)), const((2, D_MODEL)), const((1, D_MODEL)), const((1, D_MODEL)),
            const((D_MODEL, LANES)), const((1, LANES)), const((tm, tm)),
        ],
        out_specs=[
            pl.BlockSpec((tm, D_MODEL), lambda r: (r, 0)),
            pl.BlockSpec((tm, ROUTE_COLS), lambda r: (r, 0)),
            pl.BlockSpec((8, LANES), lambda r: (0, 0)),
        ],
        out_shape=[
            jax.ShapeDtypeStruct((n_tok, D_MODEL), F32),
            jax.ShapeDtypeStruct((n_tok, ROUTE_COLS), F32),
            jax.ShapeDtypeStruct((8, LANES), F32),
        ],
        scratch_shapes=[
            pltpu.VMEM((HALO + tm, D_MODEL), F32),
            pltpu.VMEM((1, LANES), F32),
        ],
        compiler_params=pltpu.CompilerParams(
            dimension_semantics=("arbitrary",), vmem_limit_bytes=VMEM_LIMIT),
        name="mix_route",
    )(o_n, p, p, p, p, p, p, p, xn, wa, wb, wo, conv_w, b_gate, g1, b1, wr, br, tri)


def _row_copy(src, src_row, dst, dst_row, sem):
    return pltpu.make_async_copy(src.at[pl.ds(src_row, 1), :], dst.at[pl.ds(dst_row, 1), :], sem)


def _dispatch_kernel(dest_ref, x_ref, init_ref, out_ref, sem, *, td):
    del init_ref

    def issue(t, carry):
        for k in range(TOP_K):
            _row_copy(x_ref, t, out_ref, dest_ref[0, 0, k * td + t], sem).start()
        return carry

    lax.fori_loop(0, td, issue, 0, unroll=8)
    for k in range(TOP_K):
        pltpu.make_async_copy(x_ref, out_ref.at[pl.ds(0, td), :], sem).wait()


def _dispatch(x1, dest3, n_rows, td):
    n_tok = x1.shape[0]
    init = jnp.zeros((n_rows, D_MODEL), F32)
    return pl.pallas_call(
        functools.partial(_dispatch_kernel, td=td),
        grid=(n_tok // td,),
        in_specs=[
            pl.BlockSpec((1, 1, TOP_K * td), lambda s: (s, 0, 0), memory_space=pltpu.SMEM),
            pl.BlockSpec((td, D_MODEL), lambda s: (s, 0)),
            pl.BlockSpec(memory_space=pl.ANY),
        ],
        out_specs=pl.BlockSpec(memory_space=pl.ANY),
        out_shape=jax.ShapeDtypeStruct((n_rows, D_MODEL), F32),
        scratch_shapes=[pltpu.SemaphoreType.DMA(())],
        input_output_aliases={2: 0},
        compiler_params=pltpu.CompilerParams(dimension_semantics=("arbitrary",)),
        name="dispatch",
    )(dest3, x1, init)


def _expert_ffn_kernel(be_ref, na_ref, x_ref, wg_ref, wu_ref, wd_ref, y_ref):
    del be_ref
    active = pl.program_id(0) < na_ref[0]

    @pl.when(active)
    def _():
        x = x_ref[...].astype(BF16)
        g = jnp.dot(x, wg_ref[...], preferred_element_type=F32)
        u = jnp.dot(x, wu_ref[...], preferred_element_type=F32)
        hid = (jax.nn.silu(g) * u).astype(BF16)
        y_ref[...] = jnp.dot(hid, wd_ref[...], preferred_element_type=F32)

    @pl.when(jnp.logical_not(active))
    def _():
        y_ref[...] = jnp.zeros_like(y_ref)


def _expert_ffn(xs, block_expert, n_active, wg, wu, wd):
    n_rows = xs.shape[0]
    n_blocks = n_rows // MOE_BLOCK

    def blk(i, be, na):
        return (jnp.minimum(i, na[0] - 1), 0)

    def wmap(i, be, na):
        return (be[jnp.minimum(i, na[0] - 1)], 0, 0)

    grid_spec = pltpu.PrefetchScalarGridSpec(
        num_scalar_prefetch=2,
        grid=(n_blocks,),
        in_specs=[
            pl.BlockSpec((MOE_BLOCK, D_MODEL), blk),
            pl.BlockSpec((None, D_MODEL, D_EXPERT), wmap),
            pl.BlockSpec((None, D_MODEL, D_EXPERT), wmap),
            pl.BlockSpec((None, D_EXPERT, D_MODEL), wmap),
        ],
        out_specs=pl.BlockSpec((MOE_BLOCK, D_MODEL), lambda i, be, na: (i, 0)),
    )
    return pl.pallas_call(
        _expert_ffn_kernel,
        grid_spec=grid_spec,
        out_shape=jax.ShapeDtypeStruct((n_rows, D_MODEL), F32),
        compiler_params=pltpu.CompilerParams(
            dimension_semantics=("arbitrary",), vmem_limit_bytes=VMEM_LIMIT),
        name="expert_ffn",
    )(block_expert, n_active, xs, wg, wu, wd)


def _combine_kernel(dcur_ref, dnxt_ref, x_ref, route_ref, g_ref, b_ref, y_ref, o_ref,
                    buf, sem, *, td):
    s = pl.program_id(0)
    n = pl.num_programs(0)
    slot = s % 2

    def issue(d_ref, to_slot):
        def body(t, carry):
            for k in range(TOP_K):
                _row_copy(y_ref, d_ref[0, 0, k * td + t], buf.at[to_slot, k], t,
                          sem.at[to_slot]).start()
            return carry
        lax.fori_loop(0, td, body, 0, unroll=8)

    @pl.when(s == 0)
    def _():
        issue(dcur_ref, 0)

    @pl.when(s + 1 < n)
    def _():
        issue(dnxt_ref, 1 - slot)

    for k in range(TOP_K):
        pltpu.make_async_copy(y_ref.at[pl.ds(0, td), :], buf.at[slot, k], sem.at[slot]).wait()

    route = route_ref[...]
    ffn = route[:, 2:3] * buf[slot, 0] + route[:, 3:4] * buf[slot, 1]
    o_ref[...] = _layer_norm(DEEPNORM_ALPHA * x_ref[...] + ffn, g_ref[...], b_ref[...])


def _combine(ys, dest3, x1, route, g2, b2, td):
    n_tok = x1.shape[0]
    n_steps = n_tok // td
    return pl.pallas_call(
        functools.partial(_combine_kernel, td=td),
        grid=(n_steps,),
        in_specs=[
            pl.BlockSpec((1, 1, TOP_K * td), lambda s: (s, 0, 0), memory_space=pltpu.SMEM),
            pl.BlockSpec((1, 1, TOP_K * td), lambda s: (jnp.minimum(s + 1, n_steps - 1), 0, 0),
                         memory_space=pltpu.SMEM),
            pl.BlockSpec((td, D_MODEL), lambda s: (s, 0)),
            pl.BlockSpec((td, ROUTE_COLS), lambda s: (s, 0)),
            pl.BlockSpec((1, D_MODEL), lambda s: (0, 0)),
            pl.BlockSpec((1, D_MODEL), lambda s: (0, 0)),
            pl.BlockSpec(memory_space=pl.ANY),
        ],
        out_specs=pl.BlockSpec((td, D_MODEL), lambda s: (s, 0)),
        out_shape=jax.ShapeDtypeStruct((n_tok, D_MODEL), F32),
        scratch_shapes=[
            pltpu.VMEM((2, TOP_K, td, D_MODEL), F32),
            pltpu.SemaphoreType.DMA((2,)),
        ],
        compiler_params=pltpu.CompilerParams(
            dimension_semantics=("arbitrary",), vmem_limit_bytes=VMEM_LIMIT),
        name="combine",
    )(dest3, dest3, x1, route, g2, b2, ys)


def _row_tile(n, want):
    t = min(want, n)
    while n % t:
        t //= 2
    return t


def kernel(x, ln_in_g, ln_in_b, w_in, b_gate, lambda_q, lambda_k, subln_g, rel_bias, conv_w,
           w_a_proj, w_b_proj, w_o, ln1_g, ln1_b, w_group, b_group, w_sub, b_sub,
           w_gate_e, w_up_e, w_down_e, ln2_g, ln2_b):
    bsz, seq, d = x.shape
    assert DEPTH == 1 and d == D_MODEL and w_in.shape == (DEPTH, D_MODEL, N_IN)
    n_tok = bsz * seq
    tq = _row_tile(seq, 256)
    assert tq % LANES == 0 and tq % CHUNK == 0
    tm1 = _row_tile(n_tok, 512)
    tm3 = _row_tile(seq, 256)
    td = _row_tile(n_tok, 256)
    row = lambda v: v.reshape(1, -1).astype(F32)

    cur = x.reshape(n_tok, d)
    g_in, b_in = row(ln_in_g), row(ln_in_b)
    for l in range(DEPTH):
        lam_init = 0.8 - 0.6 * math.exp(-0.3 * l)
        xn, p = _ln_proj(cur, g_in, b_in, w_in[l].astype(BF16), tm1)

        band = _band_bias(rel_bias, tq)
        o_n = _diff_attn(p, band, lambda_q[l].astype(F32), lambda_k[l].astype(F32),
                         row(subln_g[l]), bsz, seq, tq, lam_init)

        w_r = jnp.concatenate(
            [w_group[l].astype(F32),
             jnp.transpose(w_sub[l].astype(F32), (1, 0, 2)).reshape(d, N_EXPERTS)], axis=1)
        w_r = jnp.pad(w_r, ((0, 0), (0, LANES - w_r.shape[1])))
        b_r = jnp.concatenate([b_group[l].astype(F32), b_sub[l].astype(F32).reshape(-1)])
        b_r = jnp.pad(b_r, (0, LANES - b_r.shape[0])).reshape(1, LANES)
        x1, route, counts = _mix_route(
            o_n, p, xn, w_a_proj[l].astype(BF16), w_b_proj[l].astype(BF16), w_o[l].astype(BF16),
            conv_w[l].reshape(3, d).astype(F32), b_gate[l].astype(F32), row(ln1_g[l]),
            row(ln1_b[l]), w_r, b_r, seq, tm3)

        n_assign = n_tok * TOP_K
        n_blocks = -(-n_assign // MOE_BLOCK) + N_EXPERTS
        cnt = counts[0, :N_EXPERTS].astype(jnp.int32)
        padded = ((cnt + MOE_BLOCK - 1) // MOE_BLOCK) * MOE_BLOCK
        pad_end = jnp.cumsum(padded)
        pad_start = pad_end - padded
        n_active = (pad_end[-1:] // MOE_BLOCK).astype(jnp.int32)
        block_expert = jnp.minimum(
            jnp.searchsorted(pad_end, jnp.arange(n_blocks, dtype=jnp.int32) * MOE_BLOCK,
                             side='right'), N_EXPERTS - 1).astype(jnp.int32)
        experts = route[:, 0:TOP_K].astype(jnp.int32)
        ranks = route[:, 4:4 + TOP_K].astype(jnp.int32)
        dest = pad_start[experts] + ranks
        dest3 = jnp.transpose(dest.reshape(n_tok // td, td, TOP_K), (0, 2, 1))
        dest3 = dest3.reshape(n_tok // td, 1, TOP_K * td).astype(jnp.int32)

        xs = _dispatch(x1, dest3, n_blocks * MOE_BLOCK, td)
        ys = _expert_ffn(xs, block_expert, n_active, w_gate_e[l].astype(BF16),
                         w_up_e[l].astype(BF16), w_down_e[l].astype(BF16))
        cur = _combine(ys, dest3, x1, route, row(ln2_g[l]), row(ln2_b[l]), td)
    return cur.reshape(bsz, seq, d)
```

```python
import functools
import math

import numpy as np
import jax
import jax.numpy as jnp
from jax import lax
from jax.experimental import pallas as pl
from jax.experimental.pallas import tpu as pltpu

F32 = jnp.float32
BF16 = jnp.bfloat16

D_MODEL = 1024
N_HEADS = 4
HEAD_DIM = 128
D_HEAD_V = 2 * HEAD_DIM
CHUNK = 64
N_BUCKETS = 32
MAX_DISTANCE = 128
N_GROUPS = 4
EXPERTS_PER_GROUP = 8
N_EXPERTS = N_GROUPS * EXPERTS_PER_GROUP
TOP_K = 2
D_EXPERT = 512
MOE_BLOCK = 512
LN_EPS = 1e-5
RMS_EPS = 1e-6
DEPTH = 1
DEEPNORM_ALPHA = (2.0 * DEPTH) ** 0.25
N_IN = 8 * D_MODEL
LANES = 128
NEG_BIG = -1e30

VMEM_LIMIT = 56 * 1024 * 1024


def _layer_norm(x, g, b):
    mu = jnp.mean(x, axis=-1, keepdims=True)
    xc = x - mu
    var = jnp.mean(xc * xc, axis=-1, keepdims=True)
    return xc * lax.rsqrt(var + LN_EPS) * g + b


def _ln_proj_kernel(x_ref, g_ref, b_ref, w_ref, xn_ref, p_ref, *, q_scale):
    xn = _layer_norm(x_ref[...], g_ref[...], b_ref[...])
    xn_ref[...] = xn
    xb = xn.astype(BF16)
    for c in range(N_IN // D_MODEL):
        cols = slice(c * D_MODEL, (c + 1) * D_MODEL)
        acc = jnp.dot(xb, w_ref[:, cols], preferred_element_type=F32)
        if c == 0:
            acc = acc * q_scale
        p_ref[:, cols] = acc.astype(BF16)


def _ln_proj(x2, g, b, w_bf, tm):
    n_tok = x2.shape[0]
    kern = functools.partial(_ln_proj_kernel, q_scale=HEAD_DIM ** -0.5)
    return pl.pallas_call(
        kern,
        grid=(n_tok // tm,),
        in_specs=[
            pl.BlockSpec((tm, D_MODEL), lambda r: (r, 0)),
            pl.BlockSpec((1, D_MODEL), lambda r: (0, 0)),
            pl.BlockSpec((1, D_MODEL), lambda r: (0, 0)),
            pl.BlockSpec((D_MODEL, N_IN), lambda r: (0, 0), pipeline_mode=pl.Buffered(1)),
        ],
        out_specs=[
            pl.BlockSpec((tm, D_MODEL), lambda r: (r, 0)),
            pl.BlockSpec((tm, N_IN), lambda r: (r, 0)),
        ],
        out_shape=[
            jax.ShapeDtypeStruct((n_tok, D_MODEL), F32),
            jax.ShapeDtypeStruct((n_tok, N_IN), BF16),
        ],
        compiler_params=pltpu.CompilerParams(
            dimension_semantics=("arbitrary",), vmem_limit_bytes=VMEM_LIMIT),
        name="ln_proj",
    )(x2, g, b, w_bf)


def _t5_bucket_np(rel):
    nb = N_BUCKETS // 2
    max_exact = nb // 2
    n = np.abs(rel)
    large = np.full(n.shape, max_exact, np.int64)
    for d in range(max_exact, MAX_DISTANCE + 1):
        val = max_exact + int(math.log(d / max_exact) / math.log(MAX_DISTANCE / max_exact)
                              * (nb - max_exact))
        large = np.where(n >= d, min(val, nb - 1), large)
    return np.where(rel > 0, nb, 0) + np.where(n < max_exact, n, large)


def _band_bias(rel_bias, tq):
    qi = np.arange(tq)[:, None]
    kj = np.arange(2 * tq)[None, :] - tq
    bucket = _t5_bucket_np(kj - qi)
    allowed = (kj // CHUNK) <= (qi // CHUNK)
    far_bucket = int(_t5_bucket_np(np.array([-(tq + 1)]))[0])
    rb = rel_bias.astype(F32)
    tile = jnp.transpose(rb[jnp.asarray(bucket)], (2, 0, 1)) - rb[far_bucket][:, None, None]
    return jnp.where(jnp.asarray(allowed)[None], tile, NEG_BIG)


def _diff_attn_kernel(q_ref, k_ref, v_ref, bias_ref, lq_ref, lk_ref, sg_ref, o_ref,
                      s_ref, mx_ref, ls_ref, acc_ref, *, tq, lam_init):
    i = pl.program_id(1)
    n_lane_tiles = tq // LANES

    lq = lq_ref[...]
    lk = lk_ref[...]
    dots = jnp.sum(lq * lk, axis=-1, keepdims=True)
    lam = jnp.exp(dots[0:1]) - jnp.exp(dots[1:2]) + lam_init

    def tile_max(s):
        m = s[:, 0:LANES]
        for c in range(1, n_lane_tiles):
            m = jnp.maximum(m, s[:, c * LANES:(c + 1) * LANES])
        return m

    def tile_sum(s):
        m = s[:, 0:LANES]
        for c in range(1, n_lane_tiles):
            m = m + s[:, c * LANES:(c + 1) * LANES]
        return m

    for h in range(N_HEADS):
        c0 = h * D_HEAD_V
        q_maps = (q_ref[:, c0:c0 + HEAD_DIM], q_ref[:, c0 + HEAD_DIM:c0 + D_HEAD_V])

        def logits(j, m, c0=c0, q_maps=q_maps):
            ks = pl.multiple_of(j * tq, tq)
            k_blk = k_ref[pl.ds(ks, tq), c0 + m * HEAD_DIM:c0 + (m + 1) * HEAD_DIM]
            return lax.dot_general(q_maps[m], k_blk, (((1,), (1,)), ((), ())),
                                   preferred_element_type=F32)

        for m in range(2):
            s = logits(i, m) + bias_ref[h, :, tq:2 * tq]
            s_ref[m, i] = s
            mx_ref[m] = tile_max(s)

        @pl.when(i >= 1)
        def _():
            for m in range(2):
                s = logits(i - 1, m) + bias_ref[h, :, 0:tq]
                s_ref[m, i - 1] = s
                mx_ref[m] = jnp.maximum(mx_ref[m], tile_max(s))

        def far_body(j, carry):
            for m in range(2):
                s = logits(j, m)
                s_ref[m, j] = s
                mx_ref[m] = jnp.maximum(mx_ref[m], tile_max(s))
            return carry

        lax.fori_loop(0, jnp.maximum(i - 1, 0), far_body, 0)

        row_max = [jnp.max(mx_ref[m], axis=-1, keepdims=True) for m in range(2)]

        def pv(j, first, c0=c0, row_max=row_max):
            ks = pl.multiple_of(j * tq, tq)
            v_blk = v_ref[pl.ds(ks, tq), c0:c0 + D_HEAD_V]
            e = [jnp.exp(s_ref[m, j] - row_max[m]) for m in range(2)]
            p = jnp.concatenate(e, axis=0).astype(BF16)
            contrib = jnp.dot(p, v_blk, preferred_element_type=F32)
            if first:
                for m in range(2):
                    ls_ref[m] = tile_sum(e[m])
                acc_ref[...] = contrib
            else:
                for m in range(2):
                    ls_ref[m] = ls_ref[m] + tile_sum(e[m])
                acc_ref[...] = acc_ref[...] + contrib

        pv(i, True)

        def pv_body(j, carry):
            pv(j, False)
            return carry

        lax.fori_loop(0, i, pv_body, 0)

        l1 = jnp.sum(ls_ref[0], axis=-1, keepdims=True)
        l2 = jnp.sum(ls_ref[1], axis=-1, keepdims=True)
        o = acc_ref[0:tq, :] / l1 - lam * (acc_ref[tq:2 * tq, :] / l2)
        o = o * lax.rsqrt(jnp.mean(o * o, axis=-1, keepdims=True) + RMS_EPS)
        o = o * sg_ref[...] * (1.0 - lam_init)
        o_ref[:, c0:c0 + D_HEAD_V] = o.astype(BF16)


def _diff_attn(p, band, lam_q, lam_k, subln_g, bsz, seq, tq, lam_init):
    n_tok = bsz * seq
    nq = seq // tq
    kern = functools.partial(_diff_attn_kernel, tq=tq, lam_init=lam_init)
    return pl.pallas_call(
        kern,
        grid=(bsz, nq),
        in_specs=[
            pl.BlockSpec((tq, D_MODEL), lambda b, i: (b * nq + i, 0)),
            pl.BlockSpec((seq, D_MODEL), lambda b, i: (b, 1)),
            pl.BlockSpec((seq, D_MODEL), lambda b, i: (b, 2)),
            pl.BlockSpec((N_HEADS, tq, 2 * tq), lambda b, i: (0, 0, 0),
                         pipeline_mode=pl.Buffered(1)),
            pl.BlockSpec((2, HEAD_DIM), lambda b, i: (0, 0)),
            pl.BlockSpec((2, HEAD_DIM), lambda b, i: (0, 0)),
            pl.BlockSpec((1, D_HEAD_V), lambda b, i: (0, 0)),
        ],
        out_specs=pl.BlockSpec((tq, D_MODEL), lambda b, i: (b * nq + i, 0)),
        out_shape=jax.ShapeDtypeStruct((n_tok, D_MODEL), BF16),
        scratch_shapes=[
            pltpu.VMEM((2, nq, tq, tq), F32),
            pltpu.VMEM((2, tq, LANES), F32),
            pltpu.VMEM((2, tq, LANES), F32),
            pltpu.VMEM((2 * tq, D_HEAD_V), F32),
        ],
        compiler_params=pltpu.CompilerParams(
            dimension_semantics=("arbitrary", "arbitrary"), vmem_limit_bytes=VMEM_LIMIT),
        name="diff_attn",
    )(p, p, p, band, lam_q, lam_k, subln_g)


HALO = 16


def _mix_route_kernel(on_ref, cb_ref, cc_ref, ch_ref, ga_ref, gb_ref, cch_ref, chh_ref, xn_ref,
                      wa_ref, wb_ref, wo_ref, cw_ref, bg_ref, g1_ref, b1_ref, wr_ref, br_ref,
                      tri_ref, x1_ref, route_ref, cnt_ref, zs_ref, run_ref, *, tm, tiles_per_seq):
    r = pl.program_id(0)

    @pl.when(r == 0)
    def _():
        run_ref[...] = jnp.zeros_like(run_ref)

    z = cc_ref[...].astype(F32) * ch_ref[...].astype(F32)
    zh = cch_ref[...].astype(F32) * chh_ref[...].astype(F32)
    seq_start = (r % tiles_per_seq) == 0
    zh = jnp.where(seq_start, 0.0, zh)
    zs_ref[0:HALO, :] = zh
    zs_ref[HALO:HALO + tm, :] = z
    z1 = zs_ref[HALO - 1:HALO - 1 + tm, :]
    z2 = zs_ref[HALO - 2:HALO - 2 + tm, :]
    cw = cw_ref[...]
    zc = cw[0:1] * z2 + cw[1:2] * z1 + cw[2:3] * z
    yb_in = (cb_ref[...].astype(F32) * zc).astype(BF16)
    y_b = jnp.dot(yb_in, wb_ref[...], preferred_element_type=F32)
    y_a = jnp.dot(on_ref[...], wa_ref[...], preferred_element_type=F32)

    bg = bg_ref[...]
    g_a = jax.nn.sigmoid(ga_ref[...].astype(F32) + bg[0:1])
    g_b = jax.nn.sigmoid(gb_ref[...].astype(F32) + bg[1:2])
    merged = (g_a * y_a + g_b * y_b).astype(BF16)
    mix = jnp.dot(merged, wo_ref[...], preferred_element_type=F32)
    x1 = _layer_norm(DEEPNORM_ALPHA * xn_ref[...] + mix, g1_ref[...], b1_ref[...])
    x1_ref[...] = x1

    logit = jnp.dot(x1, wr_ref[...], preferred_element_type=F32,
                    precision=lax.Precision.HIGHEST) + br_ref[...]
    lane = lax.broadcasted_iota(jnp.int32, (tm, LANES), 1).astype(F32)
    big = float(4 * LANES)

    def first_argmax(vals, vmax):
        return jnp.min(jnp.where(vals == vmax, lane, big), axis=-1, keepdims=True)

    gl = jnp.where(lane < N_GROUPS, logit, -jnp.inf)
    gmax = jnp.max(gl, axis=-1, keepdims=True)
    gsum = jnp.sum(jnp.exp(gl - gmax), axis=-1, keepdims=True)
    g_p = 1.0 / gsum
    g_idx = first_argmax(gl, gmax)
    lo = N_GROUPS + EXPERTS_PER_GROUP * g_idx
    in_group = (lane >= lo) & (lane < lo + EXPERTS_PER_GROUP)
    sl = jnp.where(in_group, logit, -jnp.inf)
    s1 = jnp.max(sl, axis=-1, keepdims=True)
    i1 = first_argmax(sl, s1)
    sl2 = jnp.where(lane == i1, -jnp.inf, sl)
    s2 = jnp.max(sl2, axis=-1, keepdims=True)
    i2 = first_argmax(sl2, s2)
    t = jnp.exp(s2 - s1)
    w1 = g_p / (1.0 + t)
    w2 = g_p * t / (1.0 + t)
    e1 = i1 - N_GROUPS
    e2 = i2 - N_GROUPS

    oh1 = lane == e1
    oh2 = lane == e2
    onehot = jnp.where(oh1 | oh2, 1.0, 0.0)
    before = jnp.dot(tri_ref[...], onehot.astype(BF16), preferred_element_type=F32) + run_ref[...]
    rank1 = jnp.sum(jnp.where(oh1, before, 0.0), axis=-1, keepdims=True)
    rank2 = jnp.sum(jnp.where(oh2, before, 0.0), axis=-1, keepdims=True)
    run_ref[...] = run_ref[...] + jnp.sum(onehot, axis=0, keepdims=True)
    cnt_ref[...] = jnp.broadcast_to(run_ref[...], cnt_ref.shape)

    route = jnp.where(lane == 0, e1, 0.0)
    for idx, val in enumerate((e2, w1, w2, rank1, rank2), start=1):
        route = jnp.where(lane == idx, val, route)
    route_ref[...] = route[:, 0:route_ref.shape[1]]


ROUTE_COLS = 8


def _mix_route(o_n, p, xn, wa, wb, wo, conv_w, b_gate, g1, b1, wr, br, seq, tm):
    n_tok = xn.shape[0]
    halo_blocks = tm // HALO
    tri = jnp.asarray(np.tril(np.ones((tm, tm), np.float32), k=-1), BF16)
    kern = functools.partial(_mix_route_kernel, tm=tm, tiles_per_seq=seq // tm)

    def col(c):
        return pl.BlockSpec((tm, D_MODEL), lambda r, c=c: (r, c))

    def halo(c):
        return pl.BlockSpec((HALO, D_MODEL),
                            lambda r, c=c: (jnp.maximum(r * halo_blocks - 1, 0), c))

    def const(shape):
        return pl.BlockSpec(shape, lambda r: (0,) * len(shape), pipeline_mode=pl.Buffered(1))

    return pl.pallas_call(
        kern,
        grid=(n_tok // tm,),
        in_specs=[
            pl.BlockSpec((tm, D_MODEL), lambda r: (r, 0)),
            col(3), col(4), col(5), col(6), col(7),
            halo(4), halo(5),
            pl.BlockSpec((tm, D_MODEL), lambda r: (r, 0)),
            const((D_MODEL, D_MODEL)), const((D_MODEL, D_MODEL)), const((D_MODEL, D_MODEL)),
            const((3, D_MODEL)), const((2, D_MODEL)), const((1, D_MODEL)), const((1, D_MODEL)),
            const((D_MODEL, LANES)), const((1, LANES)), const((tm, tm)),
        ],
        out_specs=[
            pl.BlockSpec((tm, D_MODEL), lambda r: (r, 0)),
            pl.BlockSpec((tm, ROUTE_COLS), lambda r: (r, 0)),
            pl.BlockSpec((8, LANES), lambda r: (0, 0)),
        ],
        out_shape=[
            jax.ShapeDtypeStruct((n_tok, D_MODEL), F32),
            jax.ShapeDtypeStruct((n_tok, ROUTE_COLS), F32),
            jax.ShapeDtypeStruct((8, LANES), F32),
        ],
        scratch_shapes=[
            pltpu.VMEM((HALO + tm, D_MODEL), F32),
            pltpu.VMEM((1, LANES), F32),
        ],
        compiler_params=pltpu.CompilerParams(
            dimension_semantics=("arbitrary",), vmem_limit_bytes=VMEM_LIMIT),
        name="mix_route",
    )(o_n, p, p, p, p, p, p, p, xn, wa, wb, wo, conv_w, b_gate, g1, b1, wr, br, tri)


def _row_copy(src, src_row, dst, dst_row, sem):
    return pltpu.make_async_copy(src.at[pl.ds(src_row, 1), :], dst.at[pl.ds(dst_row, 1), :], sem)


N_ZERO_BLOCKS = 2 * N_EXPERTS


def _dispatch_kernel(zb_ref, dest_ref, x_ref, out_ref, zeros, sem, zsem, *, td):
    @pl.when(pl.program_id(0) == 0)
    def _():
        zeros[...] = jnp.zeros_like(zeros)

        def block_copy(n):
            row0 = pl.multiple_of(zb_ref[n] * MOE_BLOCK, MOE_BLOCK)
            return pltpu.make_async_copy(zeros, out_ref.at[pl.ds(row0, MOE_BLOCK), :], zsem)

        n_zero = zb_ref[N_ZERO_BLOCKS]

        def start(n, carry):
            block_copy(n).start()
            return carry

        def wait(n, carry):
            block_copy(n).wait()
            return carry

        lax.fori_loop(0, n_zero, start, 0)
        lax.fori_loop(0, n_zero, wait, 0)

    def issue(t, carry):
        for k in range(TOP_K):
            _row_copy(x_ref, t, out_ref, dest_ref[0, 0, k * td + t], sem).start()
        return carry

    lax.fori_loop(0, td, issue, 0, unroll=8)
    for k in range(TOP_K):
        pltpu.make_async_copy(x_ref, out_ref.at[pl.ds(0, td), :], sem).wait()


def _dispatch(x1, dest3, zero_blocks, n_rows, td):
    n_tok = x1.shape[0]
    grid_spec = pltpu.PrefetchScalarGridSpec(
        num_scalar_prefetch=1,
        grid=(n_tok // td,),
        in_specs=[
            pl.BlockSpec((1, 1, TOP_K * td), lambda s, zb: (s, 0, 0), memory_space=pltpu.SMEM),
            pl.BlockSpec((td, D_MODEL), lambda s, zb: (s, 0)),
        ],
        out_specs=pl.BlockSpec(memory_space=pl.ANY),
        scratch_shapes=[
            pltpu.VMEM((MOE_BLOCK, D_MODEL), F32),
            pltpu.SemaphoreType.DMA(()),
            pltpu.SemaphoreType.DMA(()),
        ],
    )
    return pl.pallas_call(
        functools.partial(_dispatch_kernel, td=td),
        grid_spec=grid_spec,
        out_shape=jax.ShapeDtypeStruct((n_rows, D_MODEL), F32),
        compiler_params=pltpu.CompilerParams(dimension_semantics=("arbitrary",)),
        name="dispatch",
    )(zero_blocks, dest3, x1)


def _expert_ffn_kernel(be_ref, na_ref, x_ref, wg_ref, wu_ref, wd_ref, y_ref):
    del be_ref
    active = pl.program_id(0) < na_ref[0]

    @pl.when(active)
    def _():
        x = x_ref[...].astype(BF16)
        g = jnp.dot(x, wg_ref[...], preferred_element_type=F32)
        u = jnp.dot(x, wu_ref[...], preferred_element_type=F32)
        hid = (jax.nn.silu(g) * u).astype(BF16)
        y_ref[...] = jnp.dot(hid, wd_ref[...], preferred_element_type=F32)

    @pl.when(jnp.logical_not(active))
    def _():
        y_ref[...] = jnp.zeros_like(y_ref)


def _expert_ffn(xs, block_expert, n_active, wg, wu, wd):
    n_rows = xs.shape[0]
    n_blocks = n_rows // MOE_BLOCK

    def blk(i, be, na):
        return (jnp.minimum(i, na[0] - 1), 0)

    def wmap(i, be, na):
        return (be[jnp.minimum(i, na[0] - 1)], 0, 0)

    grid_spec = pltpu.PrefetchScalarGridSpec(
        num_scalar_prefetch=2,
        grid=(n_blocks,),
        in_specs=[
            pl.BlockSpec((MOE_BLOCK, D_MODEL), blk),
            pl.BlockSpec((None, D_MODEL, D_EXPERT), wmap),
            pl.BlockSpec((None, D_MODEL, D_EXPERT), wmap),
            pl.BlockSpec((None, D_EXPERT, D_MODEL), wmap),
        ],
        out_specs=pl.BlockSpec((MOE_BLOCK, D_MODEL), lambda i, be, na: (i, 0)),
    )
    return pl.pallas_call(
        _expert_ffn_kernel,
        grid_spec=grid_spec,
        out_shape=jax.ShapeDtypeStruct((n_rows, D_MODEL), F32),
        compiler_params=pltpu.CompilerParams(
            dimension_semantics=("arbitrary",), vmem_limit_bytes=VMEM_LIMIT),
        name="expert_ffn",
    )(block_expert, n_active, xs, wg, wu, wd)


def _combine_kernel(dcur_ref, dnxt_ref, x_ref, route_ref, g_ref, b_ref, y_ref, o_ref,
                    buf, sem, *, td):
    s = pl.program_id(0)
    n = pl.num_programs(0)
    slot = s % 2

    def issue(d_ref, to_slot):
        def body(t, carry):
            for k in range(TOP_K):
                _row_copy(y_ref, d_ref[0, 0, k * td + t], buf.at[to_slot, k], t,
                          sem.at[to_slot]).start()
            return carry
        lax.fori_loop(0, td, body, 0, unroll=8)

    @pl.when(s == 0)
    def _():
        issue(dcur_ref, 0)

    @pl.when(s + 1 < n)
    def _():
        issue(dnxt_ref, 1 - slot)

    for k in range(TOP_K):
        pltpu.make_async_copy(y_ref.at[pl.ds(0, td), :], buf.at[slot, k], sem.at[slot]).wait()

    route = route_ref[...]
    ffn = route[:, 2:3] * buf[slot, 0] + route[:, 3:4] * buf[slot, 1]
    o_ref[...] = _layer_norm(DEEPNORM_ALPHA * x_ref[...] + ffn, g_ref[...], b_ref[...])


def _combine(ys, dest3, x1, route, g2, b2, td):
    n_tok = x1.shape[0]
    n_steps = n_tok // td
    return pl.pallas_call(
        functools.partial(_combine_kernel, td=td),
        grid=(n_steps,),
        in_specs=[
            pl.BlockSpec((1, 1, TOP_K * td), lambda s: (s, 0, 0), memory_space=pltpu.SMEM),
            pl.BlockSpec((1, 1, TOP_K * td), lambda s: (jnp.minimum(s + 1, n_steps - 1), 0, 0),
                         memory_space=pltpu.SMEM),
            pl.BlockSpec((td, D_MODEL), lambda s: (s, 0)),
            pl.BlockSpec((td, ROUTE_COLS), lambda s: (s, 0)),
            pl.BlockSpec((1, D_MODEL), lambda s: (0, 0)),
            pl.BlockSpec((1, D_MODEL), lambda s: (0, 0)),
            pl.BlockSpec(memory_space=pl.ANY),
        ],
        out_specs=pl.BlockSpec((td, D_MODEL), lambda s: (s, 0)),
        out_shape=jax.ShapeDtypeStruct((n_tok, D_MODEL), F32),
        scratch_shapes=[
            pltpu.VMEM((2, TOP_K, td, D_MODEL), F32),
            pltpu.SemaphoreType.DMA((2,)),
        ],
        compiler_params=pltpu.CompilerParams(
            dimension_semantics=("arbitrary",), vmem_limit_bytes=VMEM_LIMIT),
        name="combine",
    )(dest3, dest3, x1, route, g2, b2, ys)


def _row_tile(n, want):
    t = min(want, n)
    while n % t:
        t //= 2
    return t


def kernel(x, ln_in_g, ln_in_b, w_in, b_gate, lambda_q, lambda_k, subln_g, rel_bias, conv_w,
           w_a_proj, w_b_proj, w_o, ln1_g, ln1_b, w_group, b_group, w_sub, b_sub,
           w_gate_e, w_up_e, w_down_e, ln2_g, ln2_b):
    bsz, seq, d = x.shape
    assert DEPTH == 1 and d == D_MODEL and w_in.shape == (DEPTH, D_MODEL, N_IN)
    n_tok = bsz * seq
    tq = _row_tile(seq, 256)
    assert tq % LANES == 0 and tq % CHUNK == 0
    tm1 = _row_tile(n_tok, 512)
    tm3 = _row_tile(seq, 256)
    td = _row_tile(n_tok, 256)
    row = lambda v: v.reshape(1, -1).astype(F32)

    cur = x.reshape(n_tok, d)
    g_in, b_in = row(ln_in_g), row(ln_in_b)
    for l in range(DEPTH):
        lam_init = 0.8 - 0.6 * math.exp(-0.3 * l)
        xn, p = _ln_proj(cur, g_in, b_in, w_in[l].astype(BF16), tm1)

        band = _band_bias(rel_bias, tq)
        o_n = _diff_attn(p, band, lambda_q[l].astype(F32), lambda_k[l].astype(F32),
                         row(subln_g[l]), bsz, seq, tq, lam_init)

        w_r = jnp.concatenate(
            [w_group[l].astype(F32),
             jnp.transpose(w_sub[l].astype(F32), (1, 0, 2)).reshape(d, N_EXPERTS)], axis=1)
        w_r = jnp.pad(w_r, ((0, 0), (0, LANES - w_r.shape[1])))
        b_r = jnp.concatenate([b_group[l].astype(F32), b_sub[l].astype(F32).reshape(-1)])
        b_r = jnp.pad(b_r, (0, LANES - b_r.shape[0])).reshape(1, LANES)
        x1, route, counts = _mix_route(
            o_n, p, xn, w_a_proj[l].astype(BF16), w_b_proj[l].astype(BF16), w_o[l].astype(BF16),
            conv_w[l].reshape(3, d).astype(F32), b_gate[l].astype(F32), row(ln1_g[l]),
            row(ln1_b[l]), w_r, b_r, seq, tm3)

        n_assign = n_tok * TOP_K
        n_blocks = -(-n_assign // MOE_BLOCK) + N_EXPERTS
        cnt = counts[0, :N_EXPERTS].astype(jnp.int32)
        padded = ((cnt + MOE_BLOCK - 1) // MOE_BLOCK) * MOE_BLOCK
        pad_end = jnp.cumsum(padded)
        pad_start = pad_end - padded
        n_active = (pad_end[-1:] // MOE_BLOCK).astype(jnp.int32)
        blk_row0 = jnp.arange(n_blocks, dtype=jnp.int32) * MOE_BLOCK
        block_expert = jnp.minimum(
            jnp.sum((pad_end[None, :] <= blk_row0[:, None]).astype(jnp.int32), axis=1),
            N_EXPERTS - 1).astype(jnp.int32)
        experts = route[:, 0:TOP_K].astype(jnp.int32)
        ranks = route[:, 4:4 + TOP_K].astype(jnp.int32)
        is_e = experts[:, :, None] == jnp.arange(N_EXPERTS, dtype=jnp.int32)
        dest = jnp.sum(jnp.where(is_e, pad_start, 0), axis=-1) + ranks
        dest3 = jnp.transpose(dest.reshape(n_tok // td, td, TOP_K), (0, 2, 1))
        dest3 = dest3.reshape(n_tok // td, 1, TOP_K * td).astype(jnp.int32)

        last_blk = jnp.maximum(pad_end // MOE_BLOCK - 1, 0)
        idle_blk = jnp.minimum(n_active[0] + jnp.arange(N_EXPERTS), n_blocks - 1)
        n_zero = N_EXPERTS + n_blocks - n_active
        zero_blocks = jnp.concatenate([last_blk, idle_blk, n_zero]).astype(jnp.int32)
        xs = _dispatch(x1, dest3, zero_blocks, n_blocks * MOE_BLOCK, td)
        ys = _expert_ffn(xs, block_expert, n_active, w_gate_e[l].astype(BF16),
                         w_up_e[l].astype(BF16), w_down_e[l].astype(BF16))
        cur = _combine(ys, dest3, x1, route, row(ln2_g[l]), row(ln2_b[l]), td)
    return cur.reshape(bsz, seq, d)
```

```python
import functools
import math

import numpy as np
import jax
import jax.numpy as jnp
from jax import lax
from jax.experimental import pallas as pl
from jax.experimental.pallas import tpu as pltpu

F32 = jnp.float32
BF16 = jnp.bfloat16

D_MODEL = 1024
N_HEADS = 4
HEAD_DIM = 128
D_HEAD_V = 2 * HEAD_DIM
CHUNK = 64
N_BUCKETS = 32
MAX_DISTANCE = 128
N_GROUPS = 4
EXPERTS_PER_GROUP = 8
N_EXPERTS = N_GROUPS * EXPERTS_PER_GROUP
TOP_K = 2
D_EXPERT = 512
MOE_BLOCK = 512
LN_EPS = 1e-5
RMS_EPS = 1e-6
DEPTH = 1
DEEPNORM_ALPHA = (2.0 * DEPTH) ** 0.25
N_IN = 8 * D_MODEL
LANES = 128
NEG_BIG = -1e30
LOG2_E = math.log2(math.e)

VMEM_LIMIT = 56 * 1024 * 1024


def _layer_norm(x, g, b):
    mu = jnp.mean(x, axis=-1, keepdims=True)
    xc = x - mu
    var = jnp.mean(xc * xc, axis=-1, keepdims=True)
    return xc * lax.rsqrt(var + LN_EPS) * g + b


def _const_spec(shape):
    return pl.BlockSpec(shape, lambda *_: (0,) * len(shape), pipeline_mode=pl.Buffered(1))


def _ln_proj_kernel(x_ref, g_ref, b_ref, w_ref, xn_ref, p_ref, *, q_scale):
    xn = _layer_norm(x_ref[...], g_ref[...], b_ref[...])
    xn_ref[...] = xn
    xb = xn.astype(BF16)
    for c in range(N_IN // D_MODEL):
        cols = slice(c * D_MODEL, (c + 1) * D_MODEL)
        acc = jnp.dot(xb, w_ref[:, cols], preferred_element_type=F32)
        if c == 0:
            acc = acc * q_scale
        p_ref[:, cols] = acc.astype(BF16)


def _ln_proj(x2, g, b, w_bf, tm):
    n_tok = x2.shape[0]
    kern = functools.partial(_ln_proj_kernel, q_scale=HEAD_DIM ** -0.5 * LOG2_E)
    return pl.pallas_call(
        kern,
        grid=(n_tok // tm,),
        in_specs=[
            pl.BlockSpec((tm, D_MODEL), lambda r: (r, 0)),
            _const_spec((1, D_MODEL)),
            _const_spec((1, D_MODEL)),
            _const_spec((D_MODEL, N_IN)),
        ],
        out_specs=[
            pl.BlockSpec((tm, D_MODEL), lambda r: (r, 0)),
            pl.BlockSpec((tm, N_IN), lambda r: (r, 0)),
        ],
        out_shape=[
            jax.ShapeDtypeStruct((n_tok, D_MODEL), F32),
            jax.ShapeDtypeStruct((n_tok, N_IN), BF16),
        ],
        compiler_params=pltpu.CompilerParams(
            dimension_semantics=("arbitrary",), vmem_limit_bytes=VMEM_LIMIT),
        name="ln_proj",
    )(x2, g, b, w_bf)


def _t5_bucket_np(rel):
    nb = N_BUCKETS // 2
    max_exact = nb // 2
    n = np.abs(rel)
    large = np.full(n.shape, max_exact, np.int64)
    for d in range(max_exact, MAX_DISTANCE + 1):
        val = max_exact + int(math.log(d / max_exact) / math.log(MAX_DISTANCE / max_exact)
                              * (nb - max_exact))
        large = np.where(n >= d, min(val, nb - 1), large)
    return np.where(rel > 0, nb, 0) + np.where(n < max_exact, n, large)


def _band_bias(rel_bias, tq):
    qi = np.arange(tq)[:, None]
    kj = np.arange(2 * tq)[None, :] - tq
    bucket = _t5_bucket_np(kj - qi)
    allowed = (kj // CHUNK) <= (qi // CHUNK)
    far_bucket = int(_t5_bucket_np(np.array([-(tq + 1)]))[0])
    rb = rel_bias.astype(F32)
    rb = ((rb - rb[far_bucket][None, :]) * LOG2_E).T
    onehot = (jnp.asarray(bucket, jnp.int32)[..., None]
              == jnp.arange(N_BUCKETS, dtype=jnp.int32))
    tile = jnp.sum(jnp.where(onehot[None], rb[:, None, None, :], 0.0), axis=-1)
    return jnp.where(jnp.asarray(allowed)[None], tile, NEG_BIG)


def _diff_attn_kernel(q_ref, k_ref, v_ref, bias_ref, lq_ref, lk_ref, sg_ref, o_ref,
                      s_ref, mx_ref, ls_ref, acc_ref, *, tq, lam_init):
    i = pl.program_id(1)
    heads_maps = [(h, m) for h in range(N_HEADS) for m in range(2)]

    def lane_tiles(s):
        return [s[:, c * LANES:(c + 1) * LANES] for c in range(s.shape[1] // LANES)]

    def qk_block(j, bias_cols, first):
        ks = pl.multiple_of(j * tq, tq)
        for h, m in heads_maps:
            c = h * D_HEAD_V + m * HEAD_DIM
            s = lax.dot_general(q_ref[:, c:c + HEAD_DIM], k_ref[pl.ds(ks, tq), c:c + HEAD_DIM],
                                (((1,), (1,)), ((), ())), preferred_element_type=F32)
            if bias_cols is not None:
                s = s + bias_ref[h, :, bias_cols]
            s_ref[h, m, j] = s
            t = functools.reduce(jnp.maximum, lane_tiles(s))
            mx_ref[h, m] = t if first else jnp.maximum(mx_ref[h, m], t)

    qk_block(i, slice(tq, 2 * tq), True)

    @pl.when(i >= 1)
    def _():
        qk_block(i - 1, slice(0, tq), False)

    n_far = jnp.maximum(i - 1, 0)

    def far_pair(t, carry):
        qk_block(2 * t, None, False)
        qk_block(2 * t + 1, None, False)
        return carry

    lax.fori_loop(0, lax.shift_right_logical(n_far, 1), far_pair, 0)

    @pl.when((n_far & 1) == 1)
    def _():
        qk_block(n_far - 1, None, False)

    for h, m in heads_maps:
        mx_ref[h, m] = jnp.broadcast_to(jnp.max(mx_ref[h, m], axis=-1, keepdims=True),
                                        (tq, LANES))

    def pv_blocks(j0, nb, first):
        ks = pl.multiple_of(j0 * tq, tq)
        for h in range(N_HEADS):
            c0 = h * D_HEAD_V
            rows = []
            for m in range(2):
                rmax = mx_ref[h, m]
                tiles = []
                for t in range(nb):
                    tiles += [jnp.exp2(x - rmax) for x in lane_tiles(s_ref[h, m, j0 + t])]
                tsum = functools.reduce(jnp.add, tiles)
                ls_ref[h, m] = tsum if first else ls_ref[h, m] + tsum
                rows.append(jnp.concatenate(tiles, axis=1))
            p = jnp.concatenate(rows, axis=0).astype(BF16)
            contrib = jnp.dot(p, v_ref[pl.ds(ks, nb * tq), c0:c0 + D_HEAD_V],
                              preferred_element_type=F32)
            acc_ref[h] = contrib if first else acc_ref[h] + contrib

    pv_blocks(i, 1, True)

    def pv_pair(t, carry):
        pv_blocks(2 * t, 2, False)
        return carry

    lax.fori_loop(0, lax.shift_right_logical(i, 1), pv_pair, 0)

    @pl.when((i & 1) == 1)
    def _():
        pv_blocks(i - 1, 1, False)

    dots = jnp.sum(lq_ref[...] * lk_ref[...], axis=-1, keepdims=True)
    lam = jnp.exp(dots[0:1]) - jnp.exp(dots[1:2]) + lam_init
    for h in range(N_HEADS):
        c0 = h * D_HEAD_V
        l1 = jnp.sum(ls_ref[h, 0], axis=-1, keepdims=True)
        l2 = jnp.sum(ls_ref[h, 1], axis=-1, keepdims=True)
        o = acc_ref[h, 0:tq, :] / l1 - lam * (acc_ref[h, tq:2 * tq, :] / l2)
        o = o * lax.rsqrt(jnp.mean(o * o, axis=-1, keepdims=True) + RMS_EPS)
        o = o * sg_ref[...] * (1.0 - lam_init)
        o_ref[:, c0:c0 + D_HEAD_V] = o.astype(BF16)


def _diff_attn(p, band, lam_q, lam_k, subln_g, bsz, seq, tq, lam_init):
    n_tok = bsz * seq
    nq = seq // tq
    kern = functools.partial(_diff_attn_kernel, tq=tq, lam_init=lam_init)
    return pl.pallas_call(
        kern,
        grid=(bsz, nq),
        in_specs=[
            pl.BlockSpec((tq, D_MODEL), lambda b, i: (b * nq + i, 0)),
            pl.BlockSpec((seq, D_MODEL), lambda b, i: (b, 1)),
            pl.BlockSpec((seq, D_MODEL), lambda b, i: (b, 2)),
            _const_spec((N_HEADS, tq, 2 * tq)),
            _const_spec((2, HEAD_DIM)),
            _const_spec((2, HEAD_DIM)),
            _const_spec((1, D_HEAD_V)),
        ],
        out_specs=pl.BlockSpec((tq, D_MODEL), lambda b, i: (b * nq + i, 0)),
        out_shape=jax.ShapeDtypeStruct((n_tok, D_MODEL), BF16),
        scratch_shapes=[
            pltpu.VMEM((N_HEADS, 2, nq, tq, tq), F32),
            pltpu.VMEM((N_HEADS, 2, tq, LANES), F32),
            pltpu.VMEM((N_HEADS, 2, tq, LANES), F32),
            pltpu.VMEM((N_HEADS, 2 * tq, D_HEAD_V), F32),
        ],
        compiler_params=pltpu.CompilerParams(
            dimension_semantics=("arbitrary", "arbitrary"), vmem_limit_bytes=VMEM_LIMIT),
        name="diff_attn",
    )(p, p, p, band, lam_q, lam_k, subln_g)


HALO = 16


def _mix_kernel(on_ref, cb_ref, cc_ref, ch_ref, ga_ref, gb_ref, cch_ref, chh_ref, xn_ref,
                wa_ref, wb_ref, wo_ref, cw_ref, bg_ref, g1_ref, b1_ref, wr_ref, br_ref,
                x1_ref, lg_ref, zs_ref, *, tm, sub, tiles_per_seq):
    r = pl.program_id(0)

    zh = cch_ref[...].astype(F32) * chh_ref[...].astype(F32)
    seq_start = (r % tiles_per_seq) == 0
    zs_ref[0:HALO, :] = jnp.where(seq_start, 0.0, zh)
    zs_ref[HALO:HALO + tm, :] = cc_ref[...].astype(F32) * ch_ref[...].astype(F32)
    cw = cw_ref[...]
    bg = bg_ref[...]

    for lo in range(0, tm, sub):
        rows = slice(lo, lo + sub)
        z = zs_ref[HALO + lo:HALO + lo + sub, :]
        z1 = zs_ref[HALO + lo - 1:HALO + lo - 1 + sub, :]
        z2 = zs_ref[HALO + lo - 2:HALO + lo - 2 + sub, :]
        zc = cw[0:1] * z2 + cw[1:2] * z1 + cw[2:3] * z
        yb_in = (cb_ref[rows, :].astype(F32) * zc).astype(BF16)
        y_b = jnp.dot(yb_in, wb_ref[...], preferred_element_type=F32)
        y_a = jnp.dot(on_ref[rows, :], wa_ref[...], preferred_element_type=F32)
        g_a = jax.nn.sigmoid(ga_ref[rows, :].astype(F32) + bg[0:1])
        g_b = jax.nn.sigmoid(gb_ref[rows, :].astype(F32) + bg[1:2])
        merged = (g_a * y_a + g_b * y_b).astype(BF16)
        mix = jnp.dot(merged, wo_ref[...], preferred_element_type=F32)
        x1 = _layer_norm(DEEPNORM_ALPHA * xn_ref[rows, :] + mix, g1_ref[...], b1_ref[...])
        x1_ref[rows, :] = x1
        x_hi = x1.astype(BF16)
        x_lo = (x1 - x_hi.astype(F32)).astype(BF16)
        part = (jnp.dot(x_hi, wr_ref[...], preferred_element_type=F32)
                + jnp.dot(x_lo, wr_ref[...], preferred_element_type=F32))
        lg_ref[rows, :] = part[:, 0:LANES] + part[:, LANES:2 * LANES] + br_ref[...]


def _mix(o_n, p, xn, wa, wb, wo, conv_w, b_gate, g1, b1, wr2, br, seq, tm, sub):
    n_tok = xn.shape[0]
    halo_blocks = tm // HALO
    kern = functools.partial(_mix_kernel, tm=tm, sub=sub, tiles_per_seq=seq // tm)

    def col(c):
        return pl.BlockSpec((tm, D_MODEL), lambda r, c=c: (r, c))

    def halo(c):
        return pl.BlockSpec((HALO, D_MODEL),
                            lambda r, c=c: (jnp.maximum(r * halo_blocks - 1, 0), c))

    return pl.pallas_call(
        kern,
        grid=(n_tok // tm,),
        in_specs=[
            pl.BlockSpec((tm, D_MODEL), lambda r: (r, 0)),
            col(3), col(4), col(5), col(6), col(7),
            halo(4), halo(5),
            pl.BlockSpec((tm, D_MODEL), lambda r: (r, 0)),
            _const_spec((D_MODEL, D_MODEL)), _const_spec((D_MODEL, D_MODEL)),
            _const_spec((D_MODEL, D_MODEL)),
            _const_spec((3, D_MODEL)), _const_spec((2, D_MODEL)),
            _const_spec((1, D_MODEL)), _const_spec((1, D_MODEL)),
            _const_spec((D_MODEL, 2 * LANES)), _const_spec((1, LANES)),
        ],
        out_specs=[
            pl.BlockSpec((tm, D_MODEL), lambda r: (r, 0)),
            pl.BlockSpec((tm, LANES), lambda r: (r, 0)),
        ],
        out_shape=[
            jax.ShapeDtypeStruct((n_tok, D_MODEL), F32),
            jax.ShapeDtypeStruct((n_tok, LANES), F32),
        ],
        scratch_shapes=[pltpu.VMEM((HALO + tm, D_MODEL), F32)],
        compiler_params=pltpu.CompilerParams(
            dimension_semantics=("arbitrary",), vmem_limit_bytes=VMEM_LIMIT),
        name="mix",
    )(o_n, p, p, p, p, p, p, p, xn, wa, wb, wo, conv_w, b_gate, g1, b1, wr2, br)


ROUTE_COLS = 8


def _route_kernel(lg_ref, tri_ref, route_ref, cnt_ref, run_ref, *, tr):
    @pl.when(pl.program_id(0) == 0)
    def _():
        run_ref[...] = jnp.zeros_like(run_ref)

    logit = lg_ref[...]
    lane = lax.broadcasted_iota(jnp.int32, (tr, LANES), 1).astype(F32)
    big = float(4 * LANES)

    def first_argmax(vals, vmax):
        return jnp.min(jnp.where(vals == vmax, lane, big), axis=-1, keepdims=True)

    gl = jnp.where(lane < N_GROUPS, logit, -jnp.inf)
    gmax = jnp.max(gl, axis=-1, keepdims=True)
    gsum = jnp.sum(jnp.exp(gl - gmax), axis=-1, keepdims=True)
    g_p = 1.0 / gsum
    g_idx = first_argmax(gl, gmax)
    lo = N_GROUPS + EXPERTS_PER_GROUP * g_idx
    in_group = (lane >= lo) & (lane < lo + EXPERTS_PER_GROUP)
    sl = jnp.where(in_group, logit, -jnp.inf)
    s1 = jnp.max(sl, axis=-1, keepdims=True)
    i1 = first_argmax(sl, s1)
    sl2 = jnp.where(lane == i1, -jnp.inf, sl)
    s2 = jnp.max(sl2, axis=-1, keepdims=True)
    i2 = first_argmax(sl2, s2)
    t = jnp.exp(s2 - s1)
    w1 = g_p / (1.0 + t)
    w2 = g_p * t / (1.0 + t)
    e1 = i1 - N_GROUPS
    e2 = i2 - N_GROUPS

    oh1 = lane == e1
    oh2 = lane == e2
    onehot = jnp.where(oh1 | oh2, 1.0, 0.0)
    before = jnp.dot(tri_ref[...], onehot.astype(BF16), preferred_element_type=F32) + run_ref[...]
    rank1 = jnp.sum(jnp.where(oh1, before, 0.0), axis=-1, keepdims=True)
    rank2 = jnp.sum(jnp.where(oh2, before, 0.0), axis=-1, keepdims=True)
    run_ref[...] = run_ref[...] + jnp.sum(onehot, axis=0, keepdims=True)
    cnt_ref[...] = jnp.broadcast_to(run_ref[...], cnt_ref.shape)

    route = jnp.where(lane == 0, e1, 0.0)
    for idx, val in enumerate((e2, w1, w2, rank1, rank2), start=1):
        route = jnp.where(lane == idx, val, route)
    route_ref[...] = route[:, 0:ROUTE_COLS]


def _route(logits, tr):
    n_tok = logits.shape[0]
    tri = jnp.asarray(np.tril(np.ones((tr, tr), np.float32), k=-1), BF16)
    return pl.pallas_call(
        functools.partial(_route_kernel, tr=tr),
        grid=(n_tok // tr,),
        in_specs=[pl.BlockSpec((tr, LANES), lambda r: (r, 0)), _const_spec((tr, tr))],
        out_specs=[
            pl.BlockSpec((tr, ROUTE_COLS), lambda r: (r, 0)),
            pl.BlockSpec((8, LANES), lambda r: (0, 0)),
        ],
        out_shape=[
            jax.ShapeDtypeStruct((n_tok, ROUTE_COLS), F32),
            jax.ShapeDtypeStruct((8, LANES), F32),
        ],
        scratch_shapes=[pltpu.VMEM((1, LANES), F32)],
        compiler_params=pltpu.CompilerParams(
            dimension_semantics=("arbitrary",), vmem_limit_bytes=VMEM_LIMIT),
        name="route",
    )(logits, tri)


def _row_copy(src, src_row, dst, dst_row, sem):
    return pltpu.make_async_copy(src.at[pl.ds(src_row, 1), :], dst.at[pl.ds(dst_row, 1), :], sem)


N_ZERO_BLOCKS = 2 * N_EXPERTS


def _dispatch_kernel(zb_ref, dest_ref, x_ref, out_ref, zeros, sem, zsem, *, td):
    @pl.when(pl.program_id(0) == 0)
    def _():
        zeros[...] = jnp.zeros_like(zeros)

        def block_copy(n):
            row0 = pl.multiple_of(zb_ref[n] * MOE_BLOCK, MOE_BLOCK)
            return pltpu.make_async_copy(zeros, out_ref.at[pl.ds(row0, MOE_BLOCK), :], zsem)

        n_zero = zb_ref[N_ZERO_BLOCKS]

        def start(n, carry):
            block_copy(n).start()
            return carry

        def wait(n, carry):
            block_copy(n).wait()
            return carry

        lax.fori_loop(0, n_zero, start, 0)
        lax.fori_loop(0, n_zero, wait, 0)

    def issue(t, carry):
        for k in range(TOP_K):
            _row_copy(x_ref, t, out_ref, dest_ref[0, 0, k * td + t], sem).start()
        return carry

    lax.fori_loop(0, td, issue, 0, unroll=8)
    for k in range(TOP_K):
        pltpu.make_async_copy(x_ref, out_ref.at[pl.ds(0, td), :], sem).wait()


def _dispatch(x1, dest3, zero_blocks, n_rows, td):
    n_tok = x1.shape[0]
    grid_spec = pltpu.PrefetchScalarGridSpec(
        num_scalar_prefetch=1,
        grid=(n_tok // td,),
        in_specs=[
            pl.BlockSpec((1, 1, TOP_K * td), lambda s, zb: (s, 0, 0), memory_space=pltpu.SMEM),
            pl.BlockSpec((td, D_MODEL), lambda s, zb: (s, 0)),
        ],
        out_specs=pl.BlockSpec(memory_space=pl.ANY),
        scratch_shapes=[
            pltpu.VMEM((MOE_BLOCK, D_MODEL), F32),
            pltpu.SemaphoreType.DMA(()),
            pltpu.SemaphoreType.DMA(()),
        ],
    )
    return pl.pallas_call(
        functools.partial(_dispatch_kernel, td=td),
        grid_spec=grid_spec,
        out_shape=jax.ShapeDtypeStruct((n_rows, D_MODEL), F32),
        compiler_params=pltpu.CompilerParams(dimension_semantics=("arbitrary",)),
        name="dispatch",
    )(zero_blocks, dest3, x1)


def _expert_ffn_kernel(be_ref, na_ref, x_ref, wg_ref, wu_ref, wd_ref, y_ref,
                       wg_bf, wu_bf, wd_bf):
    i = pl.program_id(0)
    active = i < na_ref[0]
    new_expert = (i == 0) | (be_ref[i] != be_ref[jnp.maximum(i - 1, 0)])

    @pl.when(active & new_expert)
    def _():
        wg_bf[...] = wg_ref[...].astype(BF16)
        wu_bf[...] = wu_ref[...].astype(BF16)
        wd_bf[...] = wd_ref[...].astype(BF16)

    @pl.when(active)
    def _():
        x = x_ref[...].astype(BF16)
        g = jnp.dot(x, wg_bf[...], preferred_element_type=F32)
        u = jnp.dot(x, wu_bf[...], preferred_element_type=F32)
        hid = (jax.nn.silu(g) * u).astype(BF16)
        y_ref[...] = jnp.dot(hid, wd_bf[...], preferred_element_type=F32)

    @pl.when(jnp.logical_not(active))
    def _():
        y_ref[...] = jnp.zeros_like(y_ref)


def _expert_ffn(xs, block_expert, n_active, wg, wu, wd):
    n_rows = xs.shape[0]
    n_blocks = n_rows // MOE_BLOCK

    def last_active(i, na):
        return jnp.maximum(jnp.minimum(i, na[0] - 1), 0)

    def blk(i, be, na):
        return (last_active(i, na), 0)

    def wmap(i, be, na):
        return (be[last_active(i, na)], 0, 0)

    grid_spec = pltpu.PrefetchScalarGridSpec(
        num_scalar_prefetch=2,
        grid=(n_blocks,),
        in_specs=[
            pl.BlockSpec((MOE_BLOCK, D_MODEL), blk),
            pl.BlockSpec((None, D_MODEL, D_EXPERT), wmap),
            pl.BlockSpec((None, D_MODEL, D_EXPERT), wmap),
            pl.BlockSpec((None, D_EXPERT, D_MODEL), wmap),
        ],
        out_specs=pl.BlockSpec((MOE_BLOCK, D_MODEL), lambda i, be, na: (i, 0)),
        scratch_shapes=[
            pltpu.VMEM((D_MODEL, D_EXPERT), BF16),
            pltpu.VMEM((D_MODEL, D_EXPERT), BF16),
            pltpu.VMEM((D_EXPERT, D_MODEL), BF16),
        ],
    )
    return pl.pallas_call(
        _expert_ffn_kernel,
        grid_spec=grid_spec,
        out_shape=jax.ShapeDtypeStruct((n_rows, D_MODEL), F32),
        compiler_params=pltpu.CompilerParams(
            dimension_semantics=("arbitrary",), vmem_limit_bytes=VMEM_LIMIT),
        name="expert_ffn",
    )(block_expert, n_active, xs, wg, wu, wd)


def _combine_kernel(dcur_ref, dnxt_ref, x_ref, route_ref, g_ref, b_ref, y_ref, o_ref,
                    buf, sem, *, td):
    s = pl.program_id(0)
    n = pl.num_programs(0)
    slot = s % 2

    def issue(d_ref, to_slot):
        def body(t, carry):
            for k in range(TOP_K):
                _row_copy(y_ref, d_ref[0, 0, k * td + t], buf.at[to_slot, k], t,
                          sem.at[to_slot]).start()
            return carry
        lax.fori_loop(0, td, body, 0, unroll=8)

    @pl.when(s == 0)
    def _():
        issue(dcur_ref, 0)

    @pl.when(s + 1 < n)
    def _():
        issue(dnxt_ref, 1 - slot)

    for k in range(TOP_K):
        pltpu.make_async_copy(y_ref.at[pl.ds(0, td), :], buf.at[slot, k], sem.at[slot]).wait()

    route = route_ref[...]
    ffn = route[:, 2:3] * buf[slot, 0] + route[:, 3:4] * buf[slot, 1]
    o_ref[...] = _layer_norm(DEEPNORM_ALPHA * x_ref[...] + ffn, g_ref[...], b_ref[...])


def _combine(ys, dest3, x1, route, g2, b2, td):
    n_tok = x1.shape[0]
    n_steps = n_tok // td
    return pl.pallas_call(
        functools.partial(_combine_kernel, td=td),
        grid=(n_steps,),
        in_specs=[
            pl.BlockSpec((1, 1, TOP_K * td), lambda s: (s, 0, 0), memory_space=pltpu.SMEM),
            pl.BlockSpec((1, 1, TOP_K * td), lambda s: (jnp.minimum(s + 1, n_steps - 1), 0, 0),
                         memory_space=pltpu.SMEM),
            pl.BlockSpec((td, D_MODEL), lambda s: (s, 0)),
            pl.BlockSpec((td, ROUTE_COLS), lambda s: (s, 0)),
            pl.BlockSpec((1, D_MODEL), lambda s: (0, 0)),
            pl.BlockSpec((1, D_MODEL), lambda s: (0, 0)),
            pl.BlockSpec(memory_space=pl.ANY),
        ],
        out_specs=pl.BlockSpec((td, D_MODEL), lambda s: (s, 0)),
        out_shape=jax.ShapeDtypeStruct((n_tok, D_MODEL), F32),
        scratch_shapes=[
            pltpu.VMEM((2, TOP_K, td, D_MODEL), F32),
            pltpu.SemaphoreType.DMA((2,)),
        ],
        compiler_params=pltpu.CompilerParams(
            dimension_semantics=("arbitrary",), vmem_limit_bytes=VMEM_LIMIT),
        name="combine",
    )(dest3, dest3, x1, route, g2, b2, ys)


def _row_tile(n, want):
    t = min(want, n)
    while n % t:
        t //= 2
    return t


def kernel(x, ln_in_g, ln_in_b, w_in, b_gate, lambda_q, lambda_k, subln_g, rel_bias, conv_w,
           w_a_proj, w_b_proj, w_o, ln1_g, ln1_b, w_group, b_group, w_sub, b_sub,
           w_gate_e, w_up_e, w_down_e, ln2_g, ln2_b):
    bsz, seq, d = x.shape
    assert DEPTH == 1 and d == D_MODEL and w_in.shape == (DEPTH, D_MODEL, N_IN)
    n_tok = bsz * seq
    tq = _row_tile(seq, 256)
    assert tq % LANES == 0 and tq % CHUNK == 0
    tm1 = _row_tile(n_tok, 512)
    tm3 = _row_tile(seq, 512)
    sub3 = _row_tile(tm3, 256)
    tr = _row_tile(n_tok, 1024)
    td = _row_tile(n_tok, 256)
    row = lambda v: v.reshape(1, -1).astype(F32)
    lam_init = 0.8 - 0.6 * math.exp(-0.3 * 0)

    xn, p = _ln_proj(x.reshape(n_tok, d), row(ln_in_g), row(ln_in_b), w_in[0].astype(BF16), tm1)

    band = _band_bias(rel_bias, tq)
    o_n = _diff_attn(p, band, lambda_q[0].astype(F32), lambda_k[0].astype(F32),
                     row(subln_g[0]), bsz, seq, tq, lam_init)

    w_r = jnp.concatenate(
        [w_group[0].astype(F32),
         jnp.transpose(w_sub[0].astype(F32), (1, 0, 2)).reshape(d, N_EXPERTS)], axis=1)
    w_r = jnp.pad(w_r, ((0, 0), (0, LANES - w_r.shape[1])))
    w_hi = w_r.astype(BF16)
    w_lo = (w_r - w_hi.astype(F32)).astype(BF16)
    w_r2 = jnp.concatenate([w_hi, w_lo], axis=1)
    b_r = jnp.concatenate([b_group[0].astype(F32), b_sub[0].astype(F32).reshape(-1)])
    b_r = jnp.pad(b_r, (0, LANES - b_r.shape[0])).reshape(1, LANES)
    x1, logits = _mix(
        o_n, p, xn, w_a_proj[0].astype(BF16), w_b_proj[0].astype(BF16), w_o[0].astype(BF16),
        conv_w[0].reshape(3, d).astype(F32), b_gate[0].astype(F32), row(ln1_g[0]),
        row(ln1_b[0]), w_r2, b_r, seq, tm3, sub3)
    route, counts = _route(logits, tr)

    n_assign = n_tok * TOP_K
    n_blocks = -(-n_assign // MOE_BLOCK) + N_EXPERTS
    cnt = counts[0, :N_EXPERTS].astype(jnp.int32)
    padded = ((cnt + MOE_BLOCK - 1) // MOE_BLOCK) * MOE_BLOCK
    pad_end = jnp.cumsum(padded)
    pad_start = pad_end - padded
    n_active = (pad_end[-1:] // MOE_BLOCK).astype(jnp.int32)
    blk_row0 = jnp.arange(n_blocks, dtype=jnp.int32) * MOE_BLOCK
    block_expert = jnp.minimum(
        jnp.sum((pad_end[None, :] <= blk_row0[:, None]).astype(jnp.int32), axis=1),
        N_EXPERTS - 1).astype(jnp.int32)
    experts = route[:, 0:TOP_K].astype(jnp.int32)
    ranks = route[:, 4:4 + TOP_K].astype(jnp.int32)
    is_e = experts[:, :, None] == jnp.arange(N_EXPERTS, dtype=jnp.int32)
    dest = jnp.sum(jnp.where(is_e, pad_start, 0), axis=-1) + ranks
    dest3 = jnp.transpose(dest.reshape(n_tok // td, td, TOP_K), (0, 2, 1))
    dest3 = dest3.reshape(n_tok // td, 1, TOP_K * td).astype(jnp.int32)

    last_blk = jnp.maximum(pad_end // MOE_BLOCK - 1, 0)
    idle_blk = jnp.minimum(n_active[0] + jnp.arange(N_EXPERTS), n_blocks - 1)
    n_zero = N_EXPERTS + n_blocks - n_active
    zero_blocks = jnp.concatenate([last_blk, idle_blk, n_zero]).astype(jnp.int32)
    xs = _dispatch(x1, dest3, zero_blocks, n_blocks * MOE_BLOCK, td)
    ys = _expert_ffn(xs, block_expert, n_active, w_gate_e[0].astype(F32),
                     w_up_e[0].astype(F32), w_down_e[0].astype(F32))
    out = _combine(ys, dest3, x1, route, row(ln2_g[0]), row(ln2_b[0]), td)
    return out.reshape(bsz, seq, d)
```

```python
import functools
import math

import numpy as np
import jax
import jax.numpy as jnp
from jax import lax
from jax.experimental import pallas as pl
from jax.experimental.pallas import tpu as pltpu

F32 = jnp.float32
BF16 = jnp.bfloat16

D_MODEL = 1024
N_HEADS = 4
HEAD_DIM = 128
D_HEAD_V = 2 * HEAD_DIM
CHUNK = 64
N_BUCKETS = 32
MAX_DISTANCE = 128
N_GROUPS = 4
EXPERTS_PER_GROUP = 8
N_EXPERTS = N_GROUPS * EXPERTS_PER_GROUP
TOP_K = 2
D_EXPERT = 512
MOE_BLOCK = 512
LN_EPS = 1e-5
RMS_EPS = 1e-6
DEPTH = 1
DEEPNORM_ALPHA = (2.0 * DEPTH) ** 0.25
N_IN = 8 * D_MODEL
LANES = 128
NEG_BIG = -1e30
LOG2_E = math.log2(math.e)

VMEM_LIMIT = 56 * 1024 * 1024


def _layer_norm(x, g, b):
    mu = jnp.mean(x, axis=-1, keepdims=True)
    xc = x - mu
    var = jnp.mean(xc * xc, axis=-1, keepdims=True)
    return xc * lax.rsqrt(var + LN_EPS) * g + b


def _const_spec(shape):
    return pl.BlockSpec(shape, lambda *_: (0,) * len(shape), pipeline_mode=pl.Buffered(1))


HALO = 16
P_Q, P_K, P_V, P_CONV, P_GATE_A, P_GATE_B = range(6)
N_P = 6 * D_MODEL


def _ln_proj_kernel(x_ref, g_ref, b_ref, w_ref, cw_ref, xn_ref, p_ref, zs_ref,
                    *, q_scale, tm, tiles_per_seq):
    r = pl.program_id(0)
    xn = _layer_norm(x_ref[...], g_ref[...], b_ref[...])
    xn_ref[...] = xn
    xb = xn.astype(BF16)

    def proj(c):
        return jnp.dot(xb, w_ref[:, c * D_MODEL:(c + 1) * D_MODEL], preferred_element_type=F32)

    def put(block, val):
        p_ref[:, block * D_MODEL:(block + 1) * D_MODEL] = val.astype(BF16)

    put(P_Q, proj(0) * q_scale)
    put(P_K, proj(1))
    put(P_V, proj(2))

    @pl.when((r % tiles_per_seq) == 0)
    def _():
        zs_ref[0:HALO, :] = jnp.zeros((HALO, D_MODEL), F32)

    zs_ref[HALO:HALO + tm, :] = proj(4) * proj(5)
    cw = cw_ref[...]
    zc = (cw[0:1] * zs_ref[HALO - 2:HALO - 2 + tm, :] + cw[1:2] * zs_ref[HALO - 1:HALO - 1 + tm, :]
          + cw[2:3] * zs_ref[HALO:HALO + tm, :])
    put(P_CONV, proj(3) * zc)
    zs_ref[0:HALO, :] = zs_ref[tm:tm + HALO, :]

    put(P_GATE_A, proj(6))
    put(P_GATE_B, proj(7))


def _ln_proj(x2, g, b, w_bf, conv_w, seq, tm):
    n_tok = x2.shape[0]
    kern = functools.partial(_ln_proj_kernel, q_scale=HEAD_DIM ** -0.5 * LOG2_E, tm=tm,
                             tiles_per_seq=seq // tm)
    return pl.pallas_call(
        kern,
        grid=(n_tok // tm,),
        in_specs=[
            pl.BlockSpec((tm, D_MODEL), lambda r: (r, 0)),
            _const_spec((1, D_MODEL)),
            _const_spec((1, D_MODEL)),
            _const_spec((D_MODEL, N_IN)),
            _const_spec((3, D_MODEL)),
        ],
        out_specs=[
            pl.BlockSpec((tm, D_MODEL), lambda r: (r, 0)),
            pl.BlockSpec((tm, N_P), lambda r: (r, 0)),
        ],
        out_shape=[
            jax.ShapeDtypeStruct((n_tok, D_MODEL), F32),
            jax.ShapeDtypeStruct((n_tok, N_P), BF16),
        ],
        scratch_shapes=[pltpu.VMEM((HALO + tm, D_MODEL), F32)],
        compiler_params=pltpu.CompilerParams(
            dimension_semantics=("arbitrary",), vmem_limit_bytes=VMEM_LIMIT),
        name="ln_proj",
    )(x2, g, b, w_bf, conv_w)


def _t5_bucket_np(rel):
    nb = N_BUCKETS // 2
    max_exact = nb // 2
    n = np.abs(rel)
    large = np.full(n.shape, max_exact, np.int64)
    for d in range(max_exact, MAX_DISTANCE + 1):
        val = max_exact + int(math.log(d / max_exact) / math.log(MAX_DISTANCE / max_exact)
                              * (nb - max_exact))
        large = np.where(n >= d, min(val, nb - 1), large)
    return np.where(rel > 0, nb, 0) + np.where(n < max_exact, n, large)


def _band_bias(rel_bias, tq):
    qi = np.arange(tq)[:, None]
    kj = np.arange(2 * tq)[None, :] - tq
    bucket = _t5_bucket_np(kj - qi)
    allowed = (kj // CHUNK) <= (qi // CHUNK)
    far_bucket = int(_t5_bucket_np(np.array([-(tq + 1)]))[0])
    rb = rel_bias.astype(F32)
    rb = ((rb - rb[far_bucket][None, :]) * LOG2_E).T
    onehot = (jnp.asarray(bucket, jnp.int32)[..., None]
              == jnp.arange(N_BUCKETS, dtype=jnp.int32))
    tile = jnp.sum(jnp.where(onehot[None], rb[:, None, None, :], 0.0), axis=-1)
    return jnp.where(jnp.asarray(allowed)[None], tile, NEG_BIG)


def _diff_attn_kernel(q_ref, k_ref, v_ref, bias_ref, lq_ref, lk_ref, sg_ref, o_ref,
                      s_ref, mx_ref, ls_ref, acc_ref, *, tq, lam_init):
    i = pl.program_id(1)
    heads_maps = [(h, m) for h in range(N_HEADS) for m in range(2)]

    def lane_tiles(s):
        return [s[:, c * LANES:(c + 1) * LANES] for c in range(s.shape[1] // LANES)]

    def qk_block(j, bias_cols, first):
        ks = pl.multiple_of(j * tq, tq)
        for h, m in heads_maps:
            c = h * D_HEAD_V + m * HEAD_DIM
            s = lax.dot_general(q_ref[:, c:c + HEAD_DIM], k_ref[pl.ds(ks, tq), c:c + HEAD_DIM],
                                (((1,), (1,)), ((), ())), preferred_element_type=F32)
            if bias_cols is not None:
                s = s + bias_ref[h, :, bias_cols]
            s_ref[h, m, j] = s
            t = functools.reduce(jnp.maximum, lane_tiles(s))
            mx_ref[h, m] = t if first else jnp.maximum(mx_ref[h, m], t)

    qk_block(i, slice(tq, 2 * tq), True)

    @pl.when(i >= 1)
    def _():
        qk_block(i - 1, slice(0, tq), False)

    n_far = jnp.maximum(i - 1, 0)

    def far_pair(t, carry):
        qk_block(2 * t, None, False)
        qk_block(2 * t + 1, None, False)
        return carry

    lax.fori_loop(0, lax.shift_right_logical(n_far, 1), far_pair, 0)

    @pl.when((n_far & 1) == 1)
    def _():
        qk_block(n_far - 1, None, False)

    for h, m in heads_maps:
        mx_ref[h, m] = jnp.broadcast_to(jnp.max(mx_ref[h, m], axis=-1, keepdims=True),
                                        (tq, LANES))

    def pv_blocks(j0, nb, first):
        ks = pl.multiple_of(j0 * tq, tq)
        for h in range(N_HEADS):
            c0 = h * D_HEAD_V
            rows = []
            for m in range(2):
                rmax = mx_ref[h, m]
                tiles = []
                for t in range(nb):
                    tiles += [jnp.exp2(x - rmax) for x in lane_tiles(s_ref[h, m, j0 + t])]
                tsum = functools.reduce(jnp.add, tiles)
                ls_ref[h, m] = tsum if first else ls_ref[h, m] + tsum
                rows.append(jnp.concatenate(tiles, axis=1))
            p = jnp.concatenate(rows, axis=0).astype(BF16)
            contrib = jnp.dot(p, v_ref[pl.ds(ks, nb * tq), c0:c0 + D_HEAD_V],
                              preferred_element_type=F32)
            acc_ref[h] = contrib if first else acc_ref[h] + contrib

    pv_blocks(i, 1, True)

    def pv_pair(t, carry):
        pv_blocks(2 * t, 2, False)
        return carry

    lax.fori_loop(0, lax.shift_right_logical(i, 1), pv_pair, 0)

    @pl.when((i & 1) == 1)
    def _():
        pv_blocks(i - 1, 1, False)

    dots = jnp.sum(lq_ref[...] * lk_ref[...], axis=-1, keepdims=True)
    lam = jnp.exp(dots[0:1]) - jnp.exp(dots[1:2]) + lam_init
    for h in range(N_HEADS):
        c0 = h * D_HEAD_V
        l1 = jnp.sum(ls_ref[h, 0], axis=-1, keepdims=True)
        l2 = jnp.sum(ls_ref[h, 1], axis=-1, keepdims=True)
        o = acc_ref[h, 0:tq, :] / l1 - lam * (acc_ref[h, tq:2 * tq, :] / l2)
        o = o * lax.rsqrt(jnp.mean(o * o, axis=-1, keepdims=True) + RMS_EPS)
        o = o * sg_ref[...] * (1.0 - lam_init)
        o_ref[:, c0:c0 + D_HEAD_V] = o.astype(BF16)


def _diff_attn(p, band, lam_q, lam_k, subln_g, bsz, seq, tq, lam_init):
    n_tok = bsz * seq
    nq = seq // tq
    kern = functools.partial(_diff_attn_kernel, tq=tq, lam_init=lam_init)
    return pl.pallas_call(
        kern,
        grid=(bsz, nq),
        in_specs=[
            pl.BlockSpec((tq, D_MODEL), lambda b, i: (b * nq + i, 0)),
            pl.BlockSpec((seq, D_MODEL), lambda b, i: (b, 1)),
            pl.BlockSpec((seq, D_MODEL), lambda b, i: (b, 2)),
            _const_spec((N_HEADS, tq, 2 * tq)),
            _const_spec((2, HEAD_DIM)),
            _const_spec((2, HEAD_DIM)),
            _const_spec((1, D_HEAD_V)),
        ],
        out_specs=pl.BlockSpec((tq, D_MODEL), lambda b, i: (b * nq + i, 0)),
        out_shape=jax.ShapeDtypeStruct((n_tok, D_MODEL), BF16),
        scratch_shapes=[
            pltpu.VMEM((N_HEADS, 2, nq, tq, tq), F32),
            pltpu.VMEM((N_HEADS, 2, tq, LANES), F32),
            pltpu.VMEM((N_HEADS, 2, tq, LANES), F32),
            pltpu.VMEM((N_HEADS, 2 * tq, D_HEAD_V), F32),
        ],
        compiler_params=pltpu.CompilerParams(
            dimension_semantics=("arbitrary", "arbitrary"), vmem_limit_bytes=VMEM_LIMIT),
        name="diff_attn",
    )(p, p, p, band, lam_q, lam_k, subln_g)


HALF = D_MODEL // 2


def _pack_rows(x_bf):
    bits = pltpu.bitcast(x_bf.astype(F32), jnp.uint32)
    return (bits[:, 0:HALF] >> 16) | bits[:, HALF:D_MODEL]


def _unpack_rows(words):
    lo = pltpu.bitcast(words << 16, F32)
    hi = pltpu.bitcast(words & jnp.uint32(0xFFFF0000), F32)
    return jnp.concatenate([lo, hi], axis=1)


def _mix_kernel(on_ref, yb_ref, ga_ref, gb_ref, xn_ref, wa_ref, wb_ref, wo_ref, bg_ref,
                g1_ref, b1_ref, wr_ref, br_ref, x1_ref, xp_ref, lg_ref, *, tm, sub):
    bg = bg_ref[...]
    for lo in range(0, tm, sub):
        rows = slice(lo, lo + sub)
        y_b = jnp.dot(yb_ref[rows, :], wb_ref[...], preferred_element_type=F32)
        y_a = jnp.dot(on_ref[rows, :], wa_ref[...], preferred_element_type=F32)
        g_a = jax.nn.sigmoid(ga_ref[rows, :].astype(F32) + bg[0:1])
        g_b = jax.nn.sigmoid(gb_ref[rows, :].astype(F32) + bg[1:2])
        merged = (g_a * y_a + g_b * y_b).astype(BF16)
        mix = jnp.dot(merged, wo_ref[...], preferred_element_type=F32)
        x1 = _layer_norm(DEEPNORM_ALPHA * xn_ref[rows, :] + mix, g1_ref[...], b1_ref[...])
        x1_ref[rows, :] = x1
        x_hi = x1.astype(BF16)
        x_lo = (x1 - x_hi.astype(F32)).astype(BF16)
        part = (jnp.dot(x_hi, wr_ref[...], preferred_element_type=F32)
                + jnp.dot(x_lo, wr_ref[...], preferred_element_type=F32))
        lg_ref[rows, :] = part[:, 0:LANES] + part[:, LANES:2 * LANES] + br_ref[...]
        xp_ref[rows, :] = _pack_rows(x_hi)


def _mix(o_n, p, xn, wa, wb, wo, b_gate, g1, b1, wr2, br, tm, sub):
    n_tok = xn.shape[0]
    kern = functools.partial(_mix_kernel, tm=tm, sub=sub)

    def col(c):
        return pl.BlockSpec((tm, D_MODEL), lambda r, c=c: (r, c))

    return pl.pallas_call(
        kern,
        grid=(n_tok // tm,),
        in_specs=[
            pl.BlockSpec((tm, D_MODEL), lambda r: (r, 0)),
            col(P_CONV), col(P_GATE_A), col(P_GATE_B),
            pl.BlockSpec((tm, D_MODEL), lambda r: (r, 0)),
            _const_spec((D_MODEL, D_MODEL)), _const_spec((D_MODEL, D_MODEL)),
            _const_spec((D_MODEL, D_MODEL)),
            _const_spec((2, D_MODEL)), _const_spec((1, D_MODEL)), _const_spec((1, D_MODEL)),
            _const_spec((D_MODEL, 2 * LANES)), _const_spec((1, LANES)),
        ],
        out_specs=[
            pl.BlockSpec((tm, D_MODEL), lambda r: (r, 0)),
            pl.BlockSpec((tm, HALF), lambda r: (r, 0)),
            pl.BlockSpec((tm, LANES), lambda r: (r, 0)),
        ],
        out_shape=[
            jax.ShapeDtypeStruct((n_tok, D_MODEL), F32),
            jax.ShapeDtypeStruct((n_tok, HALF), jnp.uint32),
            jax.ShapeDtypeStruct((n_tok, LANES), F32),
        ],
        compiler_params=pltpu.CompilerParams(
            dimension_semantics=("arbitrary",), vmem_limit_bytes=VMEM_LIMIT),
        name="mix",
    )(o_n, p, p, p, xn, wa, wb, wo, b_gate, g1, b1, wr2, br)


ROUTE_COLS = 8


def _route_kernel(lg_ref, tri_ref, route_ref, cnt_ref, run_ref, *, tr):
    @pl.when(pl.program_id(0) == 0)
    def _():
        run_ref[...] = jnp.zeros_like(run_ref)

    logit = lg_ref[...]
    lane = lax.broadcasted_iota(jnp.int32, (tr, LANES), 1).astype(F32)
    big = float(4 * LANES)

    def first_argmax(vals, vmax):
        return jnp.min(jnp.where(vals == vmax, lane, big), axis=-1, keepdims=True)

    gl = jnp.where(lane < N_GROUPS, logit, -jnp.inf)
    gmax = jnp.max(gl, axis=-1, keepdims=True)
    gsum = jnp.sum(jnp.exp(gl - gmax), axis=-1, keepdims=True)
    g_p = 1.0 / gsum
    g_idx = first_argmax(gl, gmax)
    lo = N_GROUPS + EXPERTS_PER_GROUP * g_idx
    in_group = (lane >= lo) & (lane < lo + EXPERTS_PER_GROUP)
    sl = jnp.where(in_group, logit, -jnp.inf)
    s1 = jnp.max(sl, axis=-1, keepdims=True)
    i1 = first_argmax(sl, s1)
    sl2 = jnp.where(lane == i1, -jnp.inf, sl)
    s2 = jnp.max(sl2, axis=-1, keepdims=True)
    i2 = first_argmax(sl2, s2)
    t = jnp.exp(s2 - s1)
    w1 = g_p / (1.0 + t)
    w2 = g_p * t / (1.0 + t)
    e1 = i1 - N_GROUPS
    e2 = i2 - N_GROUPS

    oh1 = lane == e1
    oh2 = lane == e2
    onehot = jnp.where(oh1 | oh2, 1.0, 0.0)
    before = jnp.dot(tri_ref[...], onehot.astype(BF16), preferred_element_type=F32) + run_ref[...]
    rank1 = jnp.sum(jnp.where(oh1, before, 0.0), axis=-1, keepdims=True)
    rank2 = jnp.sum(jnp.where(oh2, before, 0.0), axis=-1, keepdims=True)
    run_ref[...] = run_ref[...] + jnp.sum(onehot, axis=0, keepdims=True)
    cnt_ref[...] = jnp.broadcast_to(run_ref[...], cnt_ref.shape)

    route = jnp.where(lane == 0, e1, 0.0)
    for idx, val in enumerate((e2, w1, w2, rank1, rank2), start=1):
        route = jnp.where(lane == idx, val, route)
    route_ref[...] = route[:, 0:ROUTE_COLS]


def _route(logits, tr):
    n_tok = logits.shape[0]
    tri = jnp.asarray(np.tril(np.ones((tr, tr), np.float32), k=-1), BF16)
    return pl.pallas_call(
        functools.partial(_route_kernel, tr=tr),
        grid=(n_tok // tr,),
        in_specs=[pl.BlockSpec((tr, LANES), lambda r: (r, 0)), _const_spec((tr, tr))],
        out_specs=[
            pl.BlockSpec((tr, ROUTE_COLS), lambda r: (r, 0)),
            pl.BlockSpec((8, LANES), lambda r: (0, 0)),
        ],
        out_shape=[
            jax.ShapeDtypeStruct((n_tok, ROUTE_COLS), F32),
            jax.ShapeDtypeStruct((8, LANES), F32),
        ],
        scratch_shapes=[pltpu.VMEM((1, LANES), F32)],
        compiler_params=pltpu.CompilerParams(
            dimension_semantics=("arbitrary",), vmem_limit_bytes=VMEM_LIMIT),
        name="route",
    )(logits, tri)


def _row_copy(src, src_row, dst, dst_row, sem):
    return pltpu.make_async_copy(src.at[pl.ds(src_row, 1), :], dst.at[pl.ds(dst_row, 1), :], sem)


N_ZERO_BLOCKS = 2 * N_EXPERTS


def _dispatch_kernel(zb_ref, dest_ref, x_ref, out_ref, zeros, sem, zsem, *, td):
    @pl.when(pl.program_id(0) == 0)
    def _():
        zeros[...] = jnp.zeros_like(zeros)

        def block_copy(n):
            row0 = pl.multiple_of(zb_ref[n] * MOE_BLOCK, MOE_BLOCK)
            return pltpu.make_async_copy(zeros, out_ref.at[pl.ds(row0, MOE_BLOCK), :], zsem)

        n_zero = zb_ref[N_ZERO_BLOCKS]

        def start(n, carry):
            block_copy(n).start()
            return carry

        def wait(n, carry):
            block_copy(n).wait()
            return carry

        lax.fori_loop(0, n_zero, start, 0)
        lax.fori_loop(0, n_zero, wait, 0)

    def issue(t, carry):
        for k in range(TOP_K):
            _row_copy(x_ref, t, out_ref, dest_ref[0, 0, k * td + t], sem).start(priority=k)
        return carry

    lax.fori_loop(0, td, issue, 0, unroll=8)
    for k in range(TOP_K):
        pltpu.make_async_copy(x_ref, out_ref.at[pl.ds(0, td), :], sem).wait()


def _dispatch(xp, dest3, zero_blocks, n_rows, td):
    n_tok = xp.shape[0]
    grid_spec = pltpu.PrefetchScalarGridSpec(
        num_scalar_prefetch=1,
        grid=(n_tok // td,),
        in_specs=[
            pl.BlockSpec((1, 1, TOP_K * td), lambda s, zb: (s, 0, 0), memory_space=pltpu.SMEM),
            pl.BlockSpec((td, HALF), lambda s, zb: (s, 0)),
        ],
        out_specs=pl.BlockSpec(memory_space=pl.ANY),
        scratch_shapes=[
            pltpu.VMEM((MOE_BLOCK, HALF), jnp.uint32),
            pltpu.SemaphoreType.DMA(()),
            pltpu.SemaphoreType.DMA(()),
        ],
    )
    return pl.pallas_call(
        functools.partial(_dispatch_kernel, td=td),
        grid_spec=grid_spec,
        out_shape=jax.ShapeDtypeStruct((n_rows, HALF), jnp.uint32),
        compiler_params=pltpu.CompilerParams(dimension_semantics=("arbitrary",)),
        name="dispatch",
    )(zero_blocks, dest3, xp)


FFN_SUB = 256


def _expert_ffn_kernel(be_ref, na_ref, x_ref, wg_ref, wu_ref, wd_ref, y_ref,
                       wg_bf, wu_bf, wd_bf):
    i = pl.program_id(0)
    active = i < na_ref[0]
    new_expert = (i == 0) | (be_ref[i] != be_ref[jnp.maximum(i - 1, 0)])

    @pl.when(active & new_expert)
    def _():
        wg_bf[...] = wg_ref[...].astype(BF16)
        wu_bf[...] = wu_ref[...].astype(BF16)
        wd_bf[...] = wd_ref[...].astype(BF16)

    @pl.when(active)
    def _():
        for lo in range(0, MOE_BLOCK, FFN_SUB):
            rows = slice(lo, lo + FFN_SUB)
            x = _unpack_rows(x_ref[rows, :]).astype(BF16)
            g = jnp.dot(x, wg_bf[...], preferred_element_type=F32)
            u = jnp.dot(x, wu_bf[...], preferred_element_type=F32)
            hid = (jax.nn.silu(g) * u).astype(BF16)
            y = jnp.dot(hid, wd_bf[...], preferred_element_type=F32)
            y_ref[rows, :] = _pack_rows(y.astype(BF16))

    @pl.when(jnp.logical_not(active))
    def _():
        y_ref[...] = jnp.zeros_like(y_ref)


def _expert_ffn(xs, block_expert, n_active, wg, wu, wd):
    n_rows = xs.shape[0]
    n_blocks = n_rows // MOE_BLOCK

    def last_active(i, na):
        return jnp.maximum(jnp.minimum(i, na[0] - 1), 0)

    def blk(i, be, na):
        return (last_active(i, na), 0)

    def wmap(i, be, na):
        return (be[last_active(i, na)], 0, 0)

    grid_spec = pltpu.PrefetchScalarGridSpec(
        num_scalar_prefetch=2,
        grid=(n_blocks,),
        in_specs=[
            pl.BlockSpec((MOE_BLOCK, HALF), blk),
            pl.BlockSpec((None, D_MODEL, D_EXPERT), wmap),
            pl.BlockSpec((None, D_MODEL, D_EXPERT), wmap),
            pl.BlockSpec((None, D_EXPERT, D_MODEL), wmap),
        ],
        out_specs=pl.BlockSpec((MOE_BLOCK, HALF), lambda i, be, na: (i, 0)),
        scratch_shapes=[
            pltpu.VMEM((D_MODEL, D_EXPERT), BF16),
            pltpu.VMEM((D_MODEL, D_EXPERT), BF16),
            pltpu.VMEM((D_EXPERT, D_MODEL), BF16),
        ],
    )
    return pl.pallas_call(
        _expert_ffn_kernel,
        grid_spec=grid_spec,
        out_shape=jax.ShapeDtypeStruct((n_rows, HALF), jnp.uint32),
        compiler_params=pltpu.CompilerParams(
            dimension_semantics=("arbitrary",), vmem_limit_bytes=VMEM_LIMIT),
        name="expert_ffn",
    )(block_expert, n_active, xs, wg, wu, wd)


def _combine_kernel(dcur_ref, dnxt_ref, x_ref, route_ref, g_ref, b_ref, y_ref, o_ref,
                    buf, sem, *, td):
    s = pl.program_id(0)
    n = pl.num_programs(0)
    slot = s % 2

    def issue(d_ref, to_slot):
        def body(t, carry):
            for k in range(TOP_K):
                _row_copy(y_ref, d_ref[0, 0, k * td + t], buf.at[to_slot, k], t,
                          sem.at[to_slot]).start(priority=k)
            return carry
        lax.fori_loop(0, td, body, 0, unroll=8)

    @pl.when(s == 0)
    def _():
        issue(dcur_ref, 0)

    @pl.when(s + 1 < n)
    def _():
        issue(dnxt_ref, 1 - slot)

    for k in range(TOP_K):
        pltpu.make_async_copy(y_ref.at[pl.ds(0, td), :], buf.at[slot, k], sem.at[slot]).wait()

    route = route_ref[...]
    ffn = route[:, 2:3] * _unpack_rows(buf[slot, 0]) + route[:, 3:4] * _unpack_rows(buf[slot, 1])
    o_ref[...] = _layer_norm(DEEPNORM_ALPHA * x_ref[...] + ffn, g_ref[...], b_ref[...])


def _combine(ys, dest3, x1, route, g2, b2, td):
    n_tok = x1.shape[0]
    n_steps = n_tok // td
    return pl.pallas_call(
        functools.partial(_combine_kernel, td=td),
        grid=(n_steps,),
        in_specs=[
            pl.BlockSpec((1, 1, TOP_K * td), lambda s: (s, 0, 0), memory_space=pltpu.SMEM),
            pl.BlockSpec((1, 1, TOP_K * td), lambda s: (jnp.minimum(s + 1, n_steps - 1), 0, 0),
                         memory_space=pltpu.SMEM),
            pl.BlockSpec((td, D_MODEL), lambda s: (s, 0)),
            pl.BlockSpec((td, ROUTE_COLS), lambda s: (s, 0)),
            pl.BlockSpec((1, D_MODEL), lambda s: (0, 0)),
            pl.BlockSpec((1, D_MODEL), lambda s: (0, 0)),
            pl.BlockSpec(memory_space=pl.ANY),
        ],
        out_specs=pl.BlockSpec((td, D_MODEL), lambda s: (s, 0)),
        out_shape=jax.ShapeDtypeStruct((n_tok, D_MODEL), F32),
        scratch_shapes=[
            pltpu.VMEM((2, TOP_K, td, HALF), jnp.uint32),
            pltpu.SemaphoreType.DMA((2,)),
        ],
        compiler_params=pltpu.CompilerParams(
            dimension_semantics=("arbitrary",), vmem_limit_bytes=VMEM_LIMIT),
        name="combine",
    )(dest3, dest3, x1, route, g2, b2, ys)


def _row_tile(n, want):
    t = min(want, n)
    while n % t:
        t //= 2
    return t


def kernel(x, ln_in_g, ln_in_b, w_in, b_gate, lambda_q, lambda_k, subln_g, rel_bias, conv_w,
           w_a_proj, w_b_proj, w_o, ln1_g, ln1_b, w_group, b_group, w_sub, b_sub,
           w_gate_e, w_up_e, w_down_e, ln2_g, ln2_b):
    bsz, seq, d = x.shape
    assert DEPTH == 1 and d == D_MODEL and w_in.shape == (DEPTH, D_MODEL, N_IN)
    n_tok = bsz * seq
    tq = _row_tile(seq, 256)
    assert tq % LANES == 0 and tq % CHUNK == 0
    tm1 = _row_tile(seq, 512)
    tm3 = _row_tile(n_tok, 512)
    sub3 = _row_tile(tm3, 256)
    tr = _row_tile(n_tok, 1024)
    td = _row_tile(n_tok, 256)
    row = lambda v: v.reshape(1, -1).astype(F32)
    lam_init = 0.8 - 0.6 * math.exp(-0.3 * 0)

    xn, p = _ln_proj(x.reshape(n_tok, d), row(ln_in_g), row(ln_in_b), w_in[0].astype(BF16),
                     conv_w[0].reshape(3, d).astype(F32), seq, tm1)

    band = _band_bias(rel_bias, tq)
    o_n = _diff_attn(p, band, lambda_q[0].astype(F32), lambda_k[0].astype(F32),
                     row(subln_g[0]), bsz, seq, tq, lam_init)

    w_r = jnp.concatenate(
        [w_group[0].astype(F32),
         jnp.transpose(w_sub[0].astype(F32), (1, 0, 2)).reshape(d, N_EXPERTS)], axis=1)
    w_r = jnp.pad(w_r, ((0, 0), (0, LANES - w_r.shape[1])))
    w_hi = w_r.astype(BF16)
    w_lo = (w_r - w_hi.astype(F32)).astype(BF16)
    w_r2 = jnp.concatenate([w_hi, w_lo], axis=1)
    b_r = jnp.concatenate([b_group[0].astype(F32), b_sub[0].astype(F32).reshape(-1)])
    b_r = jnp.pad(b_r, (0, LANES - b_r.shape[0])).reshape(1, LANES)
    x1, xp, logits = _mix(
        o_n, p, xn, w_a_proj[0].astype(BF16), w_b_proj[0].astype(BF16), w_o[0].astype(BF16),
        b_gate[0].astype(F32), row(ln1_g[0]), row(ln1_b[0]), w_r2, b_r, tm3, sub3)
    route, counts = _route(logits, tr)

    n_assign = n_tok * TOP_K
    n_blocks = -(-n_assign // MOE_BLOCK) + N_EXPERTS
    cnt = counts[0, :N_EXPERTS].astype(jnp.int32)
    padded = ((cnt + MOE_BLOCK - 1) // MOE_BLOCK) * MOE_BLOCK
    pad_end = jnp.cumsum(padded)
    pad_start = pad_end - padded
    n_active = (pad_end[-1:] // MOE_BLOCK).astype(jnp.int32)
    blk_row0 = jnp.arange(n_blocks, dtype=jnp.int32) * MOE_BLOCK
    block_expert = jnp.minimum(
        jnp.sum((pad_end[None, :] <= blk_row0[:, None]).astype(jnp.int32), axis=1),
        N_EXPERTS - 1).astype(jnp.int32)
    experts = route[:, 0:TOP_K].astype(jnp.int32)
    ranks = route[:, 4:4 + TOP_K].astype(jnp.int32)
    is_e = experts[:, :, None] == jnp.arange(N_EXPERTS, dtype=jnp.int32)
    dest = jnp.sum(jnp.where(is_e, pad_start, 0), axis=-1) + ranks
    dest3 = jnp.transpose(dest.reshape(n_tok // td, td, TOP_K), (0, 2, 1))
    dest3 = dest3.reshape(n_tok // td, 1, TOP_K * td).astype(jnp.int32)

    last_blk = jnp.maximum(pad_end // MOE_BLOCK - 1, 0)
    idle_blk = jnp.minimum(n_active[0] + jnp.arange(N_EXPERTS), n_blocks - 1)
    n_zero = N_EXPERTS + n_blocks - n_active
    zero_blocks = jnp.concatenate([last_blk, idle_blk, n_zero]).astype(jnp.int32)
    xs = _dispatch(xp, dest3, zero_blocks, n_blocks * MOE_BLOCK, td)
    ys = _expert_ffn(xs, block_expert, n_active, w_gate_e[0].astype(F32),
                     w_up_e[0].astype(F32), w_down_e[0].astype(F32))
    out = _combine(ys, dest3, x1, route, row(ln2_g[0]), row(ln2_b[0]), td)
    return out.reshape(bsz, seq, d)
```

```python
import functools
import math

import numpy as np
import jax
import jax.numpy as jnp
from jax import lax
from jax.experimental import pallas as pl
from jax.experimental.pallas import tpu as pltpu

F32 = jnp.float32
BF16 = jnp.bfloat16

D_MODEL = 1024
N_HEADS = 4
HEAD_DIM = 128
D_HEAD_V = 2 * HEAD_DIM
CHUNK = 64
N_BUCKETS = 32
MAX_DISTANCE = 128
N_GROUPS = 4
EXPERTS_PER_GROUP = 8
N_EXPERTS = N_GROUPS * EXPERTS_PER_GROUP
TOP_K = 2
D_EXPERT = 512
MOE_BLOCK = 512
LN_EPS = 1e-5
RMS_EPS = 1e-6
DEPTH = 1
DEEPNORM_ALPHA = (2.0 * DEPTH) ** 0.25
N_IN = 8 * D_MODEL
LANES = 128
NEG_BIG = -1e30
LOG2_E = math.log2(math.e)

VMEM_LIMIT = 56 * 1024 * 1024


def _layer_norm(x, g, b):
    mu = jnp.mean(x, axis=-1, keepdims=True)
    xc = x - mu
    var = jnp.mean(xc * xc, axis=-1, keepdims=True)
    return xc * lax.rsqrt(var + LN_EPS) * g + b


def _const_spec(shape):
    return pl.BlockSpec(shape, lambda *_: (0,) * len(shape), pipeline_mode=pl.Buffered(1))


HALO = 16
P_Q, P_K, P_V, P_CONV, P_GATE_A, P_GATE_B = range(6)
N_P = 6 * D_MODEL


def _ln_proj_kernel(x_ref, g_ref, b_ref, w_ref, cw_ref, xn_ref, p_ref, zs_ref,
                    *, q_scale, tm, tiles_per_seq):
    r = pl.program_id(0)
    xn = _layer_norm(x_ref[...], g_ref[...], b_ref[...])
    xn_ref[...] = xn
    xb = xn.astype(BF16)

    def proj(c):
        return jnp.dot(xb, w_ref[:, c * D_MODEL:(c + 1) * D_MODEL], preferred_element_type=F32)

    def put(block, val):
        p_ref[:, block * D_MODEL:(block + 1) * D_MODEL] = val.astype(BF16)

    put(P_Q, proj(0) * q_scale)
    put(P_K, proj(1))
    put(P_V, proj(2))

    @pl.when((r % tiles_per_seq) == 0)
    def _():
        zs_ref[0:HALO, :] = jnp.zeros((HALO, D_MODEL), F32)

    zs_ref[HALO:HALO + tm, :] = proj(4) * proj(5)
    cw = cw_ref[...]
    zc = (cw[0:1] * zs_ref[HALO - 2:HALO - 2 + tm, :] + cw[1:2] * zs_ref[HALO - 1:HALO - 1 + tm, :]
          + cw[2:3] * zs_ref[HALO:HALO + tm, :])
    put(P_CONV, proj(3) * zc)
    zs_ref[0:HALO, :] = zs_ref[tm:tm + HALO, :]

    put(P_GATE_A, proj(6))
    put(P_GATE_B, proj(7))


def _ln_proj(x2, g, b, w_bf, conv_w, seq, tm):
    n_tok = x2.shape[0]
    kern = functools.partial(_ln_proj_kernel, q_scale=HEAD_DIM ** -0.5 * LOG2_E, tm=tm,
                             tiles_per_seq=seq // tm)
    return pl.pallas_call(
        kern,
        grid=(n_tok // tm,),
        in_specs=[
            pl.BlockSpec((tm, D_MODEL), lambda r: (r, 0)),
            _const_spec((1, D_MODEL)),
            _const_spec((1, D_MODEL)),
            _const_spec((D_MODEL, N_IN)),
            _const_spec((3, D_MODEL)),
        ],
        out_specs=[
            pl.BlockSpec((tm, D_MODEL), lambda r: (r, 0)),
            pl.BlockSpec((tm, N_P), lambda r: (r, 0)),
        ],
        out_shape=[
            jax.ShapeDtypeStruct((n_tok, D_MODEL), F32),
            jax.ShapeDtypeStruct((n_tok, N_P), BF16),
        ],
        scratch_shapes=[pltpu.VMEM((HALO + tm, D_MODEL), F32)],
        compiler_params=pltpu.CompilerParams(
            dimension_semantics=("arbitrary",), vmem_limit_bytes=VMEM_LIMIT),
        name="ln_proj",
    )(x2, g, b, w_bf, conv_w)


def _t5_bucket_np(rel):
    nb = N_BUCKETS // 2
    max_exact = nb // 2
    n = np.abs(rel)
    large = np.full(n.shape, max_exact, np.int64)
    for d in range(max_exact, MAX_DISTANCE + 1):
        val = max_exact + int(math.log(d / max_exact) / math.log(MAX_DISTANCE / max_exact)
                              * (nb - max_exact))
        large = np.where(n >= d, min(val, nb - 1), large)
    return np.where(rel > 0, nb, 0) + np.where(n < max_exact, n, large)


def _band_bias(rel_bias, tq):
    qi = np.arange(tq)[:, None]
    kj = np.arange(2 * tq)[None, :] - tq
    bucket = _t5_bucket_np(kj - qi)
    allowed = (kj // CHUNK) <= (qi // CHUNK)
    far_bucket = int(_t5_bucket_np(np.array([-(tq + 1)]))[0])
    rb = rel_bias.astype(F32)
    rb = ((rb - rb[far_bucket][None, :]) * LOG2_E).T
    onehot = (jnp.asarray(bucket, jnp.int32)[..., None]
              == jnp.arange(N_BUCKETS, dtype=jnp.int32))
    tile = jnp.sum(jnp.where(onehot[None], rb[:, None, None, :], 0.0), axis=-1)
    return jnp.where(jnp.asarray(allowed)[None], tile, NEG_BIG)


def _diff_attn_kernel(q_ref, k_ref, v_ref, bias_ref, lq_ref, lk_ref, sg_ref, o_ref,
                      s_ref, mx_ref, ls_ref, acc_ref, *, tq, lam_init):
    i = pl.program_id(1)
    heads_maps = [(h, m) for h in range(N_HEADS) for m in range(2)]

    def lane_tiles(s):
        return [s[:, c * LANES:(c + 1) * LANES] for c in range(s.shape[1] // LANES)]

    def qk_block(j, bias_cols, first):
        ks = pl.multiple_of(j * tq, tq)
        for h, m in heads_maps:
            c = h * D_HEAD_V + m * HEAD_DIM
            s = lax.dot_general(q_ref[:, c:c + HEAD_DIM], k_ref[pl.ds(ks, tq), c:c + HEAD_DIM],
                                (((1,), (1,)), ((), ())), preferred_element_type=F32)
            if bias_cols is not None:
                s = s + bias_ref[h, :, bias_cols]
            s_ref[h, m, j] = s
            t = functools.reduce(jnp.maximum, lane_tiles(s))
            mx_ref[h, m] = t if first else jnp.maximum(mx_ref[h, m], t)

    qk_block(i, slice(tq, 2 * tq), True)

    @pl.when(i >= 1)
    def _():
        qk_block(i - 1, slice(0, tq), False)

    n_far = jnp.maximum(i - 1, 0)

    def far_pair(t, carry):
        qk_block(2 * t, None, False)
        qk_block(2 * t + 1, None, False)
        return carry

    lax.fori_loop(0, lax.shift_right_logical(n_far, 1), far_pair, 0)

    @pl.when((n_far & 1) == 1)
    def _():
        qk_block(n_far - 1, None, False)

    for h, m in heads_maps:
        mx_ref[h, m] = jnp.broadcast_to(jnp.max(mx_ref[h, m], axis=-1, keepdims=True),
                                        (tq, LANES))

    def pv_blocks(j0, nb, first):
        ks = pl.multiple_of(j0 * tq, tq)
        for h in range(N_HEADS):
            c0 = h * D_HEAD_V
            rows = []
            for m in range(2):
                rmax = mx_ref[h, m]
                tiles = []
                for t in range(nb):
                    tiles += [jnp.exp2(x - rmax) for x in lane_tiles(s_ref[h, m, j0 + t])]
                tsum = functools.reduce(jnp.add, tiles)
                ls_ref[h, m] = tsum if first else ls_ref[h, m] + tsum
                rows.append(jnp.concatenate(tiles, axis=1))
            p = jnp.concatenate(rows, axis=0).astype(BF16)
            contrib = jnp.dot(p, v_ref[pl.ds(ks, nb * tq), c0:c0 + D_HEAD_V],
                              preferred_element_type=F32)
            acc_ref[h] = contrib if first else acc_ref[h] + contrib

    pv_blocks(i, 1, True)

    def pv_pair(t, carry):
        pv_blocks(2 * t, 2, False)
        return carry

    lax.fori_loop(0, lax.shift_right_logical(i, 1), pv_pair, 0)

    @pl.when((i & 1) == 1)
    def _():
        pv_blocks(i - 1, 1, False)

    dots = jnp.sum(lq_ref[...] * lk_ref[...], axis=-1, keepdims=True)
    lam = jnp.exp(dots[0:1]) - jnp.exp(dots[1:2]) + lam_init
    for h in range(N_HEADS):
        c0 = h * D_HEAD_V
        l1 = jnp.sum(ls_ref[h, 0], axis=-1, keepdims=True)
        l2 = jnp.sum(ls_ref[h, 1], axis=-1, keepdims=True)
        o = acc_ref[h, 0:tq, :] / l1 - lam * (acc_ref[h, tq:2 * tq, :] / l2)
        o = o * lax.rsqrt(jnp.mean(o * o, axis=-1, keepdims=True) + RMS_EPS)
        o = o * sg_ref[...] * (1.0 - lam_init)
        o_ref[:, c0:c0 + D_HEAD_V] = o.astype(BF16)


def _diff_attn(p, band, lam_q, lam_k, subln_g, bsz, seq, tq, lam_init):
    n_tok = bsz * seq
    nq = seq // tq
    kern = functools.partial(_diff_attn_kernel, tq=tq, lam_init=lam_init)
    return pl.pallas_call(
        kern,
        grid=(bsz, nq),
        in_specs=[
            pl.BlockSpec((tq, D_MODEL), lambda b, i: (b * nq + i, 0)),
            pl.BlockSpec((seq, D_MODEL), lambda b, i: (b, 1)),
            pl.BlockSpec((seq, D_MODEL), lambda b, i: (b, 2)),
            _const_spec((N_HEADS, tq, 2 * tq)),
            _const_spec((2, HEAD_DIM)),
            _const_spec((2, HEAD_DIM)),
            _const_spec((1, D_HEAD_V)),
        ],
        out_specs=pl.BlockSpec((tq, D_MODEL), lambda b, i: (b * nq + i, 0)),
        out_shape=jax.ShapeDtypeStruct((n_tok, D_MODEL), BF16),
        scratch_shapes=[
            pltpu.VMEM((N_HEADS, 2, nq, tq, tq), F32),
            pltpu.VMEM((N_HEADS, 2, tq, LANES), F32),
            pltpu.VMEM((N_HEADS, 2, tq, LANES), F32),
            pltpu.VMEM((N_HEADS, 2 * tq, D_HEAD_V), F32),
        ],
        compiler_params=pltpu.CompilerParams(
            dimension_semantics=("arbitrary", "arbitrary"), vmem_limit_bytes=VMEM_LIMIT),
        name="diff_attn",
    )(p, p, p, band, lam_q, lam_k, subln_g)


SLAB_ROWS = D_MODEL // (2 * LANES)


def _store_slabs(ref, row0, x_bf):
    n = x_bf.shape[0]
    bits = pltpu.bitcast(x_bf.astype(F32), jnp.uint32)
    for c in range(SLAB_ROWS):
        lo = bits[:, 2 * c * LANES:(2 * c + 1) * LANES]
        hi = bits[:, (2 * c + 1) * LANES:(2 * c + 2) * LANES]
        ref[pl.ds(SLAB_ROWS * row0 + c, n, stride=SLAB_ROWS), :] = (lo >> 16) | hi


def _load_slabs(ref, row0, n):
    parts = []
    for c in range(SLAB_ROWS):
        words = ref[pl.ds(SLAB_ROWS * row0 + c, n, stride=SLAB_ROWS), :]
        parts.append(pltpu.bitcast(words << 16, F32))
        parts.append(pltpu.bitcast(words & jnp.uint32(0xFFFF0000), F32))
    return jnp.concatenate(parts, axis=1)


def _mix_kernel(on_ref, yb_ref, ga_ref, gb_ref, xn_ref, wa_ref, wb_ref, wo_ref, bg_ref,
                g1_ref, b1_ref, wr_ref, br_ref, x1_ref, xp_ref, lg_ref, *, tm, sub):
    bg = bg_ref[...]
    for lo in range(0, tm, sub):
        rows = slice(lo, lo + sub)
        y_b = jnp.dot(yb_ref[rows, :], wb_ref[...], preferred_element_type=F32)
        y_a = jnp.dot(on_ref[rows, :], wa_ref[...], preferred_element_type=F32)
        g_a = jax.nn.sigmoid(ga_ref[rows, :].astype(F32) + bg[0:1])
        g_b = jax.nn.sigmoid(gb_ref[rows, :].astype(F32) + bg[1:2])
        merged = (g_a * y_a + g_b * y_b).astype(BF16)
        mix = jnp.dot(merged, wo_ref[...], preferred_element_type=F32)
        x1 = _layer_norm(DEEPNORM_ALPHA * xn_ref[rows, :] + mix, g1_ref[...], b1_ref[...])
        x1_ref[rows, :] = x1
        x_hi = x1.astype(BF16)
        x_lo = (x1 - x_hi.astype(F32)).astype(BF16)
        part = (jnp.dot(x_hi, wr_ref[...], preferred_element_type=F32)
                + jnp.dot(x_lo, wr_ref[...], preferred_element_type=F32))
        lg_ref[rows, :] = part[:, 0:LANES] + part[:, LANES:2 * LANES] + br_ref[...]
        _store_slabs(xp_ref, lo, x_hi)


def _mix(o_n, p, xn, wa, wb, wo, b_gate, g1, b1, wr2, br, tm, sub):
    n_tok = xn.shape[0]
    kern = functools.partial(_mix_kernel, tm=tm, sub=sub)

    def col(c):
        return pl.BlockSpec((tm, D_MODEL), lambda r, c=c: (r, c))

    return pl.pallas_call(
        kern,
        grid=(n_tok // tm,),
        in_specs=[
            pl.BlockSpec((tm, D_MODEL), lambda r: (r, 0)),
            col(P_CONV), col(P_GATE_A), col(P_GATE_B),
            pl.BlockSpec((tm, D_MODEL), lambda r: (r, 0)),
            _const_spec((D_MODEL, D_MODEL)), _const_spec((D_MODEL, D_MODEL)),
            _const_spec((D_MODEL, D_MODEL)),
            _const_spec((2, D_MODEL)), _const_spec((1, D_MODEL)), _const_spec((1, D_MODEL)),
            _const_spec((D_MODEL, 2 * LANES)), _const_spec((1, LANES)),
        ],
        out_specs=[
            pl.BlockSpec((tm, D_MODEL), lambda r: (r, 0)),
            pl.BlockSpec((tm * SLAB_ROWS, LANES), lambda r: (r, 0)),
            pl.BlockSpec((tm, LANES), lambda r: (r, 0)),
        ],
        out_shape=[
            jax.ShapeDtypeStruct((n_tok, D_MODEL), F32),
            jax.ShapeDtypeStruct((n_tok * SLAB_ROWS, LANES), jnp.uint32),
            jax.ShapeDtypeStruct((n_tok, LANES), F32),
        ],
        compiler_params=pltpu.CompilerParams(
            dimension_semantics=("arbitrary",), vmem_limit_bytes=VMEM_LIMIT),
        name="mix",
    )(o_n, p, p, p, xn, wa, wb, wo, b_gate, g1, b1, wr2, br)


ROUTE_COLS = 8


def _route_kernel(lg_ref, tri_ref, route_ref, cnt_ref, run_ref, *, tr):
    @pl.when(pl.program_id(0) == 0)
    def _():
        run_ref[...] = jnp.zeros_like(run_ref)

    logit = lg_ref[...]
    lane = lax.broadcasted_iota(jnp.int32, (tr, LANES), 1).astype(F32)
    big = float(4 * LANES)

    def first_argmax(vals, vmax):
        return jnp.min(jnp.where(vals == vmax, lane, big), axis=-1, keepdims=True)

    gl = jnp.where(lane < N_GROUPS, logit, -jnp.inf)
    gmax = jnp.max(gl, axis=-1, keepdims=True)
    gsum = jnp.sum(jnp.exp(gl - gmax), axis=-1, keepdims=True)
    g_p = 1.0 / gsum
    g_idx = first_argmax(gl, gmax)
    lo = N_GROUPS + EXPERTS_PER_GROUP * g_idx
    in_group = (lane >= lo) & (lane < lo + EXPERTS_PER_GROUP)
    sl = jnp.where(in_group, logit, -jnp.inf)
    s1 = jnp.max(sl, axis=-1, keepdims=True)
    i1 = first_argmax(sl, s1)
    sl2 = jnp.where(lane == i1, -jnp.inf, sl)
    s2 = jnp.max(sl2, axis=-1, keepdims=True)
    i2 = first_argmax(sl2, s2)
    t = jnp.exp(s2 - s1)
    w1 = g_p / (1.0 + t)
    w2 = g_p * t / (1.0 + t)
    e1 = i1 - N_GROUPS
    e2 = i2 - N_GROUPS

    oh1 = lane == e1
    oh2 = lane == e2
    onehot = jnp.where(oh1 | oh2, 1.0, 0.0)
    before = jnp.dot(tri_ref[...], onehot.astype(BF16), preferred_element_type=F32) + run_ref[...]
    rank1 = jnp.sum(jnp.where(oh1, before, 0.0), axis=-1, keepdims=True)
    rank2 = jnp.sum(jnp.where(oh2, before, 0.0), axis=-1, keepdims=True)
    run_ref[...] = run_ref[...] + jnp.sum(onehot, axis=0, keepdims=True)
    cnt_ref[...] = jnp.broadcast_to(run_ref[...], cnt_ref.shape)

    route = jnp.where(lane == 0, e1, 0.0)
    for idx, val in enumerate((e2, w1, w2, rank1, rank2), start=1):
        route = jnp.where(lane == idx, val, route)
    route_ref[...] = route[:, 0:ROUTE_COLS]


def _route(logits, tr):
    n_tok = logits.shape[0]
    tri = jnp.asarray(np.tril(np.ones((tr, tr), np.float32), k=-1), BF16)
    return pl.pallas_call(
        functools.partial(_route_kernel, tr=tr),
        grid=(n_tok // tr,),
        in_specs=[pl.BlockSpec((tr, LANES), lambda r: (r, 0)), _const_spec((tr, tr))],
        out_specs=[
            pl.BlockSpec((tr, ROUTE_COLS), lambda r: (r, 0)),
            pl.BlockSpec((8, LANES), lambda r: (0, 0)),
        ],
        out_shape=[
            jax.ShapeDtypeStruct((n_tok, ROUTE_COLS), F32),
            jax.ShapeDtypeStruct((8, LANES), F32),
        ],
        scratch_shapes=[pltpu.VMEM((1, LANES), F32)],
        compiler_params=pltpu.CompilerParams(
            dimension_semantics=("arbitrary",), vmem_limit_bytes=VMEM_LIMIT),
        name="route",
    )(logits, tri)


def _slab_copy(src, src_row, dst, dst_row, sem):
    return pltpu.make_async_copy(src.at[src_row], dst.at[dst_row], sem)


N_ZERO_BLOCKS = 2 * N_EXPERTS


def _dispatch_kernel(zb_ref, dest_ref, x_ref, out_ref, zeros, sem, zsem, *, td):
    @pl.when(pl.program_id(0) == 0)
    def _():
        zeros[...] = jnp.zeros_like(zeros)

        def block_copy(n):
            row0 = pl.multiple_of(zb_ref[n] * MOE_BLOCK, MOE_BLOCK)
            return pltpu.make_async_copy(zeros, out_ref.at[pl.ds(row0, MOE_BLOCK)], zsem)

        n_zero = zb_ref[N_ZERO_BLOCKS]

        def start(n, carry):
            block_copy(n).start()
            return carry

        def wait(n, carry):
            block_copy(n).wait()
            return carry

        lax.fori_loop(0, n_zero, start, 0)
        lax.fori_loop(0, n_zero, wait, 0)

    def issue(t, carry):
        for k in range(TOP_K):
            _slab_copy(x_ref, t, out_ref, dest_ref[0, 0, k * td + t], sem).start(priority=k)
        return carry

    lax.fori_loop(0, td, issue, 0, unroll=8)
    for k in range(TOP_K):
        pltpu.make_async_copy(x_ref, out_ref.at[pl.ds(0, td)], sem).wait()


def _dispatch(xp, dest3, zero_blocks, n_rows, td):
    n_tok = xp.shape[0]
    grid_spec = pltpu.PrefetchScalarGridSpec(
        num_scalar_prefetch=1,
        grid=(n_tok // td,),
        in_specs=[
            pl.BlockSpec((1, 1, TOP_K * td), lambda s, zb: (s, 0, 0), memory_space=pltpu.SMEM),
            pl.BlockSpec((td, SLAB_ROWS, LANES), lambda s, zb: (s, 0, 0)),
        ],
        out_specs=pl.BlockSpec(memory_space=pl.ANY),
        scratch_shapes=[
            pltpu.VMEM((MOE_BLOCK, SLAB_ROWS, LANES), jnp.uint32),
            pltpu.SemaphoreType.DMA(()),
            pltpu.SemaphoreType.DMA(()),
        ],
    )
    return pl.pallas_call(
        functools.partial(_dispatch_kernel, td=td),
        grid_spec=grid_spec,
        out_shape=jax.ShapeDtypeStruct((n_rows, SLAB_ROWS, LANES), jnp.uint32),
        compiler_params=pltpu.CompilerParams(dimension_semantics=("arbitrary",)),
        name="dispatch",
    )(zero_blocks, dest3, xp)


FFN_SUB = 256


def _expert_ffn_kernel(be_ref, na_ref, x_ref, wg_ref, wu_ref, wd_ref, y_ref,
                       wg_bf, wu_bf, wd_bf):
    i = pl.program_id(0)
    active = i < na_ref[0]
    new_expert = (i == 0) | (be_ref[i] != be_ref[jnp.maximum(i - 1, 0)])

    @pl.when(active & new_expert)
    def _():
        wg_bf[...] = wg_ref[...].astype(BF16)
        wu_bf[...] = wu_ref[...].astype(BF16)
        wd_bf[...] = wd_ref[...].astype(BF16)

    @pl.when(active)
    def _():
        for lo in range(0, MOE_BLOCK, FFN_SUB):
            x = _load_slabs(x_ref, lo, FFN_SUB).astype(BF16)
            g = jnp.dot(x, wg_bf[...], preferred_element_type=F32)
            u = jnp.dot(x, wu_bf[...], preferred_element_type=F32)
            hid = (jax.nn.silu(g) * u).astype(BF16)
            y = jnp.dot(hid, wd_bf[...], preferred_element_type=F32)
            _store_slabs(y_ref, lo, y.astype(BF16))

    @pl.when(jnp.logical_not(active))
    def _():
        y_ref[...] = jnp.zeros_like(y_ref)


def _expert_ffn(xs, block_expert, n_active, wg, wu, wd):
    n_rows = xs.shape[0] // SLAB_ROWS
    n_blocks = n_rows // MOE_BLOCK

    def last_active(i, na):
        return jnp.maximum(jnp.minimum(i, na[0] - 1), 0)

    def blk(i, be, na):
        return (last_active(i, na), 0)

    def wmap(i, be, na):
        return (be[last_active(i, na)], 0, 0)

    grid_spec = pltpu.PrefetchScalarGridSpec(
        num_scalar_prefetch=2,
        grid=(n_blocks,),
        in_specs=[
            pl.BlockSpec((MOE_BLOCK * SLAB_ROWS, LANES), blk),
            pl.BlockSpec((None, D_MODEL, D_EXPERT), wmap),
            pl.BlockSpec((None, D_MODEL, D_EXPERT), wmap),
            pl.BlockSpec((None, D_EXPERT, D_MODEL), wmap),
        ],
        out_specs=pl.BlockSpec((MOE_BLOCK * SLAB_ROWS, LANES), lambda i, be, na: (i, 0)),
        scratch_shapes=[
            pltpu.VMEM((D_MODEL, D_EXPERT), BF16),
            pltpu.VMEM((D_MODEL, D_EXPERT), BF16),
            pltpu.VMEM((D_EXPERT, D_MODEL), BF16),
        ],
    )
    return pl.pallas_call(
        _expert_ffn_kernel,
        grid_spec=grid_spec,
        out_shape=jax.ShapeDtypeStruct((n_rows * SLAB_ROWS, LANES), jnp.uint32),
        compiler_params=pltpu.CompilerParams(
            dimension_semantics=("arbitrary",), vmem_limit_bytes=VMEM_LIMIT),
        name="expert_ffn",
    )(block_expert, n_active, xs, wg, wu, wd)


def _combine_kernel(dcur_ref, dnxt_ref, x_ref, route_ref, g_ref, b_ref, y_ref, o_ref,
                    buf, sem, *, td):
    s = pl.program_id(0)
    n = pl.num_programs(0)
    slot = s % 2

    def base(slot_, k):
        return pl.multiple_of((slot_ * TOP_K + k) * td, td)

    def issue(d_ref, to_slot):
        def body(t, carry):
            for k in range(TOP_K):
                _slab_copy(y_ref, d_ref[0, 0, k * td + t], buf, base(to_slot, k) + t,
                           sem.at[to_slot]).start(priority=k)
            return carry
        lax.fori_loop(0, td, body, 0, unroll=8)

    @pl.when(s == 0)
    def _():
        issue(dcur_ref, 0)

    @pl.when(s + 1 < n)
    def _():
        issue(dnxt_ref, 1 - slot)

    for k in range(TOP_K):
        pltpu.make_async_copy(y_ref.at[pl.ds(0, td)], buf.at[pl.ds(base(slot, k), td)],
                              sem.at[slot]).wait()

    route = route_ref[...]
    buf2 = buf.reshape(2 * TOP_K * td * SLAB_ROWS, LANES)
    ffn = (route[:, 2:3] * _load_slabs(buf2, base(slot, 0), td)
           + route[:, 3:4] * _load_slabs(buf2, base(slot, 1), td))
    o_ref[...] = _layer_norm(DEEPNORM_ALPHA * x_ref[...] + ffn, g_ref[...], b_ref[...])


def _combine(ys, dest3, x1, route, g2, b2, td):
    n_tok = x1.shape[0]
    n_steps = n_tok // td
    return pl.pallas_call(
        functools.partial(_combine_kernel, td=td),
        grid=(n_steps,),
        in_specs=[
            pl.BlockSpec((1, 1, TOP_K * td), lambda s: (s, 0, 0), memory_space=pltpu.SMEM),
            pl.BlockSpec((1, 1, TOP_K * td), lambda s: (jnp.minimum(s + 1, n_steps - 1), 0, 0),
                         memory_space=pltpu.SMEM),
            pl.BlockSpec((td, D_MODEL), lambda s: (s, 0)),
            pl.BlockSpec((td, ROUTE_COLS), lambda s: (s, 0)),
            pl.BlockSpec((1, D_MODEL), lambda s: (0, 0)),
            pl.BlockSpec((1, D_MODEL), lambda s: (0, 0)),
            pl.BlockSpec(memory_space=pl.ANY),
        ],
        out_specs=pl.BlockSpec((td, D_MODEL), lambda s: (s, 0)),
        out_shape=jax.ShapeDtypeStruct((n_tok, D_MODEL), F32),
        scratch_shapes=[
            pltpu.VMEM((2 * TOP_K * td, SLAB_ROWS, LANES), jnp.uint32),
            pltpu.SemaphoreType.DMA((2,)),
        ],
        compiler_params=pltpu.CompilerParams(
            dimension_semantics=("arbitrary",), vmem_limit_bytes=VMEM_LIMIT),
        name="combine",
    )(dest3, dest3, x1, route, g2, b2, ys)


def _row_tile(n, want):
    t = min(want, n)
    while n % t:
        t //= 2
    return t


def kernel(x, ln_in_g, ln_in_b, w_in, b_gate, lambda_q, lambda_k, subln_g, rel_bias, conv_w,
           w_a_proj, w_b_proj, w_o, ln1_g, ln1_b, w_group, b_group, w_sub, b_sub,
           w_gate_e, w_up_e, w_down_e, ln2_g, ln2_b):
    bsz, seq, d = x.shape
    assert DEPTH == 1 and d == D_MODEL and w_in.shape == (DEPTH, D_MODEL, N_IN)
    n_tok = bsz * seq
    tq = _row_tile(seq, 256)
    assert tq % LANES == 0 and tq % CHUNK == 0
    tm1 = _row_tile(seq, 512)
    tm3 = _row_tile(n_tok, 512)
    sub3 = _row_tile(tm3, 256)
    tr = _row_tile(n_tok, 1024)
    td = _row_tile(n_tok, 256)
    row = lambda v: v.reshape(1, -1).astype(F32)
    lam_init = 0.8 - 0.6 * math.exp(-0.3 * 0)

    xn, p = _ln_proj(x.reshape(n_tok, d), row(ln_in_g), row(ln_in_b), w_in[0].astype(BF16),
                     conv_w[0].reshape(3, d).astype(F32), seq, tm1)

    band = _band_bias(rel_bias, tq)
    o_n = _diff_attn(p, band, lambda_q[0].astype(F32), lambda_k[0].astype(F32),
                     row(subln_g[0]), bsz, seq, tq, lam_init)

    w_r = jnp.concatenate(
        [w_group[0].astype(F32),
         jnp.transpose(w_sub[0].astype(F32), (1, 0, 2)).reshape(d, N_EXPERTS)], axis=1)
    w_r = jnp.pad(w_r, ((0, 0), (0, LANES - w_r.shape[1])))
    w_hi = w_r.astype(BF16)
    w_lo = (w_r - w_hi.astype(F32)).astype(BF16)
    w_r2 = jnp.concatenate([w_hi, w_lo], axis=1)
    b_r = jnp.concatenate([b_group[0].astype(F32), b_sub[0].astype(F32).reshape(-1)])
    b_r = jnp.pad(b_r, (0, LANES - b_r.shape[0])).reshape(1, LANES)
    x1, xp, logits = _mix(
        o_n, p, xn, w_a_proj[0].astype(BF16), w_b_proj[0].astype(BF16), w_o[0].astype(BF16),
        b_gate[0].astype(F32), row(ln1_g[0]), row(ln1_b[0]), w_r2, b_r, tm3, sub3)
    route, counts = _route(logits, tr)

    n_assign = n_tok * TOP_K
    n_blocks = -(-n_assign // MOE_BLOCK) + N_EXPERTS
    cnt = counts[0, :N_EXPERTS].astype(jnp.int32)
    padded = ((cnt + MOE_BLOCK - 1) // MOE_BLOCK) * MOE_BLOCK
    pad_end = jnp.cumsum(padded)
    pad_start = pad_end - padded
    n_active = (pad_end[-1:] // MOE_BLOCK).astype(jnp.int32)
    blk_row0 = jnp.arange(n_blocks, dtype=jnp.int32) * MOE_BLOCK
    block_expert = jnp.minimum(
        jnp.sum((pad_end[None, :] <= blk_row0[:, None]).astype(jnp.int32), axis=1),
        N_EXPERTS - 1).astype(jnp.int32)
    experts = route[:, 0:TOP_K].astype(jnp.int32)
    ranks = route[:, 4:4 + TOP_K].astype(jnp.int32)
    is_e = experts[:, :, None] == jnp.arange(N_EXPERTS, dtype=jnp.int32)
    dest = jnp.sum(jnp.where(is_e, pad_start, 0), axis=-1) + ranks
    dest3 = jnp.transpose(dest.reshape(n_tok // td, td, TOP_K), (0, 2, 1))
    dest3 = dest3.reshape(n_tok // td, 1, TOP_K * td).astype(jnp.int32)

    last_blk = jnp.maximum(pad_end // MOE_BLOCK - 1, 0)
    idle_blk = jnp.minimum(n_active[0] + jnp.arange(N_EXPERTS), n_blocks - 1)
    n_zero = N_EXPERTS + n_blocks - n_active
    zero_blocks = jnp.concatenate([last_blk, idle_blk, n_zero]).astype(jnp.int32)
    n_rows = n_blocks * MOE_BLOCK
    xs = _dispatch(xp.reshape(n_tok, SLAB_ROWS, LANES), dest3, zero_blocks, n_rows, td)
    ys = _expert_ffn(xs.reshape(n_rows * SLAB_ROWS, LANES), block_expert, n_active,
                     w_gate_e[0].astype(F32), w_up_e[0].astype(F32), w_down_e[0].astype(F32))
    out = _combine(ys.reshape(n_rows, SLAB_ROWS, LANES), dest3, x1, route,
                   row(ln2_g[0]), row(ln2_b[0]), td)
    return out.reshape(bsz, seq, d)
```

```python
import functools
import math

import numpy as np
import jax
import jax.numpy as jnp
from jax import lax
from jax.experimental import pallas as pl
from jax.experimental.pallas import tpu as pltpu

F32 = jnp.float32
BF16 = jnp.bfloat16

D_MODEL = 1024
N_HEADS = 4
HEAD_DIM = 128
D_HEAD_V = 2 * HEAD_DIM
CHUNK = 64
N_BUCKETS = 32
MAX_DISTANCE = 128
N_GROUPS = 4
EXPERTS_PER_GROUP = 8
N_EXPERTS = N_GROUPS * EXPERTS_PER_GROUP
TOP_K = 2
D_EXPERT = 512
MOE_BLOCK = 512
LN_EPS = 1e-5
RMS_EPS = 1e-6
DEPTH = 1
DEEPNORM_ALPHA = (2.0 * DEPTH) ** 0.25
N_IN = 8 * D_MODEL
LANES = 128
NEG_BIG = -1e30
LOG2_E = math.log2(math.e)

VMEM_LIMIT = 56 * 1024 * 1024


def _layer_norm(x, g, b):
    mu = jnp.mean(x, axis=-1, keepdims=True)
    xc = x - mu
    var = jnp.mean(xc * xc, axis=-1, keepdims=True)
    return xc * lax.rsqrt(var + LN_EPS) * g + b


def _const_spec(shape):
    return pl.BlockSpec(shape, lambda *_: (0,) * len(shape), pipeline_mode=pl.Buffered(1))


HALO = 16
P_Q, P_K, P_V, P_CONV, P_GATE_A, P_GATE_B = range(6)
N_P = 6 * D_MODEL


def _ln_proj_kernel(x_ref, g_ref, b_ref, w_ref, cw_ref, xn_ref, p_ref, zs_ref,
                    *, q_scale, tm, tiles_per_seq):
    r = pl.program_id(0)
    xn = _layer_norm(x_ref[...], g_ref[...], b_ref[...])
    xn_ref[...] = xn
    xb = xn.astype(BF16)

    def proj(c):
        return jnp.dot(xb, w_ref[:, c * D_MODEL:(c + 1) * D_MODEL], preferred_element_type=F32)

    def put(block, val):
        p_ref[:, block * D_MODEL:(block + 1) * D_MODEL] = val.astype(BF16)

    put(P_Q, proj(0) * q_scale)
    put(P_K, proj(1))
    put(P_V, proj(2))

    @pl.when((r % tiles_per_seq) == 0)
    def _():
        zs_ref[0:HALO, :] = jnp.zeros((HALO, D_MODEL), F32)

    zs_ref[HALO:HALO + tm, :] = proj(4) * proj(5)
    cw = cw_ref[...]
    zc = (cw[0:1] * zs_ref[HALO - 2:HALO - 2 + tm, :] + cw[1:2] * zs_ref[HALO - 1:HALO - 1 + tm, :]
          + cw[2:3] * zs_ref[HALO:HALO + tm, :])
    put(P_CONV, proj(3) * zc)
    zs_ref[0:HALO, :] = zs_ref[tm:tm + HALO, :]

    put(P_GATE_A, proj(6))
    put(P_GATE_B, proj(7))


def _ln_proj(x2, g, b, w_bf, conv_w, seq, tm):
    n_tok = x2.shape[0]
    kern = functools.partial(_ln_proj_kernel, q_scale=HEAD_DIM ** -0.5 * LOG2_E, tm=tm,
                             tiles_per_seq=seq // tm)
    return pl.pallas_call(
        kern,
        grid=(n_tok // tm,),
        in_specs=[
            pl.BlockSpec((tm, D_MODEL), lambda r: (r, 0)),
            _const_spec((1, D_MODEL)),
            _const_spec((1, D_MODEL)),
            _const_spec((D_MODEL, N_IN)),
            _const_spec((3, D_MODEL)),
        ],
        out_specs=[
            pl.BlockSpec((tm, D_MODEL), lambda r: (r, 0)),
            pl.BlockSpec((tm, N_P), lambda r: (r, 0)),
        ],
        out_shape=[
            jax.ShapeDtypeStruct((n_tok, D_MODEL), F32),
            jax.ShapeDtypeStruct((n_tok, N_P), BF16),
        ],
        scratch_shapes=[pltpu.VMEM((HALO + tm, D_MODEL), F32)],
        compiler_params=pltpu.CompilerParams(
            dimension_semantics=("arbitrary",), vmem_limit_bytes=VMEM_LIMIT),
        name="ln_proj",
    )(x2, g, b, w_bf, conv_w)


def _t5_bucket_np(rel):
    nb = N_BUCKETS // 2
    max_exact = nb // 2
    n = np.abs(rel)
    large = np.full(n.shape, max_exact, np.int64)
    for d in range(max_exact, MAX_DISTANCE + 1):
        val = max_exact + int(math.log(d / max_exact) / math.log(MAX_DISTANCE / max_exact)
                              * (nb - max_exact))
        large = np.where(n >= d, min(val, nb - 1), large)
    return np.where(rel > 0, nb, 0) + np.where(n < max_exact, n, large)


def _band_bias(rel_bias, tq):
    qi = np.arange(tq)[:, None]
    kj = np.arange(2 * tq)[None, :] - tq
    allowed = (kj // CHUNK) <= (qi // CHUNK)
    far_bucket = int(_t5_bucket_np(np.array([-(tq + 1)]))[0])
    rb = rel_bias.astype(F32)
    rb = ((rb - rb[far_bucket][None, :]) * LOG2_E).T
    n_rel = 3 * tq
    bucket = _t5_bucket_np(np.arange(n_rel) - (2 * tq - 1))
    onehot = jnp.asarray(bucket[:, None] == np.arange(N_BUCKETS)[None, :])
    per_rel = jnp.sum(jnp.where(onehot[None], rb[:, None, :], 0.0), axis=-1)
    skew = jnp.tile(per_rel, (1, tq))[:, :tq * (n_rel - 1)].reshape(N_HEADS, tq, n_rel - 1)
    tile = skew[:, :, tq - 1:3 * tq - 1]
    return jnp.where(jnp.asarray(allowed)[None], tile, NEG_BIG)


PV_BLOCKS = 2


def _diff_attn_kernel(q_ref, k_ref, v_ref, bias_ref, lq_ref, lk_ref, sg_ref, o_ref, s_ref,
                      *, tq, nq, lam_init):
    dots = jnp.sum(lq_ref[...] * lk_ref[...], axis=-1, keepdims=True)
    lam = jnp.exp(dots[0:1]) - jnp.exp(dots[1:2]) + lam_init
    gain = sg_ref[...] * (1.0 - lam_init)

    def lane_tiles(s):
        return [s[:, c * LANES:(c + 1) * LANES] for c in range(s.shape[1] // LANES)]

    def query_block(i):
        n_blk = i + 1
        for h in range(N_HEADS):
            c0 = h * D_HEAD_V
            row_max = []
            for m in range(2):
                c = c0 + m * HEAD_DIM
                q = q_ref[:, c:c + HEAD_DIM]
                mx = None
                for j in range(n_blk):
                    s = lax.dot_general(q, k_ref[j * tq:(j + 1) * tq, c:c + HEAD_DIM],
                                        (((1,), (1,)), ((), ())), preferred_element_type=F32)
                    if j == i:
                        s = s + bias_ref[h, :, tq:2 * tq]
                    elif j == i - 1:
                        s = s + bias_ref[h, :, 0:tq]
                    s_ref[h, m, j] = s
                    t = functools.reduce(jnp.maximum, lane_tiles(s))
                    mx = t if mx is None else jnp.maximum(mx, t)
                row_max.append(jnp.broadcast_to(jnp.max(mx, axis=-1, keepdims=True), (tq, LANES)))
            acc = None
            lsum = [None, None]
            for j0 in range(0, n_blk, PV_BLOCKS):
                nb = min(PV_BLOCKS, n_blk - j0)
                rows = []
                for m in range(2):
                    tiles = []
                    for j in range(j0, j0 + nb):
                        tiles += [jnp.exp2(x - row_max[m]) for x in lane_tiles(s_ref[h, m, j])]
                    tsum = functools.reduce(jnp.add, tiles)
                    lsum[m] = tsum if lsum[m] is None else lsum[m] + tsum
                    rows.append(jnp.concatenate(tiles, axis=1))
                p = jnp.concatenate(rows, axis=0).astype(BF16)
                part = jnp.dot(p, v_ref[j0 * tq:(j0 + nb) * tq, c0:c0 + D_HEAD_V],
                               preferred_element_type=F32)
                acc = part if acc is None else acc + part
            r1 = 1.0 / jnp.sum(lsum[0], axis=-1, keepdims=True)
            r2 = lam / jnp.sum(lsum[1], axis=-1, keepdims=True)
            o = acc[0:tq, :] * r1 - acc[tq:2 * tq, :] * r2
            o = o * lax.rsqrt(jnp.mean(o * o, axis=-1, keepdims=True) + RMS_EPS)
            o_ref[:, c0:c0 + D_HEAD_V] = (o * gain).astype(BF16)

    lax.switch(pl.program_id(1), [functools.partial(query_block, i) for i in range(nq)])


def _diff_attn(p, band, lam_q, lam_k, subln_g, bsz, seq, tq, lam_init):
    n_tok = bsz * seq
    nq = seq // tq
    kern = functools.partial(_diff_attn_kernel, tq=tq, nq=nq, lam_init=lam_init)
    return pl.pallas_call(
        kern,
        grid=(bsz, nq),
        in_specs=[
            pl.BlockSpec((tq, D_MODEL), lambda b, i: (b * nq + i, 0)),
            pl.BlockSpec((seq, D_MODEL), lambda b, i: (b, 1)),
            pl.BlockSpec((seq, D_MODEL), lambda b, i: (b, 2)),
            _const_spec((N_HEADS, tq, 2 * tq)),
            _const_spec((2, HEAD_DIM)),
            _const_spec((2, HEAD_DIM)),
            _const_spec((1, D_HEAD_V)),
        ],
        out_specs=pl.BlockSpec((tq, D_MODEL), lambda b, i: (b * nq + i, 0)),
        out_shape=jax.ShapeDtypeStruct((n_tok, D_MODEL), BF16),
        scratch_shapes=[
            pltpu.VMEM((N_HEADS, 2, nq, tq, tq), F32),
        ],
        compiler_params=pltpu.CompilerParams(
            dimension_semantics=("arbitrary", "arbitrary"), vmem_limit_bytes=VMEM_LIMIT),
        name="diff_attn",
    )(p, p, p, band, lam_q, lam_k, subln_g)


SLAB_ROWS = D_MODEL // (2 * LANES)


def _store_slabs(ref, row0, x_bf):
    n = x_bf.shape[0]
    bits = pltpu.bitcast(x_bf.astype(F32), jnp.uint32)
    for c in range(SLAB_ROWS):
        lo = bits[:, 2 * c * LANES:(2 * c + 1) * LANES]
        hi = bits[:, (2 * c + 1) * LANES:(2 * c + 2) * LANES]
        ref[pl.ds(SLAB_ROWS * row0 + c, n, stride=SLAB_ROWS), :] = (lo >> 16) | hi


def _load_slabs(ref, row0, n):
    parts = []
    for c in range(SLAB_ROWS):
        words = ref[pl.ds(SLAB_ROWS * row0 + c, n, stride=SLAB_ROWS), :]
        parts.append(pltpu.bitcast(words << 16, F32))
        parts.append(pltpu.bitcast(words & jnp.uint32(0xFFFF0000), F32))
    return jnp.concatenate(parts, axis=1)


def _mix_kernel(on_ref, yb_ref, ga_ref, gb_ref, xn_ref, wa_ref, wb_ref, wo_ref, bg_ref,
                g1_ref, b1_ref, wr_ref, br_ref, x1_ref, xp_ref, lg_ref, *, tm, sub):
    bg = bg_ref[...]
    for lo in range(0, tm, sub):
        rows = slice(lo, lo + sub)
        y_b = jnp.dot(yb_ref[rows, :], wb_ref[...], preferred_element_type=F32)
        y_a = jnp.dot(on_ref[rows, :], wa_ref[...], preferred_element_type=F32)
        g_a = jax.nn.sigmoid(ga_ref[rows, :].astype(F32) + bg[0:1])
        g_b = jax.nn.sigmoid(gb_ref[rows, :].astype(F32) + bg[1:2])
        merged = (g_a * y_a + g_b * y_b).astype(BF16)
        mix = jnp.dot(merged, wo_ref[...], preferred_element_type=F32)
        x1 = _layer_norm(DEEPNORM_ALPHA * xn_ref[rows, :] + mix, g1_ref[...], b1_ref[...])
        x1_ref[rows, :] = x1
        x_hi = x1.astype(BF16)
        x_lo = (x1 - x_hi.astype(F32)).astype(BF16)
        part = (jnp.dot(x_hi, wr_ref[...], preferred_element_type=F32)
                + jnp.dot(x_lo, wr_ref[...], preferred_element_type=F32))
        lg_ref[rows, :] = part[:, 0:LANES] + part[:, LANES:2 * LANES] + br_ref[...]
        _store_slabs(xp_ref, lo, x_hi)


def _mix(o_n, p, xn, wa, wb, wo, b_gate, g1, b1, wr2, br, tm, sub):
    n_tok = xn.shape[0]
    kern = functools.partial(_mix_kernel, tm=tm, sub=sub)

    def col(c):
        return pl.BlockSpec((tm, D_MODEL), lambda r, c=c: (r, c))

    return pl.pallas_call(
        kern,
        grid=(n_tok // tm,),
        in_specs=[
            pl.BlockSpec((tm, D_MODEL), lambda r: (r, 0)),
            col(P_CONV), col(P_GATE_A), col(P_GATE_B),
            pl.BlockSpec((tm, D_MODEL), lambda r: (r, 0)),
            _const_spec((D_MODEL, D_MODEL)), _const_spec((D_MODEL, D_MODEL)),
            _const_spec((D_MODEL, D_MODEL)),
            _const_spec((2, D_MODEL)), _const_spec((1, D_MODEL)), _const_spec((1, D_MODEL)),
            _const_spec((D_MODEL, 2 * LANES)), _const_spec((1, LANES)),
        ],
        out_specs=[
            pl.BlockSpec((tm, D_MODEL), lambda r: (r, 0)),
            pl.BlockSpec((tm * SLAB_ROWS, LANES), lambda r: (r, 0)),
            pl.BlockSpec((tm, LANES), lambda r: (r, 0)),
        ],
        out_shape=[
            jax.ShapeDtypeStruct((n_tok, D_MODEL), F32),
            jax.ShapeDtypeStruct((n_tok * SLAB_ROWS, LANES), jnp.uint32),
            jax.ShapeDtypeStruct((n_tok, LANES), F32),
        ],
        compiler_params=pltpu.CompilerParams(
            dimension_semantics=("arbitrary",), vmem_limit_bytes=VMEM_LIMIT),
        name="mix",
    )(o_n, p, p, p, xn, wa, wb, wo, b_gate, g1, b1, wr2, br)


ROUTE_COLS = 8


def _route_kernel(lg_ref, tri_ref, route_ref, route_t_ref, cnt_ref, run_ref, *, tr):
    @pl.when(pl.program_id(0) == 0)
    def _():
        run_ref[...] = jnp.zeros_like(run_ref)

    logit = lg_ref[...]
    lane = lax.broadcasted_iota(jnp.int32, (tr, LANES), 1).astype(F32)
    big = float(4 * LANES)

    def first_argmax(vals, vmax):
        return jnp.min(jnp.where(vals == vmax, lane, big), axis=-1, keepdims=True)

    gl = jnp.where(lane < N_GROUPS, logit, -jnp.inf)
    gmax = jnp.max(gl, axis=-1, keepdims=True)
    gsum = jnp.sum(jnp.exp(gl - gmax), axis=-1, keepdims=True)
    g_p = 1.0 / gsum
    g_idx = first_argmax(gl, gmax)
    lo = N_GROUPS + EXPERTS_PER_GROUP * g_idx
    in_group = (lane >= lo) & (lane < lo + EXPERTS_PER_GROUP)
    sl = jnp.where(in_group, logit, -jnp.inf)
    s1 = jnp.max(sl, axis=-1, keepdims=True)
    i1 = first_argmax(sl, s1)
    sl2 = jnp.where(lane == i1, -jnp.inf, sl)
    s2 = jnp.max(sl2, axis=-1, keepdims=True)
    i2 = first_argmax(sl2, s2)
    t = jnp.exp(s2 - s1)
    w1 = g_p / (1.0 + t)
    w2 = g_p * t / (1.0 + t)
    e1 = i1 - N_GROUPS
    e2 = i2 - N_GROUPS

    oh1 = lane == e1
    oh2 = lane == e2
    onehot = jnp.where(oh1 | oh2, 1.0, 0.0)
    before = jnp.dot(tri_ref[...], onehot.astype(BF16), preferred_element_type=F32) + run_ref[...]
    rank1 = jnp.sum(jnp.where(oh1, before, 0.0), axis=-1, keepdims=True)
    rank2 = jnp.sum(jnp.where(oh2, before, 0.0), axis=-1, keepdims=True)
    run_ref[...] = run_ref[...] + jnp.sum(onehot, axis=0, keepdims=True)
    cnt_ref[...] = jnp.broadcast_to(run_ref[...], cnt_ref.shape)

    route = jnp.where(lane == 0, e1, 0.0)
    for idx, val in enumerate((e2, w1, w2, rank1, rank2), start=1):
        route = jnp.where(lane == idx, val, route)
    route_ref[...] = route[:, 0:ROUTE_COLS]
    route_t_ref[...] = jnp.transpose(route)[0:ROUTE_COLS, :]


def _route(logits, tr):
    n_tok = logits.shape[0]
    tri = jnp.asarray(np.tril(np.ones((tr, tr), np.float32), k=-1), BF16)
    return pl.pallas_call(
        functools.partial(_route_kernel, tr=tr),
        grid=(n_tok // tr,),
        in_specs=[pl.BlockSpec((tr, LANES), lambda r: (r, 0)), _const_spec((tr, tr))],
        out_specs=[
            pl.BlockSpec((tr, ROUTE_COLS), lambda r: (r, 0)),
            pl.BlockSpec((ROUTE_COLS, tr), lambda r: (0, r)),
            pl.BlockSpec((8, LANES), lambda r: (0, 0)),
        ],
        out_shape=[
            jax.ShapeDtypeStruct((n_tok, ROUTE_COLS), F32),
            jax.ShapeDtypeStruct((ROUTE_COLS, n_tok), F32),
            jax.ShapeDtypeStruct((8, LANES), F32),
        ],
        scratch_shapes=[pltpu.VMEM((1, LANES), F32)],
        compiler_params=pltpu.CompilerParams(
            dimension_semantics=("arbitrary",), vmem_limit_bytes=VMEM_LIMIT),
        name="route",
    )(logits, tri)


def _slab_copy(src, src_row, dst, dst_row, sem):
    return pltpu.make_async_copy(src.at[src_row], dst.at[dst_row], sem)


N_ZERO_BLOCKS = 2 * N_EXPERTS


def _dispatch_kernel(zb_ref, dest_ref, x_ref, out_ref, zeros, sem, zsem, *, td):
    @pl.when(pl.program_id(0) == 0)
    def _():
        zeros[...] = jnp.zeros_like(zeros)

        def block_copy(n):
            row0 = pl.multiple_of(zb_ref[n] * MOE_BLOCK, MOE_BLOCK)
            return pltpu.make_async_copy(zeros, out_ref.at[pl.ds(row0, MOE_BLOCK)], zsem)

        n_zero = zb_ref[N_ZERO_BLOCKS]

        def start(n, carry):
            block_copy(n).start()
            return carry

        def wait(n, carry):
            block_copy(n).wait()
            return carry

        lax.fori_loop(0, n_zero, start, 0)
        lax.fori_loop(0, n_zero, wait, 0)

    def issue(t, carry):
        for k in range(TOP_K):
            _slab_copy(x_ref, t, out_ref, dest_ref[0, 0, k * td + t], sem).start(priority=k)
        return carry

    lax.fori_loop(0, td, issue, 0, unroll=8)
    for k in range(TOP_K):
        pltpu.make_async_copy(x_ref, out_ref.at[pl.ds(0, td)], sem).wait()


def _dispatch(xp, dest3, zero_blocks, n_rows, td):
    n_tok = xp.shape[0]
    grid_spec = pltpu.PrefetchScalarGridSpec(
        num_scalar_prefetch=1,
        grid=(n_tok // td,),
        in_specs=[
            pl.BlockSpec((1, 1, TOP_K * td), lambda s, zb: (s, 0, 0), memory_space=pltpu.SMEM),
            pl.BlockSpec((td, SLAB_ROWS, LANES), lambda s, zb: (s, 0, 0)),
        ],
        out_specs=pl.BlockSpec(memory_space=pl.ANY),
        scratch_shapes=[
            pltpu.VMEM((MOE_BLOCK, SLAB_ROWS, LANES), jnp.uint32),
            pltpu.SemaphoreType.DMA(()),
            pltpu.SemaphoreType.DMA(()),
        ],
    )
    return pl.pallas_call(
        functools.partial(_dispatch_kernel, td=td),
        grid_spec=grid_spec,
        out_shape=jax.ShapeDtypeStruct((n_rows, SLAB_ROWS, LANES), jnp.uint32),
        compiler_params=pltpu.CompilerParams(dimension_semantics=("arbitrary",)),
        name="dispatch",
    )(zero_blocks, dest3, xp)


FFN_SUB = 256


def _expert_ffn_kernel(be_ref, na_ref, x_ref, wg_ref, wu_ref, wd_ref, y_ref,
                       wg_bf, wu_bf, wd_bf):
    i = pl.program_id(0)
    active = i < na_ref[0]
    new_expert = (i == 0) | (be_ref[i] != be_ref[jnp.maximum(i - 1, 0)])

    @pl.when(active & new_expert)
    def _():
        wg_bf[...] = wg_ref[...].astype(BF16)
        wu_bf[...] = wu_ref[...].astype(BF16)
        wd_bf[...] = wd_ref[...].astype(BF16)

    @pl.when(active)
    def _():
        for lo in range(0, MOE_BLOCK, FFN_SUB):
            x = _load_slabs(x_ref, lo, FFN_SUB).astype(BF16)
            g = jnp.dot(x, wg_bf[...], preferred_element_type=F32)
            u = jnp.dot(x, wu_bf[...], preferred_element_type=F32)
            hid = (jax.nn.silu(g) * u).astype(BF16)
            y = jnp.dot(hid, wd_bf[...], preferred_element_type=F32)
            _store_slabs(y_ref, lo, y.astype(BF16))

    @pl.when(jnp.logical_not(active))
    def _():
        y_ref[...] = jnp.zeros_like(y_ref)


def _expert_ffn(xs, block_expert, n_active, wg, wu, wd):
    n_rows = xs.shape[0] // SLAB_ROWS
    n_blocks = n_rows // MOE_BLOCK

    def last_active(i, na):
        return jnp.maximum(jnp.minimum(i, na[0] - 1), 0)

    def blk(i, be, na):
        return (last_active(i, na), 0)

    def wmap(i, be, na):
        return (be[last_active(i, na)], 0, 0)

    grid_spec = pltpu.PrefetchScalarGridSpec(
        num_scalar_prefetch=2,
        grid=(n_blocks,),
        in_specs=[
            pl.BlockSpec((MOE_BLOCK * SLAB_ROWS, LANES), blk),
            pl.BlockSpec((None, D_MODEL, D_EXPERT), wmap),
            pl.BlockSpec((None, D_MODEL, D_EXPERT), wmap),
            pl.BlockSpec((None, D_EXPERT, D_MODEL), wmap),
        ],
        out_specs=pl.BlockSpec((MOE_BLOCK * SLAB_ROWS, LANES), lambda i, be, na: (i, 0)),
        scratch_shapes=[
            pltpu.VMEM((D_MODEL, D_EXPERT), BF16),
            pltpu.VMEM((D_MODEL, D_EXPERT), BF16),
            pltpu.VMEM((D_EXPERT, D_MODEL), BF16),
        ],
    )
    return pl.pallas_call(
        _expert_ffn_kernel,
        grid_spec=grid_spec,
        out_shape=jax.ShapeDtypeStruct((n_rows * SLAB_ROWS, LANES), jnp.uint32),
        compiler_params=pltpu.CompilerParams(
            dimension_semantics=("arbitrary",), vmem_limit_bytes=VMEM_LIMIT),
        name="expert_ffn",
    )(block_expert, n_active, xs, wg, wu, wd)


def _combine_kernel(dcur_ref, dnxt_ref, x_ref, route_ref, g_ref, b_ref, y_ref, o_ref,
                    buf, sem, *, td):
    s = pl.program_id(0)
    n = pl.num_programs(0)
    slot = s % 2

    def base(slot_, k):
        return pl.multiple_of((slot_ * TOP_K + k) * td, td)

    def issue(d_ref, to_slot):
        def body(t, carry):
            for k in range(TOP_K):
                _slab_copy(y_ref, d_ref[0, 0, k * td + t], buf, base(to_slot, k) + t,
                           sem.at[to_slot]).start(priority=k)
            return carry
        lax.fori_loop(0, td, body, 0, unroll=8)

    @pl.when(s == 0)
    def _():
        issue(dcur_ref, 0)

    @pl.when(s + 1 < n)
    def _():
        issue(dnxt_ref, 1 - slot)

    for k in range(TOP_K):
        pltpu.make_async_copy(y_ref.at[pl.ds(0, td)], buf.at[pl.ds(base(slot, k), td)],
                              sem.at[slot]).wait()

    route = route_ref[...]
    buf2 = buf.reshape(2 * TOP_K * td * SLAB_ROWS, LANES)
    ffn = (route[:, 2:3] * _load_slabs(buf2, base(slot, 0), td)
           + route[:, 3:4] * _load_slabs(buf2, base(slot, 1), td))
    o_ref[...] = _layer_norm(DEEPNORM_ALPHA * x_ref[...] + ffn, g_ref[...], b_ref[...])


def _combine(ys, dest3, x1, route, g2, b2, td):
    n_tok = x1.shape[0]
    n_steps = n_tok // td
    return pl.pallas_call(
        functools.partial(_combine_kernel, td=td),
        grid=(n_steps,),
        in_specs=[
            pl.BlockSpec((1, 1, TOP_K * td), lambda s: (s, 0, 0), memory_space=pltpu.SMEM),
            pl.BlockSpec((1, 1, TOP_K * td), lambda s: (jnp.minimum(s + 1, n_steps - 1), 0, 0),
                         memory_space=pltpu.SMEM),
            pl.BlockSpec((td, D_MODEL), lambda s: (s, 0)),
            pl.BlockSpec((td, ROUTE_COLS), lambda s: (s, 0)),
            pl.BlockSpec((1, D_MODEL), lambda s: (0, 0)),
            pl.BlockSpec((1, D_MODEL), lambda s: (0, 0)),
            pl.BlockSpec(memory_space=pl.ANY),
        ],
        out_specs=pl.BlockSpec((td, D_MODEL), lambda s: (s, 0)),
        out_shape=jax.ShapeDtypeStruct((n_tok, D_MODEL), F32),
        scratch_shapes=[
            pltpu.VMEM((2 * TOP_K * td, SLAB_ROWS, LANES), jnp.uint32),
            pltpu.SemaphoreType.DMA((2,)),
        ],
        compiler_params=pltpu.CompilerParams(
            dimension_semantics=("arbitrary",), vmem_limit_bytes=VMEM_LIMIT),
        name="combine",
    )(dest3, dest3, x1, route, g2, b2, ys)


def _row_tile(n, want):
    t = min(want, n)
    while n % t:
        t //= 2
    return t


def kernel(x, ln_in_g, ln_in_b, w_in, b_gate, lambda_q, lambda_k, subln_g, rel_bias, conv_w,
           w_a_proj, w_b_proj, w_o, ln1_g, ln1_b, w_group, b_group, w_sub, b_sub,
           w_gate_e, w_up_e, w_down_e, ln2_g, ln2_b):
    bsz, seq, d = x.shape
    assert DEPTH == 1 and d == D_MODEL and w_in.shape == (DEPTH, D_MODEL, N_IN)
    n_tok = bsz * seq
    tq = _row_tile(seq, 256)
    assert tq % LANES == 0 and tq % CHUNK == 0
    tm1 = _row_tile(seq, 512)
    tm3 = _row_tile(n_tok, 512)
    sub3 = _row_tile(tm3, 256)
    tr = _row_tile(n_tok, 1024)
    td = _row_tile(n_tok, 512)
    row = lambda v: v.reshape(1, -1).astype(F32)
    lam_init = 0.8 - 0.6 * math.exp(-0.3 * 0)

    xn, p = _ln_proj(x.reshape(n_tok, d), row(ln_in_g), row(ln_in_b), w_in[0].astype(BF16),
                     conv_w[0].reshape(3, d).astype(F32), seq, tm1)

    band = _band_bias(rel_bias, tq)
    o_n = _diff_attn(p, band, lambda_q[0].astype(F32), lambda_k[0].astype(F32),
                     row(subln_g[0]), bsz, seq, tq, lam_init)

    w_r = jnp.concatenate(
        [w_group[0].astype(F32),
         jnp.transpose(w_sub[0].astype(F32), (1, 0, 2)).reshape(d, N_EXPERTS)], axis=1)
    w_r = jnp.pad(w_r, ((0, 0), (0, LANES - w_r.shape[1])))
    w_hi = w_r.astype(BF16)
    w_lo = (w_r - w_hi.astype(F32)).astype(BF16)
    w_r2 = jnp.concatenate([w_hi, w_lo], axis=1)
    b_r = jnp.concatenate([b_group[0].astype(F32), b_sub[0].astype(F32).reshape(-1)])
    b_r = jnp.pad(b_r, (0, LANES - b_r.shape[0])).reshape(1, LANES)
    x1, xp, logits = _mix(
        o_n, p, xn, w_a_proj[0].astype(BF16), w_b_proj[0].astype(BF16), w_o[0].astype(BF16),
        b_gate[0].astype(F32), row(ln1_g[0]), row(ln1_b[0]), w_r2, b_r, tm3, sub3)
    route, route_t, counts = _route(logits, tr)

    n_assign = n_tok * TOP_K
    n_blocks = -(-n_assign // MOE_BLOCK) + N_EXPERTS
    cnt = counts[0, :N_EXPERTS].astype(jnp.int32)
    padded = ((cnt + MOE_BLOCK - 1) // MOE_BLOCK) * MOE_BLOCK
    pad_end = jnp.cumsum(padded)
    pad_start = pad_end - padded
    n_active = (pad_end[-1:] // MOE_BLOCK).astype(jnp.int32)
    blk_row0 = jnp.arange(n_blocks, dtype=jnp.int32) * MOE_BLOCK
    block_expert = jnp.minimum(
        jnp.sum((pad_end[None, :] <= blk_row0[:, None]).astype(jnp.int32), axis=1),
        N_EXPERTS - 1).astype(jnp.int32)
    experts = route_t[0:TOP_K].astype(jnp.int32)
    ranks = route_t[4:4 + TOP_K].astype(jnp.int32)
    is_e = experts[None] == jnp.arange(N_EXPERTS, dtype=jnp.int32)[:, None, None]
    dest = jnp.sum(jnp.where(is_e, pad_start[:, None, None], 0), axis=0) + ranks
    dest3 = jnp.transpose(dest.reshape(TOP_K, n_tok // td, td), (1, 0, 2))
    dest3 = dest3.reshape(n_tok // td, 1, TOP_K * td).astype(jnp.int32)

    last_blk = jnp.maximum(pad_end // MOE_BLOCK - 1, 0)
    idle_blk = jnp.minimum(n_active[0] + jnp.arange(N_EXPERTS), n_blocks - 1)
    n_zero = N_EXPERTS + n_blocks - n_active
    zero_blocks = jnp.concatenate([last_blk, idle_blk, n_zero]).astype(jnp.int32)
    n_rows = n_blocks * MOE_BLOCK
    xs = _dispatch(xp.reshape(n_tok, SLAB_ROWS, LANES), dest3, zero_blocks, n_rows, td)
    ys = _expert_ffn(xs.reshape(n_rows * SLAB_ROWS, LANES), block_expert, n_active,
                     w_gate_e[0].astype(F32), w_up_e[0].astype(F32), w_down_e[0].astype(F32))
    out = _combine(ys.reshape(n_rows, SLAB_ROWS, LANES), dest3, x1, route,
                   row(ln2_g[0]), row(ln2_b[0]), td)
    return out.reshape(bsz, seq, d)
```

```python
import functools
import math

import numpy as np
import jax
import jax.numpy as jnp
from jax import lax
from jax.experimental import pallas as pl
from jax.experimental.pallas import tpu as pltpu

F32 = jnp.float32
BF16 = jnp.bfloat16

D_MODEL = 1024
N_HEADS = 4
HEAD_DIM = 128
D_HEAD_V = 2 * HEAD_DIM
CHUNK = 64
N_BUCKETS = 32
MAX_DISTANCE = 128
N_GROUPS = 4
EXPERTS_PER_GROUP = 8
N_EXPERTS = N_GROUPS * EXPERTS_PER_GROUP
TOP_K = 2
D_EXPERT = 512
MOE_BLOCK = 512
LN_EPS = 1e-5
RMS_EPS = 1e-6
DEPTH = 1
DEEPNORM_ALPHA = (2.0 * DEPTH) ** 0.25
N_IN = 8 * D_MODEL
LANES = 128
NEG_BIG = -1e30
LOG2_E = math.log2(math.e)

VMEM_LIMIT = 56 * 1024 * 1024


def _layer_norm(x, g, b):
    mu = jnp.mean(x, axis=-1, keepdims=True)
    xc = x - mu
    var = jnp.mean(xc * xc, axis=-1, keepdims=True)
    return xc * lax.rsqrt(var + LN_EPS) * g + b


def _const_spec(shape):
    return pl.BlockSpec(shape, lambda *_: (0,) * len(shape), pipeline_mode=pl.Buffered(1))


HALO = 16
P_Q, P_K, P_V, P_CONV, P_GATE_A, P_GATE_B = range(6)
N_P = 6 * D_MODEL


def _ln_proj_kernel(x_ref, g_ref, b_ref, w_ref, cw_ref, xn_ref, p_ref, zs_ref,
                    *, q_scale, tm, tiles_per_seq):
    r = pl.program_id(0)
    xn = _layer_norm(x_ref[...], g_ref[...], b_ref[...])
    xn_ref[...] = xn
    xb = xn.astype(BF16)

    def proj(c):
        return jnp.dot(xb, w_ref[:, c * D_MODEL:(c + 1) * D_MODEL], preferred_element_type=F32)

    def put(block, val):
        p_ref[:, block * D_MODEL:(block + 1) * D_MODEL] = val.astype(BF16)

    put(P_Q, proj(0) * q_scale)
    put(P_K, proj(1))
    put(P_V, proj(2))

    @pl.when((r % tiles_per_seq) == 0)
    def _():
        zs_ref[0:HALO, :] = jnp.zeros((HALO, D_MODEL), F32)

    zs_ref[HALO:HALO + tm, :] = proj(4) * proj(5)
    cw = cw_ref[...]
    zc = (cw[0:1] * zs_ref[HALO - 2:HALO - 2 + tm, :] + cw[1:2] * zs_ref[HALO - 1:HALO - 1 + tm, :]
          + cw[2:3] * zs_ref[HALO:HALO + tm, :])
    put(P_CONV, proj(3) * zc)
    zs_ref[0:HALO, :] = zs_ref[tm:tm + HALO, :]

    put(P_GATE_A, proj(6))
    put(P_GATE_B, proj(7))


def _ln_proj(x2, g, b, w_bf, conv_w, seq, tm):
    n_tok = x2.shape[0]
    kern = functools.partial(_ln_proj_kernel, q_scale=HEAD_DIM ** -0.5 * LOG2_E, tm=tm,
                             tiles_per_seq=seq // tm)
    return pl.pallas_call(
        kern,
        grid=(n_tok // tm,),
        in_specs=[
            pl.BlockSpec((tm, D_MODEL), lambda r: (r, 0)),
            _const_spec((1, D_MODEL)),
            _const_spec((1, D_MODEL)),
            _const_spec((D_MODEL, N_IN)),
            _const_spec((3, D_MODEL)),
        ],
        out_specs=[
            pl.BlockSpec((tm, D_MODEL), lambda r: (r, 0)),
            pl.BlockSpec((tm, N_P), lambda r: (r, 0)),
        ],
        out_shape=[
            jax.ShapeDtypeStruct((n_tok, D_MODEL), F32),
            jax.ShapeDtypeStruct((n_tok, N_P), BF16),
        ],
        scratch_shapes=[pltpu.VMEM((HALO + tm, D_MODEL), F32)],
        compiler_params=pltpu.CompilerParams(
            dimension_semantics=("arbitrary",), vmem_limit_bytes=VMEM_LIMIT),
        name="ln_proj",
    )(x2, g, b, w_bf, conv_w)


def _t5_bucket_np(rel):
    nb = N_BUCKETS // 2
    max_exact = nb // 2
    n = np.abs(rel)
    large = np.full(n.shape, max_exact, np.int64)
    for d in range(max_exact, MAX_DISTANCE + 1):
        val = max_exact + int(math.log(d / max_exact) / math.log(MAX_DISTANCE / max_exact)
                              * (nb - max_exact))
        large = np.where(n >= d, min(val, nb - 1), large)
    return np.where(rel > 0, nb, 0) + np.where(n < max_exact, n, large)


def _band_bias(rel_bias, tq):
    qi = np.arange(tq)[:, None]
    kj = np.arange(2 * tq)[None, :] - tq
    allowed = (kj // CHUNK) <= (qi // CHUNK)
    far_bucket = int(_t5_bucket_np(np.array([-(tq + 1)]))[0])
    rb = rel_bias.astype(F32)
    rb = ((rb - rb[far_bucket][None, :]) * LOG2_E).T
    n_rel = 3 * tq
    bucket = _t5_bucket_np(np.arange(n_rel) - (2 * tq - 1))
    onehot = jnp.asarray(bucket[:, None] == np.arange(N_BUCKETS)[None, :])
    per_rel = jnp.sum(jnp.where(onehot[None], rb[:, None, :], 0.0), axis=-1)
    skew = jnp.tile(per_rel, (1, tq))[:, :tq * (n_rel - 1)].reshape(N_HEADS, tq, n_rel - 1)
    tile = skew[:, :, tq - 1:3 * tq - 1]
    return jnp.where(jnp.asarray(allowed)[None], tile, NEG_BIG)


PV_BLOCKS = 2


def _diff_attn_kernel(q_ref, k_ref, v_ref, bias_ref, lq_ref, lk_ref, sg_ref, o_ref, s_ref,
                      *, tq, nq, qpb, lam_init):
    dots = jnp.sum(lq_ref[...] * lk_ref[...], axis=-1, keepdims=True)
    lam = jnp.exp(dots[0:1]) - jnp.exp(dots[1:2]) + lam_init
    gain = sg_ref[...] * (1.0 - lam_init)

    def lane_tiles(s):
        return [s[:, c * LANES:(c + 1) * LANES] for c in range(s.shape[1] // LANES)]

    def logits_pass(h, a, i):
        rows = slice(a * tq, (a + 1) * tq)
        row_max = []
        for m in range(2):
            c = h * D_HEAD_V + m * HEAD_DIM
            q = q_ref[rows, c:c + HEAD_DIM]
            mx = None
            for j in range(i + 1):
                s = lax.dot_general(q, k_ref[j * tq:(j + 1) * tq, c:c + HEAD_DIM],
                                    (((1,), (1,)), ((), ())), preferred_element_type=F32)
                if j == i:
                    s = s + bias_ref[h, :, tq:2 * tq]
                elif j == i - 1:
                    s = s + bias_ref[h, :, 0:tq]
                s_ref[h, a, m, j] = s
                t = functools.reduce(jnp.maximum, lane_tiles(s))
                mx = t if mx is None else jnp.maximum(mx, t)
            row_max.append(jnp.broadcast_to(jnp.max(mx, axis=-1, keepdims=True), (tq, LANES)))
        return row_max

    def values_pass(h, a, i, row_max):
        c0 = h * D_HEAD_V
        acc = None
        lsum = [None, None]
        for j0 in range(0, i + 1, PV_BLOCKS):
            nb = min(PV_BLOCKS, i + 1 - j0)
            stacked = []
            for m in range(2):
                tiles = []
                for j in range(j0, j0 + nb):
                    tiles += [jnp.exp2(x - row_max[m]) for x in lane_tiles(s_ref[h, a, m, j])]
                tsum = functools.reduce(jnp.add, tiles)
                lsum[m] = tsum if lsum[m] is None else lsum[m] + tsum
                stacked.append(jnp.concatenate(tiles, axis=1))
            p = jnp.concatenate(stacked, axis=0).astype(BF16)
            part = jnp.dot(p, v_ref[j0 * tq:(j0 + nb) * tq, c0:c0 + D_HEAD_V],
                           preferred_element_type=F32)
            acc = part if acc is None else acc + part
        r1 = 1.0 / jnp.sum(lsum[0], axis=-1, keepdims=True)
        r2 = lam / jnp.sum(lsum[1], axis=-1, keepdims=True)
        o = acc[0:tq, :] * r1 - acc[tq:2 * tq, :] * r2
        o = o * lax.rsqrt(jnp.mean(o * o, axis=-1, keepdims=True) + RMS_EPS)
        o_ref[a * tq:(a + 1) * tq, c0:c0 + D_HEAD_V] = (o * gain).astype(BF16)

    def query_tile(step):
        for h in range(N_HEADS):
            maxima = [logits_pass(h, a, step * qpb + a) for a in range(qpb)]
            for a in range(qpb):
                values_pass(h, a, step * qpb + a, maxima[a])

    lax.switch(pl.program_id(1), [functools.partial(query_tile, s) for s in range(nq // qpb)])


def _diff_attn(p, band, lam_q, lam_k, subln_g, bsz, seq, tq, qpb, lam_init):
    n_tok = bsz * seq
    nq = seq // tq
    steps = nq // qpb
    kern = functools.partial(_diff_attn_kernel, tq=tq, nq=nq, qpb=qpb, lam_init=lam_init)
    return pl.pallas_call(
        kern,
        grid=(bsz, steps),
        in_specs=[
            pl.BlockSpec((qpb * tq, D_MODEL), lambda b, i: (b * steps + i, P_Q)),
            pl.BlockSpec((seq, D_MODEL), lambda b, i: (b, P_K)),
            pl.BlockSpec((seq, D_MODEL), lambda b, i: (b, P_V)),
            _const_spec((N_HEADS, tq, 2 * tq)),
            _const_spec((2, HEAD_DIM)),
            _const_spec((2, HEAD_DIM)),
            _const_spec((1, D_HEAD_V)),
        ],
        out_specs=pl.BlockSpec((qpb * tq, D_MODEL), lambda b, i: (b * steps + i, 0)),
        out_shape=jax.ShapeDtypeStruct((n_tok, D_MODEL), BF16),
        scratch_shapes=[
            pltpu.VMEM((N_HEADS, qpb, 2, nq, tq, tq), F32),
        ],
        compiler_params=pltpu.CompilerParams(
            dimension_semantics=("arbitrary", "arbitrary"), vmem_limit_bytes=VMEM_LIMIT),
        name="diff_attn",
    )(p, p, p, band, lam_q, lam_k, subln_g)


SLAB_ROWS = D_MODEL // (2 * LANES)


def _store_slabs(ref, row0, x_bf):
    n = x_bf.shape[0]
    bits = pltpu.bitcast(x_bf.astype(F32), jnp.uint32)
    for c in range(SLAB_ROWS):
        lo = bits[:, 2 * c * LANES:(2 * c + 1) * LANES]
        hi = bits[:, (2 * c + 1) * LANES:(2 * c + 2) * LANES]
        ref[pl.ds(SLAB_ROWS * row0 + c, n, stride=SLAB_ROWS), :] = (lo >> 16) | hi


def _load_slabs(ref, row0, n):
    parts = []
    for c in range(SLAB_ROWS):
        words = ref[pl.ds(SLAB_ROWS * row0 + c, n, stride=SLAB_ROWS), :]
        parts.append(pltpu.bitcast(words << 16, F32))
        parts.append(pltpu.bitcast(words & jnp.uint32(0xFFFF0000), F32))
    return jnp.concatenate(parts, axis=1)


def _mix_kernel(on_ref, yb_ref, ga_ref, gb_ref, xn_ref, wa_ref, wb_ref, wo_ref, bg_ref,
                g1_ref, b1_ref, wr_ref, br_ref, x1_ref, xp_ref, lg_ref, *, tm, sub):
    bg = bg_ref[...]
    for lo in range(0, tm, sub):
        rows = slice(lo, lo + sub)
        y_b = jnp.dot(yb_ref[rows, :], wb_ref[...], preferred_element_type=F32)
        y_a = jnp.dot(on_ref[rows, :], wa_ref[...], preferred_element_type=F32)
        g_a = jax.nn.sigmoid(ga_ref[rows, :].astype(F32) + bg[0:1])
        g_b = jax.nn.sigmoid(gb_ref[rows, :].astype(F32) + bg[1:2])
        merged = (g_a * y_a + g_b * y_b).astype(BF16)
        mix = jnp.dot(merged, wo_ref[...], preferred_element_type=F32)
        x1 = _layer_norm(DEEPNORM_ALPHA * xn_ref[rows, :] + mix, g1_ref[...], b1_ref[...])
        x1_ref[rows, :] = x1
        x_hi = x1.astype(BF16)
        x_lo = (x1 - x_hi.astype(F32)).astype(BF16)
        part = (jnp.dot(x_hi, wr_ref[...], preferred_element_type=F32)
                + jnp.dot(x_lo, wr_ref[...], preferred_element_type=F32))
        lg_ref[rows, :] = part[:, 0:LANES] + part[:, LANES:2 * LANES] + br_ref[...]
        _store_slabs(xp_ref, lo, x_hi)


def _mix(o_n, p, xn, wa, wb, wo, b_gate, g1, b1, wr2, br, tm, sub):
    n_tok = xn.shape[0]
    kern = functools.partial(_mix_kernel, tm=tm, sub=sub)

    def col(c):
        return pl.BlockSpec((tm, D_MODEL), lambda r, c=c: (r, c))

    return pl.pallas_call(
        kern,
        grid=(n_tok // tm,),
        in_specs=[
            pl.BlockSpec((tm, D_MODEL), lambda r: (r, 0)),
            col(P_CONV), col(P_GATE_A), col(P_GATE_B),
            pl.BlockSpec((tm, D_MODEL), lambda r: (r, 0)),
            _const_spec((D_MODEL, D_MODEL)), _const_spec((D_MODEL, D_MODEL)),
            _const_spec((D_MODEL, D_MODEL)),
            _const_spec((2, D_MODEL)), _const_spec((1, D_MODEL)), _const_spec((1, D_MODEL)),
            _const_spec((D_MODEL, 2 * LANES)), _const_spec((1, LANES)),
        ],
        out_specs=[
            pl.BlockSpec((tm, D_MODEL), lambda r: (r, 0)),
            pl.BlockSpec((tm * SLAB_ROWS, LANES), lambda r: (r, 0)),
            pl.BlockSpec((tm, LANES), lambda r: (r, 0)),
        ],
        out_shape=[
            jax.ShapeDtypeStruct((n_tok, D_MODEL), F32),
            jax.ShapeDtypeStruct((n_tok * SLAB_ROWS, LANES), jnp.uint32),
            jax.ShapeDtypeStruct((n_tok, LANES), F32),
        ],
        compiler_params=pltpu.CompilerParams(
            dimension_semantics=("arbitrary",), vmem_limit_bytes=VMEM_LIMIT),
        name="mix",
    )(o_n, p, p, p, xn, wa, wb, wo, b_gate, g1, b1, wr2, br)


ROUTE_COLS = 8


def _route_kernel(lg_ref, tri_ref, route_ref, route_t_ref, cnt_ref, run_ref, *, tr):
    @pl.when(pl.program_id(0) == 0)
    def _():
        run_ref[...] = jnp.zeros_like(run_ref)

    logit = lg_ref[...]
    lane = lax.broadcasted_iota(jnp.int32, (tr, LANES), 1).astype(F32)
    big = float(4 * LANES)

    def first_argmax(vals, vmax):
        return jnp.min(jnp.where(vals == vmax, lane, big), axis=-1, keepdims=True)

    gl = jnp.where(lane < N_GROUPS, logit, -jnp.inf)
    gmax = jnp.max(gl, axis=-1, keepdims=True)
    gsum = jnp.sum(jnp.exp(gl - gmax), axis=-1, keepdims=True)
    g_p = 1.0 / gsum
    g_idx = first_argmax(gl, gmax)
    lo = N_GROUPS + EXPERTS_PER_GROUP * g_idx
    in_group = (lane >= lo) & (lane < lo + EXPERTS_PER_GROUP)
    sl = jnp.where(in_group, logit, -jnp.inf)
    s1 = jnp.max(sl, axis=-1, keepdims=True)
    i1 = first_argmax(sl, s1)
    sl2 = jnp.where(lane == i1, -jnp.inf, sl)
    s2 = jnp.max(sl2, axis=-1, keepdims=True)
    i2 = first_argmax(sl2, s2)
    t = jnp.exp(s2 - s1)
    w1 = g_p / (1.0 + t)
    w2 = g_p * t / (1.0 + t)
    e1 = i1 - N_GROUPS
    e2 = i2 - N_GROUPS

    oh1 = lane == e1
    oh2 = lane == e2
    onehot = jnp.where(oh1 | oh2, 1.0, 0.0)
    before = jnp.dot(tri_ref[...], onehot.astype(BF16), preferred_element_type=F32) + run_ref[...]
    rank1 = jnp.sum(jnp.where(oh1, before, 0.0), axis=-1, keepdims=True)
    rank2 = jnp.sum(jnp.where(oh2, before, 0.0), axis=-1, keepdims=True)
    run_ref[...] = run_ref[...] + jnp.sum(onehot, axis=0, keepdims=True)
    cnt_ref[...] = jnp.broadcast_to(run_ref[...], cnt_ref.shape)

    route = jnp.where(lane == 0, e1, 0.0)
    for idx, val in enumerate((e2, w1, w2, rank1, rank2), start=1):
        route = jnp.where(lane == idx, val, route)
    route_ref[...] = route[:, 0:ROUTE_COLS]
    route_t_ref[...] = jnp.transpose(route)[0:ROUTE_COLS, :]


def _route(logits, tr):
    n_tok = logits.shape[0]
    tri = jnp.asarray(np.tril(np.ones((tr, tr), np.float32), k=-1), BF16)
    return pl.pallas_call(
        functools.partial(_route_kernel, tr=tr),
        grid=(n_tok // tr,),
        in_specs=[pl.BlockSpec((tr, LANES), lambda r: (r, 0)), _const_spec((tr, tr))],
        out_specs=[
            pl.BlockSpec((tr, ROUTE_COLS), lambda r: (r, 0)),
            pl.BlockSpec((ROUTE_COLS, tr), lambda r: (0, r)),
            pl.BlockSpec((8, LANES), lambda r: (0, 0)),
        ],
        out_shape=[
            jax.ShapeDtypeStruct((n_tok, ROUTE_COLS), F32),
            jax.ShapeDtypeStruct((ROUTE_COLS, n_tok), F32),
            jax.ShapeDtypeStruct((8, LANES), F32),
        ],
        scratch_shapes=[pltpu.VMEM((1, LANES), F32)],
        compiler_params=pltpu.CompilerParams(
            dimension_semantics=("arbitrary",), vmem_limit_bytes=VMEM_LIMIT),
        name="route",
    )(logits, tri)


def _slab_copy(src, src_row, dst, dst_row, sem):
    return pltpu.make_async_copy(src.at[src_row], dst.at[dst_row], sem)


N_ZERO_BLOCKS = 2 * N_EXPERTS


def _dispatch_kernel(zb_ref, dest_ref, x_ref, out_ref, zeros, sem, zsem, *, td):
    @pl.when(pl.program_id(0) == 0)
    def _():
        zeros[...] = jnp.zeros_like(zeros)

        def block_copy(n):
            row0 = pl.multiple_of(zb_ref[n] * MOE_BLOCK, MOE_BLOCK)
            return pltpu.make_async_copy(zeros, out_ref.at[pl.ds(row0, MOE_BLOCK)], zsem)

        n_zero = zb_ref[N_ZERO_BLOCKS]

        def start(n, carry):
            block_copy(n).start()
            return carry

        def wait(n, carry):
            block_copy(n).wait()
            return carry

        lax.fori_loop(0, n_zero, start, 0)
        lax.fori_loop(0, n_zero, wait, 0)

    def issue(t, carry):
        for k in range(TOP_K):
            _slab_copy(x_ref, t, out_ref, dest_ref[0, 0, k * td + t], sem).start(priority=k)
        return carry

    lax.fori_loop(0, td, issue, 0, unroll=8)
    for k in range(TOP_K):
        pltpu.make_async_copy(x_ref, out_ref.at[pl.ds(0, td)], sem).wait()


def _dispatch(xp, dest3, zero_blocks, n_rows, td):
    n_tok = xp.shape[0]
    grid_spec = pltpu.PrefetchScalarGridSpec(
        num_scalar_prefetch=1,
        grid=(n_tok // td,),
        in_specs=[
            pl.BlockSpec((1, 1, TOP_K * td), lambda s, zb: (s, 0, 0), memory_space=pltpu.SMEM),
            pl.BlockSpec((td, SLAB_ROWS, LANES), lambda s, zb: (s, 0, 0)),
        ],
        out_specs=pl.BlockSpec(memory_space=pl.ANY),
        scratch_shapes=[
            pltpu.VMEM((MOE_BLOCK, SLAB_ROWS, LANES), jnp.uint32),
            pltpu.SemaphoreType.DMA(()),
            pltpu.SemaphoreType.DMA(()),
        ],
    )
    return pl.pallas_call(
        functools.partial(_dispatch_kernel, td=td),
        grid_spec=grid_spec,
        out_shape=jax.ShapeDtypeStruct((n_rows, SLAB_ROWS, LANES), jnp.uint32),
        compiler_params=pltpu.CompilerParams(dimension_semantics=("arbitrary",)),
        name="dispatch",
    )(zero_blocks, dest3, xp)


FFN_SUB = 256


def _expert_ffn_kernel(be_ref, na_ref, nx_ref, x_ref, wg_hbm, wu_hbm, wd_hbm, y_ref,
                       wg_f32, wu_f32, wd_f32, wg_bf, wu_bf, wd_bf, sem):
    i = pl.program_id(0)
    active = i < na_ref[0]
    expert = be_ref[i]
    new_expert = (i == 0) | (expert != be_ref[jnp.maximum(i - 1, 0)])

    def fetch(e):
        return (pltpu.make_async_copy(wg_hbm.at[e], wg_f32, sem.at[0]),
                pltpu.make_async_copy(wu_hbm.at[e], wu_f32, sem.at[1]),
                pltpu.make_async_copy(wd_hbm.at[e], wd_f32, sem.at[2]))

    @pl.when(active & (i == 0))
    def _():
        for cp in fetch(expert):
            cp.start()

    @pl.when(active & new_expert)
    def _():
        for cp in fetch(expert):
            cp.wait()
        wg_bf[...] = wg_f32[...].astype(BF16)
        wu_bf[...] = wu_f32[...].astype(BF16)
        wd_bf[...] = wd_f32[...].astype(BF16)
        nxt = nx_ref[expert]

        @pl.when(nxt >= 0)
        def _():
            for cp in fetch(nxt):
                cp.start()

    @pl.when(active)
    def _():
        for lo in range(0, MOE_BLOCK, FFN_SUB):
            x = _load_slabs(x_ref, lo, FFN_SUB).astype(BF16)
            g = jnp.dot(x, wg_bf[...], preferred_element_type=F32)
            u = jnp.dot(x, wu_bf[...], preferred_element_type=F32)
            hid = (jax.nn.silu(g) * u).astype(BF16)
            y = jnp.dot(hid, wd_bf[...], preferred_element_type=F32)
            _store_slabs(y_ref, lo, y.astype(BF16))

    @pl.when(jnp.logical_not(active))
    def _():
        y_ref[...] = jnp.zeros_like(y_ref)


def _expert_ffn(xs, block_expert, n_active, next_expert, wg, wu, wd):
    n_rows = xs.shape[0] // SLAB_ROWS
    n_blocks = n_rows // MOE_BLOCK

    def blk(i, be, na, nx):
        return (jnp.maximum(jnp.minimum(i, na[0] - 1), 0), 0)

    grid_spec = pltpu.PrefetchScalarGridSpec(
        num_scalar_prefetch=3,
        grid=(n_blocks,),
        in_specs=[
            pl.BlockSpec((MOE_BLOCK * SLAB_ROWS, LANES), blk),
            pl.BlockSpec(memory_space=pl.ANY),
            pl.BlockSpec(memory_space=pl.ANY),
            pl.BlockSpec(memory_space=pl.ANY),
        ],
        out_specs=pl.BlockSpec((MOE_BLOCK * SLAB_ROWS, LANES), lambda i, be, na, nx: (i, 0)),
        scratch_shapes=[
            pltpu.VMEM((D_MODEL, D_EXPERT), F32),
            pltpu.VMEM((D_MODEL, D_EXPERT), F32),
            pltpu.VMEM((D_EXPERT, D_MODEL), F32),
            pltpu.VMEM((D_MODEL, D_EXPERT), BF16),
            pltpu.VMEM((D_MODEL, D_EXPERT), BF16),
            pltpu.VMEM((D_EXPERT, D_MODEL), BF16),
            pltpu.SemaphoreType.DMA((3,)),
        ],
    )
    return pl.pallas_call(
        _expert_ffn_kernel,
        grid_spec=grid_spec,
        out_shape=jax.ShapeDtypeStruct((n_rows * SLAB_ROWS, LANES), jnp.uint32),
        compiler_params=pltpu.CompilerParams(
            dimension_semantics=("arbitrary",), vmem_limit_bytes=VMEM_LIMIT),
        name="expert_ffn",
    )(block_expert, n_active, next_expert, xs, wg, wu, wd)


def _combine_kernel(dcur_ref, dnxt_ref, x_ref, route_ref, g_ref, b_ref, y_ref, o_ref,
                    buf, sem, *, td):
    s = pl.program_id(0)
    n = pl.num_programs(0)
    slot = s % 2

    def base(slot_, k):
        return pl.multiple_of((slot_ * TOP_K + k) * td, td)

    def issue(d_ref, to_slot):
        def body(t, carry):
            for k in range(TOP_K):
                _slab_copy(y_ref, d_ref[0, 0, k * td + t], buf, base(to_slot, k) + t,
                           sem.at[to_slot]).start(priority=k)
            return carry
        lax.fori_loop(0, td, body, 0, unroll=8)

    @pl.when(s == 0)
    def _():
        issue(dcur_ref, 0)

    @pl.when(s + 1 < n)
    def _():
        issue(dnxt_ref, 1 - slot)

    for k in range(TOP_K):
        pltpu.make_async_copy(y_ref.at[pl.ds(0, td)], buf.at[pl.ds(base(slot, k), td)],
                              sem.at[slot]).wait()

    route = route_ref[...]
    buf2 = buf.reshape(2 * TOP_K * td * SLAB_ROWS, LANES)
    ffn = (route[:, 2:3] * _load_slabs(buf2, base(slot, 0), td)
           + route[:, 3:4] * _load_slabs(buf2, base(slot, 1), td))
    o_ref[...] = _layer_norm(DEEPNORM_ALPHA * x_ref[...] + ffn, g_ref[...], b_ref[...])


def _combine(ys, dest3, x1, route, g2, b2, td):
    n_tok = x1.shape[0]
    n_steps = n_tok // td
    return pl.pallas_call(
        functools.partial(_combine_kernel, td=td),
        grid=(n_steps,),
        in_specs=[
            pl.BlockSpec((1, 1, TOP_K * td), lambda s: (s, 0, 0), memory_space=pltpu.SMEM),
            pl.BlockSpec((1, 1, TOP_K * td), lambda s: (jnp.minimum(s + 1, n_steps - 1), 0, 0),
                         memory_space=pltpu.SMEM),
            pl.BlockSpec((td, D_MODEL), lambda s: (s, 0)),
            pl.BlockSpec((td, ROUTE_COLS), lambda s: (s, 0)),
            pl.BlockSpec((1, D_MODEL), lambda s: (0, 0)),
            pl.BlockSpec((1, D_MODEL), lambda s: (0, 0)),
            pl.BlockSpec(memory_space=pl.ANY),
        ],
        out_specs=pl.BlockSpec((td, D_MODEL), lambda s: (s, 0)),
        out_shape=jax.ShapeDtypeStruct((n_tok, D_MODEL), F32),
        scratch_shapes=[
            pltpu.VMEM((2 * TOP_K * td, SLAB_ROWS, LANES), jnp.uint32),
            pltpu.SemaphoreType.DMA((2,)),
        ],
        compiler_params=pltpu.CompilerParams(
            dimension_semantics=("arbitrary",), vmem_limit_bytes=VMEM_LIMIT),
        name="combine",
    )(dest3, dest3, x1, route, g2, b2, ys)


def _row_tile(n, want):
    t = min(want, n)
    while n % t:
        t //= 2
    return t


def kernel(x, ln_in_g, ln_in_b, w_in, b_gate, lambda_q, lambda_k, subln_g, rel_bias, conv_w,
           w_a_proj, w_b_proj, w_o, ln1_g, ln1_b, w_group, b_group, w_sub, b_sub,
           w_gate_e, w_up_e, w_down_e, ln2_g, ln2_b):
    bsz, seq, d = x.shape
    assert DEPTH == 1 and d == D_MODEL and w_in.shape == (DEPTH, D_MODEL, N_IN)
    n_tok = bsz * seq
    tq = _row_tile(seq, 256)
    assert tq % LANES == 0 and tq % CHUNK == 0
    qpb = 1
    tm1 = _row_tile(seq, 512)
    tm3 = _row_tile(n_tok, 512)
    sub3 = _row_tile(tm3, 256)
    tr = _row_tile(n_tok, 1024)
    td = _row_tile(n_tok, 512)
    row = lambda v: v.reshape(1, -1).astype(F32)
    lam_init = 0.8 - 0.6 * math.exp(-0.3 * 0)

    xn, p = _ln_proj(x.reshape(n_tok, d), row(ln_in_g), row(ln_in_b), w_in[0].astype(BF16),
                     conv_w[0].reshape(3, d).astype(F32), seq, tm1)

    band = _band_bias(rel_bias, tq)
    o_n = _diff_attn(p, band, lambda_q[0].astype(F32), lambda_k[0].astype(F32),
                     row(subln_g[0]), bsz, seq, tq, qpb, lam_init)

    w_r = jnp.concatenate(
        [w_group[0].astype(F32),
         jnp.transpose(w_sub[0].astype(F32), (1, 0, 2)).reshape(d, N_EXPERTS)], axis=1)
    w_r = jnp.pad(w_r, ((0, 0), (0, LANES - w_r.shape[1])))
    w_hi = w_r.astype(BF16)
    w_lo = (w_r - w_hi.astype(F32)).astype(BF16)
    w_r2 = jnp.concatenate([w_hi, w_lo], axis=1)
    b_r = jnp.concatenate([b_group[0].astype(F32), b_sub[0].astype(F32).reshape(-1)])
    b_r = jnp.pad(b_r, (0, LANES - b_r.shape[0])).reshape(1, LANES)
    x1, xp, logits = _mix(
        o_n, p, xn, w_a_proj[0].astype(BF16), w_b_proj[0].astype(BF16), w_o[0].astype(BF16),
        b_gate[0].astype(F32), row(ln1_g[0]), row(ln1_b[0]), w_r2, b_r, tm3, sub3)
    route, route_t, counts = _route(logits, tr)

    n_assign = n_tok * TOP_K
    n_blocks = -(-n_assign // MOE_BLOCK) + N_EXPERTS
    cnt = counts[0, :N_EXPERTS].astype(jnp.int32)
    padded = ((cnt + MOE_BLOCK - 1) // MOE_BLOCK) * MOE_BLOCK
    pad_end = jnp.cumsum(padded)
    pad_start = pad_end - padded
    n_active = (pad_end[-1:] // MOE_BLOCK).astype(jnp.int32)
    blk_row0 = jnp.arange(n_blocks, dtype=jnp.int32) * MOE_BLOCK
    block_expert = jnp.minimum(
        jnp.sum((pad_end[None, :] <= blk_row0[:, None]).astype(jnp.int32), axis=1),
        N_EXPERTS - 1).astype(jnp.int32)
    experts = route_t[0:TOP_K].astype(jnp.int32)
    ranks = route_t[4:4 + TOP_K].astype(jnp.int32)
    is_e = experts[None] == jnp.arange(N_EXPERTS, dtype=jnp.int32)[:, None, None]
    dest = jnp.sum(jnp.where(is_e, pad_start[:, None, None], 0), axis=0) + ranks
    dest3 = jnp.transpose(dest.reshape(TOP_K, n_tok // td, td), (1, 0, 2))
    dest3 = dest3.reshape(n_tok // td, 1, TOP_K * td).astype(jnp.int32)

    last_blk = jnp.maximum(pad_end // MOE_BLOCK - 1, 0)
    idle_blk = jnp.minimum(n_active[0] + jnp.arange(N_EXPERTS), n_blocks - 1)
    n_zero = N_EXPERTS + n_blocks - n_active
    zero_blocks = jnp.concatenate([last_blk, idle_blk, n_zero]).astype(jnp.int32)
    n_rows = n_blocks * MOE_BLOCK
    xs = _dispatch(xp.reshape(n_tok, SLAB_ROWS, LANES), dest3, zero_blocks, n_rows, td)
    e_ids = jnp.arange(N_EXPERTS, dtype=jnp.int32)
    later = (padded > 0)[None, :] & (e_ids[None, :] > e_ids[:, None])
    next_expert = jnp.min(jnp.where(later, e_ids[None, :], N_EXPERTS), axis=1)
    next_expert = jnp.where(next_expert < N_EXPERTS, next_expert, -1).astype(jnp.int32)
    ys = _expert_ffn(xs.reshape(n_rows * SLAB_ROWS, LANES), block_expert, n_active, next_expert,
                     w_gate_e[0].astype(F32), w_up_e[0].astype(F32), w_down_e[0].astype(F32))
    out = _combine(ys.reshape(n_rows, SLAB_ROWS, LANES), dest3, x1, route,
                   row(ln2_g[0]), row(ln2_b[0]), td)
    return out.reshape(bsz, seq, d)
```

```python
import functools
import math

import numpy as np
import jax
import jax.numpy as jnp
from jax import lax
from jax.experimental import pallas as pl
from jax.experimental.pallas import tpu as pltpu

F32 = jnp.float32
BF16 = jnp.bfloat16

D_MODEL = 1024
N_HEADS = 4
HEAD_DIM = 128
D_HEAD_V = 2 * HEAD_DIM
CHUNK = 64
N_BUCKETS = 32
MAX_DISTANCE = 128
N_GROUPS = 4
EXPERTS_PER_GROUP = 8
N_EXPERTS = N_GROUPS * EXPERTS_PER_GROUP
TOP_K = 2
D_EXPERT = 512
MOE_BLOCK = 512
LN_EPS = 1e-5
RMS_EPS = 1e-6
DEPTH = 1
DEEPNORM_ALPHA = (2.0 * DEPTH) ** 0.25
N_IN = 8 * D_MODEL
LANES = 128
NEG_BIG = -1e30
LOG2_E = math.log2(math.e)

VMEM_LIMIT = 56 * 1024 * 1024


def _layer_norm(x, g, b):
    mu = jnp.mean(x, axis=-1, keepdims=True)
    xc = x - mu
    var = jnp.mean(xc * xc, axis=-1, keepdims=True)
    return xc * lax.rsqrt(var + LN_EPS) * g + b


def _const_spec(shape):
    return pl.BlockSpec(shape, lambda *_: (0,) * len(shape), pipeline_mode=pl.Buffered(1))


HALO = 16
P_Q, P_K, P_V, P_CONV, P_GATE_A, P_GATE_B = range(6)
N_P = 6 * D_MODEL


def _ln_proj_kernel(x_ref, g_ref, b_ref, w_ref, cw_ref, xn_ref, p_ref, zs_ref,
                    *, q_scale, tm, tiles_per_seq):
    r = pl.program_id(0)
    xn = _layer_norm(x_ref[...], g_ref[...], b_ref[...])
    xn_ref[...] = xn
    xb = xn.astype(BF16)

    def proj(c):
        return jnp.dot(xb, w_ref[:, c * D_MODEL:(c + 1) * D_MODEL], preferred_element_type=F32)

    def put(block, val):
        p_ref[:, block * D_MODEL:(block + 1) * D_MODEL] = val.astype(BF16)

    put(P_Q, proj(0) * q_scale)
    put(P_K, proj(1))
    put(P_V, proj(2))

    @pl.when((r % tiles_per_seq) == 0)
    def _():
        zs_ref[0:HALO, :] = jnp.zeros((HALO, D_MODEL), F32)

    zs_ref[HALO:HALO + tm, :] = proj(4) * proj(5)
    cw = cw_ref[...]
    zc = (cw[0:1] * zs_ref[HALO - 2:HALO - 2 + tm, :] + cw[1:2] * zs_ref[HALO - 1:HALO - 1 + tm, :]
          + cw[2:3] * zs_ref[HALO:HALO + tm, :])
    put(P_CONV, proj(3) * zc)
    zs_ref[0:HALO, :] = zs_ref[tm:tm + HALO, :]

    put(P_GATE_A, proj(6))
    put(P_GATE_B, proj(7))


def _ln_proj(x2, g, b, w_bf, conv_w, seq, tm):
    n_tok = x2.shape[0]
    kern = functools.partial(_ln_proj_kernel, q_scale=HEAD_DIM ** -0.5 * LOG2_E, tm=tm,
                             tiles_per_seq=seq // tm)
    return pl.pallas_call(
        kern,
        grid=(n_tok // tm,),
        in_specs=[
            pl.BlockSpec((tm, D_MODEL), lambda r: (r, 0)),
            _const_spec((1, D_MODEL)),
            _const_spec((1, D_MODEL)),
            _const_spec((D_MODEL, N_IN)),
            _const_spec((3, D_MODEL)),
        ],
        out_specs=[
            pl.BlockSpec((tm, D_MODEL), lambda r: (r, 0)),
            pl.BlockSpec((tm, N_P), lambda r: (r, 0)),
        ],
        out_shape=[
            jax.ShapeDtypeStruct((n_tok, D_MODEL), F32),
            jax.ShapeDtypeStruct((n_tok, N_P), BF16),
        ],
        scratch_shapes=[pltpu.VMEM((HALO + tm, D_MODEL), F32)],
        compiler_params=pltpu.CompilerParams(
            dimension_semantics=("arbitrary",), vmem_limit_bytes=VMEM_LIMIT),
        name="ln_proj",
    )(x2, g, b, w_bf, conv_w)


def _t5_bucket_np(rel):
    nb = N_BUCKETS // 2
    max_exact = nb // 2
    n = np.abs(rel)
    large = np.full(n.shape, max_exact, np.int64)
    for d in range(max_exact, MAX_DISTANCE + 1):
        val = max_exact + int(math.log(d / max_exact) / math.log(MAX_DISTANCE / max_exact)
                              * (nb - max_exact))
        large = np.where(n >= d, min(val, nb - 1), large)
    return np.where(rel > 0, nb, 0) + np.where(n < max_exact, n, large)


def _band_bias(rel_bias, tq):
    qi = np.arange(tq)[:, None]
    kj = np.arange(2 * tq)[None, :] - tq
    allowed = (kj // CHUNK) <= (qi // CHUNK)
    far_bucket = int(_t5_bucket_np(np.array([-(tq + 1)]))[0])
    rb = rel_bias.astype(F32)
    rb = ((rb - rb[far_bucket][None, :]) * LOG2_E).T
    n_rel = 3 * tq
    bucket = _t5_bucket_np(np.arange(n_rel) - (2 * tq - 1))
    onehot = jnp.asarray(bucket[:, None] == np.arange(N_BUCKETS)[None, :])
    per_rel = jnp.sum(jnp.where(onehot[None], rb[:, None, :], 0.0), axis=-1)
    skew = jnp.tile(per_rel, (1, tq))[:, :tq * (n_rel - 1)].reshape(N_HEADS, tq, n_rel - 1)
    tile = skew[:, :, tq - 1:3 * tq - 1]
    return jnp.where(jnp.asarray(allowed)[None], tile, NEG_BIG)


QK_BLOCKS = 2
PV_BLOCKS = 2


def _diff_attn_kernel(q_ref, k_ref, v_ref, bias_ref, lq_ref, lk_ref, sg_ref, o_ref, s_ref,
                      *, tq, nq, qpb, lam_init):
    dots = jnp.sum(lq_ref[...] * lk_ref[...], axis=-1, keepdims=True)
    lam = jnp.exp(dots[0:1]) - jnp.exp(dots[1:2]) + lam_init
    gain = sg_ref[...] * (1.0 - lam_init)

    def lane_tiles(s):
        return [s[:, c * LANES:(c + 1) * LANES] for c in range(s.shape[1] // LANES)]

    def logits_pass(h, a, i):
        rows = slice(a * tq, (a + 1) * tq)
        row_max = []
        for m in range(2):
            c = h * D_HEAD_V + m * HEAD_DIM
            q = q_ref[rows, c:c + HEAD_DIM]
            mx = None
            for j0 in range(0, i + 1, QK_BLOCKS):
                nb = min(QK_BLOCKS, i + 1 - j0)
                wide = lax.dot_general(q, k_ref[j0 * tq:(j0 + nb) * tq, c:c + HEAD_DIM],
                                       (((1,), (1,)), ((), ())), preferred_element_type=F32)
                for j in range(j0, j0 + nb):
                    s = wide[:, (j - j0) * tq:(j - j0 + 1) * tq]
                    if j == i:
                        s = s + bias_ref[h, :, tq:2 * tq]
                    elif j == i - 1:
                        s = s + bias_ref[h, :, 0:tq]
                    s_ref[h, a, m, j] = s
                    t = functools.reduce(jnp.maximum, lane_tiles(s))
                    mx = t if mx is None else jnp.maximum(mx, t)
            row_max.append(jnp.broadcast_to(jnp.max(mx, axis=-1, keepdims=True), (tq, LANES)))
        return row_max

    def values_pass(h, a, i, row_max):
        c0 = h * D_HEAD_V
        acc = [None, None]
        lsum = [None, None]
        for j0 in range(0, i + 1, PV_BLOCKS):
            nb = min(PV_BLOCKS, i + 1 - j0)
            v_blk = v_ref[j0 * tq:(j0 + nb) * tq, c0:c0 + D_HEAD_V]
            for m in range(2):
                tiles = []
                for j in range(j0, j0 + nb):
                    tiles += [jnp.exp2(x - row_max[m]) for x in lane_tiles(s_ref[h, a, m, j])]
                tsum = functools.reduce(jnp.add, tiles)
                lsum[m] = tsum if lsum[m] is None else lsum[m] + tsum
                p = jnp.concatenate(tiles, axis=1).astype(BF16)
                part = jnp.dot(p, v_blk, preferred_element_type=F32)
                acc[m] = part if acc[m] is None else acc[m] + part
        r1 = 1.0 / jnp.sum(lsum[0], axis=-1, keepdims=True)
        r2 = lam / jnp.sum(lsum[1], axis=-1, keepdims=True)
        o = acc[0] * r1 - acc[1] * r2
        o = o * lax.rsqrt(jnp.mean(o * o, axis=-1, keepdims=True) + RMS_EPS)
        o_ref[a * tq:(a + 1) * tq, c0:c0 + D_HEAD_V] = (o * gain).astype(BF16)

    def query_tile(step):
        for h in range(N_HEADS):
            maxima = [logits_pass(h, a, step * qpb + a) for a in range(qpb)]
            for a in range(qpb):
                values_pass(h, a, step * qpb + a, maxima[a])

    lax.switch(pl.program_id(1), [functools.partial(query_tile, s) for s in range(nq // qpb)])


def _diff_attn(p, band, lam_q, lam_k, subln_g, bsz, seq, tq, qpb, lam_init):
    n_tok = bsz * seq
    nq = seq // tq
    steps = nq // qpb
    kern = functools.partial(_diff_attn_kernel, tq=tq, nq=nq, qpb=qpb, lam_init=lam_init)
    return pl.pallas_call(
        kern,
        grid=(bsz, steps),
        in_specs=[
            pl.BlockSpec((qpb * tq, D_MODEL), lambda b, i: (b * steps + i, P_Q)),
            pl.BlockSpec((seq, D_MODEL), lambda b, i: (b, P_K)),
            pl.BlockSpec((seq, D_MODEL), lambda b, i: (b, P_V)),
            _const_spec((N_HEADS, tq, 2 * tq)),
            _const_spec((2, HEAD_DIM)),
            _const_spec((2, HEAD_DIM)),
            _const_spec((1, D_HEAD_V)),
        ],
        out_specs=pl.BlockSpec((qpb * tq, D_MODEL), lambda b, i: (b * steps + i, 0)),
        out_shape=jax.ShapeDtypeStruct((n_tok, D_MODEL), BF16),
        scratch_shapes=[
            pltpu.VMEM((N_HEADS, qpb, 2, nq, tq, tq), F32),
        ],
        compiler_params=pltpu.CompilerParams(
            dimension_semantics=("arbitrary", "arbitrary"), vmem_limit_bytes=VMEM_LIMIT),
        name="diff_attn",
    )(p, p, p, band, lam_q, lam_k, subln_g)


SLAB_ROWS = D_MODEL // (2 * LANES)


def _store_slabs(ref, row0, x_bf):
    n = x_bf.shape[0]
    bits = pltpu.bitcast(x_bf.astype(F32), jnp.uint32)
    for c in range(SLAB_ROWS):
        lo = bits[:, 2 * c * LANES:(2 * c + 1) * LANES]
        hi = bits[:, (2 * c + 1) * LANES:(2 * c + 2) * LANES]
        ref[pl.ds(SLAB_ROWS * row0 + c, n, stride=SLAB_ROWS), :] = (lo >> 16) | hi


def _load_slabs(ref, row0, n):
    parts = []
    for c in range(SLAB_ROWS):
        words = ref[pl.ds(SLAB_ROWS * row0 + c, n, stride=SLAB_ROWS), :]
        parts.append(pltpu.bitcast(words << 16, F32))
        parts.append(pltpu.bitcast(words & jnp.uint32(0xFFFF0000), F32))
    return jnp.concatenate(parts, axis=1)


def _mix_kernel(on_ref, yb_ref, ga_ref, gb_ref, xn_ref, wa_ref, wb_ref, wo_ref, bg_ref,
                g1_ref, b1_ref, wr_ref, br_ref, x1_ref, xp_ref, lg_ref, *, tm, sub):
    bg = bg_ref[...]
    for lo in range(0, tm, sub):
        rows = slice(lo, lo + sub)
        y_b = jnp.dot(yb_ref[rows, :], wb_ref[...], preferred_element_type=F32)
        y_a = jnp.dot(on_ref[rows, :], wa_ref[...], preferred_element_type=F32)
        g_a = jax.nn.sigmoid(ga_ref[rows, :].astype(F32) + bg[0:1])
        g_b = jax.nn.sigmoid(gb_ref[rows, :].astype(F32) + bg[1:2])
        merged = (g_a * y_a + g_b * y_b).astype(BF16)
        mix = jnp.dot(merged, wo_ref[...], preferred_element_type=F32)
        x1 = _layer_norm(DEEPNORM_ALPHA * xn_ref[rows, :] + mix, g1_ref[...], b1_ref[...])
        x1_ref[rows, :] = x1
        x_hi = x1.astype(BF16)
        x_lo = (x1 - x_hi.astype(F32)).astype(BF16)
        part = (jnp.dot(x_hi, wr_ref[...], preferred_element_type=F32)
                + jnp.dot(x_lo, wr_ref[...], preferred_element_type=F32))
        lg_ref[rows, :] = part[:, 0:LANES] + part[:, LANES:2 * LANES] + br_ref[...]
        _store_slabs(xp_ref, lo, x_hi)


def _mix(o_n, p, xn, wa, wb, wo, b_gate, g1, b1, wr2, br, tm, sub):
    n_tok = xn.shape[0]
    kern = functools.partial(_mix_kernel, tm=tm, sub=sub)

    def col(c):
        return pl.BlockSpec((tm, D_MODEL), lambda r, c=c: (r, c))

    return pl.pallas_call(
        kern,
        grid=(n_tok // tm,),
        in_specs=[
            pl.BlockSpec((tm, D_MODEL), lambda r: (r, 0)),
            col(P_CONV), col(P_GATE_A), col(P_GATE_B),
            pl.BlockSpec((tm, D_MODEL), lambda r: (r, 0)),
            _const_spec((D_MODEL, D_MODEL)), _const_spec((D_MODEL, D_MODEL)),
            _const_spec((D_MODEL, D_MODEL)),
            _const_spec((2, D_MODEL)), _const_spec((1, D_MODEL)), _const_spec((1, D_MODEL)),
            _const_spec((D_MODEL, 2 * LANES)), _const_spec((1, LANES)),
        ],
        out_specs=[
            pl.BlockSpec((tm, D_MODEL), lambda r: (r, 0)),
            pl.BlockSpec((tm * SLAB_ROWS, LANES), lambda r: (r, 0)),
            pl.BlockSpec((tm, LANES), lambda r: (r, 0)),
        ],
        out_shape=[
            jax.ShapeDtypeStruct((n_tok, D_MODEL), F32),
            jax.ShapeDtypeStruct((n_tok * SLAB_ROWS, LANES), jnp.uint32),
            jax.ShapeDtypeStruct((n_tok, LANES), F32),
        ],
        compiler_params=pltpu.CompilerParams(
            dimension_semantics=("arbitrary",), vmem_limit_bytes=VMEM_LIMIT),
        name="mix",
    )(o_n, p, p, p, xn, wa, wb, wo, b_gate, g1, b1, wr2, br)


ROUTE_COLS = 8
EXPERT_LANE0 = 8


def _route_kernel(lg_ref, tri_ref, route_ref, route_t_ref, cnt_ref, run_ref, *, tr):
    @pl.when(pl.program_id(0) == 0)
    def _():
        run_ref[...] = jnp.zeros_like(run_ref)

    lt = jnp.transpose(lg_ref[...])
    row = lax.broadcasted_iota(jnp.int32, (EXPERTS_PER_GROUP, tr), 0).astype(F32)

    def first_argmax(vals, vmax):
        return jnp.min(jnp.where(vals == vmax, row, float(EXPERTS_PER_GROUP)),
                       axis=0, keepdims=True)

    gl = jnp.where(row < N_GROUPS, lt[0:EXPERTS_PER_GROUP], -jnp.inf)
    gmax = jnp.max(gl, axis=0, keepdims=True)
    gsum = jnp.sum(jnp.exp(gl - gmax), axis=0, keepdims=True)
    g_p = 1.0 / gsum
    g_idx = first_argmax(gl, gmax)

    def group_rows(g):
        lo = EXPERT_LANE0 + EXPERTS_PER_GROUP * g
        return lt[lo:lo + EXPERTS_PER_GROUP]

    sl = group_rows(N_GROUPS - 1)
    for g in reversed(range(N_GROUPS - 1)):
        sl = jnp.where(g_idx == g, group_rows(g), sl)
    s1 = jnp.max(sl, axis=0, keepdims=True)
    i1 = first_argmax(sl, s1)
    sl2 = jnp.where(row == i1, -jnp.inf, sl)
    s2 = jnp.max(sl2, axis=0, keepdims=True)
    i2 = first_argmax(sl2, s2)
    t = jnp.exp(s2 - s1)
    w1 = g_p / (1.0 + t)
    w2 = g_p * t / (1.0 + t)
    e1 = EXPERTS_PER_GROUP * g_idx + i1
    e2 = EXPERTS_PER_GROUP * g_idx + i2

    e_row = lax.broadcasted_iota(jnp.int32, (N_EXPERTS, tr), 0).astype(F32)
    oh1 = e_row == e1
    oh2 = e_row == e2
    onehot_t = jnp.concatenate([jnp.where(oh1 | oh2, 1.0, 0.0),
                                jnp.zeros((LANES - N_EXPERTS, tr), F32)], axis=0)
    onehot = jnp.transpose(onehot_t)
    before = jnp.dot(tri_ref[...], onehot.astype(BF16), preferred_element_type=F32) + run_ref[...]
    before_t = jnp.transpose(before)[0:N_EXPERTS]
    rank1 = jnp.sum(jnp.where(oh1, before_t, 0.0), axis=0, keepdims=True)
    rank2 = jnp.sum(jnp.where(oh2, before_t, 0.0), axis=0, keepdims=True)
    run_ref[...] = run_ref[...] + jnp.sum(onehot, axis=0, keepdims=True)
    cnt_ref[...] = jnp.broadcast_to(run_ref[...], cnt_ref.shape)

    zero = jnp.zeros((1, tr), F32)
    route_t = jnp.concatenate([e1, e2, w1, w2, rank1, rank2, zero, zero], axis=0)
    route_t_ref[...] = route_t
    padded = jnp.concatenate([route_t, jnp.zeros((LANES - ROUTE_COLS, tr), F32)], axis=0)
    route_ref[...] = jnp.transpose(padded)[:, 0:ROUTE_COLS]


def _route(logits, tr):
    n_tok = logits.shape[0]
    tri = jnp.asarray(np.tril(np.ones((tr, tr), np.float32), k=-1), BF16)
    return pl.pallas_call(
        functools.partial(_route_kernel, tr=tr),
        grid=(n_tok // tr,),
        in_specs=[pl.BlockSpec((tr, LANES), lambda r: (r, 0)), _const_spec((tr, tr))],
        out_specs=[
            pl.BlockSpec((tr, ROUTE_COLS), lambda r: (r, 0)),
            pl.BlockSpec((ROUTE_COLS, tr), lambda r: (0, r)),
            pl.BlockSpec((8, LANES), lambda r: (0, 0)),
        ],
        out_shape=[
            jax.ShapeDtypeStruct((n_tok, ROUTE_COLS), F32),
            jax.ShapeDtypeStruct((ROUTE_COLS, n_tok), F32),
            jax.ShapeDtypeStruct((8, LANES), F32),
        ],
        scratch_shapes=[pltpu.VMEM((1, LANES), F32)],
        compiler_params=pltpu.CompilerParams(
            dimension_semantics=("arbitrary",), vmem_limit_bytes=VMEM_LIMIT),
        name="route",
    )(logits, tri)


def _slab_copy(src, src_row, dst, dst_row, sem):
    return pltpu.make_async_copy(src.at[src_row], dst.at[dst_row], sem)


N_ZERO_BLOCKS = 2 * N_EXPERTS


def _dispatch_kernel(zb_ref, dest_ref, x_ref, out_ref, zeros, sem, zsem, *, td):
    @pl.when(pl.program_id(0) == 0)
    def _():
        zeros[...] = jnp.zeros_like(zeros)

        def block_copy(n):
            row0 = pl.multiple_of(zb_ref[n] * MOE_BLOCK, MOE_BLOCK)
            return pltpu.make_async_copy(zeros, out_ref.at[pl.ds(row0, MOE_BLOCK)], zsem)

        n_zero = zb_ref[N_ZERO_BLOCKS]

        def start(n, carry):
            block_copy(n).start()
            return carry

        def wait(n, carry):
            block_copy(n).wait()
            return carry

        lax.fori_loop(0, n_zero, start, 0)
        lax.fori_loop(0, n_zero, wait, 0)

    def issue(t, carry):
        for k in range(TOP_K):
            _slab_copy(x_ref, t, out_ref, dest_ref[0, 0, k * td + t], sem).start(priority=k)
        return carry

    lax.fori_loop(0, td, issue, 0, unroll=8)
    for k in range(TOP_K):
        pltpu.make_async_copy(x_ref, out_ref.at[pl.ds(0, td)], sem).wait()


def _dispatch(xp, dest3, zero_blocks, n_rows, td):
    n_tok = xp.shape[0]
    grid_spec = pltpu.PrefetchScalarGridSpec(
        num_scalar_prefetch=1,
        grid=(n_tok // td,),
        in_specs=[
            pl.BlockSpec((1, 1, TOP_K * td), lambda s, zb: (s, 0, 0), memory_space=pltpu.SMEM),
            pl.BlockSpec((td, SLAB_ROWS, LANES), lambda s, zb: (s, 0, 0)),
        ],
        out_specs=pl.BlockSpec(memory_space=pl.ANY),
        scratch_shapes=[
            pltpu.VMEM((MOE_BLOCK, SLAB_ROWS, LANES), jnp.uint32),
            pltpu.SemaphoreType.DMA(()),
            pltpu.SemaphoreType.DMA(()),
        ],
    )
    return pl.pallas_call(
        functools.partial(_dispatch_kernel, td=td),
        grid_spec=grid_spec,
        out_shape=jax.ShapeDtypeStruct((n_rows, SLAB_ROWS, LANES), jnp.uint32),
        compiler_params=pltpu.CompilerParams(dimension_semantics=("arbitrary",)),
        name="dispatch",
    )(zero_blocks, dest3, xp)


FFN_SUB = 256


def _expert_ffn_kernel(be_ref, na_ref, nx_ref, x_ref, wg_hbm, wu_hbm, wd_hbm, y_ref,
                       wg_f32, wu_f32, wd_f32, wg_bf, wu_bf, wd_bf, sem):
    i = pl.program_id(0)
    active = i < na_ref[0]
    expert = be_ref[i]
    new_expert = (i == 0) | (expert != be_ref[jnp.maximum(i - 1, 0)])

    def fetch(e):
        return (pltpu.make_async_copy(wg_hbm.at[e], wg_f32, sem.at[0]),
                pltpu.make_async_copy(wu_hbm.at[e], wu_f32, sem.at[1]),
                pltpu.make_async_copy(wd_hbm.at[e], wd_f32, sem.at[2]))

    @pl.when(active & (i == 0))
    def _():
        for cp in fetch(expert):
            cp.start()

    @pl.when(active & new_expert)
    def _():
        for cp in fetch(expert):
            cp.wait()
        wg_bf[...] = wg_f32[...].astype(BF16)
        wu_bf[...] = wu_f32[...].astype(BF16)
        wd_bf[...] = wd_f32[...].astype(BF16)
        nxt = nx_ref[expert]

        @pl.when(nxt >= 0)
        def _():
            for cp in fetch(nxt):
                cp.start()

    @pl.when(active)
    def _():
        for lo in range(0, MOE_BLOCK, FFN_SUB):
            x = _load_slabs(x_ref, lo, FFN_SUB).astype(BF16)
            g = jnp.dot(x, wg_bf[...], preferred_element_type=F32)
            u = jnp.dot(x, wu_bf[...], preferred_element_type=F32)
            hid = (jax.nn.silu(g) * u).astype(BF16)
            y = jnp.dot(hid, wd_bf[...], preferred_element_type=F32)
            _store_slabs(y_ref, lo, y.astype(BF16))

    @pl.when(jnp.logical_not(active))
    def _():
        y_ref[...] = jnp.zeros_like(y_ref)


def _expert_ffn(xs, block_expert, n_active, next_expert, wg, wu, wd):
    n_rows = xs.shape[0] // SLAB_ROWS
    n_blocks = n_rows // MOE_BLOCK

    def blk(i, be, na, nx):
        return (jnp.maximum(jnp.minimum(i, na[0] - 1), 0), 0)

    grid_spec = pltpu.PrefetchScalarGridSpec(
        num_scalar_prefetch=3,
        grid=(n_blocks,),
        in_specs=[
            pl.BlockSpec((MOE_BLOCK * SLAB_ROWS, LANES), blk),
            pl.BlockSpec(memory_space=pl.ANY),
            pl.BlockSpec(memory_space=pl.ANY),
            pl.BlockSpec(memory_space=pl.ANY),
        ],
        out_specs=pl.BlockSpec((MOE_BLOCK * SLAB_ROWS, LANES), lambda i, be, na, nx: (i, 0)),
        scratch_shapes=[
            pltpu.VMEM((D_MODEL, D_EXPERT), F32),
            pltpu.VMEM((D_MODEL, D_EXPERT), F32),
            pltpu.VMEM((D_EXPERT, D_MODEL), F32),
            pltpu.VMEM((D_MODEL, D_EXPERT), BF16),
            pltpu.VMEM((D_MODEL, D_EXPERT), BF16),
            pltpu.VMEM((D_EXPERT, D_MODEL), BF16),
            pltpu.SemaphoreType.DMA((3,)),
        ],
    )
    return pl.pallas_call(
        _expert_ffn_kernel,
        grid_spec=grid_spec,
        out_shape=jax.ShapeDtypeStruct((n_rows * SLAB_ROWS, LANES), jnp.uint32),
        compiler_params=pltpu.CompilerParams(
            dimension_semantics=("arbitrary",), vmem_limit_bytes=VMEM_LIMIT),
        name="expert_ffn",
    )(block_expert, n_active, next_expert, xs, wg, wu, wd)


def _combine_kernel(dcur_ref, dnxt_ref, x_ref, route_ref, g_ref, b_ref, y_ref, o_ref,
                    buf, sem, *, td):
    s = pl.program_id(0)
    n = pl.num_programs(0)
    slot = s % 2

    def base(slot_, k):
        return pl.multiple_of((slot_ * TOP_K + k) * td, td)

    def issue(d_ref, to_slot):
        def body(t, carry):
            for k in range(TOP_K):
                _slab_copy(y_ref, d_ref[0, 0, k * td + t], buf, base(to_slot, k) + t,
                           sem.at[to_slot]).start(priority=k)
            return carry
        lax.fori_loop(0, td, body, 0, unroll=8)

    @pl.when(s == 0)
    def _():
        issue(dcur_ref, 0)

    @pl.when(s + 1 < n)
    def _():
        issue(dnxt_ref, 1 - slot)

    for k in range(TOP_K):
        pltpu.make_async_copy(y_ref.at[pl.ds(0, td)], buf.at[pl.ds(base(slot, k), td)],
                              sem.at[slot]).wait()

    route = route_ref[...]
    buf2 = buf.reshape(2 * TOP_K * td * SLAB_ROWS, LANES)
    ffn = (route[:, 2:3] * _load_slabs(buf2, base(slot, 0), td)
           + route[:, 3:4] * _load_slabs(buf2, base(slot, 1), td))
    o_ref[...] = _layer_norm(DEEPNORM_ALPHA * x_ref[...] + ffn, g_ref[...], b_ref[...])


def _combine(ys, dest3, x1, route, g2, b2, td):
    n_tok = x1.shape[0]
    n_steps = n_tok // td
    return pl.pallas_call(
        functools.partial(_combine_kernel, td=td),
        grid=(n_steps,),
        in_specs=[
            pl.BlockSpec((1, 1, TOP_K * td), lambda s: (s, 0, 0), memory_space=pltpu.SMEM),
            pl.BlockSpec((1, 1, TOP_K * td), lambda s: (jnp.minimum(s + 1, n_steps - 1), 0, 0),
                         memory_space=pltpu.SMEM),
            pl.BlockSpec((td, D_MODEL), lambda s: (s, 0)),
            pl.BlockSpec((td, ROUTE_COLS), lambda s: (s, 0)),
            pl.BlockSpec((1, D_MODEL), lambda s: (0, 0)),
            pl.BlockSpec((1, D_MODEL), lambda s: (0, 0)),
            pl.BlockSpec(memory_space=pl.ANY),
        ],
        out_specs=pl.BlockSpec((td, D_MODEL), lambda s: (s, 0)),
        out_shape=jax.ShapeDtypeStruct((n_tok, D_MODEL), F32),
        scratch_shapes=[
            pltpu.VMEM((2 * TOP_K * td, SLAB_ROWS, LANES), jnp.uint32),
            pltpu.SemaphoreType.DMA((2,)),
        ],
        compiler_params=pltpu.CompilerParams(
            dimension_semantics=("arbitrary",), vmem_limit_bytes=VMEM_LIMIT),
        name="combine",
    )(dest3, dest3, x1, route, g2, b2, ys)


def _row_tile(n, want):
    t = min(want, n)
    while n % t:
        t //= 2
    return t


def kernel(x, ln_in_g, ln_in_b, w_in, b_gate, lambda_q, lambda_k, subln_g, rel_bias, conv_w,
           w_a_proj, w_b_proj, w_o, ln1_g, ln1_b, w_group, b_group, w_sub, b_sub,
           w_gate_e, w_up_e, w_down_e, ln2_g, ln2_b):
    bsz, seq, d = x.shape
    assert DEPTH == 1 and d == D_MODEL and w_in.shape == (DEPTH, D_MODEL, N_IN)
    n_tok = bsz * seq
    tq = _row_tile(seq, 256)
    assert tq % LANES == 0 and tq % CHUNK == 0
    qpb = 1
    tm1 = _row_tile(seq, 512)
    tm3 = _row_tile(n_tok, 1024)
    sub3 = _row_tile(tm3, 256)
    tr = _row_tile(n_tok, 1024)
    td = _row_tile(n_tok, 1024)
    row = lambda v: v.reshape(1, -1).astype(F32)
    lam_init = 0.8 - 0.6 * math.exp(-0.3 * 0)

    xn, p = _ln_proj(x.reshape(n_tok, d), row(ln_in_g), row(ln_in_b), w_in[0].astype(BF16),
                     conv_w[0].reshape(3, d).astype(F32), seq, tm1)

    band = _band_bias(rel_bias, tq)
    o_n = _diff_attn(p, band, lambda_q[0].astype(F32), lambda_k[0].astype(F32),
                     row(subln_g[0]), bsz, seq, tq, qpb, lam_init)

    gap = EXPERT_LANE0 - N_GROUPS
    w_r = jnp.concatenate(
        [w_group[0].astype(F32), jnp.zeros((d, gap), F32),
         jnp.transpose(w_sub[0].astype(F32), (1, 0, 2)).reshape(d, N_EXPERTS)], axis=1)
    w_r = jnp.pad(w_r, ((0, 0), (0, LANES - w_r.shape[1])))
    w_hi = w_r.astype(BF16)
    w_lo = (w_r - w_hi.astype(F32)).astype(BF16)
    w_r2 = jnp.concatenate([w_hi, w_lo], axis=1)
    b_r = jnp.concatenate([b_group[0].astype(F32), jnp.zeros((gap,), F32),
                           b_sub[0].astype(F32).reshape(-1)])
    b_r = jnp.pad(b_r, (0, LANES - b_r.shape[0])).reshape(1, LANES)
    x1, xp, logits = _mix(
        o_n, p, xn, w_a_proj[0].astype(BF16), w_b_proj[0].astype(BF16), w_o[0].astype(BF16),
        b_gate[0].astype(F32), row(ln1_g[0]), row(ln1_b[0]), w_r2, b_r, tm3, sub3)
    route, route_t, counts = _route(logits, tr)

    n_assign = n_tok * TOP_K
    n_blocks = -(-n_assign // MOE_BLOCK) + N_EXPERTS
    cnt = counts[0, :N_EXPERTS].astype(jnp.int32)
    padded = ((cnt + MOE_BLOCK - 1) // MOE_BLOCK) * MOE_BLOCK
    pad_end = jnp.cumsum(padded)
    pad_start = pad_end - padded
    n_active = (pad_end[-1:] // MOE_BLOCK).astype(jnp.int32)
    blk_row0 = jnp.arange(n_blocks, dtype=jnp.int32) * MOE_BLOCK
    block_expert = jnp.minimum(
        jnp.sum((pad_end[None, :] <= blk_row0[:, None]).astype(jnp.int32), axis=1),
        N_EXPERTS - 1).astype(jnp.int32)
    experts = route_t[0:TOP_K].astype(jnp.int32)
    ranks = route_t[4:4 + TOP_K].astype(jnp.int32)
    is_e = experts[None] == jnp.arange(N_EXPERTS, dtype=jnp.int32)[:, None, None]
    dest = jnp.sum(jnp.where(is_e, pad_start[:, None, None], 0), axis=0) + ranks
    dest3 = jnp.transpose(dest.reshape(TOP_K, n_tok // td, td), (1, 0, 2))
    dest3 = dest3.reshape(n_tok // td, 1, TOP_K * td).astype(jnp.int32)

    last_blk = jnp.maximum(pad_end // MOE_BLOCK - 1, 0)
    idle_blk = jnp.minimum(n_active[0] + jnp.arange(N_EXPERTS), n_blocks - 1)
    n_zero = N_EXPERTS + n_blocks - n_active
    zero_blocks = jnp.concatenate([last_blk, idle_blk, n_zero]).astype(jnp.int32)
    n_rows = n_blocks * MOE_BLOCK
    xs = _dispatch(xp.reshape(n_tok, SLAB_ROWS, LANES), dest3, zero_blocks, n_rows, td)
    e_ids = jnp.arange(N_EXPERTS, dtype=jnp.int32)
    later = (padded > 0)[None, :] & (e_ids[None, :] > e_ids[:, None])
    next_expert = jnp.min(jnp.where(later, e_ids[None, :], N_EXPERTS), axis=1)
    next_expert = jnp.where(next_expert < N_EXPERTS, next_expert, -1).astype(jnp.int32)
    ys = _expert_ffn(xs.reshape(n_rows * SLAB_ROWS, LANES), block_expert, n_active, next_expert,
                     w_gate_e[0].astype(F32), w_up_e[0].astype(F32), w_down_e[0].astype(F32))
    out = _combine(ys.reshape(n_rows, SLAB_ROWS, LANES), dest3, x1, route,
                   row(ln2_g[0]), row(ln2_b[0]), td)
    return out.reshape(bsz, seq, d)
```

```python
import functools
import math

import numpy as np
import jax
import jax.numpy as jnp
from jax import lax
from jax.experimental import pallas as pl
from jax.experimental.pallas import tpu as pltpu

F32 = jnp.float32
BF16 = jnp.bfloat16

D_MODEL = 1024
N_HEADS = 4
HEAD_DIM = 128
D_HEAD_V = 2 * HEAD_DIM
CHUNK = 64
N_BUCKETS = 32
MAX_DISTANCE = 128
N_GROUPS = 4
EXPERTS_PER_GROUP = 8
N_EXPERTS = N_GROUPS * EXPERTS_PER_GROUP
TOP_K = 2
D_EXPERT = 512
MOE_BLOCK = 512
LN_EPS = 1e-5
RMS_EPS = 1e-6
DEPTH = 1
DEEPNORM_ALPHA = (2.0 * DEPTH) ** 0.25
N_IN = 8 * D_MODEL
LANES = 128
NEG_BIG = -1e30
LOG2_E = math.log2(math.e)

VMEM_LIMIT = 56 * 1024 * 1024


def _layer_norm(x, g, b):
    mu = jnp.mean(x, axis=-1, keepdims=True)
    xc = x - mu
    var = jnp.mean(xc * xc, axis=-1, keepdims=True)
    return xc * lax.rsqrt(var + LN_EPS) * g + b


def _const_spec(shape):
    return pl.BlockSpec(shape, lambda *_: (0,) * len(shape), pipeline_mode=pl.Buffered(1))


HALO = 16
P_Q, P_K, P_V, P_CONV, P_GATE_A, P_GATE_B = range(6)
N_P = 6 * D_MODEL


def _ln_proj_kernel(x_ref, g_ref, b_ref, w_ref, cw_ref, xn_ref, p_ref, zs_ref,
                    *, q_scale, tm, sub, tiles_per_seq):
    r = pl.program_id(0)

    @pl.when((r % tiles_per_seq) == 0)
    def _():
        zs_ref[0:HALO, :] = jnp.zeros((HALO, D_MODEL), F32)

    cw = cw_ref[...]
    for lo in range(0, tm, sub):
        rows = slice(lo, lo + sub)
        xn = _layer_norm(x_ref[rows, :], g_ref[...], b_ref[...])
        xn_ref[rows, :] = xn
        xb = xn.astype(BF16)

        def proj(c, xb=xb):
            return jnp.dot(xb, w_ref[:, c * D_MODEL:(c + 1) * D_MODEL],
                           preferred_element_type=F32)

        def put(block, val, rows=rows):
            p_ref[rows, block * D_MODEL:(block + 1) * D_MODEL] = val.astype(BF16)

        put(P_Q, proj(0) * q_scale)
        put(P_K, proj(1))
        put(P_V, proj(2))
        z0 = HALO + lo
        zs_ref[z0:z0 + sub, :] = proj(4) * proj(5)
        zc = (cw[0:1] * zs_ref[z0 - 2:z0 - 2 + sub, :] + cw[1:2] * zs_ref[z0 - 1:z0 - 1 + sub, :]
              + cw[2:3] * zs_ref[z0:z0 + sub, :])
        put(P_CONV, proj(3) * zc)
        put(P_GATE_A, proj(6))
        put(P_GATE_B, proj(7))

    zs_ref[0:HALO, :] = zs_ref[tm:tm + HALO, :]


def _ln_proj(x2, g, b, w_bf, conv_w, seq, tm):
    n_tok = x2.shape[0]
    kern = functools.partial(_ln_proj_kernel, q_scale=HEAD_DIM ** -0.5 * LOG2_E, tm=tm,
                             sub=_row_tile(tm, 256), tiles_per_seq=seq // tm)
    return pl.pallas_call(
        kern,
        grid=(n_tok // tm,),
        in_specs=[
            pl.BlockSpec((tm, D_MODEL), lambda r: (r, 0)),
            _const_spec((1, D_MODEL)),
            _const_spec((1, D_MODEL)),
            _const_spec((D_MODEL, N_IN)),
            _const_spec((3, D_MODEL)),
        ],
        out_specs=[
            pl.BlockSpec((tm, D_MODEL), lambda r: (r, 0)),
            pl.BlockSpec((tm, N_P), lambda r: (r, 0)),
        ],
        out_shape=[
            jax.ShapeDtypeStruct((n_tok, D_MODEL), F32),
            jax.ShapeDtypeStruct((n_tok, N_P), BF16),
        ],
        scratch_shapes=[pltpu.VMEM((HALO + tm, D_MODEL), F32)],
        compiler_params=pltpu.CompilerParams(
            dimension_semantics=("arbitrary",), vmem_limit_bytes=VMEM_LIMIT),
        name="ln_proj",
    )(x2, g, b, w_bf, conv_w)


def _t5_bucket_np(rel):
    nb = N_BUCKETS // 2
    max_exact = nb // 2
    n = np.abs(rel)
    large = np.full(n.shape, max_exact, np.int64)
    for d in range(max_exact, MAX_DISTANCE + 1):
        val = max_exact + int(math.log(d / max_exact) / math.log(MAX_DISTANCE / max_exact)
                              * (nb - max_exact))
        large = np.where(n >= d, min(val, nb - 1), large)
    return np.where(rel > 0, nb, 0) + np.where(n < max_exact, n, large)


def _band_bias(rel_bias, tq):
    qi = np.arange(tq)[:, None]
    kj = np.arange(2 * tq)[None, :] - tq
    allowed = (kj // CHUNK) <= (qi // CHUNK)
    far_bucket = int(_t5_bucket_np(np.array([-(tq + 1)]))[0])
    rb = rel_bias.astype(F32)
    rb = ((rb - rb[far_bucket][None, :]) * LOG2_E).T
    n_rel = 3 * tq
    bucket = _t5_bucket_np(np.arange(n_rel) - (2 * tq - 1))
    onehot = jnp.asarray(bucket[:, None] == np.arange(N_BUCKETS)[None, :])
    per_rel = jnp.sum(jnp.where(onehot[None], rb[:, None, :], 0.0), axis=-1)
    skew = jnp.tile(per_rel, (1, tq))[:, :tq * (n_rel - 1)].reshape(N_HEADS, tq, n_rel - 1)
    tile = skew[:, :, tq - 1:3 * tq - 1]
    return jnp.where(jnp.asarray(allowed)[None], tile, NEG_BIG)


QK_BLOCKS = 2
PV_BLOCKS = 2


def _diff_attn_kernel(q_ref, k_ref, v_ref, bias_ref, lq_ref, lk_ref, sg_ref, o_ref, s_ref,
                      *, tq, nq, qpb, lam_init):
    dots = jnp.sum(lq_ref[...] * lk_ref[...], axis=-1, keepdims=True)
    lam = jnp.exp(dots[0:1]) - jnp.exp(dots[1:2]) + lam_init
    gain = sg_ref[...] * (1.0 - lam_init)

    def lane_tiles(s):
        return [s[:, c * LANES:(c + 1) * LANES] for c in range(s.shape[1] // LANES)]

    def logits_pass(h, a, i):
        rows = slice(a * tq, (a + 1) * tq)
        row_max = []
        for m in range(2):
            c = h * D_HEAD_V + m * HEAD_DIM
            q = q_ref[rows, c:c + HEAD_DIM]
            mx = None
            for j0 in range(0, i + 1, QK_BLOCKS):
                nb = min(QK_BLOCKS, i + 1 - j0)
                wide = lax.dot_general(q, k_ref[j0 * tq:(j0 + nb) * tq, c:c + HEAD_DIM],
                                       (((1,), (1,)), ((), ())), preferred_element_type=F32)
                for j in range(j0, j0 + nb):
                    s = wide[:, (j - j0) * tq:(j - j0 + 1) * tq]
                    if j == i:
                        s = s + bias_ref[h, :, tq:2 * tq]
                    elif j == i - 1:
                        s = s + bias_ref[h, :, 0:tq]
                    s_ref[h, a, m, j] = s
                    t = functools.reduce(jnp.maximum, lane_tiles(s))
                    mx = t if mx is None else jnp.maximum(mx, t)
            row_max.append(jnp.broadcast_to(jnp.max(mx, axis=-1, keepdims=True), (tq, LANES)))
        return row_max

    def values_pass(h, a, i, row_max):
        c0 = h * D_HEAD_V
        acc = [None, None]
        lsum = [None, None]
        for j0 in range(0, i + 1, PV_BLOCKS):
            nb = min(PV_BLOCKS, i + 1 - j0)
            v_blk = v_ref[j0 * tq:(j0 + nb) * tq, c0:c0 + D_HEAD_V]
            for m in range(2):
                tiles = []
                for j in range(j0, j0 + nb):
                    tiles += [jnp.exp2(x - row_max[m]) for x in lane_tiles(s_ref[h, a, m, j])]
                tsum = functools.reduce(jnp.add, tiles)
                lsum[m] = tsum if lsum[m] is None else lsum[m] + tsum
                p = jnp.concatenate(tiles, axis=1).astype(BF16)
                part = jnp.dot(p, v_blk, preferred_element_type=F32)
                acc[m] = part if acc[m] is None else acc[m] + part
        r1 = 1.0 / jnp.sum(lsum[0], axis=-1, keepdims=True)
        r2 = lam / jnp.sum(lsum[1], axis=-1, keepdims=True)
        o = acc[0] * r1 - acc[1] * r2
        o = o * lax.rsqrt(jnp.mean(o * o, axis=-1, keepdims=True) + RMS_EPS)
        o_ref[a * tq:(a + 1) * tq, c0:c0 + D_HEAD_V] = (o * gain).astype(BF16)

    def query_tile(step):
        for h in range(N_HEADS):
            maxima = [logits_pass(h, a, step * qpb + a) for a in range(qpb)]
            for a in range(qpb):
                values_pass(h, a, step * qpb + a, maxima[a])

    lax.switch(pl.program_id(1), [functools.partial(query_tile, s) for s in range(nq // qpb)])


def _diff_attn(p, band, lam_q, lam_k, subln_g, bsz, seq, tq, qpb, lam_init):
    n_tok = bsz * seq
    nq = seq // tq
    steps = nq // qpb
    kern = functools.partial(_diff_attn_kernel, tq=tq, nq=nq, qpb=qpb, lam_init=lam_init)
    return pl.pallas_call(
        kern,
        grid=(bsz, steps),
        in_specs=[
            pl.BlockSpec((qpb * tq, D_MODEL), lambda b, i: (b * steps + i, P_Q)),
            pl.BlockSpec((seq, D_MODEL), lambda b, i: (b, P_K)),
            pl.BlockSpec((seq, D_MODEL), lambda b, i: (b, P_V)),
            _const_spec((N_HEADS, tq, 2 * tq)),
            _const_spec((2, HEAD_DIM)),
            _const_spec((2, HEAD_DIM)),
            _const_spec((1, D_HEAD_V)),
        ],
        out_specs=pl.BlockSpec((qpb * tq, D_MODEL), lambda b, i: (b * steps + i, 0)),
        out_shape=jax.ShapeDtypeStruct((n_tok, D_MODEL), BF16),
        scratch_shapes=[
            pltpu.VMEM((N_HEADS, qpb, 2, nq, tq, tq), F32),
        ],
        compiler_params=pltpu.CompilerParams(
            dimension_semantics=("arbitrary", "arbitrary"), vmem_limit_bytes=VMEM_LIMIT),
        name="diff_attn",
    )(p, p, p, band, lam_q, lam_k, subln_g)


SLAB_ROWS = D_MODEL // (2 * LANES)


def _store_slabs(ref, row0, x_bf):
    n = x_bf.shape[0]
    bits = pltpu.bitcast(x_bf.astype(F32), jnp.uint32)
    for c in range(SLAB_ROWS):
        lo = bits[:, 2 * c * LANES:(2 * c + 1) * LANES]
        hi = bits[:, (2 * c + 1) * LANES:(2 * c + 2) * LANES]
        ref[pl.ds(SLAB_ROWS * row0 + c, n, stride=SLAB_ROWS), :] = (lo >> 16) | hi


def _load_slabs(ref, row0, n, every=1):
    parts = []
    for c in range(SLAB_ROWS):
        words = ref[pl.ds(SLAB_ROWS * row0 + c, n, stride=SLAB_ROWS * every), :]
        parts.append(pltpu.bitcast(words << 16, F32))
        parts.append(pltpu.bitcast(words & jnp.uint32(0xFFFF0000), F32))
    return jnp.concatenate(parts, axis=1)


def _mix_kernel(on_ref, yb_ref, ga_ref, gb_ref, xn_ref, wa_ref, wb_ref, wo_ref, bg_ref,
                g1_ref, b1_ref, wr_ref, br_ref, x1_ref, xp_ref, lg_ref, *, tm, sub):
    bg = bg_ref[...]
    for lo in range(0, tm, sub):
        rows = slice(lo, lo + sub)
        y_b = jnp.dot(yb_ref[rows, :], wb_ref[...], preferred_element_type=F32)
        y_a = jnp.dot(on_ref[rows, :], wa_ref[...], preferred_element_type=F32)
        g_a = jax.nn.sigmoid(ga_ref[rows, :].astype(F32) + bg[0:1])
        g_b = jax.nn.sigmoid(gb_ref[rows, :].astype(F32) + bg[1:2])
        merged = (g_a * y_a + g_b * y_b).astype(BF16)
        mix = jnp.dot(merged, wo_ref[...], preferred_element_type=F32)
        x1 = _layer_norm(DEEPNORM_ALPHA * xn_ref[rows, :] + mix, g1_ref[...], b1_ref[...])
        x1_ref[rows, :] = x1
        x_hi = x1.astype(BF16)
        x_lo = (x1 - x_hi.astype(F32)).astype(BF16)
        part = (jnp.dot(x_hi, wr_ref[...], preferred_element_type=F32)
                + jnp.dot(x_lo, wr_ref[...], preferred_element_type=F32))
        lg_ref[rows, :] = part[:, 0:LANES] + part[:, LANES:2 * LANES] + br_ref[...]
        _store_slabs(xp_ref, lo, x_hi)


def _mix(o_n, p, xn, wa, wb, wo, b_gate, g1, b1, wr2, br, tm, sub):
    n_tok = xn.shape[0]
    kern = functools.partial(_mix_kernel, tm=tm, sub=sub)

    def col(c):
        return pl.BlockSpec((tm, D_MODEL), lambda r, c=c: (r, c))

    return pl.pallas_call(
        kern,
        grid=(n_tok // tm,),
        in_specs=[
            pl.BlockSpec((tm, D_MODEL), lambda r: (r, 0)),
            col(P_CONV), col(P_GATE_A), col(P_GATE_B),
            pl.BlockSpec((tm, D_MODEL), lambda r: (r, 0)),
            _const_spec((D_MODEL, D_MODEL)), _const_spec((D_MODEL, D_MODEL)),
            _const_spec((D_MODEL, D_MODEL)),
            _const_spec((2, D_MODEL)), _const_spec((1, D_MODEL)), _const_spec((1, D_MODEL)),
            _const_spec((D_MODEL, 2 * LANES)), _const_spec((1, LANES)),
        ],
        out_specs=[
            pl.BlockSpec((tm, D_MODEL), lambda r: (r, 0)),
            pl.BlockSpec((tm * SLAB_ROWS, LANES), lambda r: (r, 0)),
            pl.BlockSpec((tm, LANES), lambda r: (r, 0)),
        ],
        out_shape=[
            jax.ShapeDtypeStruct((n_tok, D_MODEL), F32),
            jax.ShapeDtypeStruct((n_tok * SLAB_ROWS, LANES), jnp.uint32),
            jax.ShapeDtypeStruct((n_tok, LANES), F32),
        ],
        compiler_params=pltpu.CompilerParams(
            dimension_semantics=("arbitrary",), vmem_limit_bytes=VMEM_LIMIT),
        name="mix",
    )(o_n, p, p, p, xn, wa, wb, wo, b_gate, g1, b1, wr2, br)


ROUTE_COLS = 8
EXPERT_LANE0 = 8


def _route_kernel(lg_ref, tri_ref, route_ref, route_t_ref, cnt_ref, run_ref, *, tr):
    @pl.when(pl.program_id(0) == 0)
    def _():
        run_ref[...] = jnp.zeros_like(run_ref)

    lt = jnp.transpose(lg_ref[...])
    row = lax.broadcasted_iota(jnp.int32, (EXPERTS_PER_GROUP, tr), 0).astype(F32)

    def first_argmax(vals, vmax):
        return jnp.min(jnp.where(vals == vmax, row, float(EXPERTS_PER_GROUP)),
                       axis=0, keepdims=True)

    gl = jnp.where(row < N_GROUPS, lt[0:EXPERTS_PER_GROUP], -jnp.inf)
    gmax = jnp.max(gl, axis=0, keepdims=True)
    gsum = jnp.sum(jnp.exp(gl - gmax), axis=0, keepdims=True)
    g_p = 1.0 / gsum
    g_idx = first_argmax(gl, gmax)

    def group_rows(g):
        lo = EXPERT_LANE0 + EXPERTS_PER_GROUP * g
        return lt[lo:lo + EXPERTS_PER_GROUP]

    sl = group_rows(N_GROUPS - 1)
    for g in reversed(range(N_GROUPS - 1)):
        sl = jnp.where(g_idx == g, group_rows(g), sl)
    s1 = jnp.max(sl, axis=0, keepdims=True)
    i1 = first_argmax(sl, s1)
    sl2 = jnp.where(row == i1, -jnp.inf, sl)
    s2 = jnp.max(sl2, axis=0, keepdims=True)
    i2 = first_argmax(sl2, s2)
    t = jnp.exp(s2 - s1)
    w1 = g_p / (1.0 + t)
    w2 = g_p * t / (1.0 + t)
    e1 = EXPERTS_PER_GROUP * g_idx + i1
    e2 = EXPERTS_PER_GROUP * g_idx + i2

    e_row = lax.broadcasted_iota(jnp.int32, (N_EXPERTS, tr), 0).astype(F32)
    oh1 = e_row == e1
    oh2 = e_row == e2
    onehot_t = jnp.concatenate([jnp.where(oh1 | oh2, 1.0, 0.0),
                                jnp.zeros((LANES - N_EXPERTS, tr), F32)], axis=0)
    onehot = jnp.transpose(onehot_t)
    before = jnp.dot(tri_ref[...], onehot.astype(BF16), preferred_element_type=F32) + run_ref[...]
    before_t = jnp.transpose(before)[0:N_EXPERTS]
    rank1 = jnp.sum(jnp.where(oh1, before_t, 0.0), axis=0, keepdims=True)
    rank2 = jnp.sum(jnp.where(oh2, before_t, 0.0), axis=0, keepdims=True)
    run_ref[...] = run_ref[...] + jnp.sum(onehot, axis=0, keepdims=True)
    cnt_ref[...] = jnp.broadcast_to(run_ref[...], cnt_ref.shape)

    zero = jnp.zeros((1, tr), F32)
    route_t = jnp.concatenate([e1, e2, w1, w2, rank1, rank2, zero, zero], axis=0)
    route_t_ref[...] = route_t
    padded = jnp.concatenate([route_t, jnp.zeros((LANES - ROUTE_COLS, tr), F32)], axis=0)
    route_ref[...] = jnp.transpose(padded)[:, 0:ROUTE_COLS]


def _route(logits, tr):
    n_tok = logits.shape[0]
    tri = jnp.asarray(np.tril(np.ones((tr, tr), np.float32), k=-1), BF16)
    return pl.pallas_call(
        functools.partial(_route_kernel, tr=tr),
        grid=(n_tok // tr,),
        in_specs=[pl.BlockSpec((tr, LANES), lambda r: (r, 0)), _const_spec((tr, tr))],
        out_specs=[
            pl.BlockSpec((tr, ROUTE_COLS), lambda r: (r, 0)),
            pl.BlockSpec((ROUTE_COLS, tr), lambda r: (0, r)),
            pl.BlockSpec((8, LANES), lambda r: (0, 0)),
        ],
        out_shape=[
            jax.ShapeDtypeStruct((n_tok, ROUTE_COLS), F32),
            jax.ShapeDtypeStruct((ROUTE_COLS, n_tok), F32),
            jax.ShapeDtypeStruct((8, LANES), F32),
        ],
        scratch_shapes=[pltpu.VMEM((1, LANES), F32)],
        compiler_params=pltpu.CompilerParams(
            dimension_semantics=("arbitrary",), vmem_limit_bytes=VMEM_LIMIT),
        name="route",
    )(logits, tri)


def _slab_copy(src, src_row, dst, dst_row, sem):
    return pltpu.make_async_copy(src.at[src_row], dst.at[dst_row], sem)


N_ZERO_BLOCKS = 2 * N_EXPERTS


def _dispatch_kernel(zb_ref, dest_ref, x_ref, out_ref, zeros, sem, zsem, *, td):
    @pl.when(pl.program_id(0) == 0)
    def _():
        zeros[...] = jnp.zeros_like(zeros)

        def block_copy(n):
            row0 = pl.multiple_of(zb_ref[n] * MOE_BLOCK, MOE_BLOCK)
            return pltpu.make_async_copy(zeros, out_ref.at[pl.ds(row0, MOE_BLOCK)], zsem)

        n_zero = zb_ref[N_ZERO_BLOCKS]

        def start(n, carry):
            block_copy(n).start()
            return carry

        def wait(n, carry):
            block_copy(n).wait()
            return carry

        lax.fori_loop(0, n_zero, start, 0)
        lax.fori_loop(0, n_zero, wait, 0)

    def issue(t, carry):
        for k in range(TOP_K):
            _slab_copy(x_ref, t, out_ref, dest_ref[0, 0, k * td + t], sem).start(priority=k)
        return carry

    lax.fori_loop(0, td, issue, 0, unroll=8)
    for k in range(TOP_K):
        pltpu.make_async_copy(x_ref, out_ref.at[pl.ds(0, td)], sem).wait()


def _dispatch(xp, dest3, zero_blocks, n_rows, td):
    n_tok = xp.shape[0]
    grid_spec = pltpu.PrefetchScalarGridSpec(
        num_scalar_prefetch=1,
        grid=(n_tok // td,),
        in_specs=[
            pl.BlockSpec((1, 1, TOP_K * td), lambda s, zb: (s, 0, 0), memory_space=pltpu.SMEM),
            pl.BlockSpec((td, SLAB_ROWS, LANES), lambda s, zb: (s, 0, 0)),
        ],
        out_specs=pl.BlockSpec(memory_space=pl.ANY),
        scratch_shapes=[
            pltpu.VMEM((MOE_BLOCK, SLAB_ROWS, LANES), jnp.uint32),
            pltpu.SemaphoreType.DMA(()),
            pltpu.SemaphoreType.DMA(()),
        ],
    )
    return pl.pallas_call(
        functools.partial(_dispatch_kernel, td=td),
        grid_spec=grid_spec,
        out_shape=jax.ShapeDtypeStruct((n_rows, SLAB_ROWS, LANES), jnp.uint32),
        compiler_params=pltpu.CompilerParams(dimension_semantics=("arbitrary",)),
        name="dispatch",
    )(zero_blocks, dest3, xp)


FFN_SUB = 256


def _expert_ffn_kernel(be_ref, na_ref, nx_ref, x_ref, wg_hbm, wu_hbm, wd_hbm, y_ref,
                       wg_f32, wu_f32, wd_f32, wg_bf, wu_bf, wd_bf, sem):
    i = pl.program_id(0)
    active = i < na_ref[0]
    expert = be_ref[i]
    new_expert = (i == 0) | (expert != be_ref[jnp.maximum(i - 1, 0)])

    def fetch(e):
        return (pltpu.make_async_copy(wg_hbm.at[e], wg_f32, sem.at[0]),
                pltpu.make_async_copy(wu_hbm.at[e], wu_f32, sem.at[1]),
                pltpu.make_async_copy(wd_hbm.at[e], wd_f32, sem.at[2]))

    @pl.when(active & (i == 0))
    def _():
        for cp in fetch(expert):
            cp.start()

    @pl.when(active & new_expert)
    def _():
        for cp in fetch(expert):
            cp.wait()
        wg_bf[...] = wg_f32[...].astype(BF16)
        wu_bf[...] = wu_f32[...].astype(BF16)
        wd_bf[...] = wd_f32[...].astype(BF16)
        nxt = nx_ref[expert]

        @pl.when(nxt >= 0)
        def _():
            for cp in fetch(nxt):
                cp.start()

    @pl.when(active)
    def _():
        for lo in range(0, MOE_BLOCK, FFN_SUB):
            x = _load_slabs(x_ref, lo, FFN_SUB).astype(BF16)
            g = jnp.dot(x, wg_bf[...], preferred_element_type=F32)
            u = jnp.dot(x, wu_bf[...], preferred_element_type=F32)
            hid = (jax.nn.silu(g) * u).astype(BF16)
            y = jnp.dot(hid, wd_bf[...], preferred_element_type=F32)
            _store_slabs(y_ref, lo, y.astype(BF16))

    @pl.when(jnp.logical_not(active))
    def _():
        y_ref[...] = jnp.zeros_like(y_ref)


def _expert_ffn(xs, block_expert, n_active, next_expert, wg, wu, wd):
    n_rows = xs.shape[0] // SLAB_ROWS
    n_blocks = n_rows // MOE_BLOCK

    def blk(i, be, na, nx):
        return (jnp.maximum(jnp.minimum(i, na[0] - 1), 0), 0)

    grid_spec = pltpu.PrefetchScalarGridSpec(
        num_scalar_prefetch=3,
        grid=(n_blocks,),
        in_specs=[
            pl.BlockSpec((MOE_BLOCK * SLAB_ROWS, LANES), blk),
            pl.BlockSpec(memory_space=pl.ANY),
            pl.BlockSpec(memory_space=pl.ANY),
            pl.BlockSpec(memory_space=pl.ANY),
        ],
        out_specs=pl.BlockSpec((MOE_BLOCK * SLAB_ROWS, LANES), lambda i, be, na, nx: (i, 0)),
        scratch_shapes=[
            pltpu.VMEM((D_MODEL, D_EXPERT), F32),
            pltpu.VMEM((D_MODEL, D_EXPERT), F32),
            pltpu.VMEM((D_EXPERT, D_MODEL), F32),
            pltpu.VMEM((D_MODEL, D_EXPERT), BF16),
            pltpu.VMEM((D_MODEL, D_EXPERT), BF16),
            pltpu.VMEM((D_EXPERT, D_MODEL), BF16),
            pltpu.SemaphoreType.DMA((3,)),
        ],
    )
    return pl.pallas_call(
        _expert_ffn_kernel,
        grid_spec=grid_spec,
        out_shape=jax.ShapeDtypeStruct((n_rows * SLAB_ROWS, LANES), jnp.uint32),
        compiler_params=pltpu.CompilerParams(
            dimension_semantics=("arbitrary",), vmem_limit_bytes=VMEM_LIMIT),
        name="expert_ffn",
    )(block_expert, n_active, next_expert, xs, wg, wu, wd)


def _combine_kernel(dcur_ref, dnxt_ref, x_ref, route_ref, g_ref, b_ref, y_ref, o_ref,
                    buf, sem, *, td):
    s = pl.program_id(0)
    n = pl.num_programs(0)
    slot = s % 2

    def base(slot_):
        return pl.multiple_of(slot_ * TOP_K * td, TOP_K * td)

    def issue(d_ref, to_slot):
        def body(t, carry):
            for k in range(TOP_K):
                _slab_copy(y_ref, d_ref[0, 0, k * td + t], buf, base(to_slot) + TOP_K * t + k,
                           sem.at[to_slot]).start(priority=k)
            return carry
        lax.fori_loop(0, td, body, 0, unroll=8)

    @pl.when(s == 0)
    def _():
        issue(dcur_ref, 0)

    @pl.when(s + 1 < n)
    def _():
        issue(dnxt_ref, 1 - slot)

    pltpu.make_async_copy(y_ref.at[pl.ds(0, TOP_K * td)], buf.at[pl.ds(base(slot), TOP_K * td)],
                          sem.at[slot]).wait()

    route = route_ref[...]
    buf2 = buf.reshape(2 * TOP_K * td * SLAB_ROWS, LANES)
    ffn = (route[:, 2:3] * _load_slabs(buf2, base(slot), td, every=TOP_K)
           + route[:, 3:4] * _load_slabs(buf2, base(slot) + 1, td, every=TOP_K))
    o_ref[...] = _layer_norm(DEEPNORM_ALPHA * x_ref[...] + ffn, g_ref[...], b_ref[...])


def _combine(ys, dest3, x1, route, g2, b2, td):
    n_tok = x1.shape[0]
    n_steps = n_tok // td
    return pl.pallas_call(
        functools.partial(_combine_kernel, td=td),
        grid=(n_steps,),
        in_specs=[
            pl.BlockSpec((1, 1, TOP_K * td), lambda s: (s, 0, 0), memory_space=pltpu.SMEM),
            pl.BlockSpec((1, 1, TOP_K * td), lambda s: (jnp.minimum(s + 1, n_steps - 1), 0, 0),
                         memory_space=pltpu.SMEM),
            pl.BlockSpec((td, D_MODEL), lambda s: (s, 0)),
            pl.BlockSpec((td, ROUTE_COLS), lambda s: (s, 0)),
            pl.BlockSpec((1, D_MODEL), lambda s: (0, 0)),
            pl.BlockSpec((1, D_MODEL), lambda s: (0, 0)),
            pl.BlockSpec(memory_space=pl.ANY),
        ],
        out_specs=pl.BlockSpec((td, D_MODEL), lambda s: (s, 0)),
        out_shape=jax.ShapeDtypeStruct((n_tok, D_MODEL), F32),
        scratch_shapes=[
            pltpu.VMEM((2 * TOP_K * td, SLAB_ROWS, LANES), jnp.uint32),
            pltpu.SemaphoreType.DMA((2,)),
        ],
        compiler_params=pltpu.CompilerParams(
            dimension_semantics=("arbitrary",), vmem_limit_bytes=VMEM_LIMIT),
        name="combine",
    )(dest3, dest3, x1, route, g2, b2, ys)


def _row_tile(n, want):
    t = min(want, n)
    while n % t:
        t //= 2
    return t


def kernel(x, ln_in_g, ln_in_b, w_in, b_gate, lambda_q, lambda_k, subln_g, rel_bias, conv_w,
           w_a_proj, w_b_proj, w_o, ln1_g, ln1_b, w_group, b_group, w_sub, b_sub,
           w_gate_e, w_up_e, w_down_e, ln2_g, ln2_b):
    bsz, seq, d = x.shape
    assert DEPTH == 1 and d == D_MODEL and w_in.shape == (DEPTH, D_MODEL, N_IN)
    n_tok = bsz * seq
    tq = _row_tile(seq, 256)
    assert tq % LANES == 0 and tq % CHUNK == 0
    qpb = 1
    tm1 = _row_tile(seq, 512)
    tm3 = _row_tile(n_tok, 1024)
    sub3 = _row_tile(tm3, 256)
    tr = _row_tile(n_tok, 1024)
    td = _row_tile(n_tok, 1024)
    tc = _row_tile(n_tok, 512)
    row = lambda v: v.reshape(1, -1).astype(F32)
    lam_init = 0.8 - 0.6 * math.exp(-0.3 * 0)

    xn, p = _ln_proj(x.reshape(n_tok, d), row(ln_in_g), row(ln_in_b), w_in[0].astype(BF16),
                     conv_w[0].reshape(3, d).astype(F32), seq, tm1)

    band = _band_bias(rel_bias, tq)
    o_n = _diff_attn(p, band, lambda_q[0].astype(F32), lambda_k[0].astype(F32),
                     row(subln_g[0]), bsz, seq, tq, qpb, lam_init)

    gap = EXPERT_LANE0 - N_GROUPS
    w_r = jnp.concatenate(
        [w_group[0].astype(F32), jnp.zeros((d, gap), F32),
         jnp.transpose(w_sub[0].astype(F32), (1, 0, 2)).reshape(d, N_EXPERTS)], axis=1)
    w_r = jnp.pad(w_r, ((0, 0), (0, LANES - w_r.shape[1])))
    w_hi = w_r.astype(BF16)
    w_lo = (w_r - w_hi.astype(F32)).astype(BF16)
    w_r2 = jnp.concatenate([w_hi, w_lo], axis=1)
    b_r = jnp.concatenate([b_group[0].astype(F32), jnp.zeros((gap,), F32),
                           b_sub[0].astype(F32).reshape(-1)])
    b_r = jnp.pad(b_r, (0, LANES - b_r.shape[0])).reshape(1, LANES)
    x1, xp, logits = _mix(
        o_n, p, xn, w_a_proj[0].astype(BF16), w_b_proj[0].astype(BF16), w_o[0].astype(BF16),
        b_gate[0].astype(F32), row(ln1_g[0]), row(ln1_b[0]), w_r2, b_r, tm3, sub3)
    route, route_t, counts = _route(logits, tr)

    n_assign = n_tok * TOP_K
    n_blocks = -(-n_assign // MOE_BLOCK) + N_EXPERTS
    cnt = counts[0, :N_EXPERTS].astype(jnp.int32)
    padded = ((cnt + MOE_BLOCK - 1) // MOE_BLOCK) * MOE_BLOCK
    pad_end = jnp.cumsum(padded)
    pad_start = pad_end - padded
    n_active = (pad_end[-1:] // MOE_BLOCK).astype(jnp.int32)
    blk_row0 = jnp.arange(n_blocks, dtype=jnp.int32) * MOE_BLOCK
    block_expert = jnp.minimum(
        jnp.sum((pad_end[None, :] <= blk_row0[:, None]).astype(jnp.int32), axis=1),
        N_EXPERTS - 1).astype(jnp.int32)
    experts = route_t[0:TOP_K].astype(jnp.int32)
    ranks = route_t[4:4 + TOP_K].astype(jnp.int32)
    is_e = experts[None] == jnp.arange(N_EXPERTS, dtype=jnp.int32)[:, None, None]
    dest = jnp.sum(jnp.where(is_e, pad_start[:, None, None], 0), axis=0) + ranks

    def per_tile(t):
        tiles = jnp.transpose(dest.reshape(TOP_K, n_tok // t, t), (1, 0, 2))
        return tiles.reshape(n_tok // t, 1, TOP_K * t).astype(jnp.int32)

    last_blk = jnp.maximum(pad_end // MOE_BLOCK - 1, 0)
    idle_blk = jnp.minimum(n_active[0] + jnp.arange(N_EXPERTS), n_blocks - 1)
    n_zero = N_EXPERTS + n_blocks - n_active
    zero_blocks = jnp.concatenate([last_blk, idle_blk, n_zero]).astype(jnp.int32)
    n_rows = n_blocks * MOE_BLOCK
    xs = _dispatch(xp.reshape(n_tok, SLAB_ROWS, LANES), per_tile(td), zero_blocks, n_rows, td)
    e_ids = jnp.arange(N_EXPERTS, dtype=jnp.int32)
    later = (padded > 0)[None, :] & (e_ids[None, :] > e_ids[:, None])
    next_expert = jnp.min(jnp.where(later, e_ids[None, :], N_EXPERTS), axis=1)
    next_expert = jnp.where(next_expert < N_EXPERTS, next_expert, -1).astype(jnp.int32)
    ys = _expert_ffn(xs.reshape(n_rows * SLAB_ROWS, LANES), block_expert, n_active, next_expert,
                     w_gate_e[0].astype(F32), w_up_e[0].astype(F32), w_down_e[0].astype(F32))
    out = _combine(ys.reshape(n_rows, SLAB_ROWS, LANES), per_tile(tc), x1, route,
                   row(ln2_g[0]), row(ln2_b[0]), tc)
    return out.reshape(bsz, seq, d)
```

```python
import functools
import math

import numpy as np
import jax
import jax.numpy as jnp
from jax import lax
from jax.experimental import pallas as pl
from jax.experimental.pallas import tpu as pltpu

F32 = jnp.float32
BF16 = jnp.bfloat16

D_MODEL = 1024
N_HEADS = 4
HEAD_DIM = 128
D_HEAD_V = 2 * HEAD_DIM
CHUNK = 64
N_BUCKETS = 32
MAX_DISTANCE = 128
N_GROUPS = 4
EXPERTS_PER_GROUP = 8
N_EXPERTS = N_GROUPS * EXPERTS_PER_GROUP
TOP_K = 2
D_EXPERT = 512
MOE_BLOCK = 512
LN_EPS = 1e-5
RMS_EPS = 1e-6
DEPTH = 1
DEEPNORM_ALPHA = (2.0 * DEPTH) ** 0.25
N_IN = 8 * D_MODEL
LANES = 128
NEG_BIG = -1e30
LOG2_E = math.log2(math.e)

VMEM_LIMIT = 56 * 1024 * 1024


def _layer_norm(x, g, b):
    mu = jnp.mean(x, axis=-1, keepdims=True)
    xc = x - mu
    var = jnp.mean(xc * xc, axis=-1, keepdims=True)
    return xc * lax.rsqrt(var + LN_EPS) * g + b


def _const_spec(shape):
    return pl.BlockSpec(shape, lambda *_: (0,) * len(shape), pipeline_mode=pl.Buffered(1))


HALO = 16
P_Q, P_V, P_CONV, P_GATE_A, P_GATE_B = range(5)
N_P = 5 * D_MODEL


def _ln_proj_kernel(x_ref, g_ref, b_ref, w_ref, wkt_ref, cw_ref, xn_ref, p_ref, kt_ref, zs_ref,
                    *, q_scale, tm, sub, tiles_per_seq):
    r = pl.program_id(0)

    @pl.when((r % tiles_per_seq) == 0)
    def _():
        zs_ref[0:HALO, :] = jnp.zeros((HALO, D_MODEL), F32)

    cw = cw_ref[...]
    for lo in range(0, tm, sub):
        rows = slice(lo, lo + sub)
        xn = _layer_norm(x_ref[rows, :], g_ref[...], b_ref[...])
        xn_ref[rows, :] = xn
        xb = xn.astype(BF16)

        def proj(c, xb=xb):
            return jnp.dot(xb, w_ref[:, c * D_MODEL:(c + 1) * D_MODEL],
                           preferred_element_type=F32)

        def put(block, val, rows=rows):
            p_ref[rows, block * D_MODEL:(block + 1) * D_MODEL] = val.astype(BF16)

        put(P_Q, proj(0) * q_scale)
        put(P_V, proj(2))
        kt = lax.dot_general(wkt_ref[...], xb, (((1,), (1,)), ((), ())),
                             preferred_element_type=F32)
        kt_ref[:, rows] = kt.astype(BF16)
        z0 = HALO + lo
        zs_ref[z0:z0 + sub, :] = proj(4) * proj(5)
        zc = (cw[0:1] * zs_ref[z0 - 2:z0 - 2 + sub, :] + cw[1:2] * zs_ref[z0 - 1:z0 - 1 + sub, :]
              + cw[2:3] * zs_ref[z0:z0 + sub, :])
        put(P_CONV, proj(3) * zc)
        put(P_GATE_A, proj(6))
        put(P_GATE_B, proj(7))

    zs_ref[0:HALO, :] = zs_ref[tm:tm + HALO, :]


def _ln_proj(x2, g, b, w_bf, conv_w, seq, tm):
    n_tok = x2.shape[0]
    wkt = jnp.transpose(w_bf[:, D_MODEL:2 * D_MODEL])
    kern = functools.partial(_ln_proj_kernel, q_scale=HEAD_DIM ** -0.5 * LOG2_E, tm=tm,
                             sub=_row_tile(tm, 256), tiles_per_seq=seq // tm)
    return pl.pallas_call(
        kern,
        grid=(n_tok // tm,),
        in_specs=[
            pl.BlockSpec((tm, D_MODEL), lambda r: (r, 0)),
            _const_spec((1, D_MODEL)),
            _const_spec((1, D_MODEL)),
            _const_spec((D_MODEL, N_IN)),
            _const_spec((D_MODEL, D_MODEL)),
            _const_spec((3, D_MODEL)),
        ],
        out_specs=[
            pl.BlockSpec((tm, D_MODEL), lambda r: (r, 0)),
            pl.BlockSpec((tm, N_P), lambda r: (r, 0)),
            pl.BlockSpec((D_MODEL, tm), lambda r: (0, r)),
        ],
        out_shape=[
            jax.ShapeDtypeStruct((n_tok, D_MODEL), F32),
            jax.ShapeDtypeStruct((n_tok, N_P), BF16),
            jax.ShapeDtypeStruct((D_MODEL, n_tok), BF16),
        ],
        scratch_shapes=[pltpu.VMEM((HALO + tm, D_MODEL), F32)],
        compiler_params=pltpu.CompilerParams(
            dimension_semantics=("arbitrary",), vmem_limit_bytes=VMEM_LIMIT),
        name="ln_proj",
    )(x2, g, b, w_bf, wkt, conv_w)


def _t5_bucket_np(rel):
    nb = N_BUCKETS // 2
    max_exact = nb // 2
    n = np.abs(rel)
    large = np.full(n.shape, max_exact, np.int64)
    for d in range(max_exact, MAX_DISTANCE + 1):
        val = max_exact + int(math.log(d / max_exact) / math.log(MAX_DISTANCE / max_exact)
                              * (nb - max_exact))
        large = np.where(n >= d, min(val, nb - 1), large)
    return np.where(rel > 0, nb, 0) + np.where(n < max_exact, n, large)


def _band_bias(rel_bias, tq):
    qi = np.arange(tq)[:, None]
    kj = np.arange(2 * tq)[None, :] - tq
    allowed = (kj // CHUNK) <= (qi // CHUNK)
    far_bucket = int(_t5_bucket_np(np.array([-(tq + 1)]))[0])
    rb = rel_bias.astype(F32)
    rb = ((rb - rb[far_bucket][None, :]) * LOG2_E).T
    n_rel = 3 * tq
    bucket = _t5_bucket_np(np.arange(n_rel) - (2 * tq - 1))
    onehot = jnp.asarray(bucket[:, None] == np.arange(N_BUCKETS)[None, :])
    per_rel = jnp.sum(jnp.where(onehot[None], rb[:, None, :], 0.0), axis=-1)
    skew = jnp.tile(per_rel, (1, tq))[:, :tq * (n_rel - 1)].reshape(N_HEADS, tq, n_rel - 1)
    tile = skew[:, :, tq - 1:3 * tq - 1]
    return jnp.where(jnp.asarray(allowed)[None], tile, NEG_BIG)


QK_BLOCKS = 2
PV_BLOCKS = 2


def _diff_attn_kernel(q_ref, kt_ref, v_ref, bias_ref, lq_ref, lk_ref, sg_ref, o_ref, s_ref,
                      *, tq, nq, qpb, lam_init):
    dots = jnp.sum(lq_ref[...] * lk_ref[...], axis=-1, keepdims=True)
    lam = jnp.exp(dots[0:1]) - jnp.exp(dots[1:2]) + lam_init
    gain = sg_ref[...] * (1.0 - lam_init)

    def lane_tiles(s):
        return [s[:, c * LANES:(c + 1) * LANES] for c in range(s.shape[1] // LANES)]

    def logits_pass(h, a, i):
        rows = slice(a * tq, (a + 1) * tq)
        row_max = []
        for m in range(2):
            c = h * D_HEAD_V + m * HEAD_DIM
            q = q_ref[rows, c:c + HEAD_DIM]
            mx = None
            for j0 in range(0, i + 1, QK_BLOCKS):
                nb = min(QK_BLOCKS, i + 1 - j0)
                wide = jnp.dot(q, kt_ref[c:c + HEAD_DIM, j0 * tq:(j0 + nb) * tq],
                               preferred_element_type=F32)
                for j in range(j0, j0 + nb):
                    s = wide[:, (j - j0) * tq:(j - j0 + 1) * tq]
                    if j == i:
                        s = s + bias_ref[h, :, tq:2 * tq]
                    elif j == i - 1:
                        s = s + bias_ref[h, :, 0:tq]
                    s_ref[h, a, m, j] = s
                    t = functools.reduce(jnp.maximum, lane_tiles(s))
                    mx = t if mx is None else jnp.maximum(mx, t)
            row_max.append(jnp.broadcast_to(jnp.max(mx, axis=-1, keepdims=True), (tq, LANES)))
        return row_max

    def values_pass(h, a, i, row_max):
        c0 = h * D_HEAD_V
        acc = [None, None]
        lsum = [None, None]
        for j0 in range(0, i + 1, PV_BLOCKS):
            nb = min(PV_BLOCKS, i + 1 - j0)
            v_blk = v_ref[j0 * tq:(j0 + nb) * tq, c0:c0 + D_HEAD_V]
            for m in range(2):
                tiles = []
                for j in range(j0, j0 + nb):
                    tiles += [jnp.exp2(x - row_max[m]) for x in lane_tiles(s_ref[h, a, m, j])]
                tsum = functools.reduce(jnp.add, tiles)
                lsum[m] = tsum if lsum[m] is None else lsum[m] + tsum
                p = jnp.concatenate(tiles, axis=1).astype(BF16)
                part = jnp.dot(p, v_blk, preferred_element_type=F32)
                acc[m] = part if acc[m] is None else acc[m] + part
        r1 = 1.0 / jnp.sum(lsum[0], axis=-1, keepdims=True)
        r2 = lam / jnp.sum(lsum[1], axis=-1, keepdims=True)
        o = acc[0] * r1 - acc[1] * r2
        o = o * lax.rsqrt(jnp.mean(o * o, axis=-1, keepdims=True) + RMS_EPS)
        o_ref[a * tq:(a + 1) * tq, c0:c0 + D_HEAD_V] = (o * gain).astype(BF16)

    def query_tile(step):
        for h in range(N_HEADS):
            maxima = [logits_pass(h, a, step * qpb + a) for a in range(qpb)]
            for a in range(qpb):
                values_pass(h, a, step * qpb + a, maxima[a])

    lax.switch(pl.program_id(1), [functools.partial(query_tile, s) for s in range(nq // qpb)])


def _diff_attn(p, kt, band, lam_q, lam_k, subln_g, bsz, seq, tq, qpb, lam_init):
    n_tok = bsz * seq
    nq = seq // tq
    steps = nq // qpb
    kern = functools.partial(_diff_attn_kernel, tq=tq, nq=nq, qpb=qpb, lam_init=lam_init)
    return pl.pallas_call(
        kern,
        grid=(bsz, steps),
        in_specs=[
            pl.BlockSpec((qpb * tq, D_MODEL), lambda b, i: (b * steps + i, P_Q)),
            pl.BlockSpec((D_MODEL, seq), lambda b, i: (0, b)),
            pl.BlockSpec((seq, D_MODEL), lambda b, i: (b, P_V)),
            _const_spec((N_HEADS, tq, 2 * tq)),
            _const_spec((2, HEAD_DIM)),
            _const_spec((2, HEAD_DIM)),
            _const_spec((1, D_HEAD_V)),
        ],
        out_specs=pl.BlockSpec((qpb * tq, D_MODEL), lambda b, i: (b * steps + i, 0)),
        out_shape=jax.ShapeDtypeStruct((n_tok, D_MODEL), BF16),
        scratch_shapes=[
            pltpu.VMEM((N_HEADS, qpb, 2, nq, tq, tq), F32),
        ],
        compiler_params=pltpu.CompilerParams(
            dimension_semantics=("arbitrary", "arbitrary"), vmem_limit_bytes=VMEM_LIMIT),
        name="diff_attn",
    )(p, kt, p, band, lam_q, lam_k, subln_g)


SLAB_ROWS = D_MODEL // (2 * LANES)


def _store_slabs(ref, row0, x_bf):
    n = x_bf.shape[0]
    bits = pltpu.bitcast(x_bf.astype(F32), jnp.uint32)
    for c in range(SLAB_ROWS):
        lo = bits[:, 2 * c * LANES:(2 * c + 1) * LANES]
        hi = bits[:, (2 * c + 1) * LANES:(2 * c + 2) * LANES]
        ref[pl.ds(SLAB_ROWS * row0 + c, n, stride=SLAB_ROWS), :] = (lo >> 16) | hi


def _load_slabs(ref, row0, n, every=1):
    parts = []
    for c in range(SLAB_ROWS):
        words = ref[pl.ds(SLAB_ROWS * row0 + c, n, stride=SLAB_ROWS * every), :]
        parts.append(pltpu.bitcast(words << 16, F32))
        parts.append(pltpu.bitcast(words & jnp.uint32(0xFFFF0000), F32))
    return jnp.concatenate(parts, axis=1)


def _mix_kernel(on_ref, yb_ref, ga_ref, gb_ref, xn_ref, wa_ref, wb_ref, wo_ref, bg_ref,
                g1_ref, b1_ref, wr_ref, br_ref, x1_ref, xp_ref, lg_ref, *, tm, sub):
    bg = bg_ref[...]
    for lo in range(0, tm, sub):
        rows = slice(lo, lo + sub)
        y_b = jnp.dot(yb_ref[rows, :], wb_ref[...], preferred_element_type=F32)
        y_a = jnp.dot(on_ref[rows, :], wa_ref[...], preferred_element_type=F32)
        g_a = jax.nn.sigmoid(ga_ref[rows, :].astype(F32) + bg[0:1])
        g_b = jax.nn.sigmoid(gb_ref[rows, :].astype(F32) + bg[1:2])
        merged = (g_a * y_a + g_b * y_b).astype(BF16)
        mix = jnp.dot(merged, wo_ref[...], preferred_element_type=F32)
        x1 = _layer_norm(DEEPNORM_ALPHA * xn_ref[rows, :] + mix, g1_ref[...], b1_ref[...])
        x1_ref[rows, :] = x1
        x_hi = x1.astype(BF16)
        x_lo = (x1 - x_hi.astype(F32)).astype(BF16)
        part = (jnp.dot(x_hi, wr_ref[...], preferred_element_type=F32)
                + jnp.dot(x_lo, wr_ref[...], preferred_element_type=F32))
        lg_ref[rows, :] = part[:, 0:LANES] + part[:, LANES:2 * LANES] + br_ref[...]
        _store_slabs(xp_ref, lo, x_hi)


def _mix(o_n, p, xn, wa, wb, wo, b_gate, g1, b1, wr2, br, tm, sub):
    n_tok = xn.shape[0]
    kern = functools.partial(_mix_kernel, tm=tm, sub=sub)

    def col(c):
        return pl.BlockSpec((tm, D_MODEL), lambda r, c=c: (r, c))

    return pl.pallas_call(
        kern,
        grid=(n_tok // tm,),
        in_specs=[
            pl.BlockSpec((tm, D_MODEL), lambda r: (r, 0)),
            col(P_CONV), col(P_GATE_A), col(P_GATE_B),
            pl.BlockSpec((tm, D_MODEL), lambda r: (r, 0)),
            _const_spec((D_MODEL, D_MODEL)), _const_spec((D_MODEL, D_MODEL)),
            _const_spec((D_MODEL, D_MODEL)),
            _const_spec((2, D_MODEL)), _const_spec((1, D_MODEL)), _const_spec((1, D_MODEL)),
            _const_spec((D_MODEL, 2 * LANES)), _const_spec((1, LANES)),
        ],
        out_specs=[
            pl.BlockSpec((tm, D_MODEL), lambda r: (r, 0)),
            pl.BlockSpec((tm * SLAB_ROWS, LANES), lambda r: (r, 0)),
            pl.BlockSpec((tm, LANES), lambda r: (r, 0)),
        ],
        out_shape=[
            jax.ShapeDtypeStruct((n_tok, D_MODEL), F32),
            jax.ShapeDtypeStruct((n_tok * SLAB_ROWS, LANES), jnp.uint32),
            jax.ShapeDtypeStruct((n_tok, LANES), F32),
        ],
        compiler_params=pltpu.CompilerParams(
            dimension_semantics=("arbitrary",), vmem_limit_bytes=VMEM_LIMIT),
        name="mix",
    )(o_n, p, p, p, xn, wa, wb, wo, b_gate, g1, b1, wr2, br)


ROUTE_COLS = 8
EXPERT_LANE0 = 8


def _route_kernel(lg_ref, tri_ref, route_ref, route_t_ref, cnt_ref, run_ref, *, tr):
    @pl.when(pl.program_id(0) == 0)
    def _():
        run_ref[...] = jnp.zeros_like(run_ref)

    lt = jnp.transpose(lg_ref[...])
    row = lax.broadcasted_iota(jnp.int32, (EXPERTS_PER_GROUP, tr), 0).astype(F32)

    def first_argmax(vals, vmax):
        return jnp.min(jnp.where(vals == vmax, row, float(EXPERTS_PER_GROUP)),
                       axis=0, keepdims=True)

    gl = jnp.where(row < N_GROUPS, lt[0:EXPERTS_PER_GROUP], -jnp.inf)
    gmax = jnp.max(gl, axis=0, keepdims=True)
    gsum = jnp.sum(jnp.exp(gl - gmax), axis=0, keepdims=True)
    g_p = 1.0 / gsum
    g_idx = first_argmax(gl, gmax)

    def group_rows(g):
        lo = EXPERT_LANE0 + EXPERTS_PER_GROUP * g
        return lt[lo:lo + EXPERTS_PER_GROUP]

    sl = group_rows(N_GROUPS - 1)
    for g in reversed(range(N_GROUPS - 1)):
        sl = jnp.where(g_idx == g, group_rows(g), sl)
    s1 = jnp.max(sl, axis=0, keepdims=True)
    i1 = first_argmax(sl, s1)
    sl2 = jnp.where(row == i1, -jnp.inf, sl)
    s2 = jnp.max(sl2, axis=0, keepdims=True)
    i2 = first_argmax(sl2, s2)
    t = jnp.exp(s2 - s1)
    w1 = g_p / (1.0 + t)
    w2 = g_p * t / (1.0 + t)
    e1 = EXPERTS_PER_GROUP * g_idx + i1
    e2 = EXPERTS_PER_GROUP * g_idx + i2

    e_row = lax.broadcasted_iota(jnp.int32, (N_EXPERTS, tr), 0).astype(F32)
    oh1 = e_row == e1
    oh2 = e_row == e2
    onehot_t = jnp.concatenate([jnp.where(oh1 | oh2, 1.0, 0.0),
                                jnp.zeros((LANES - N_EXPERTS, tr), F32)], axis=0)
    onehot = jnp.transpose(onehot_t)
    before = jnp.dot(tri_ref[...], onehot.astype(BF16), preferred_element_type=F32) + run_ref[...]
    before_t = jnp.transpose(before)[0:N_EXPERTS]
    rank1 = jnp.sum(jnp.where(oh1, before_t, 0.0), axis=0, keepdims=True)
    rank2 = jnp.sum(jnp.where(oh2, before_t, 0.0), axis=0, keepdims=True)
    run_ref[...] = run_ref[...] + jnp.sum(onehot, axis=0, keepdims=True)
    cnt_ref[...] = jnp.broadcast_to(run_ref[...], cnt_ref.shape)

    zero = jnp.zeros((1, tr), F32)
    route_t = jnp.concatenate([e1, e2, w1, w2, rank1, rank2, zero, zero], axis=0)
    route_t_ref[...] = route_t
    padded = jnp.concatenate([route_t, jnp.zeros((LANES - ROUTE_COLS, tr), F32)], axis=0)
    route_ref[...] = jnp.transpose(padded)[:, 0:ROUTE_COLS]


def _route(logits, tr):
    n_tok = logits.shape[0]
    tri = jnp.asarray(np.tril(np.ones((tr, tr), np.float32), k=-1), BF16)
    return pl.pallas_call(
        functools.partial(_route_kernel, tr=tr),
        grid=(n_tok // tr,),
        in_specs=[pl.BlockSpec((tr, LANES), lambda r: (r, 0)), _const_spec((tr, tr))],
        out_specs=[
            pl.BlockSpec((tr, ROUTE_COLS), lambda r: (r, 0)),
            pl.BlockSpec((ROUTE_COLS, tr), lambda r: (0, r)),
            pl.BlockSpec((8, LANES), lambda r: (0, 0)),
        ],
        out_shape=[
            jax.ShapeDtypeStruct((n_tok, ROUTE_COLS), F32),
            jax.ShapeDtypeStruct((ROUTE_COLS, n_tok), F32),
            jax.ShapeDtypeStruct((8, LANES), F32),
        ],
        scratch_shapes=[pltpu.VMEM((1, LANES), F32)],
        compiler_params=pltpu.CompilerParams(
            dimension_semantics=("arbitrary",), vmem_limit_bytes=VMEM_LIMIT),
        name="route",
    )(logits, tri)


def _slab_copy(src, src_row, dst, dst_row, sem):
    return pltpu.make_async_copy(src.at[src_row], dst.at[dst_row], sem)


N_ZERO_BLOCKS = 2 * N_EXPERTS


def _dispatch_kernel(zb_ref, dest_ref, x_ref, out_ref, zeros, sem, zsem, *, td):
    @pl.when(pl.program_id(0) == 0)
    def _():
        zeros[...] = jnp.zeros_like(zeros)

        def block_copy(n):
            row0 = pl.multiple_of(zb_ref[n] * MOE_BLOCK, MOE_BLOCK)
            return pltpu.make_async_copy(zeros, out_ref.at[pl.ds(row0, MOE_BLOCK)], zsem)

        n_zero = zb_ref[N_ZERO_BLOCKS]

        def start(n, carry):
            block_copy(n).start()
            return carry

        def wait(n, carry):
            block_copy(n).wait()
            return carry

        lax.fori_loop(0, n_zero, start, 0)
        lax.fori_loop(0, n_zero, wait, 0)

    def issue(t, carry):
        for k in range(TOP_K):
            _slab_copy(x_ref, t, out_ref, dest_ref[0, 0, k * td + t], sem).start(priority=k)
        return carry

    lax.fori_loop(0, td, issue, 0, unroll=8)
    for k in range(TOP_K):
        pltpu.make_async_copy(x_ref, out_ref.at[pl.ds(0, td)], sem).wait()


def _dispatch(xp, dest3, zero_blocks, n_rows, td):
    n_tok = xp.shape[0]
    grid_spec = pltpu.PrefetchScalarGridSpec(
        num_scalar_prefetch=1,
        grid=(n_tok // td,),
        in_specs=[
            pl.BlockSpec((1, 1, TOP_K * td), lambda s, zb: (s, 0, 0), memory_space=pltpu.SMEM),
            pl.BlockSpec((td, SLAB_ROWS, LANES), lambda s, zb: (s, 0, 0)),
        ],
        out_specs=pl.BlockSpec(memory_space=pl.ANY),
        scratch_shapes=[
            pltpu.VMEM((MOE_BLOCK, SLAB_ROWS, LANES), jnp.uint32),
            pltpu.SemaphoreType.DMA(()),
            pltpu.SemaphoreType.DMA(()),
        ],
    )
    return pl.pallas_call(
        functools.partial(_dispatch_kernel, td=td),
        grid_spec=grid_spec,
        out_shape=jax.ShapeDtypeStruct((n_rows, SLAB_ROWS, LANES), jnp.uint32),
        compiler_params=pltpu.CompilerParams(dimension_semantics=("arbitrary",)),
        name="dispatch",
    )(zero_blocks, dest3, xp)


FFN_SUB = 256


def _expert_ffn_kernel(be_ref, na_ref, nx_ref, x_ref, wg_hbm, wu_hbm, wd_hbm, y_ref,
                       wg_f32, wu_f32, wd_f32, wg_bf, wu_bf, wd_bf, sem):
    i = pl.program_id(0)
    active = i < na_ref[0]
    expert = be_ref[i]
    new_expert = (i == 0) | (expert != be_ref[jnp.maximum(i - 1, 0)])

    def fetch(e):
        return (pltpu.make_async_copy(wg_hbm.at[e], wg_f32, sem.at[0]),
                pltpu.make_async_copy(wu_hbm.at[e], wu_f32, sem.at[1]),
                pltpu.make_async_copy(wd_hbm.at[e], wd_f32, sem.at[2]))

    @pl.when(active & (i == 0))
    def _():
        for cp in fetch(expert):
            cp.start()

    @pl.when(active & new_expert)
    def _():
        for cp in fetch(expert):
            cp.wait()
        wg_bf[...] = wg_f32[...].astype(BF16)
        wu_bf[...] = wu_f32[...].astype(BF16)
        wd_bf[...] = wd_f32[...].astype(BF16)
        nxt = nx_ref[expert]

        @pl.when(nxt >= 0)
        def _():
            for cp in fetch(nxt):
                cp.start()

    @pl.when(active)
    def _():
        for lo in range(0, MOE_BLOCK, FFN_SUB):
            x = _load_slabs(x_ref, lo, FFN_SUB).astype(BF16)
            g = jnp.dot(x, wg_bf[...], preferred_element_type=F32)
            u = jnp.dot(x, wu_bf[...], preferred_element_type=F32)
            hid = (jax.nn.silu(g) * u).astype(BF16)
            y = jnp.dot(hid, wd_bf[...], preferred_element_type=F32)
            _store_slabs(y_ref, lo, y.astype(BF16))

    @pl.when(jnp.logical_not(active))
    def _():
        y_ref[...] = jnp.zeros_like(y_ref)


def _expert_ffn(xs, block_expert, n_active, next_expert, wg, wu, wd):
    n_rows = xs.shape[0] // SLAB_ROWS
    n_blocks = n_rows // MOE_BLOCK

    def blk(i, be, na, nx):
        return (jnp.maximum(jnp.minimum(i, na[0] - 1), 0), 0)

    grid_spec = pltpu.PrefetchScalarGridSpec(
        num_scalar_prefetch=3,
        grid=(n_blocks,),
        in_specs=[
            pl.BlockSpec((MOE_BLOCK * SLAB_ROWS, LANES), blk),
            pl.BlockSpec(memory_space=pl.ANY),
            pl.BlockSpec(memory_space=pl.ANY),
            pl.BlockSpec(memory_space=pl.ANY),
        ],
        out_specs=pl.BlockSpec((MOE_BLOCK * SLAB_ROWS, LANES), lambda i, be, na, nx: (i, 0)),
        scratch_shapes=[
            pltpu.VMEM((D_MODEL, D_EXPERT), F32),
            pltpu.VMEM((D_MODEL, D_EXPERT), F32),
            pltpu.VMEM((D_EXPERT, D_MODEL), F32),
            pltpu.VMEM((D_MODEL, D_EXPERT), BF16),
            pltpu.VMEM((D_MODEL, D_EXPERT), BF16),
            pltpu.VMEM((D_EXPERT, D_MODEL), BF16),
            pltpu.SemaphoreType.DMA((3,)),
        ],
    )
    return pl.pallas_call(
        _expert_ffn_kernel,
        grid_spec=grid_spec,
        out_shape=jax.ShapeDtypeStruct((n_rows * SLAB_ROWS, LANES), jnp.uint32),
        compiler_params=pltpu.CompilerParams(
            dimension_semantics=("arbitrary",), vmem_limit_bytes=VMEM_LIMIT),
        name="expert_ffn",
    )(block_expert, n_active, next_expert, xs, wg, wu, wd)


def _combine_kernel(dcur_ref, dnxt_ref, x_ref, route_ref, g_ref, b_ref, y_ref, o_ref,
                    buf, sem, *, td):
    s = pl.program_id(0)
    n = pl.num_programs(0)
    slot = s % 2

    def base(slot_):
        return pl.multiple_of(slot_ * TOP_K * td, TOP_K * td)

    def issue(d_ref, to_slot):
        def body(t, carry):
            for k in range(TOP_K):
                _slab_copy(y_ref, d_ref[0, 0, k * td + t], buf, base(to_slot) + TOP_K * t + k,
                           sem.at[to_slot]).start(priority=k)
            return carry
        lax.fori_loop(0, td, body, 0, unroll=8)

    @pl.when(s == 0)
    def _():
        issue(dcur_ref, 0)

    @pl.when(s + 1 < n)
    def _():
        issue(dnxt_ref, 1 - slot)

    pltpu.make_async_copy(y_ref.at[pl.ds(0, TOP_K * td)], buf.at[pl.ds(base(slot), TOP_K * td)],
                          sem.at[slot]).wait()

    route = route_ref[...]
    buf2 = buf.reshape(2 * TOP_K * td * SLAB_ROWS, LANES)
    ffn = (route[:, 2:3] * _load_slabs(buf2, base(slot), td, every=TOP_K)
           + route[:, 3:4] * _load_slabs(buf2, base(slot) + 1, td, every=TOP_K))
    o_ref[...] = _layer_norm(DEEPNORM_ALPHA * x_ref[...] + ffn, g_ref[...], b_ref[...])


def _combine(ys, dest3, x1, route, g2, b2, td):
    n_tok = x1.shape[0]
    n_steps = n_tok // td
    return pl.pallas_call(
        functools.partial(_combine_kernel, td=td),
        grid=(n_steps,),
        in_specs=[
            pl.BlockSpec((1, 1, TOP_K * td), lambda s: (s, 0, 0), memory_space=pltpu.SMEM),
            pl.BlockSpec((1, 1, TOP_K * td), lambda s: (jnp.minimum(s + 1, n_steps - 1), 0, 0),
                         memory_space=pltpu.SMEM),
            pl.BlockSpec((td, D_MODEL), lambda s: (s, 0)),
            pl.BlockSpec((td, ROUTE_COLS), lambda s: (s, 0)),
            pl.BlockSpec((1, D_MODEL), lambda s: (0, 0)),
            pl.BlockSpec((1, D_MODEL), lambda s: (0, 0)),
            pl.BlockSpec(memory_space=pl.ANY),
        ],
        out_specs=pl.BlockSpec((td, D_MODEL), lambda s: (s, 0)),
        out_shape=jax.ShapeDtypeStruct((n_tok, D_MODEL), F32),
        scratch_shapes=[
            pltpu.VMEM((2 * TOP_K * td, SLAB_ROWS, LANES), jnp.uint32),
            pltpu.SemaphoreType.DMA((2,)),
        ],
        compiler_params=pltpu.CompilerParams(
            dimension_semantics=("arbitrary",), vmem_limit_bytes=VMEM_LIMIT),
        name="combine",
    )(dest3, dest3, x1, route, g2, b2, ys)


def _row_tile(n, want):
    t = min(want, n)
    while n % t:
        t //= 2
    return t


def kernel(x, ln_in_g, ln_in_b, w_in, b_gate, lambda_q, lambda_k, subln_g, rel_bias, conv_w,
           w_a_proj, w_b_proj, w_o, ln1_g, ln1_b, w_group, b_group, w_sub, b_sub,
           w_gate_e, w_up_e, w_down_e, ln2_g, ln2_b):
    bsz, seq, d = x.shape
    assert DEPTH == 1 and d == D_MODEL and w_in.shape == (DEPTH, D_MODEL, N_IN)
    n_tok = bsz * seq
    tq = _row_tile(seq, 256)
    assert tq % LANES == 0 and tq % CHUNK == 0
    qpb = 1
    tm1 = _row_tile(seq, 512)
    tm3 = _row_tile(n_tok, 1024)
    sub3 = _row_tile(tm3, 256)
    tr = _row_tile(n_tok, 1024)
    td = _row_tile(n_tok, 1024)
    tc = _row_tile(n_tok, 512)
    row = lambda v: v.reshape(1, -1).astype(F32)
    lam_init = 0.8 - 0.6 * math.exp(-0.3 * 0)

    xn, p, kt = _ln_proj(x.reshape(n_tok, d), row(ln_in_g), row(ln_in_b), w_in[0].astype(BF16),
                         conv_w[0].reshape(3, d).astype(F32), seq, tm1)

    band = _band_bias(rel_bias, tq)
    o_n = _diff_attn(p, kt, band, lambda_q[0].astype(F32), lambda_k[0].astype(F32),
                     row(subln_g[0]), bsz, seq, tq, qpb, lam_init)

    gap = EXPERT_LANE0 - N_GROUPS
    w_r = jnp.concatenate(
        [w_group[0].astype(F32), jnp.zeros((d, gap), F32),
         jnp.transpose(w_sub[0].astype(F32), (1, 0, 2)).reshape(d, N_EXPERTS)], axis=1)
    w_r = jnp.pad(w_r, ((0, 0), (0, LANES - w_r.shape[1])))
    w_hi = w_r.astype(BF16)
    w_lo = (w_r - w_hi.astype(F32)).astype(BF16)
    w_r2 = jnp.concatenate([w_hi, w_lo], axis=1)
    b_r = jnp.concatenate([b_group[0].astype(F32), jnp.zeros((gap,), F32),
                           b_sub[0].astype(F32).reshape(-1)])
    b_r = jnp.pad(b_r, (0, LANES - b_r.shape[0])).reshape(1, LANES)
    x1, xp, logits = _mix(
        o_n, p, xn, w_a_proj[0].astype(BF16), w_b_proj[0].astype(BF16), w_o[0].astype(BF16),
        b_gate[0].astype(F32), row(ln1_g[0]), row(ln1_b[0]), w_r2, b_r, tm3, sub3)
    route, route_t, counts = _route(logits, tr)

    n_assign = n_tok * TOP_K
    n_blocks = -(-n_assign // MOE_BLOCK) + N_EXPERTS
    cnt = counts[0, :N_EXPERTS].astype(jnp.int32)
    padded = ((cnt + MOE_BLOCK - 1) // MOE_BLOCK) * MOE_BLOCK
    pad_end = jnp.cumsum(padded)
    pad_start = pad_end - padded
    n_active = (pad_end[-1:] // MOE_BLOCK).astype(jnp.int32)
    blk_row0 = jnp.arange(n_blocks, dtype=jnp.int32) * MOE_BLOCK
    block_expert = jnp.minimum(
        jnp.sum((pad_end[None, :] <= blk_row0[:, None]).astype(jnp.int32), axis=1),
        N_EXPERTS - 1).astype(jnp.int32)
    experts = route_t[0:TOP_K].astype(jnp.int32)
    ranks = route_t[4:4 + TOP_K].astype(jnp.int32)
    is_e = experts[None] == jnp.arange(N_EXPERTS, dtype=jnp.int32)[:, None, None]
    dest = jnp.sum(jnp.where(is_e, pad_start[:, None, None], 0), axis=0) + ranks

    def per_tile(t):
        tiles = jnp.transpose(dest.reshape(TOP_K, n_tok // t, t), (1, 0, 2))
        return tiles.reshape(n_tok // t, 1, TOP_K * t).astype(jnp.int32)

    last_blk = jnp.maximum(pad_end // MOE_BLOCK - 1, 0)
    idle_blk = jnp.minimum(n_active[0] + jnp.arange(N_EXPERTS), n_blocks - 1)
    n_zero = N_EXPERTS + n_blocks - n_active
    zero_blocks = jnp.concatenate([last_blk, idle_blk, n_zero]).astype(jnp.int32)
    n_rows = n_blocks * MOE_BLOCK
    xs = _dispatch(xp.reshape(n_tok, SLAB_ROWS, LANES), per_tile(td), zero_blocks, n_rows, td)
    e_ids = jnp.arange(N_EXPERTS, dtype=jnp.int32)
    later = (padded > 0)[None, :] & (e_ids[None, :] > e_ids[:, None])
    next_expert = jnp.min(jnp.where(later, e_ids[None, :], N_EXPERTS), axis=1)
    next_expert = jnp.where(next_expert < N_EXPERTS, next_expert, -1).astype(jnp.int32)
    ys = _expert_ffn(xs.reshape(n_rows * SLAB_ROWS, LANES), block_expert, n_active, next_expert,
                     w_gate_e[0].astype(F32), w_up_e[0].astype(F32), w_down_e[0].astype(F32))
    out = _combine(ys.reshape(n_rows, SLAB_ROWS, LANES), per_tile(tc), x1, route,
                   row(ln2_g[0]), row(ln2_b[0]), tc)
    return out.reshape(bsz, seq, d)
```

```python
import functools
import math

import numpy as np
import jax
import jax.numpy as jnp
from jax import lax
from jax.experimental import pallas as pl
from jax.experimental.pallas import tpu as pltpu

F32 = jnp.float32
BF16 = jnp.bfloat16

D_MODEL = 1024
N_HEADS = 4
HEAD_DIM = 128
D_HEAD_V = 2 * HEAD_DIM
CHUNK = 64
N_BUCKETS = 32
MAX_DISTANCE = 128
N_GROUPS = 4
EXPERTS_PER_GROUP = 8
N_EXPERTS = N_GROUPS * EXPERTS_PER_GROUP
TOP_K = 2
D_EXPERT = 512
MOE_BLOCK = 512
LN_EPS = 1e-5
RMS_EPS = 1e-6
DEPTH = 1
DEEPNORM_ALPHA = (2.0 * DEPTH) ** 0.25
N_IN = 8 * D_MODEL
LANES = 128
NEG_BIG = -1e30
LOG2_E = math.log2(math.e)

VMEM_LIMIT = 56 * 1024 * 1024


def _layer_norm(x, g, b):
    mu = jnp.mean(x, axis=-1, keepdims=True)
    xc = x - mu
    var = jnp.mean(xc * xc, axis=-1, keepdims=True)
    return xc * lax.rsqrt(var + LN_EPS) * g + b


def _const_spec(shape):
    return pl.BlockSpec(shape, lambda *_: (0,) * len(shape), pipeline_mode=pl.Buffered(1))


HALO = 16
P_Q, P_V, P_CONV, P_GATE_A, P_GATE_B = range(5)
N_P = 5 * D_MODEL


def _ln_proj_kernel(x_ref, g_ref, b_ref, w_ref, wkt_ref, cw_ref, xn_ref, p_ref, kt_ref, zs_ref,
                    *, q_scale, tm, sub, tiles_per_seq):
    r = pl.program_id(0)

    @pl.when((r % tiles_per_seq) == 0)
    def _():
        zs_ref[0:HALO, :] = jnp.zeros((HALO, D_MODEL), F32)

    cw = cw_ref[...]
    for lo in range(0, tm, sub):
        rows = slice(lo, lo + sub)
        xn = _layer_norm(x_ref[rows, :], g_ref[...], b_ref[...])
        xn_ref[rows, :] = xn
        xb = xn.astype(BF16)

        def proj(c, xb=xb):
            return jnp.dot(xb, w_ref[:, c * D_MODEL:(c + 1) * D_MODEL],
                           preferred_element_type=F32)

        def put(block, val, rows=rows):
            p_ref[rows, block * D_MODEL:(block + 1) * D_MODEL] = val.astype(BF16)

        put(P_Q, proj(0) * q_scale)
        put(P_V, proj(2))
        kt = lax.dot_general(wkt_ref[...], xb, (((1,), (1,)), ((), ())),
                             preferred_element_type=F32)
        kt_ref[:, rows] = kt.astype(BF16)
        z0 = HALO + lo
        zs_ref[z0:z0 + sub, :] = proj(4) * proj(5)
        zc = (cw[0:1] * zs_ref[z0 - 2:z0 - 2 + sub, :] + cw[1:2] * zs_ref[z0 - 1:z0 - 1 + sub, :]
              + cw[2:3] * zs_ref[z0:z0 + sub, :])
        put(P_CONV, proj(3) * zc)
        put(P_GATE_A, proj(6))
        put(P_GATE_B, proj(7))

    zs_ref[0:HALO, :] = zs_ref[tm:tm + HALO, :]


def _ln_proj(x2, g, b, w_bf, wkt, conv_w, seq, tm):
    n_tok = x2.shape[0]
    kern = functools.partial(_ln_proj_kernel, q_scale=HEAD_DIM ** -0.5 * LOG2_E, tm=tm,
                             sub=_row_tile(tm, 256), tiles_per_seq=seq // tm)
    return pl.pallas_call(
        kern,
        grid=(n_tok // tm,),
        in_specs=[
            pl.BlockSpec((tm, D_MODEL), lambda r: (r, 0)),
            _const_spec((1, D_MODEL)),
            _const_spec((1, D_MODEL)),
            _const_spec((D_MODEL, N_IN)),
            _const_spec((D_MODEL, D_MODEL)),
            _const_spec((3, D_MODEL)),
        ],
        out_specs=[
            pl.BlockSpec((tm, D_MODEL), lambda r: (r, 0)),
            pl.BlockSpec((tm, N_P), lambda r: (r, 0)),
            pl.BlockSpec((D_MODEL, tm), lambda r: (0, r)),
        ],
        out_shape=[
            jax.ShapeDtypeStruct((n_tok, D_MODEL), F32),
            jax.ShapeDtypeStruct((n_tok, N_P), BF16),
            jax.ShapeDtypeStruct((D_MODEL, n_tok), BF16),
        ],
        scratch_shapes=[pltpu.VMEM((HALO + tm, D_MODEL), F32)],
        compiler_params=pltpu.CompilerParams(
            dimension_semantics=("arbitrary",), vmem_limit_bytes=VMEM_LIMIT),
        name="ln_proj",
    )(x2, g, b, w_bf, wkt, conv_w)


def _t5_bucket_np(rel):
    nb = N_BUCKETS // 2
    max_exact = nb // 2
    n = np.abs(rel)
    large = np.full(n.shape, max_exact, np.int64)
    for d in range(max_exact, MAX_DISTANCE + 1):
        val = max_exact + int(math.log(d / max_exact) / math.log(MAX_DISTANCE / max_exact)
                              * (nb - max_exact))
        large = np.where(n >= d, min(val, nb - 1), large)
    return np.where(rel > 0, nb, 0) + np.where(n < max_exact, n, large)


def _band_bias(rel_bias, tq):
    qi = np.arange(tq)[:, None]
    kj = np.arange(2 * tq)[None, :] - tq
    allowed = (kj // CHUNK) <= (qi // CHUNK)
    far_bucket = int(_t5_bucket_np(np.array([-(tq + 1)]))[0])
    rb = rel_bias.astype(F32)
    rb = ((rb - rb[far_bucket][None, :]) * LOG2_E).T
    n_rel = 3 * tq
    bucket = _t5_bucket_np(np.arange(n_rel) - (2 * tq - 1))
    onehot = jnp.asarray(bucket[:, None] == np.arange(N_BUCKETS)[None, :])
    per_rel = jnp.sum(jnp.where(onehot[None], rb[:, None, :], 0.0), axis=-1)
    skew = jnp.tile(per_rel, (1, tq))[:, :tq * (n_rel - 1)].reshape(N_HEADS, tq, n_rel - 1)
    tile = skew[:, :, tq - 1:3 * tq - 1]
    return jnp.where(jnp.asarray(allowed)[None], tile, NEG_BIG)


QK_BLOCKS = 2
PV_BLOCKS = 2


def _diff_attn_kernel(q_ref, kt_ref, v_ref, bias_ref, lq_ref, lk_ref, sg_ref, o_ref, s_ref,
                      *, tq, nq, qpb, lam_init):
    dots = jnp.sum(lq_ref[...] * lk_ref[...], axis=-1, keepdims=True)
    lam = jnp.exp(dots[0:1]) - jnp.exp(dots[1:2]) + lam_init
    gain = sg_ref[...] * (1.0 - lam_init)

    def lane_tiles(s):
        return [s[:, c * LANES:(c + 1) * LANES] for c in range(s.shape[1] // LANES)]

    def logits_pass(h, a, i):
        rows = slice(a * tq, (a + 1) * tq)
        row_max = []
        for m in range(2):
            c = h * D_HEAD_V + m * HEAD_DIM
            q = q_ref[rows, c:c + HEAD_DIM]
            mx = None
            for j0 in range(0, i + 1, QK_BLOCKS):
                nb = min(QK_BLOCKS, i + 1 - j0)
                wide = jnp.dot(q, kt_ref[c:c + HEAD_DIM, j0 * tq:(j0 + nb) * tq],
                               preferred_element_type=F32)
                for j in range(j0, j0 + nb):
                    s = wide[:, (j - j0) * tq:(j - j0 + 1) * tq]
                    if j == i:
                        s = s + bias_ref[h, :, tq:2 * tq]
                    elif j == i - 1:
                        s = s + bias_ref[h, :, 0:tq]
                    s_ref[h, a, m, j] = s
                    t = functools.reduce(jnp.maximum, lane_tiles(s))
                    mx = t if mx is None else jnp.maximum(mx, t)
            row_max.append(jnp.broadcast_to(jnp.max(mx, axis=-1, keepdims=True), (tq, LANES)))
        return row_max

    def values_pass(h, a, i, row_max):
        c0 = h * D_HEAD_V
        acc = [None, None]
        lsum = [None, None]
        for j0 in range(0, i + 1, PV_BLOCKS):
            nb = min(PV_BLOCKS, i + 1 - j0)
            v_blk = v_ref[j0 * tq:(j0 + nb) * tq, c0:c0 + D_HEAD_V]
            for m in range(2):
                tiles = []
                for j in range(j0, j0 + nb):
                    tiles += [jnp.exp2(x - row_max[m]) for x in lane_tiles(s_ref[h, a, m, j])]
                tsum = functools.reduce(jnp.add, tiles)
                lsum[m] = tsum if lsum[m] is None else lsum[m] + tsum
                p = jnp.concatenate(tiles, axis=1).astype(BF16)
                part = jnp.dot(p, v_blk, preferred_element_type=F32)
                acc[m] = part if acc[m] is None else acc[m] + part
        r1 = 1.0 / jnp.sum(lsum[0], axis=-1, keepdims=True)
        r2 = lam / jnp.sum(lsum[1], axis=-1, keepdims=True)
        o = acc[0] * r1 - acc[1] * r2
        o = o * lax.rsqrt(jnp.mean(o * o, axis=-1, keepdims=True) + RMS_EPS)
        o_ref[a * tq:(a + 1) * tq, c0:c0 + D_HEAD_V] = (o * gain).astype(BF16)

    def query_tile(step):
        for h in range(N_HEADS):
            maxima = [logits_pass(h, a, step * qpb + a) for a in range(qpb)]
            for a in range(qpb):
                values_pass(h, a, step * qpb + a, maxima[a])

    lax.switch(pl.program_id(1), [functools.partial(query_tile, s) for s in range(nq // qpb)])


def _diff_attn(p, kt, band, lam_q, lam_k, subln_g, bsz, seq, tq, qpb, lam_init):
    n_tok = bsz * seq
    nq = seq // tq
    steps = nq // qpb
    kern = functools.partial(_diff_attn_kernel, tq=tq, nq=nq, qpb=qpb, lam_init=lam_init)
    return pl.pallas_call(
        kern,
        grid=(bsz, steps),
        in_specs=[
            pl.BlockSpec((qpb * tq, D_MODEL), lambda b, i: (b * steps + i, P_Q)),
            pl.BlockSpec((D_MODEL, seq), lambda b, i: (0, b)),
            pl.BlockSpec((seq, D_MODEL), lambda b, i: (b, P_V)),
            _const_spec((N_HEADS, tq, 2 * tq)),
            _const_spec((2, HEAD_DIM)),
            _const_spec((2, HEAD_DIM)),
            _const_spec((1, D_HEAD_V)),
        ],
        out_specs=pl.BlockSpec((qpb * tq, D_MODEL), lambda b, i: (b * steps + i, 0)),
        out_shape=jax.ShapeDtypeStruct((n_tok, D_MODEL), BF16),
        scratch_shapes=[
            pltpu.VMEM((N_HEADS, qpb, 2, nq, tq, tq), F32),
        ],
        compiler_params=pltpu.CompilerParams(
            dimension_semantics=("arbitrary", "arbitrary"), vmem_limit_bytes=VMEM_LIMIT),
        name="diff_attn",
    )(p, kt, p, band, lam_q, lam_k, subln_g)


SLAB_ROWS = D_MODEL // (2 * LANES)


def _store_slabs(ref, row0, x_bf):
    n = x_bf.shape[0]
    bits = pltpu.bitcast(x_bf.astype(F32), jnp.uint32)
    for c in range(SLAB_ROWS):
        lo = bits[:, 2 * c * LANES:(2 * c + 1) * LANES]
        hi = bits[:, (2 * c + 1) * LANES:(2 * c + 2) * LANES]
        ref[pl.ds(SLAB_ROWS * row0 + c, n, stride=SLAB_ROWS), :] = (lo >> 16) | hi


def _load_slabs(ref, row0, n, every=1):
    parts = []
    for c in range(SLAB_ROWS):
        words = ref[pl.ds(SLAB_ROWS * row0 + c, n, stride=SLAB_ROWS * every), :]
        parts.append(pltpu.bitcast(words << 16, F32))
        parts.append(pltpu.bitcast(words & jnp.uint32(0xFFFF0000), F32))
    return jnp.concatenate(parts, axis=1)


def _mix_kernel(on_ref, yb_ref, ga_ref, gb_ref, xn_ref, wa_ref, wb_ref, wo_ref, bg_ref,
                g1_ref, b1_ref, wr_ref, br_ref, x1_ref, xp_ref, lg_ref, *, tm, sub):
    bg = bg_ref[...]
    for lo in range(0, tm, sub):
        rows = slice(lo, lo + sub)
        y_b = jnp.dot(yb_ref[rows, :], wb_ref[...], preferred_element_type=F32)
        y_a = jnp.dot(on_ref[rows, :], wa_ref[...], preferred_element_type=F32)
        g_a = jax.nn.sigmoid(ga_ref[rows, :].astype(F32) + bg[0:1])
        g_b = jax.nn.sigmoid(gb_ref[rows, :].astype(F32) + bg[1:2])
        merged = (g_a * y_a + g_b * y_b).astype(BF16)
        mix = jnp.dot(merged, wo_ref[...], preferred_element_type=F32)
        x1 = _layer_norm(DEEPNORM_ALPHA * xn_ref[rows, :] + mix, g1_ref[...], b1_ref[...])
        x1_ref[rows, :] = x1
        x_hi = x1.astype(BF16)
        x_lo = (x1 - x_hi.astype(F32)).astype(BF16)
        part = (jnp.dot(x_hi, wr_ref[...], preferred_element_type=F32)
                + jnp.dot(x_lo, wr_ref[...], preferred_element_type=F32))
        lg_ref[rows, :] = part[:, 0:LANES] + part[:, LANES:2 * LANES] + br_ref[...]
        _store_slabs(xp_ref, lo, x_hi)


def _mix(o_n, p, xn, wa, wb, wo, b_gate, g1, b1, wr2, br, tm, sub):
    n_tok = xn.shape[0]
    kern = functools.partial(_mix_kernel, tm=tm, sub=sub)

    def col(c):
        return pl.BlockSpec((tm, D_MODEL), lambda r, c=c: (r, c))

    return pl.pallas_call(
        kern,
        grid=(n_tok // tm,),
        in_specs=[
            pl.BlockSpec((tm, D_MODEL), lambda r: (r, 0)),
            col(P_CONV), col(P_GATE_A), col(P_GATE_B),
            pl.BlockSpec((tm, D_MODEL), lambda r: (r, 0)),
            _const_spec((D_MODEL, D_MODEL)), _const_spec((D_MODEL, D_MODEL)),
            _const_spec((D_MODEL, D_MODEL)),
            _const_spec((2, D_MODEL)), _const_spec((1, D_MODEL)), _const_spec((1, D_MODEL)),
            _const_spec((D_MODEL, 2 * LANES)), _const_spec((1, LANES)),
        ],
        out_specs=[
            pl.BlockSpec((tm, D_MODEL), lambda r: (r, 0)),
            pl.BlockSpec((tm * SLAB_ROWS, LANES), lambda r: (r, 0)),
            pl.BlockSpec((tm, LANES), lambda r: (r, 0)),
        ],
        out_shape=[
            jax.ShapeDtypeStruct((n_tok, D_MODEL), F32),
            jax.ShapeDtypeStruct((n_tok * SLAB_ROWS, LANES), jnp.uint32),
            jax.ShapeDtypeStruct((n_tok, LANES), F32),
        ],
        compiler_params=pltpu.CompilerParams(
            dimension_semantics=("arbitrary",), vmem_limit_bytes=VMEM_LIMIT),
        name="mix",
    )(o_n, p, p, p, xn, wa, wb, wo, b_gate, g1, b1, wr2, br)


ROUTE_COLS = 8
EXPERT_LANE0 = 8


def _route_kernel(lg_ref, tri_ref, route_ref, route_t_ref, cnt_ref, run_ref, *, tr):
    @pl.when(pl.program_id(0) == 0)
    def _():
        run_ref[...] = jnp.zeros_like(run_ref)

    lt = jnp.transpose(lg_ref[...])
    row = lax.broadcasted_iota(jnp.int32, (EXPERTS_PER_GROUP, tr), 0).astype(F32)

    def first_argmax(vals, vmax):
        return jnp.min(jnp.where(vals == vmax, row, float(EXPERTS_PER_GROUP)),
                       axis=0, keepdims=True)

    gl = jnp.where(row < N_GROUPS, lt[0:EXPERTS_PER_GROUP], -jnp.inf)
    gmax = jnp.max(gl, axis=0, keepdims=True)
    gsum = jnp.sum(jnp.exp(gl - gmax), axis=0, keepdims=True)
    g_p = 1.0 / gsum
    g_idx = first_argmax(gl, gmax)

    def group_rows(g):
        lo = EXPERT_LANE0 + EXPERTS_PER_GROUP * g
        return lt[lo:lo + EXPERTS_PER_GROUP]

    sl = group_rows(N_GROUPS - 1)
    for g in reversed(range(N_GROUPS - 1)):
        sl = jnp.where(g_idx == g, group_rows(g), sl)
    s1 = jnp.max(sl, axis=0, keepdims=True)
    i1 = first_argmax(sl, s1)
    sl2 = jnp.where(row == i1, -jnp.inf, sl)
    s2 = jnp.max(sl2, axis=0, keepdims=True)
    i2 = first_argmax(sl2, s2)
    t = jnp.exp(s2 - s1)
    w1 = g_p / (1.0 + t)
    w2 = g_p * t / (1.0 + t)
    e1 = EXPERTS_PER_GROUP * g_idx + i1
    e2 = EXPERTS_PER_GROUP * g_idx + i2

    e_row = lax.broadcasted_iota(jnp.int32, (N_EXPERTS, tr), 0).astype(F32)
    oh1 = e_row == e1
    oh2 = e_row == e2
    onehot_t = jnp.concatenate([jnp.where(oh1 | oh2, 1.0, 0.0),
                                jnp.zeros((LANES - N_EXPERTS, tr), F32)], axis=0)
    onehot = jnp.transpose(onehot_t)
    before = jnp.dot(tri_ref[...], onehot.astype(BF16), preferred_element_type=F32) + run_ref[...]
    before_t = jnp.transpose(before)[0:N_EXPERTS]
    rank1 = jnp.sum(jnp.where(oh1, before_t, 0.0), axis=0, keepdims=True)
    rank2 = jnp.sum(jnp.where(oh2, before_t, 0.0), axis=0, keepdims=True)
    run_ref[...] = run_ref[...] + jnp.sum(onehot, axis=0, keepdims=True)
    cnt_ref[...] = jnp.broadcast_to(run_ref[...], cnt_ref.shape)

    zero = jnp.zeros((1, tr), F32)
    route_t = jnp.concatenate([e1, e2, w1, w2, rank1, rank2, zero, zero], axis=0)
    route_t_ref[...] = route_t
    padded = jnp.concatenate([route_t, jnp.zeros((LANES - ROUTE_COLS, tr), F32)], axis=0)
    route_ref[...] = jnp.transpose(padded)[:, 0:ROUTE_COLS]


def _route(logits, tr):
    n_tok = logits.shape[0]
    tri = jnp.asarray(np.tril(np.ones((tr, tr), np.float32), k=-1), BF16)
    return pl.pallas_call(
        functools.partial(_route_kernel, tr=tr),
        grid=(n_tok // tr,),
        in_specs=[pl.BlockSpec((tr, LANES), lambda r: (r, 0)), _const_spec((tr, tr))],
        out_specs=[
            pl.BlockSpec((tr, ROUTE_COLS), lambda r: (r, 0)),
            pl.BlockSpec((ROUTE_COLS, tr), lambda r: (0, r)),
            pl.BlockSpec((8, LANES), lambda r: (0, 0)),
        ],
        out_shape=[
            jax.ShapeDtypeStruct((n_tok, ROUTE_COLS), F32),
            jax.ShapeDtypeStruct((ROUTE_COLS, n_tok), F32),
            jax.ShapeDtypeStruct((8, LANES), F32),
        ],
        scratch_shapes=[pltpu.VMEM((1, LANES), F32)],
        compiler_params=pltpu.CompilerParams(
            dimension_semantics=("arbitrary",), vmem_limit_bytes=VMEM_LIMIT),
        name="route",
    )(logits, tri)


def _slab_copy(src, src_row, dst, dst_row, sem):
    return pltpu.make_async_copy(src.at[src_row], dst.at[dst_row], sem)


N_ZERO_BLOCKS = 2 * N_EXPERTS


def _dispatch_kernel(zb_ref, dest_ref, x_ref, out_ref, zeros, sem, zsem, *, td):
    @pl.when(pl.program_id(0) == 0)
    def _():
        zeros[...] = jnp.zeros_like(zeros)

        def block_copy(n):
            row0 = pl.multiple_of(zb_ref[n] * MOE_BLOCK, MOE_BLOCK)
            return pltpu.make_async_copy(zeros, out_ref.at[pl.ds(row0, MOE_BLOCK)], zsem)

        n_zero = zb_ref[N_ZERO_BLOCKS]

        def start(n, carry):
            block_copy(n).start()
            return carry

        def wait(n, carry):
            block_copy(n).wait()
            return carry

        lax.fori_loop(0, n_zero, start, 0)
        lax.fori_loop(0, n_zero, wait, 0)

    def issue(t, carry):
        for k in range(TOP_K):
            _slab_copy(x_ref, t, out_ref, dest_ref[0, 0, k * td + t], sem).start(priority=k)
        return carry

    lax.fori_loop(0, td, issue, 0, unroll=8)
    for k in range(TOP_K):
        pltpu.make_async_copy(x_ref, out_ref.at[pl.ds(0, td)], sem).wait()


def _dispatch(xp, dest3, zero_blocks, n_rows, td):
    n_tok = xp.shape[0]
    grid_spec = pltpu.PrefetchScalarGridSpec(
        num_scalar_prefetch=1,
        grid=(n_tok // td,),
        in_specs=[
            pl.BlockSpec((1, 1, TOP_K * td), lambda s, zb: (s, 0, 0), memory_space=pltpu.SMEM),
            pl.BlockSpec((td, SLAB_ROWS, LANES), lambda s, zb: (s, 0, 0)),
        ],
        out_specs=pl.BlockSpec(memory_space=pl.ANY),
        scratch_shapes=[
            pltpu.VMEM((MOE_BLOCK, SLAB_ROWS, LANES), jnp.uint32),
            pltpu.SemaphoreType.DMA(()),
            pltpu.SemaphoreType.DMA(()),
        ],
    )
    return pl.pallas_call(
        functools.partial(_dispatch_kernel, td=td),
        grid_spec=grid_spec,
        out_shape=jax.ShapeDtypeStruct((n_rows, SLAB_ROWS, LANES), jnp.uint32),
        compiler_params=pltpu.CompilerParams(dimension_semantics=("arbitrary",)),
        name="dispatch",
    )(zero_blocks, dest3, xp)


FFN_SUB = 512


def _expert_ffn_kernel(be_ref, na_ref, nx_ref, x_ref, wg_hbm, wu_hbm, wd_hbm, y_ref,
                       wg_f32, wu_f32, wd_f32, wg_bf, wu_bf, wd_bf, sem):
    i = pl.program_id(0)
    active = i < na_ref[0]
    expert = be_ref[i]
    new_expert = (i == 0) | (expert != be_ref[jnp.maximum(i - 1, 0)])

    def fetch(e):
        return (pltpu.make_async_copy(wg_hbm.at[e], wg_f32, sem.at[0]),
                pltpu.make_async_copy(wu_hbm.at[e], wu_f32, sem.at[1]),
                pltpu.make_async_copy(wd_hbm.at[e], wd_f32, sem.at[2]))

    @pl.when(active & (i == 0))
    def _():
        for cp in fetch(expert):
            cp.start()

    @pl.when(active & new_expert)
    def _():
        for cp in fetch(expert):
            cp.wait()
        wg_bf[...] = wg_f32[...].astype(BF16)
        wu_bf[...] = wu_f32[...].astype(BF16)
        wd_bf[...] = wd_f32[...].astype(BF16)
        nxt = nx_ref[expert]

        @pl.when(nxt >= 0)
        def _():
            for cp in fetch(nxt):
                cp.start()

    @pl.when(active)
    def _():
        for lo in range(0, MOE_BLOCK, FFN_SUB):
            x = _load_slabs(x_ref, lo, FFN_SUB).astype(BF16)
            g = jnp.dot(x, wg_bf[...], preferred_element_type=F32)
            u = jnp.dot(x, wu_bf[...], preferred_element_type=F32)
            hid = (jax.nn.silu(g) * u).astype(BF16)
            y = jnp.dot(hid, wd_bf[...], preferred_element_type=F32)
            _store_slabs(y_ref, lo, y.astype(BF16))

    @pl.when(jnp.logical_not(active))
    def _():
        y_ref[...] = jnp.zeros_like(y_ref)


def _expert_ffn(xs, block_expert, n_active, next_expert, wg, wu, wd):
    n_rows = xs.shape[0] // SLAB_ROWS
    n_blocks = n_rows // MOE_BLOCK

    def blk(i, be, na, nx):
        return (jnp.maximum(jnp.minimum(i, na[0] - 1), 0), 0)

    grid_spec = pltpu.PrefetchScalarGridSpec(
        num_scalar_prefetch=3,
        grid=(n_blocks,),
        in_specs=[
            pl.BlockSpec((MOE_BLOCK * SLAB_ROWS, LANES), blk),
            pl.BlockSpec(memory_space=pl.ANY),
            pl.BlockSpec(memory_space=pl.ANY),
            pl.BlockSpec(memory_space=pl.ANY),
        ],
        out_specs=pl.BlockSpec((MOE_BLOCK * SLAB_ROWS, LANES), lambda i, be, na, nx: (i, 0)),
        scratch_shapes=[
            pltpu.VMEM((D_MODEL, D_EXPERT), F32),
            pltpu.VMEM((D_MODEL, D_EXPERT), F32),
            pltpu.VMEM((D_EXPERT, D_MODEL), F32),
            pltpu.VMEM((D_MODEL, D_EXPERT), BF16),
            pltpu.VMEM((D_MODEL, D_EXPERT), BF16),
            pltpu.VMEM((D_EXPERT, D_MODEL), BF16),
            pltpu.SemaphoreType.DMA((3,)),
        ],
    )
    return pl.pallas_call(
        _expert_ffn_kernel,
        grid_spec=grid_spec,
        out_shape=jax.ShapeDtypeStruct((n_rows * SLAB_ROWS, LANES), jnp.uint32),
        compiler_params=pltpu.CompilerParams(
            dimension_semantics=("arbitrary",), vmem_limit_bytes=VMEM_LIMIT),
        name="expert_ffn",
    )(block_expert, n_active, next_expert, xs, wg, wu, wd)


def _combine_kernel(dcur_ref, dnxt_ref, x_ref, route_ref, g_ref, b_ref, y_ref, o_ref,
                    buf, sem, *, td):
    s = pl.program_id(0)
    n = pl.num_programs(0)
    slot = s % 2

    def base(slot_):
        return pl.multiple_of(slot_ * TOP_K * td, TOP_K * td)

    def issue(d_ref, to_slot):
        def body(t, carry):
            for k in range(TOP_K):
                _slab_copy(y_ref, d_ref[0, 0, k * td + t], buf, base(to_slot) + TOP_K * t + k,
                           sem.at[to_slot]).start(priority=k)
            return carry
        lax.fori_loop(0, td, body, 0, unroll=8)

    @pl.when(s == 0)
    def _():
        issue(dcur_ref, 0)

    @pl.when(s + 1 < n)
    def _():
        issue(dnxt_ref, 1 - slot)

    pltpu.make_async_copy(y_ref.at[pl.ds(0, TOP_K * td)], buf.at[pl.ds(base(slot), TOP_K * td)],
                          sem.at[slot]).wait()

    route = route_ref[...]
    buf2 = buf.reshape(2 * TOP_K * td * SLAB_ROWS, LANES)
    ffn = (route[:, 2:3] * _load_slabs(buf2, base(slot), td, every=TOP_K)
           + route[:, 3:4] * _load_slabs(buf2, base(slot) + 1, td, every=TOP_K))
    o_ref[...] = _layer_norm(DEEPNORM_ALPHA * x_ref[...] + ffn, g_ref[...], b_ref[...])


def _combine(ys, dest3, x1, route, g2, b2, td):
    n_tok = x1.shape[0]
    n_steps = n_tok // td
    return pl.pallas_call(
        functools.partial(_combine_kernel, td=td),
        grid=(n_steps,),
        in_specs=[
            pl.BlockSpec((1, 1, TOP_K * td), lambda s: (s, 0, 0), memory_space=pltpu.SMEM),
            pl.BlockSpec((1, 1, TOP_K * td), lambda s: (jnp.minimum(s + 1, n_steps - 1), 0, 0),
                         memory_space=pltpu.SMEM),
            pl.BlockSpec((td, D_MODEL), lambda s: (s, 0)),
            pl.BlockSpec((td, ROUTE_COLS), lambda s: (s, 0)),
            pl.BlockSpec((1, D_MODEL), lambda s: (0, 0)),
            pl.BlockSpec((1, D_MODEL), lambda s: (0, 0)),
            pl.BlockSpec(memory_space=pl.ANY),
        ],
        out_specs=pl.BlockSpec((td, D_MODEL), lambda s: (s, 0)),
        out_shape=jax.ShapeDtypeStruct((n_tok, D_MODEL), F32),
        scratch_shapes=[
            pltpu.VMEM((2 * TOP_K * td, SLAB_ROWS, LANES), jnp.uint32),
            pltpu.SemaphoreType.DMA((2,)),
        ],
        compiler_params=pltpu.CompilerParams(
            dimension_semantics=("arbitrary",), vmem_limit_bytes=VMEM_LIMIT),
        name="combine",
    )(dest3, dest3, x1, route, g2, b2, ys)


def _row_tile(n, want):
    t = min(want, n)
    while n % t:
        t //= 2
    return t


def kernel(x, ln_in_g, ln_in_b, w_in, b_gate, lambda_q, lambda_k, subln_g, rel_bias, conv_w,
           w_a_proj, w_b_proj, w_o, ln1_g, ln1_b, w_group, b_group, w_sub, b_sub,
           w_gate_e, w_up_e, w_down_e, ln2_g, ln2_b):
    bsz, seq, d = x.shape
    assert DEPTH == 1 and d == D_MODEL and w_in.shape == (DEPTH, D_MODEL, N_IN)
    n_tok = bsz * seq
    tq = _row_tile(seq, 256)
    assert tq % LANES == 0 and tq % CHUNK == 0
    qpb = 1
    tm1 = _row_tile(seq, 512)
    tm3 = _row_tile(n_tok, 1024)
    sub3 = _row_tile(tm3, 256)
    tr = _row_tile(n_tok, 1024)
    td = _row_tile(n_tok, 1024)
    tc = _row_tile(n_tok, 512)
    row = lambda v: v.reshape(1, -1).astype(F32)
    lam_init = 0.8 - 0.6 * math.exp(-0.3 * 0)

    wkt = jnp.transpose(w_in[0, :, D_MODEL:2 * D_MODEL]).astype(BF16)
    xn, p, kt = _ln_proj(x.reshape(n_tok, d), row(ln_in_g), row(ln_in_b), w_in[0].astype(BF16),
                         wkt, conv_w[0].reshape(3, d).astype(F32), seq, tm1)

    band = _band_bias(rel_bias, tq)
    o_n = _diff_attn(p, kt, band, lambda_q[0].astype(F32), lambda_k[0].astype(F32),
                     row(subln_g[0]), bsz, seq, tq, qpb, lam_init)

    gap = EXPERT_LANE0 - N_GROUPS
    w_r = jnp.concatenate(
        [w_group[0].astype(F32), jnp.zeros((d, gap), F32),
         jnp.transpose(w_sub[0].astype(F32), (1, 0, 2)).reshape(d, N_EXPERTS)], axis=1)
    w_r = jnp.pad(w_r, ((0, 0), (0, LANES - w_r.shape[1])))
    w_hi = w_r.astype(BF16)
    w_lo = (w_r - w_hi.astype(F32)).astype(BF16)
    w_r2 = jnp.concatenate([w_hi, w_lo], axis=1)
    b_r = jnp.concatenate([b_group[0].astype(F32), jnp.zeros((gap,), F32),
                           b_sub[0].astype(F32).reshape(-1)])
    b_r = jnp.pad(b_r, (0, LANES - b_r.shape[0])).reshape(1, LANES)
    x1, xp, logits = _mix(
        o_n, p, xn, w_a_proj[0].astype(BF16), w_b_proj[0].astype(BF16), w_o[0].astype(BF16),
        b_gate[0].astype(F32), row(ln1_g[0]), row(ln1_b[0]), w_r2, b_r, tm3, sub3)
    route, route_t, counts = _route(logits, tr)

    n_assign = n_tok * TOP_K
    n_blocks = -(-n_assign // MOE_BLOCK) + N_EXPERTS
    cnt = counts[0, :N_EXPERTS].astype(jnp.int32)
    padded = ((cnt + MOE_BLOCK - 1) // MOE_BLOCK) * MOE_BLOCK
    pad_end = jnp.cumsum(padded)
    pad_start = pad_end - padded
    n_active = (pad_end[-1:] // MOE_BLOCK).astype(jnp.int32)
    blk_row0 = jnp.arange(n_blocks, dtype=jnp.int32) * MOE_BLOCK
    block_expert = jnp.minimum(
        jnp.sum((pad_end[None, :] <= blk_row0[:, None]).astype(jnp.int32), axis=1),
        N_EXPERTS - 1).astype(jnp.int32)
    experts = route_t[0:TOP_K].astype(jnp.int32)
    ranks = route_t[4:4 + TOP_K].astype(jnp.int32)
    is_e = experts[None] == jnp.arange(N_EXPERTS, dtype=jnp.int32)[:, None, None]
    dest = jnp.sum(jnp.where(is_e, pad_start[:, None, None], 0), axis=0) + ranks

    def per_tile(t):
        tiles = jnp.transpose(dest.reshape(TOP_K, n_tok // t, t), (1, 0, 2))
        return tiles.reshape(n_tok // t, 1, TOP_K * t).astype(jnp.int32)

    last_blk = jnp.maximum(pad_end // MOE_BLOCK - 1, 0)
    idle_blk = jnp.minimum(n_active[0] + jnp.arange(N_EXPERTS), n_blocks - 1)
    n_zero = N_EXPERTS + n_blocks - n_active
    zero_blocks = jnp.concatenate([last_blk, idle_blk, n_zero]).astype(jnp.int32)
    n_rows = n_blocks * MOE_BLOCK
    xs = _dispatch(xp.reshape(n_tok, SLAB_ROWS, LANES), per_tile(td), zero_blocks, n_rows, td)
    e_ids = jnp.arange(N_EXPERTS, dtype=jnp.int32)
    later = (padded > 0)[None, :] & (e_ids[None, :] > e_ids[:, None])
    next_expert = jnp.min(jnp.where(later, e_ids[None, :], N_EXPERTS), axis=1)
    next_expert = jnp.where(next_expert < N_EXPERTS, next_expert, -1).astype(jnp.int32)
    ys = _expert_ffn(xs.reshape(n_rows * SLAB_ROWS, LANES), block_expert, n_active, next_expert,
                     w_gate_e[0].astype(F32), w_up_e[0].astype(F32), w_down_e[0].astype(F32))
    out = _combine(ys.reshape(n_rows, SLAB_ROWS, LANES), per_tile(tc), x1, route,
                   row(ln2_g[0]), row(ln2_b[0]), tc)
    return out.reshape(bsz, seq, d)
```

```python
import functools
import math

import numpy as np
import jax
import jax.numpy as jnp
from jax import lax
from jax.experimental import pallas as pl
from jax.experimental.pallas import tpu as pltpu

F32 = jnp.float32
BF16 = jnp.bfloat16

D_MODEL = 1024
N_HEADS = 4
HEAD_DIM = 128
D_HEAD_V = 2 * HEAD_DIM
CHUNK = 64
N_BUCKETS = 32
MAX_DISTANCE = 128
N_GROUPS = 4
EXPERTS_PER_GROUP = 8
N_EXPERTS = N_GROUPS * EXPERTS_PER_GROUP
TOP_K = 2
D_EXPERT = 512
MOE_BLOCK = 512
LN_EPS = 1e-5
RMS_EPS = 1e-6
DEPTH = 1
DEEPNORM_ALPHA = (2.0 * DEPTH) ** 0.25
N_IN = 8 * D_MODEL
LANES = 128
NEG_BIG = -1e30
LOG2_E = math.log2(math.e)

VMEM_LIMIT = 56 * 1024 * 1024


def _layer_norm(x, g, b):
    mu = jnp.mean(x, axis=-1, keepdims=True)
    xc = x - mu
    var = jnp.mean(xc * xc, axis=-1, keepdims=True)
    return xc * lax.rsqrt(var + LN_EPS) * g + b


def _const_spec(shape):
    return pl.BlockSpec(shape, lambda *_: (0,) * len(shape), pipeline_mode=pl.Buffered(1))


HALO = 16
P_Q, P_V, P_CONV, P_GATE_A, P_GATE_B = range(5)
N_P = 5 * D_MODEL


def _ln_proj_kernel(x_ref, g_ref, b_ref, w_ref, wkt_ref, cw_ref, xn_ref, p_ref, kt_ref, zs_ref,
                    *, q_scale, tm, sub, tiles_per_seq):
    r = pl.program_id(0)

    @pl.when((r % tiles_per_seq) == 0)
    def _():
        zs_ref[0:HALO, :] = jnp.zeros((HALO, D_MODEL), F32)

    cw = cw_ref[...]
    for lo in range(0, tm, sub):
        rows = slice(lo, lo + sub)
        xn = _layer_norm(x_ref[rows, :], g_ref[...], b_ref[...])
        xn_ref[rows, :] = xn
        xb = xn.astype(BF16)

        def proj(c, xb=xb):
            return jnp.dot(xb, w_ref[:, c * D_MODEL:(c + 1) * D_MODEL],
                           preferred_element_type=F32)

        def put(block, val, rows=rows):
            p_ref[rows, block * D_MODEL:(block + 1) * D_MODEL] = val.astype(BF16)

        put(P_Q, proj(0) * q_scale)
        put(P_V, proj(2))
        kt = lax.dot_general(wkt_ref[...], xb, (((1,), (1,)), ((), ())),
                             preferred_element_type=F32)
        kt_ref[:, rows] = kt.astype(BF16)
        z0 = HALO + lo
        zs_ref[z0:z0 + sub, :] = proj(4) * proj(5)
        zc = (cw[0:1] * zs_ref[z0 - 2:z0 - 2 + sub, :] + cw[1:2] * zs_ref[z0 - 1:z0 - 1 + sub, :]
              + cw[2:3] * zs_ref[z0:z0 + sub, :])
        put(P_CONV, proj(3) * zc)
        put(P_GATE_A, proj(6))
        put(P_GATE_B, proj(7))

    zs_ref[0:HALO, :] = zs_ref[tm:tm + HALO, :]


def _ln_proj(x2, g, b, w_bf, wkt, conv_w, seq, tm):
    n_tok = x2.shape[0]
    kern = functools.partial(_ln_proj_kernel, q_scale=HEAD_DIM ** -0.5 * LOG2_E, tm=tm,
                             sub=_row_tile(tm, 256), tiles_per_seq=seq // tm)
    return pl.pallas_call(
        kern,
        grid=(n_tok // tm,),
        in_specs=[
            pl.BlockSpec((tm, D_MODEL), lambda r: (r, 0)),
            _const_spec((1, D_MODEL)),
            _const_spec((1, D_MODEL)),
            _const_spec((D_MODEL, N_IN)),
            _const_spec((D_MODEL, D_MODEL)),
            _const_spec((3, D_MODEL)),
        ],
        out_specs=[
            pl.BlockSpec((tm, D_MODEL), lambda r: (r, 0)),
            pl.BlockSpec((tm, N_P), lambda r: (r, 0)),
            pl.BlockSpec((D_MODEL, tm), lambda r: (0, r)),
        ],
        out_shape=[
            jax.ShapeDtypeStruct((n_tok, D_MODEL), F32),
            jax.ShapeDtypeStruct((n_tok, N_P), BF16),
            jax.ShapeDtypeStruct((D_MODEL, n_tok), BF16),
        ],
        scratch_shapes=[pltpu.VMEM((HALO + tm, D_MODEL), F32)],
        compiler_params=pltpu.CompilerParams(
            dimension_semantics=("arbitrary",), vmem_limit_bytes=VMEM_LIMIT),
        name="ln_proj",
    )(x2, g, b, w_bf, wkt, conv_w)


def _t5_bucket_np(rel):
    nb = N_BUCKETS // 2
    max_exact = nb // 2
    n = np.abs(rel)
    large = np.full(n.shape, max_exact, np.int64)
    for d in range(max_exact, MAX_DISTANCE + 1):
        val = max_exact + int(math.log(d / max_exact) / math.log(MAX_DISTANCE / max_exact)
                              * (nb - max_exact))
        large = np.where(n >= d, min(val, nb - 1), large)
    return np.where(rel > 0, nb, 0) + np.where(n < max_exact, n, large)


def _band_bias(rel_bias, tq):
    qi = np.arange(tq)[:, None]
    kj = np.arange(2 * tq)[None, :] - tq
    allowed = (kj // CHUNK) <= (qi // CHUNK)
    far_bucket = int(_t5_bucket_np(np.array([-(tq + 1)]))[0])
    rb = rel_bias.astype(F32)
    rb = ((rb - rb[far_bucket][None, :]) * LOG2_E).T
    n_rel = 3 * tq
    bucket = _t5_bucket_np(np.arange(n_rel) - (2 * tq - 1))
    onehot = jnp.asarray(bucket[:, None] == np.arange(N_BUCKETS)[None, :])
    per_rel = jnp.sum(jnp.where(onehot[None], rb[:, None, :], 0.0), axis=-1)
    skew = jnp.tile(per_rel, (1, tq))[:, :tq * (n_rel - 1)].reshape(N_HEADS, tq, n_rel - 1)
    tile = skew[:, :, tq - 1:3 * tq - 1]
    return jnp.where(jnp.asarray(allowed)[None], tile, NEG_BIG)


QK_BLOCKS = 2
PV_BLOCKS = 2


def _diff_attn_kernel(q_ref, kt_ref, v_ref, bias_ref, lq_ref, lk_ref, sg_ref, o_ref, s_ref,
                      *, tq, nq, qpb, lam_init):
    dots = jnp.sum(lq_ref[...] * lk_ref[...], axis=-1, keepdims=True)
    lam = jnp.exp(dots[0:1]) - jnp.exp(dots[1:2]) + lam_init
    gain = sg_ref[...] * (1.0 - lam_init)

    def lane_tiles(s):
        return [s[:, c * LANES:(c + 1) * LANES] for c in range(s.shape[1] // LANES)]

    def logits_pass(h, a, i):
        rows = slice(a * tq, (a + 1) * tq)
        row_max = []
        for m in range(2):
            c = h * D_HEAD_V + m * HEAD_DIM
            q = q_ref[rows, c:c + HEAD_DIM]
            mx = None
            for j0 in range(0, i + 1, QK_BLOCKS):
                nb = min(QK_BLOCKS, i + 1 - j0)
                wide = jnp.dot(q, kt_ref[c:c + HEAD_DIM, j0 * tq:(j0 + nb) * tq],
                               preferred_element_type=F32)
                for j in range(j0, j0 + nb):
                    s = wide[:, (j - j0) * tq:(j - j0 + 1) * tq]
                    if j == i:
                        s = s + bias_ref[h, :, tq:2 * tq]
                    elif j == i - 1:
                        s = s + bias_ref[h, :, 0:tq]
                    s_ref[h, a, m, j] = s
                    t = functools.reduce(jnp.maximum, lane_tiles(s))
                    mx = t if mx is None else jnp.maximum(mx, t)
            row_max.append(jnp.broadcast_to(jnp.max(mx, axis=-1, keepdims=True), (tq, LANES)))
        return row_max

    def values_pass(h, a, i, row_max):
        c0 = h * D_HEAD_V
        acc = [None, None]
        lsum = [None, None]
        for j0 in range(0, i + 1, PV_BLOCKS):
            nb = min(PV_BLOCKS, i + 1 - j0)
            v_blk = v_ref[j0 * tq:(j0 + nb) * tq, c0:c0 + D_HEAD_V]
            for m in range(2):
                tiles = []
                for j in range(j0, j0 + nb):
                    tiles += [jnp.exp2(x - row_max[m]) for x in lane_tiles(s_ref[h, a, m, j])]
                tsum = functools.reduce(jnp.add, tiles)
                lsum[m] = tsum if lsum[m] is None else lsum[m] + tsum
                p = jnp.concatenate(tiles, axis=1).astype(BF16)
                part = jnp.dot(p, v_blk, preferred_element_type=F32)
                acc[m] = part if acc[m] is None else acc[m] + part
        r1 = 1.0 / jnp.sum(lsum[0], axis=-1, keepdims=True)
        r2 = lam / jnp.sum(lsum[1], axis=-1, keepdims=True)
        o = acc[0] * r1 - acc[1] * r2
        o = o * lax.rsqrt(jnp.mean(o * o, axis=-1, keepdims=True) + RMS_EPS)
        o_ref[a * tq:(a + 1) * tq, c0:c0 + D_HEAD_V] = (o * gain).astype(BF16)

    def query_tile(step):
        for h in range(N_HEADS):
            maxima = [logits_pass(h, a, step * qpb + a) for a in range(qpb)]
            for a in range(qpb):
                values_pass(h, a, step * qpb + a, maxima[a])

    lax.switch(pl.program_id(1), [functools.partial(query_tile, s) for s in range(nq // qpb)])


def _diff_attn(p, kt, band, lam_q, lam_k, subln_g, bsz, seq, tq, qpb, lam_init):
    n_tok = bsz * seq
    nq = seq // tq
    steps = nq // qpb
    kern = functools.partial(_diff_attn_kernel, tq=tq, nq=nq, qpb=qpb, lam_init=lam_init)
    return pl.pallas_call(
        kern,
        grid=(bsz, steps),
        in_specs=[
            pl.BlockSpec((qpb * tq, D_MODEL), lambda b, i: (b * steps + i, P_Q)),
            pl.BlockSpec((D_MODEL, seq), lambda b, i: (0, b)),
            pl.BlockSpec((seq, D_MODEL), lambda b, i: (b, P_V)),
            _const_spec((N_HEADS, tq, 2 * tq)),
            _const_spec((2, HEAD_DIM)),
            _const_spec((2, HEAD_DIM)),
            _const_spec((1, D_HEAD_V)),
        ],
        out_specs=pl.BlockSpec((qpb * tq, D_MODEL), lambda b, i: (b * steps + i, 0)),
        out_shape=jax.ShapeDtypeStruct((n_tok, D_MODEL), BF16),
        scratch_shapes=[
            pltpu.VMEM((N_HEADS, qpb, 2, nq, tq, tq), F32),
        ],
        compiler_params=pltpu.CompilerParams(
            dimension_semantics=("arbitrary", "arbitrary"), vmem_limit_bytes=VMEM_LIMIT),
        name="diff_attn",
    )(p, kt, p, band, lam_q, lam_k, subln_g)


SLAB_ROWS = D_MODEL // (2 * LANES)


def _store_slabs(ref, row0, x_bf):
    n = x_bf.shape[0]
    bits = pltpu.bitcast(x_bf.astype(F32), jnp.uint32)
    for c in range(SLAB_ROWS):
        lo = bits[:, 2 * c * LANES:(2 * c + 1) * LANES]
        hi = bits[:, (2 * c + 1) * LANES:(2 * c + 2) * LANES]
        ref[pl.ds(SLAB_ROWS * row0 + c, n, stride=SLAB_ROWS), :] = (lo >> 16) | hi


def _load_slabs(ref, row0, n, every=1):
    parts = []
    for c in range(SLAB_ROWS):
        words = ref[pl.ds(SLAB_ROWS * row0 + c, n, stride=SLAB_ROWS * every), :]
        parts.append(pltpu.bitcast(words << 16, F32))
        parts.append(pltpu.bitcast(words & jnp.uint32(0xFFFF0000), F32))
    return jnp.concatenate(parts, axis=1)


def _mix_kernel(on_ref, yb_ref, ga_ref, gb_ref, xn_ref, wa_ref, wb_ref, wo_ref, bg_ref,
                g1_ref, b1_ref, wr_ref, br_ref, x1_ref, xp_ref, lg_ref, *, tm, sub):
    bg = bg_ref[...]
    for lo in range(0, tm, sub):
        rows = slice(lo, lo + sub)
        y_b = jnp.dot(yb_ref[rows, :], wb_ref[...], preferred_element_type=F32)
        y_a = jnp.dot(on_ref[rows, :], wa_ref[...], preferred_element_type=F32)
        g_a = jax.nn.sigmoid(ga_ref[rows, :].astype(F32) + bg[0:1])
        g_b = jax.nn.sigmoid(gb_ref[rows, :].astype(F32) + bg[1:2])
        merged = (g_a * y_a + g_b * y_b).astype(BF16)
        mix = jnp.dot(merged, wo_ref[...], preferred_element_type=F32)
        x1 = _layer_norm(DEEPNORM_ALPHA * xn_ref[rows, :] + mix, g1_ref[...], b1_ref[...])
        x1_ref[rows, :] = x1
        x_hi = x1.astype(BF16)
        x_lo = (x1 - x_hi.astype(F32)).astype(BF16)
        part = (jnp.dot(x_hi, wr_ref[...], preferred_element_type=F32)
                + jnp.dot(x_lo, wr_ref[...], preferred_element_type=F32))
        lg_ref[rows, :] = part[:, 0:LANES] + part[:, LANES:2 * LANES] + br_ref[...]
        _store_slabs(xp_ref, lo, x_hi)


def _mix(o_n, p, xn, wa, wb, wo, b_gate, g1, b1, wr2, br, tm, sub):
    n_tok = xn.shape[0]
    kern = functools.partial(_mix_kernel, tm=tm, sub=sub)

    def col(c):
        return pl.BlockSpec((tm, D_MODEL), lambda r, c=c: (r, c))

    return pl.pallas_call(
        kern,
        grid=(n_tok // tm,),
        in_specs=[
            pl.BlockSpec((tm, D_MODEL), lambda r: (r, 0)),
            col(P_CONV), col(P_GATE_A), col(P_GATE_B),
            pl.BlockSpec((tm, D_MODEL), lambda r: (r, 0)),
            _const_spec((D_MODEL, D_MODEL)), _const_spec((D_MODEL, D_MODEL)),
            _const_spec((D_MODEL, D_MODEL)),
            _const_spec((2, D_MODEL)), _const_spec((1, D_MODEL)), _const_spec((1, D_MODEL)),
            _const_spec((D_MODEL, 2 * LANES)), _const_spec((1, LANES)),
        ],
        out_specs=[
            pl.BlockSpec((tm, D_MODEL), lambda r: (r, 0)),
            pl.BlockSpec((tm * SLAB_ROWS, LANES), lambda r: (r, 0)),
            pl.BlockSpec((tm, LANES), lambda r: (r, 0)),
        ],
        out_shape=[
            jax.ShapeDtypeStruct((n_tok, D_MODEL), F32),
            jax.ShapeDtypeStruct((n_tok * SLAB_ROWS, LANES), jnp.uint32),
            jax.ShapeDtypeStruct((n_tok, LANES), F32),
        ],
        compiler_params=pltpu.CompilerParams(
            dimension_semantics=("arbitrary",), vmem_limit_bytes=VMEM_LIMIT),
        name="mix",
    )(o_n, p, p, p, xn, wa, wb, wo, b_gate, g1, b1, wr2, br)


ROUTE_COLS = 8
EXPERT_LANE0 = 8


def _route_kernel(lg_ref, tri_ref, route_ref, route_t_ref, cnt_ref, run_ref, *, tr):
    @pl.when(pl.program_id(0) == 0)
    def _():
        run_ref[...] = jnp.zeros_like(run_ref)

    lt = jnp.transpose(lg_ref[...])
    row = lax.broadcasted_iota(jnp.int32, (EXPERTS_PER_GROUP, tr), 0).astype(F32)

    def first_argmax(vals, vmax):
        return jnp.min(jnp.where(vals == vmax, row, float(EXPERTS_PER_GROUP)),
                       axis=0, keepdims=True)

    gl = jnp.where(row < N_GROUPS, lt[0:EXPERTS_PER_GROUP], -jnp.inf)
    gmax = jnp.max(gl, axis=0, keepdims=True)
    gsum = jnp.sum(jnp.exp(gl - gmax), axis=0, keepdims=True)
    g_p = 1.0 / gsum
    g_idx = first_argmax(gl, gmax)

    def group_rows(g):
        lo = EXPERT_LANE0 + EXPERTS_PER_GROUP * g
        return lt[lo:lo + EXPERTS_PER_GROUP]

    sl = group_rows(N_GROUPS - 1)
    for g in reversed(range(N_GROUPS - 1)):
        sl = jnp.where(g_idx == g, group_rows(g), sl)
    s1 = jnp.max(sl, axis=0, keepdims=True)
    i1 = first_argmax(sl, s1)
    sl2 = jnp.where(row == i1, -jnp.inf, sl)
    s2 = jnp.max(sl2, axis=0, keepdims=True)
    i2 = first_argmax(sl2, s2)
    t = jnp.exp(s2 - s1)
    w1 = g_p / (1.0 + t)
    w2 = g_p * t / (1.0 + t)
    e1 = EXPERTS_PER_GROUP * g_idx + i1
    e2 = EXPERTS_PER_GROUP * g_idx + i2

    e_row = lax.broadcasted_iota(jnp.int32, (N_EXPERTS, tr), 0).astype(F32)
    oh1 = e_row == e1
    oh2 = e_row == e2
    onehot_t = jnp.concatenate([jnp.where(oh1 | oh2, 1.0, 0.0),
                                jnp.zeros((LANES - N_EXPERTS, tr), F32)], axis=0)
    onehot = jnp.transpose(onehot_t)
    before = jnp.dot(tri_ref[...], onehot.astype(BF16), preferred_element_type=F32) + run_ref[...]
    before_t = jnp.transpose(before)[0:N_EXPERTS]
    rank1 = jnp.sum(jnp.where(oh1, before_t, 0.0), axis=0, keepdims=True)
    rank2 = jnp.sum(jnp.where(oh2, before_t, 0.0), axis=0, keepdims=True)
    run_ref[...] = run_ref[...] + jnp.sum(onehot, axis=0, keepdims=True)
    cnt_ref[...] = jnp.broadcast_to(run_ref[...], cnt_ref.shape)

    zero = jnp.zeros((1, tr), F32)
    route_t = jnp.concatenate([e1, e2, w1, w2, rank1, rank2, zero, zero], axis=0)
    route_t_ref[...] = route_t
    padded = jnp.concatenate([route_t, jnp.zeros((LANES - ROUTE_COLS, tr), F32)], axis=0)
    route_ref[...] = jnp.transpose(padded)[:, 0:ROUTE_COLS]


def _route(logits, tr):
    n_tok = logits.shape[0]
    tri = jnp.asarray(np.tril(np.ones((tr, tr), np.float32), k=-1), BF16)
    return pl.pallas_call(
        functools.partial(_route_kernel, tr=tr),
        grid=(n_tok // tr,),
        in_specs=[pl.BlockSpec((tr, LANES), lambda r: (r, 0)), _const_spec((tr, tr))],
        out_specs=[
            pl.BlockSpec((tr, ROUTE_COLS), lambda r: (r, 0)),
            pl.BlockSpec((ROUTE_COLS, tr), lambda r: (0, r)),
            pl.BlockSpec((8, LANES), lambda r: (0, 0)),
        ],
        out_shape=[
            jax.ShapeDtypeStruct((n_tok, ROUTE_COLS), F32),
            jax.ShapeDtypeStruct((ROUTE_COLS, n_tok), F32),
            jax.ShapeDtypeStruct((8, LANES), F32),
        ],
        scratch_shapes=[pltpu.VMEM((1, LANES), F32)],
        compiler_params=pltpu.CompilerParams(
            dimension_semantics=("arbitrary",), vmem_limit_bytes=VMEM_LIMIT),
        name="route",
    )(logits, tri)


def _slab_copy(src, src_row, dst, dst_row, sem):
    return pltpu.make_async_copy(src.at[src_row], dst.at[dst_row], sem)


N_ZERO_BLOCKS = 2 * N_EXPERTS


def _dispatch_kernel(zb_ref, dest_ref, x_ref, out_ref, zeros, sem, zsem, *, td):
    @pl.when(pl.program_id(0) == 0)
    def _():
        zeros[...] = jnp.zeros_like(zeros)

        def block_copy(n):
            row0 = pl.multiple_of(zb_ref[n] * MOE_BLOCK, MOE_BLOCK)
            return pltpu.make_async_copy(zeros, out_ref.at[pl.ds(row0, MOE_BLOCK)], zsem)

        n_zero = zb_ref[N_ZERO_BLOCKS]

        def start(n, carry):
            block_copy(n).start()
            return carry

        def wait(n, carry):
            block_copy(n).wait()
            return carry

        lax.fori_loop(0, n_zero, start, 0)
        lax.fori_loop(0, n_zero, wait, 0)

    def issue(t, carry):
        for k in range(TOP_K):
            _slab_copy(x_ref, t, out_ref, dest_ref[0, 0, k * td + t], sem).start(priority=k)
        return carry

    lax.fori_loop(0, td, issue, 0, unroll=8)
    for k in range(TOP_K):
        pltpu.make_async_copy(x_ref, out_ref.at[pl.ds(0, td)], sem).wait()


def _dispatch(xp, dest3, zero_blocks, n_rows, td):
    n_tok = xp.shape[0]
    grid_spec = pltpu.PrefetchScalarGridSpec(
        num_scalar_prefetch=1,
        grid=(n_tok // td,),
        in_specs=[
            pl.BlockSpec((1, 1, TOP_K * td), lambda s, zb: (s, 0, 0), memory_space=pltpu.SMEM),
            pl.BlockSpec((td, SLAB_ROWS, LANES), lambda s, zb: (s, 0, 0)),
        ],
        out_specs=pl.BlockSpec(memory_space=pl.ANY),
        scratch_shapes=[
            pltpu.VMEM((MOE_BLOCK, SLAB_ROWS, LANES), jnp.uint32),
            pltpu.SemaphoreType.DMA(()),
            pltpu.SemaphoreType.DMA(()),
        ],
    )
    return pl.pallas_call(
        functools.partial(_dispatch_kernel, td=td),
        grid_spec=grid_spec,
        out_shape=jax.ShapeDtypeStruct((n_rows, SLAB_ROWS, LANES), jnp.uint32),
        compiler_params=pltpu.CompilerParams(dimension_semantics=("arbitrary",)),
        name="dispatch",
    )(zero_blocks, dest3, xp)


FFN_SUB = 512


def _expert_ffn_kernel(be_ref, na_ref, nx_ref, x_ref, wg_hbm, wu_hbm, wd_hbm, y_ref,
                       wg_f32, wu_f32, wd_f32, wg_bf, wu_bf, wd_bf, sem):
    i = pl.program_id(0)
    active = i < na_ref[0]
    expert = be_ref[i]
    new_expert = (i == 0) | (expert != be_ref[jnp.maximum(i - 1, 0)])

    def fetch(e):
        return (pltpu.make_async_copy(wg_hbm.at[e], wg_f32, sem.at[0]),
                pltpu.make_async_copy(wu_hbm.at[e], wu_f32, sem.at[1]),
                pltpu.make_async_copy(wd_hbm.at[e], wd_f32, sem.at[2]))

    @pl.when(active & (i == 0))
    def _():
        for cp in fetch(expert):
            cp.start()

    @pl.when(active & new_expert)
    def _():
        for cp in fetch(expert):
            cp.wait()
        wg_bf[...] = wg_f32[...].astype(BF16)
        wu_bf[...] = wu_f32[...].astype(BF16)
        wd_bf[...] = wd_f32[...].astype(BF16)
        nxt = nx_ref[expert]

        @pl.when(nxt >= 0)
        def _():
            for cp in fetch(nxt):
                cp.start()

    @pl.when(active)
    def _():
        for lo in range(0, MOE_BLOCK, FFN_SUB):
            x = _load_slabs(x_ref, lo, FFN_SUB).astype(BF16)
            g = jnp.dot(x, wg_bf[...], preferred_element_type=F32)
            u = jnp.dot(x, wu_bf[...], preferred_element_type=F32)
            hid = (jax.nn.silu(g) * u).astype(BF16)
            y = jnp.dot(hid, wd_bf[...], preferred_element_type=F32)
            _store_slabs(y_ref, lo, y.astype(BF16))

    @pl.when(jnp.logical_not(active))
    def _():
        y_ref[...] = jnp.zeros_like(y_ref)


def _expert_ffn(xs, block_expert, n_active, next_expert, wg, wu, wd):
    n_rows = xs.shape[0] // SLAB_ROWS
    n_blocks = n_rows // MOE_BLOCK

    def blk(i, be, na, nx):
        return (jnp.maximum(jnp.minimum(i, na[0] - 1), 0), 0)

    grid_spec = pltpu.PrefetchScalarGridSpec(
        num_scalar_prefetch=3,
        grid=(n_blocks,),
        in_specs=[
            pl.BlockSpec((MOE_BLOCK * SLAB_ROWS, LANES), blk),
            pl.BlockSpec(memory_space=pl.ANY),
            pl.BlockSpec(memory_space=pl.ANY),
            pl.BlockSpec(memory_space=pl.ANY),
        ],
        out_specs=pl.BlockSpec((MOE_BLOCK * SLAB_ROWS, LANES), lambda i, be, na, nx: (i, 0)),
        scratch_shapes=[
            pltpu.VMEM((D_MODEL, D_EXPERT), F32),
            pltpu.VMEM((D_MODEL, D_EXPERT), F32),
            pltpu.VMEM((D_EXPERT, D_MODEL), F32),
            pltpu.VMEM((D_MODEL, D_EXPERT), BF16),
            pltpu.VMEM((D_MODEL, D_EXPERT), BF16),
            pltpu.VMEM((D_EXPERT, D_MODEL), BF16),
            pltpu.SemaphoreType.DMA((3,)),
        ],
    )
    return pl.pallas_call(
        _expert_ffn_kernel,
        grid_spec=grid_spec,
        out_shape=jax.ShapeDtypeStruct((n_rows * SLAB_ROWS, LANES), jnp.uint32),
        compiler_params=pltpu.CompilerParams(
            dimension_semantics=("arbitrary",), vmem_limit_bytes=VMEM_LIMIT),
        name="expert_ffn",
    )(block_expert, n_active, next_expert, xs, wg, wu, wd)


def _combine_kernel(dcur_ref, dnxt_ref, x_ref, route_ref, g_ref, b_ref, y_ref, o_ref,
                    buf, sem, *, td):
    s = pl.program_id(0)
    n = pl.num_programs(0)
    slot = s % 2

    def base(slot_):
        return pl.multiple_of(slot_ * TOP_K * td, TOP_K * td)

    def issue(d_ref, to_slot):
        def body(t, carry):
            for k in range(TOP_K):
                _slab_copy(y_ref, d_ref[0, 0, k * td + t], buf, base(to_slot) + TOP_K * t + k,
                           sem.at[to_slot]).start(priority=k)
            return carry
        lax.fori_loop(0, td, body, 0, unroll=8)

    @pl.when(s == 0)
    def _():
        issue(dcur_ref, 0)

    @pl.when(s + 1 < n)
    def _():
        issue(dnxt_ref, 1 - slot)

    pltpu.make_async_copy(y_ref.at[pl.ds(0, TOP_K * td)], buf.at[pl.ds(base(slot), TOP_K * td)],
                          sem.at[slot]).wait()

    route = route_ref[...]
    buf2 = buf.reshape(2 * TOP_K * td * SLAB_ROWS, LANES)
    ffn = (route[:, 2:3] * _load_slabs(buf2, base(slot), td, every=TOP_K)
           + route[:, 3:4] * _load_slabs(buf2, base(slot) + 1, td, every=TOP_K))
    o_ref[...] = _layer_norm(DEEPNORM_ALPHA * x_ref[...] + ffn, g_ref[...], b_ref[...])


def _combine(ys, dest3, x1, route, g2, b2, td):
    n_tok = x1.shape[0]
    n_steps = n_tok // td
    return pl.pallas_call(
        functools.partial(_combine_kernel, td=td),
        grid=(n_steps,),
        in_specs=[
            pl.BlockSpec((1, 1, TOP_K * td), lambda s: (s, 0, 0), memory_space=pltpu.SMEM),
            pl.BlockSpec((1, 1, TOP_K * td), lambda s: (jnp.minimum(s + 1, n_steps - 1), 0, 0),
                         memory_space=pltpu.SMEM),
            pl.BlockSpec((td, D_MODEL), lambda s: (s, 0)),
            pl.BlockSpec((td, ROUTE_COLS), lambda s: (s, 0)),
            pl.BlockSpec((1, D_MODEL), lambda s: (0, 0)),
            pl.BlockSpec((1, D_MODEL), lambda s: (0, 0)),
            pl.BlockSpec(memory_space=pl.ANY),
        ],
        out_specs=pl.BlockSpec((td, D_MODEL), lambda s: (s, 0)),
        out_shape=jax.ShapeDtypeStruct((n_tok, D_MODEL), F32),
        scratch_shapes=[
            pltpu.VMEM((2 * TOP_K * td, SLAB_ROWS, LANES), jnp.uint32),
            pltpu.SemaphoreType.DMA((2,)),
        ],
        compiler_params=pltpu.CompilerParams(
            dimension_semantics=("arbitrary",), vmem_limit_bytes=VMEM_LIMIT),
        name="combine",
    )(dest3, dest3, x1, route, g2, b2, ys)


def _row_tile(n, want):
    t = min(want, n)
    while n % t:
        t //= 2
    return t


def kernel(x, ln_in_g, ln_in_b, w_in, b_gate, lambda_q, lambda_k, subln_g, rel_bias, conv_w,
           w_a_proj, w_b_proj, w_o, ln1_g, ln1_b, w_group, b_group, w_sub, b_sub,
           w_gate_e, w_up_e, w_down_e, ln2_g, ln2_b):
    bsz, seq, d = x.shape
    assert DEPTH == 1 and d == D_MODEL and w_in.shape == (DEPTH, D_MODEL, N_IN)
    n_tok = bsz * seq
    tq = _row_tile(seq, 256)
    assert tq % LANES == 0 and tq % CHUNK == 0
    qpb = 1
    tm1 = _row_tile(seq, 512)
    tm3 = _row_tile(n_tok, 1024)
    sub3 = _row_tile(tm3, 256)
    tr = _row_tile(n_tok, 1024)
    td = _row_tile(n_tok, 1024)
    tc = _row_tile(n_tok, 512)
    row = lambda v: v.reshape(1, -1).astype(F32)
    lam_init = 0.8 - 0.6 * math.exp(-0.3 * 0)

    wkt = jnp.transpose(lax.optimization_barrier(w_in[0, :, D_MODEL:2 * D_MODEL])).astype(BF16)
    xn, p, kt = _ln_proj(x.reshape(n_tok, d), row(ln_in_g), row(ln_in_b), w_in[0].astype(BF16),
                         wkt, conv_w[0].reshape(3, d).astype(F32), seq, tm1)

    band = _band_bias(rel_bias, tq)
    o_n = _diff_attn(p, kt, band, lambda_q[0].astype(F32), lambda_k[0].astype(F32),
                     row(subln_g[0]), bsz, seq, tq, qpb, lam_init)

    gap = EXPERT_LANE0 - N_GROUPS
    w_r = jnp.concatenate(
        [w_group[0].astype(F32), jnp.zeros((d, gap), F32),
         jnp.transpose(w_sub[0].astype(F32), (1, 0, 2)).reshape(d, N_EXPERTS)], axis=1)
    w_r = jnp.pad(w_r, ((0, 0), (0, LANES - w_r.shape[1])))
    w_hi = w_r.astype(BF16)
    w_lo = (w_r - w_hi.astype(F32)).astype(BF16)
    w_r2 = jnp.concatenate([w_hi, w_lo], axis=1)
    b_r = jnp.concatenate([b_group[0].astype(F32), jnp.zeros((gap,), F32),
                           b_sub[0].astype(F32).reshape(-1)])
    b_r = jnp.pad(b_r, (0, LANES - b_r.shape[0])).reshape(1, LANES)
    x1, xp, logits = _mix(
        o_n, p, xn, w_a_proj[0].astype(BF16), w_b_proj[0].astype(BF16), w_o[0].astype(BF16),
        b_gate[0].astype(F32), row(ln1_g[0]), row(ln1_b[0]), w_r2, b_r, tm3, sub3)
    route, route_t, counts = _route(logits, tr)

    n_assign = n_tok * TOP_K
    n_blocks = -(-n_assign // MOE_BLOCK) + N_EXPERTS
    cnt = counts[0, :N_EXPERTS].astype(jnp.int32)
    padded = ((cnt + MOE_BLOCK - 1) // MOE_BLOCK) * MOE_BLOCK
    pad_end = jnp.cumsum(padded)
    pad_start = pad_end - padded
    n_active = (pad_end[-1:] // MOE_BLOCK).astype(jnp.int32)
    blk_row0 = jnp.arange(n_blocks, dtype=jnp.int32) * MOE_BLOCK
    block_expert = jnp.minimum(
        jnp.sum((pad_end[None, :] <= blk_row0[:, None]).astype(jnp.int32), axis=1),
        N_EXPERTS - 1).astype(jnp.int32)
    experts = route_t[0:TOP_K].astype(jnp.int32)
    ranks = route_t[4:4 + TOP_K].astype(jnp.int32)
    is_e = experts[None] == jnp.arange(N_EXPERTS, dtype=jnp.int32)[:, None, None]
    dest = jnp.sum(jnp.where(is_e, pad_start[:, None, None], 0), axis=0) + ranks

    def per_tile(t):
        tiles = jnp.transpose(dest.reshape(TOP_K, n_tok // t, t), (1, 0, 2))
        return tiles.reshape(n_tok // t, 1, TOP_K * t).astype(jnp.int32)

    last_blk = jnp.maximum(pad_end // MOE_BLOCK - 1, 0)
    idle_blk = jnp.minimum(n_active[0] + jnp.arange(N_EXPERTS), n_blocks - 1)
    n_zero = N_EXPERTS + n_blocks - n_active
    zero_blocks = jnp.concatenate([last_blk, idle_blk, n_zero]).astype(jnp.int32)
    n_rows = n_blocks * MOE_BLOCK
    xs = _dispatch(xp.reshape(n_tok, SLAB_ROWS, LANES), per_tile(td), zero_blocks, n_rows, td)
    e_ids = jnp.arange(N_EXPERTS, dtype=jnp.int32)
    later = (padded > 0)[None, :] & (e_ids[None, :] > e_ids[:, None])
    next_expert = jnp.min(jnp.where(later, e_ids[None, :], N_EXPERTS), axis=1)
    next_expert = jnp.where(next_expert < N_EXPERTS, next_expert, -1).astype(jnp.int32)
    ys = _expert_ffn(xs.reshape(n_rows * SLAB_ROWS, LANES), block_expert, n_active, next_expert,
                     w_gate_e[0].astype(F32), w_up_e[0].astype(F32), w_down_e[0].astype(F32))
    out = _combine(ys.reshape(n_rows, SLAB_ROWS, LANES), per_tile(tc), x1, route,
                   row(ln2_g[0]), row(ln2_b[0]), tc)
    return out.reshape(bsz, seq, d)
```

```python
import functools
import math

import numpy as np
import jax
import jax.numpy as jnp
from jax import lax
from jax.experimental import pallas as pl
from jax.experimental.pallas import tpu as pltpu

F32 = jnp.float32
BF16 = jnp.bfloat16

D_MODEL = 1024
N_HEADS = 4
HEAD_DIM = 128
D_HEAD_V = 2 * HEAD_DIM
CHUNK = 64
N_BUCKETS = 32
MAX_DISTANCE = 128
N_GROUPS = 4
EXPERTS_PER_GROUP = 8
N_EXPERTS = N_GROUPS * EXPERTS_PER_GROUP
TOP_K = 2
D_EXPERT = 512
MOE_BLOCK = 512
LN_EPS = 1e-5
RMS_EPS = 1e-6
DEPTH = 1
DEEPNORM_ALPHA = (2.0 * DEPTH) ** 0.25
N_IN = 8 * D_MODEL
LANES = 128
NEG_BIG = -1e30
LOG2_E = math.log2(math.e)

VMEM_LIMIT = 56 * 1024 * 1024


def _layer_norm(x, g, b):
    mu = jnp.mean(x, axis=-1, keepdims=True)
    xc = x - mu
    var = jnp.mean(xc * xc, axis=-1, keepdims=True)
    return xc * lax.rsqrt(var + LN_EPS) * g + b


def _const_spec(shape):
    return pl.BlockSpec(shape, lambda *_: (0,) * len(shape), pipeline_mode=pl.Buffered(1))


HALO = 16
P_Q, P_V, P_CONV, P_GATE_A, P_GATE_B = range(5)
N_P = 5 * D_MODEL


def _ln_proj_kernel(x_ref, g_ref, b_ref, w_ref, wkt_ref, cw_ref, xn_ref, p_ref, kt_ref, zs_ref,
                    *, q_scale, tm, sub, tiles_per_seq):
    r = pl.program_id(0)

    @pl.when((r % tiles_per_seq) == 0)
    def _():
        zs_ref[0:HALO, :] = jnp.zeros((HALO, D_MODEL), F32)

    cw = cw_ref[...]
    for lo in range(0, tm, sub):
        rows = slice(lo, lo + sub)
        xn = _layer_norm(x_ref[rows, :], g_ref[...], b_ref[...])
        xn_ref[rows, :] = xn
        xb = xn.astype(BF16)

        def proj(c, xb=xb):
            return jnp.dot(xb, w_ref[:, c * D_MODEL:(c + 1) * D_MODEL],
                           preferred_element_type=F32)

        def put(block, val, rows=rows):
            p_ref[rows, block * D_MODEL:(block + 1) * D_MODEL] = val.astype(BF16)

        put(P_Q, proj(0) * q_scale)
        put(P_V, proj(2))
        kt = lax.dot_general(wkt_ref[...], xb, (((1,), (1,)), ((), ())),
                             preferred_element_type=F32)
        kt_ref[:, rows] = kt.astype(BF16)
        z0 = HALO + lo
        zs_ref[z0:z0 + sub, :] = proj(4) * proj(5)
        zc = (cw[0:1] * zs_ref[z0 - 2:z0 - 2 + sub, :] + cw[1:2] * zs_ref[z0 - 1:z0 - 1 + sub, :]
              + cw[2:3] * zs_ref[z0:z0 + sub, :])
        put(P_CONV, proj(3) * zc)
        put(P_GATE_A, proj(6))
        put(P_GATE_B, proj(7))

    zs_ref[0:HALO, :] = zs_ref[tm:tm + HALO, :]


def _ln_proj(x2, g, b, w_bf, wkt, conv_w, seq, tm):
    n_tok = x2.shape[0]
    kern = functools.partial(_ln_proj_kernel, q_scale=HEAD_DIM ** -0.5 * LOG2_E, tm=tm,
                             sub=_row_tile(tm, 256), tiles_per_seq=seq // tm)
    return pl.pallas_call(
        kern,
        grid=(n_tok // tm,),
        in_specs=[
            pl.BlockSpec((tm, D_MODEL), lambda r: (r, 0)),
            _const_spec((1, D_MODEL)),
            _const_spec((1, D_MODEL)),
            _const_spec((D_MODEL, N_IN)),
            _const_spec((D_MODEL, D_MODEL)),
            _const_spec((3, D_MODEL)),
        ],
        out_specs=[
            pl.BlockSpec((tm, D_MODEL), lambda r: (r, 0)),
            pl.BlockSpec((tm, N_P), lambda r: (r, 0)),
            pl.BlockSpec((D_MODEL, tm), lambda r: (0, r)),
        ],
        out_shape=[
            jax.ShapeDtypeStruct((n_tok, D_MODEL), F32),
            jax.ShapeDtypeStruct((n_tok, N_P), BF16),
            jax.ShapeDtypeStruct((D_MODEL, n_tok), BF16),
        ],
        scratch_shapes=[pltpu.VMEM((HALO + tm, D_MODEL), F32)],
        compiler_params=pltpu.CompilerParams(
            dimension_semantics=("arbitrary",), vmem_limit_bytes=VMEM_LIMIT),
        name="ln_proj",
    )(x2, g, b, w_bf, wkt, conv_w)


def _t5_bucket_np(rel):
    nb = N_BUCKETS // 2
    max_exact = nb // 2
    n = np.abs(rel)
    large = np.full(n.shape, max_exact, np.int64)
    for d in range(max_exact, MAX_DISTANCE + 1):
        val = max_exact + int(math.log(d / max_exact) / math.log(MAX_DISTANCE / max_exact)
                              * (nb - max_exact))
        large = np.where(n >= d, min(val, nb - 1), large)
    return np.where(rel > 0, nb, 0) + np.where(n < max_exact, n, large)


def _band_bias(rel_bias, tq):
    qi = np.arange(tq)[:, None]
    kj = np.arange(2 * tq)[None, :] - tq
    allowed = (kj // CHUNK) <= (qi // CHUNK)
    far_bucket = int(_t5_bucket_np(np.array([-(tq + 1)]))[0])
    rb = rel_bias.astype(F32)
    rb = ((rb - rb[far_bucket][None, :]) * LOG2_E).T
    n_rel = 3 * tq
    bucket = _t5_bucket_np(np.arange(n_rel) - (2 * tq - 1))
    onehot = jnp.asarray(bucket[:, None] == np.arange(N_BUCKETS)[None, :])
    per_rel = jnp.sum(jnp.where(onehot[None], rb[:, None, :], 0.0), axis=-1)
    skew = jnp.tile(per_rel, (1, tq))[:, :tq * (n_rel - 1)].reshape(N_HEADS, tq, n_rel - 1)
    tile = skew[:, :, tq - 1:3 * tq - 1]
    return jnp.where(jnp.asarray(allowed)[None], tile, NEG_BIG)


QK_BLOCKS = 2
PV_BLOCKS = 2


def _diff_attn_kernel(q_ref, kt_ref, v_ref, bias_ref, lq_ref, lk_ref, sg_ref, o_ref, s_ref,
                      *, tq, nq, qpb, lam_init):
    dots = jnp.sum(lq_ref[...] * lk_ref[...], axis=-1, keepdims=True)
    lam = jnp.exp(dots[0:1]) - jnp.exp(dots[1:2]) + lam_init
    gain = sg_ref[...] * (1.0 - lam_init)

    def lane_tiles(s):
        return [s[:, c * LANES:(c + 1) * LANES] for c in range(s.shape[1] // LANES)]

    def logits_pass(h, a, i):
        rows = slice(a * tq, (a + 1) * tq)
        row_max = []
        for m in range(2):
            c = h * D_HEAD_V + m * HEAD_DIM
            q = q_ref[rows, c:c + HEAD_DIM]
            mx = None
            for j0 in range(0, i + 1, QK_BLOCKS):
                nb = min(QK_BLOCKS, i + 1 - j0)
                wide = jnp.dot(q, kt_ref[c:c + HEAD_DIM, j0 * tq:(j0 + nb) * tq],
                               preferred_element_type=F32)
                for j in range(j0, j0 + nb):
                    s = wide[:, (j - j0) * tq:(j - j0 + 1) * tq]
                    if j == i:
                        s = s + bias_ref[h, :, tq:2 * tq]
                    elif j == i - 1:
                        s = s + bias_ref[h, :, 0:tq]
                    s_ref[h % s_ref.shape[0], a, m, j] = s
                    t = functools.reduce(jnp.maximum, lane_tiles(s))
                    mx = t if mx is None else jnp.maximum(mx, t)
            row_max.append(jnp.broadcast_to(jnp.max(mx, axis=-1, keepdims=True), (tq, LANES)))
        return row_max

    def values_pass(h, a, i, row_max):
        c0 = h * D_HEAD_V
        acc = [None, None]
        lsum = [None, None]
        for j0 in range(0, i + 1, PV_BLOCKS):
            nb = min(PV_BLOCKS, i + 1 - j0)
            v_blk = v_ref[j0 * tq:(j0 + nb) * tq, c0:c0 + D_HEAD_V]
            for m in range(2):
                tiles = []
                for j in range(j0, j0 + nb):
                    tiles += [jnp.exp2(x - row_max[m]) for x in lane_tiles(s_ref[h % s_ref.shape[0], a, m, j])]
                tsum = functools.reduce(jnp.add, tiles)
                lsum[m] = tsum if lsum[m] is None else lsum[m] + tsum
                p = jnp.concatenate(tiles, axis=1).astype(BF16)
                part = jnp.dot(p, v_blk, preferred_element_type=F32)
                acc[m] = part if acc[m] is None else acc[m] + part
        r1 = 1.0 / jnp.sum(lsum[0], axis=-1, keepdims=True)
        r2 = lam / jnp.sum(lsum[1], axis=-1, keepdims=True)
        o = acc[0] * r1 - acc[1] * r2
        o = o * lax.rsqrt(jnp.mean(o * o, axis=-1, keepdims=True) + RMS_EPS)
        o_ref[a * tq:(a + 1) * tq, c0:c0 + D_HEAD_V] = (o * gain).astype(BF16)

    def query_tile(step):
        for h in range(N_HEADS):
            maxima = [logits_pass(h, a, step * qpb + a) for a in range(qpb)]
            for a in range(qpb):
                values_pass(h, a, step * qpb + a, maxima[a])

    lax.switch(pl.program_id(1), [functools.partial(query_tile, s) for s in range(nq // qpb)])


def _diff_attn(p, kt, band, lam_q, lam_k, subln_g, bsz, seq, tq, qpb, lam_init):
    n_tok = bsz * seq
    nq = seq // tq
    steps = nq // qpb
    s_heads = N_HEADS if qpb == 1 else 1
    kern = functools.partial(_diff_attn_kernel, tq=tq, nq=nq, qpb=qpb, lam_init=lam_init)
    return pl.pallas_call(
        kern,
        grid=(bsz, steps),
        in_specs=[
            pl.BlockSpec((qpb * tq, D_MODEL), lambda b, i: (b * steps + i, P_Q)),
            pl.BlockSpec((D_MODEL, seq), lambda b, i: (0, b)),
            pl.BlockSpec((seq, D_MODEL), lambda b, i: (b, P_V)),
            _const_spec((N_HEADS, tq, 2 * tq)),
            _const_spec((2, HEAD_DIM)),
            _const_spec((2, HEAD_DIM)),
            _const_spec((1, D_HEAD_V)),
        ],
        out_specs=pl.BlockSpec((qpb * tq, D_MODEL), lambda b, i: (b * steps + i, 0)),
        out_shape=jax.ShapeDtypeStruct((n_tok, D_MODEL), BF16),
        scratch_shapes=[
            pltpu.VMEM((s_heads, qpb, 2, nq, tq, tq), F32),
        ],
        compiler_params=pltpu.CompilerParams(
            dimension_semantics=("arbitrary", "arbitrary"), vmem_limit_bytes=VMEM_LIMIT),
        name="diff_attn",
    )(p, kt, p, band, lam_q, lam_k, subln_g)


SLAB_ROWS = D_MODEL // (2 * LANES)


def _store_slabs(ref, row0, x_bf):
    n = x_bf.shape[0]
    bits = pltpu.bitcast(x_bf.astype(F32), jnp.uint32)
    for c in range(SLAB_ROWS):
        lo = bits[:, 2 * c * LANES:(2 * c + 1) * LANES]
        hi = bits[:, (2 * c + 1) * LANES:(2 * c + 2) * LANES]
        ref[pl.ds(SLAB_ROWS * row0 + c, n, stride=SLAB_ROWS), :] = (lo >> 16) | hi


def _load_slabs(ref, row0, n, every=1):
    parts = []
    for c in range(SLAB_ROWS):
        words = ref[pl.ds(SLAB_ROWS * row0 + c, n, stride=SLAB_ROWS * every), :]
        parts.append(pltpu.bitcast(words << 16, F32))
        parts.append(pltpu.bitcast(words & jnp.uint32(0xFFFF0000), F32))
    return jnp.concatenate(parts, axis=1)


def _mix_kernel(on_ref, yb_ref, ga_ref, gb_ref, xn_ref, wa_ref, wb_ref, wo_ref, bg_ref,
                g1_ref, b1_ref, wr_ref, br_ref, x1_ref, xp_ref, lg_ref, *, tm, sub):
    bg = bg_ref[...]
    for lo in range(0, tm, sub):
        rows = slice(lo, lo + sub)
        y_b = jnp.dot(yb_ref[rows, :], wb_ref[...], preferred_element_type=F32)
        y_a = jnp.dot(on_ref[rows, :], wa_ref[...], preferred_element_type=F32)
        g_a = jax.nn.sigmoid(ga_ref[rows, :].astype(F32) + bg[0:1])
        g_b = jax.nn.sigmoid(gb_ref[rows, :].astype(F32) + bg[1:2])
        merged = (g_a * y_a + g_b * y_b).astype(BF16)
        mix = jnp.dot(merged, wo_ref[...], preferred_element_type=F32)
        x1 = _layer_norm(DEEPNORM_ALPHA * xn_ref[rows, :] + mix, g1_ref[...], b1_ref[...])
        x1_ref[rows, :] = x1
        x_hi = x1.astype(BF16)
        x_lo = (x1 - x_hi.astype(F32)).astype(BF16)
        part = (jnp.dot(x_hi, wr_ref[...], preferred_element_type=F32)
                + jnp.dot(x_lo, wr_ref[...], preferred_element_type=F32))
        lg_ref[rows, :] = part[:, 0:LANES] + part[:, LANES:2 * LANES] + br_ref[...]
        _store_slabs(xp_ref, lo, x_hi)


def _mix(o_n, p, xn, wa, wb, wo, b_gate, g1, b1, wr2, br, tm, sub):
    n_tok = xn.shape[0]
    kern = functools.partial(_mix_kernel, tm=tm, sub=sub)

    def col(c):
        return pl.BlockSpec((tm, D_MODEL), lambda r, c=c: (r, c))

    return pl.pallas_call(
        kern,
        grid=(n_tok // tm,),
        in_specs=[
            pl.BlockSpec((tm, D_MODEL), lambda r: (r, 0)),
            col(P_CONV), col(P_GATE_A), col(P_GATE_B),
            pl.BlockSpec((tm, D_MODEL), lambda r: (r, 0)),
            _const_spec((D_MODEL, D_MODEL)), _const_spec((D_MODEL, D_MODEL)),
            _const_spec((D_MODEL, D_MODEL)),
            _const_spec((2, D_MODEL)), _const_spec((1, D_MODEL)), _const_spec((1, D_MODEL)),
            _const_spec((D_MODEL, 2 * LANES)), _const_spec((1, LANES)),
        ],
        out_specs=[
            pl.BlockSpec((tm, D_MODEL), lambda r: (r, 0)),
            pl.BlockSpec((tm * SLAB_ROWS, LANES), lambda r: (r, 0)),
            pl.BlockSpec((tm, LANES), lambda r: (r, 0)),
        ],
        out_shape=[
            jax.ShapeDtypeStruct((n_tok, D_MODEL), F32),
            jax.ShapeDtypeStruct((n_tok * SLAB_ROWS, LANES), jnp.uint32),
            jax.ShapeDtypeStruct((n_tok, LANES), F32),
        ],
        compiler_params=pltpu.CompilerParams(
            dimension_semantics=("arbitrary",), vmem_limit_bytes=VMEM_LIMIT),
        name="mix",
    )(o_n, p, p, p, xn, wa, wb, wo, b_gate, g1, b1, wr2, br)


ROUTE_COLS = 8
EXPERT_LANE0 = 8


def _route_kernel(lg_ref, tri_ref, route_ref, route_t_ref, cnt_ref, run_ref, *, tr):
    @pl.when(pl.program_id(0) == 0)
    def _():
        run_ref[...] = jnp.zeros_like(run_ref)

    lt = jnp.transpose(lg_ref[...])
    row = lax.broadcasted_iota(jnp.int32, (EXPERTS_PER_GROUP, tr), 0).astype(F32)

    def first_argmax(vals, vmax):
        return jnp.min(jnp.where(vals == vmax, row, float(EXPERTS_PER_GROUP)),
                       axis=0, keepdims=True)

    gl = jnp.where(row < N_GROUPS, lt[0:EXPERTS_PER_GROUP], -jnp.inf)
    gmax = jnp.max(gl, axis=0, keepdims=True)
    gsum = jnp.sum(jnp.exp(gl - gmax), axis=0, keepdims=True)
    g_p = 1.0 / gsum
    g_idx = first_argmax(gl, gmax)

    def group_rows(g):
        lo = EXPERT_LANE0 + EXPERTS_PER_GROUP * g
        return lt[lo:lo + EXPERTS_PER_GROUP]

    sl = group_rows(N_GROUPS - 1)
    for g in reversed(range(N_GROUPS - 1)):
        sl = jnp.where(g_idx == g, group_rows(g), sl)
    s1 = jnp.max(sl, axis=0, keepdims=True)
    i1 = first_argmax(sl, s1)
    sl2 = jnp.where(row == i1, -jnp.inf, sl)
    s2 = jnp.max(sl2, axis=0, keepdims=True)
    i2 = first_argmax(sl2, s2)
    t = jnp.exp(s2 - s1)
    w1 = g_p / (1.0 + t)
    w2 = g_p * t / (1.0 + t)
    e1 = EXPERTS_PER_GROUP * g_idx + i1
    e2 = EXPERTS_PER_GROUP * g_idx + i2

    e_row = lax.broadcasted_iota(jnp.int32, (N_EXPERTS, tr), 0).astype(F32)
    oh1 = e_row == e1
    oh2 = e_row == e2
    onehot_t = jnp.concatenate([jnp.where(oh1 | oh2, 1.0, 0.0),
                                jnp.zeros((LANES - N_EXPERTS, tr), F32)], axis=0)
    onehot = jnp.transpose(onehot_t)
    before = jnp.dot(tri_ref[...], onehot.astype(BF16), preferred_element_type=F32) + run_ref[...]
    before_t = jnp.transpose(before)[0:N_EXPERTS]
    rank1 = jnp.sum(jnp.where(oh1, before_t, 0.0), axis=0, keepdims=True)
    rank2 = jnp.sum(jnp.where(oh2, before_t, 0.0), axis=0, keepdims=True)
    run_ref[...] = run_ref[...] + jnp.sum(onehot, axis=0, keepdims=True)
    cnt_ref[...] = jnp.broadcast_to(run_ref[...], cnt_ref.shape)

    zero = jnp.zeros((1, tr), F32)
    route_t = jnp.concatenate([e1, e2, w1, w2, rank1, rank2, zero, zero], axis=0)
    route_t_ref[...] = route_t
    padded = jnp.concatenate([route_t, jnp.zeros((LANES - ROUTE_COLS, tr), F32)], axis=0)
    route_ref[...] = jnp.transpose(padded)[:, 0:ROUTE_COLS]


def _route(logits, tr):
    n_tok = logits.shape[0]
    tri = jnp.asarray(np.tril(np.ones((tr, tr), np.float32), k=-1), BF16)
    return pl.pallas_call(
        functools.partial(_route_kernel, tr=tr),
        grid=(n_tok // tr,),
        in_specs=[pl.BlockSpec((tr, LANES), lambda r: (r, 0)), _const_spec((tr, tr))],
        out_specs=[
            pl.BlockSpec((tr, ROUTE_COLS), lambda r: (r, 0)),
            pl.BlockSpec((ROUTE_COLS, tr), lambda r: (0, r)),
            pl.BlockSpec((8, LANES), lambda r: (0, 0)),
        ],
        out_shape=[
            jax.ShapeDtypeStruct((n_tok, ROUTE_COLS), F32),
            jax.ShapeDtypeStruct((ROUTE_COLS, n_tok), F32),
            jax.ShapeDtypeStruct((8, LANES), F32),
        ],
        scratch_shapes=[pltpu.VMEM((1, LANES), F32)],
        compiler_params=pltpu.CompilerParams(
            dimension_semantics=("arbitrary",), vmem_limit_bytes=VMEM_LIMIT),
        name="route",
    )(logits, tri)


def _slab_copy(src, src_row, dst, dst_row, sem):
    return pltpu.make_async_copy(src.at[src_row], dst.at[dst_row], sem)


N_ZERO_BLOCKS = 2 * N_EXPERTS


def _dispatch_kernel(zb_ref, dest_ref, x_ref, out_ref, zeros, sem, zsem, *, td):
    @pl.when(pl.program_id(0) == 0)
    def _():
        zeros[...] = jnp.zeros_like(zeros)

        def block_copy(n):
            row0 = pl.multiple_of(zb_ref[n] * MOE_BLOCK, MOE_BLOCK)
            return pltpu.make_async_copy(zeros, out_ref.at[pl.ds(row0, MOE_BLOCK)], zsem)

        n_zero = zb_ref[N_ZERO_BLOCKS]

        def start(n, carry):
            block_copy(n).start()
            return carry

        def wait(n, carry):
            block_copy(n).wait()
            return carry

        lax.fori_loop(0, n_zero, start, 0)
        lax.fori_loop(0, n_zero, wait, 0)

    def issue(t, carry):
        for k in range(TOP_K):
            _slab_copy(x_ref, t, out_ref, dest_ref[0, 0, k * td + t], sem).start(priority=k)
        return carry

    lax.fori_loop(0, td, issue, 0, unroll=8)
    for k in range(TOP_K):
        pltpu.make_async_copy(x_ref, out_ref.at[pl.ds(0, td)], sem).wait()


def _dispatch(xp, dest3, zero_blocks, n_rows, td):
    n_tok = xp.shape[0]
    grid_spec = pltpu.PrefetchScalarGridSpec(
        num_scalar_prefetch=1,
        grid=(n_tok // td,),
        in_specs=[
            pl.BlockSpec((1, 1, TOP_K * td), lambda s, zb: (s, 0, 0), memory_space=pltpu.SMEM),
            pl.BlockSpec((td, SLAB_ROWS, LANES), lambda s, zb: (s, 0, 0)),
        ],
        out_specs=pl.BlockSpec(memory_space=pl.ANY),
        scratch_shapes=[
            pltpu.VMEM((MOE_BLOCK, SLAB_ROWS, LANES), jnp.uint32),
            pltpu.SemaphoreType.DMA(()),
            pltpu.SemaphoreType.DMA(()),
        ],
    )
    return pl.pallas_call(
        functools.partial(_dispatch_kernel, td=td),
        grid_spec=grid_spec,
        out_shape=jax.ShapeDtypeStruct((n_rows, SLAB_ROWS, LANES), jnp.uint32),
        compiler_params=pltpu.CompilerParams(dimension_semantics=("arbitrary",)),
        name="dispatch",
    )(zero_blocks, dest3, xp)


FFN_SUB = 512


def _expert_ffn_kernel(be_ref, na_ref, nx_ref, x_ref, wg_hbm, wu_hbm, wd_hbm, y_ref,
                       wg_f32, wu_f32, wd_f32, wg_bf, wu_bf, wd_bf, sem):
    i = pl.program_id(0)
    active = i < na_ref[0]
    expert = be_ref[i]
    new_expert = (i == 0) | (expert != be_ref[jnp.maximum(i - 1, 0)])

    def fetch(e):
        return (pltpu.make_async_copy(wg_hbm.at[e], wg_f32, sem.at[0]),
                pltpu.make_async_copy(wu_hbm.at[e], wu_f32, sem.at[1]),
                pltpu.make_async_copy(wd_hbm.at[e], wd_f32, sem.at[2]))

    @pl.when(active & (i == 0))
    def _():
        for cp in fetch(expert):
            cp.start()

    @pl.when(active & new_expert)
    def _():
        for cp in fetch(expert):
            cp.wait()
        wg_bf[...] = wg_f32[...].astype(BF16)
        wu_bf[...] = wu_f32[...].astype(BF16)
        wd_bf[...] = wd_f32[...].astype(BF16)
        nxt = nx_ref[expert]

        @pl.when(nxt >= 0)
        def _():
            for cp in fetch(nxt):
                cp.start()

    @pl.when(active)
    def _():
        for lo in range(0, MOE_BLOCK, FFN_SUB):
            x = _load_slabs(x_ref, lo, FFN_SUB).astype(BF16)
            g = jnp.dot(x, wg_bf[...], preferred_element_type=F32)
            u = jnp.dot(x, wu_bf[...], preferred_element_type=F32)
            hid = (jax.nn.silu(g) * u).astype(BF16)
            y = jnp.dot(hid, wd_bf[...], preferred_element_type=F32)
            _store_slabs(y_ref, lo, y.astype(BF16))

    @pl.when(jnp.logical_not(active))
    def _():
        y_ref[...] = jnp.zeros_like(y_ref)


def _expert_ffn(xs, block_expert, n_active, next_expert, wg, wu, wd):
    n_rows = xs.shape[0] // SLAB_ROWS
    n_blocks = n_rows // MOE_BLOCK

    def blk(i, be, na, nx):
        return (jnp.maximum(jnp.minimum(i, na[0] - 1), 0), 0)

    grid_spec = pltpu.PrefetchScalarGridSpec(
        num_scalar_prefetch=3,
        grid=(n_blocks,),
        in_specs=[
            pl.BlockSpec((MOE_BLOCK * SLAB_ROWS, LANES), blk),
            pl.BlockSpec(memory_space=pl.ANY),
            pl.BlockSpec(memory_space=pl.ANY),
            pl.BlockSpec(memory_space=pl.ANY),
        ],
        out_specs=pl.BlockSpec((MOE_BLOCK * SLAB_ROWS, LANES), lambda i, be, na, nx: (i, 0)),
        scratch_shapes=[
            pltpu.VMEM((D_MODEL, D_EXPERT), F32),
            pltpu.VMEM((D_MODEL, D_EXPERT), F32),
            pltpu.VMEM((D_EXPERT, D_MODEL), F32),
            pltpu.VMEM((D_MODEL, D_EXPERT), BF16),
            pltpu.VMEM((D_MODEL, D_EXPERT), BF16),
            pltpu.VMEM((D_EXPERT, D_MODEL), BF16),
            pltpu.SemaphoreType.DMA((3,)),
        ],
    )
    return pl.pallas_call(
        _expert_ffn_kernel,
        grid_spec=grid_spec,
        out_shape=jax.ShapeDtypeStruct((n_rows * SLAB_ROWS, LANES), jnp.uint32),
        compiler_params=pltpu.CompilerParams(
            dimension_semantics=("arbitrary",), vmem_limit_bytes=VMEM_LIMIT),
        name="expert_ffn",
    )(block_expert, n_active, next_expert, xs, wg, wu, wd)


def _combine_kernel(dcur_ref, dnxt_ref, x_ref, route_ref, g_ref, b_ref, y_ref, o_ref,
                    buf, sem, *, td):
    s = pl.program_id(0)
    n = pl.num_programs(0)
    slot = s % 2

    def base(slot_):
        return pl.multiple_of(slot_ * TOP_K * td, TOP_K * td)

    def issue(d_ref, to_slot):
        def body(t, carry):
            for k in range(TOP_K):
                _slab_copy(y_ref, d_ref[0, 0, k * td + t], buf, base(to_slot) + TOP_K * t + k,
                           sem.at[to_slot]).start(priority=k)
            return carry
        lax.fori_loop(0, td, body, 0, unroll=8)

    @pl.when(s == 0)
    def _():
        issue(dcur_ref, 0)

    @pl.when(s + 1 < n)
    def _():
        issue(dnxt_ref, 1 - slot)

    pltpu.make_async_copy(y_ref.at[pl.ds(0, TOP_K * td)], buf.at[pl.ds(base(slot), TOP_K * td)],
                          sem.at[slot]).wait()

    route = route_ref[...]
    buf2 = buf.reshape(2 * TOP_K * td * SLAB_ROWS, LANES)
    ffn = (route[:, 2:3] * _load_slabs(buf2, base(slot), td, every=TOP_K)
           + route[:, 3:4] * _load_slabs(buf2, base(slot) + 1, td, every=TOP_K))
    o_ref[...] = _layer_norm(DEEPNORM_ALPHA * x_ref[...] + ffn, g_ref[...], b_ref[...])


def _combine(ys, dest3, x1, route, g2, b2, td):
    n_tok = x1.shape[0]
    n_steps = n_tok // td
    return pl.pallas_call(
        functools.partial(_combine_kernel, td=td),
        grid=(n_steps,),
        in_specs=[
            pl.BlockSpec((1, 1, TOP_K * td), lambda s: (s, 0, 0), memory_space=pltpu.SMEM),
            pl.BlockSpec((1, 1, TOP_K * td), lambda s: (jnp.minimum(s + 1, n_steps - 1), 0, 0),
                         memory_space=pltpu.SMEM),
            pl.BlockSpec((td, D_MODEL), lambda s: (s, 0)),
            pl.BlockSpec((td, ROUTE_COLS), lambda s: (s, 0)),
            pl.BlockSpec((1, D_MODEL), lambda s: (0, 0)),
            pl.BlockSpec((1, D_MODEL), lambda s: (0, 0)),
            pl.BlockSpec(memory_space=pl.ANY),
        ],
        out_specs=pl.BlockSpec((td, D_MODEL), lambda s: (s, 0)),
        out_shape=jax.ShapeDtypeStruct((n_tok, D_MODEL), F32),
        scratch_shapes=[
            pltpu.VMEM((2 * TOP_K * td, SLAB_ROWS, LANES), jnp.uint32),
            pltpu.SemaphoreType.DMA((2,)),
        ],
        compiler_params=pltpu.CompilerParams(
            dimension_semantics=("arbitrary",), vmem_limit_bytes=VMEM_LIMIT),
        name="combine",
    )(dest3, dest3, x1, route, g2, b2, ys)


def _row_tile(n, want):
    t = min(want, n)
    while n % t:
        t //= 2
    return t


def kernel(x, ln_in_g, ln_in_b, w_in, b_gate, lambda_q, lambda_k, subln_g, rel_bias, conv_w,
           w_a_proj, w_b_proj, w_o, ln1_g, ln1_b, w_group, b_group, w_sub, b_sub,
           w_gate_e, w_up_e, w_down_e, ln2_g, ln2_b):
    bsz, seq, d = x.shape
    assert DEPTH == 1 and d == D_MODEL and w_in.shape == (DEPTH, D_MODEL, N_IN)
    n_tok = bsz * seq
    tq = _row_tile(seq, 256)
    assert tq % LANES == 0 and tq % CHUNK == 0
    qpb = _row_tile(seq // tq, 2)
    tm1 = _row_tile(seq, 512)
    tm3 = _row_tile(n_tok, 1024)
    sub3 = _row_tile(tm3, 256)
    tr = _row_tile(n_tok, 1024)
    td = _row_tile(n_tok, 1024)
    tc = _row_tile(n_tok, 512)
    row = lambda v: v.reshape(1, -1).astype(F32)
    lam_init = 0.8 - 0.6 * math.exp(-0.3 * 0)

    wkt = jnp.transpose(lax.optimization_barrier(w_in[0, :, D_MODEL:2 * D_MODEL])).astype(BF16)
    xn, p, kt = _ln_proj(x.reshape(n_tok, d), row(ln_in_g), row(ln_in_b), w_in[0].astype(BF16),
                         wkt, conv_w[0].reshape(3, d).astype(F32), seq, tm1)

    band = _band_bias(rel_bias, tq)
    o_n = _diff_attn(p, kt, band, lambda_q[0].astype(F32), lambda_k[0].astype(F32),
                     row(subln_g[0]), bsz, seq, tq, qpb, lam_init)

    gap = EXPERT_LANE0 - N_GROUPS
    w_r = jnp.concatenate(
        [w_group[0].astype(F32), jnp.zeros((d, gap), F32),
         jnp.transpose(w_sub[0].astype(F32), (1, 0, 2)).reshape(d, N_EXPERTS)], axis=1)
    w_r = jnp.pad(w_r, ((0, 0), (0, LANES - w_r.shape[1])))
    w_hi = w_r.astype(BF16)
    w_lo = (w_r - w_hi.astype(F32)).astype(BF16)
    w_r2 = jnp.concatenate([w_hi, w_lo], axis=1)
    b_r = jnp.concatenate([b_group[0].astype(F32), jnp.zeros((gap,), F32),
                           b_sub[0].astype(F32).reshape(-1)])
    b_r = jnp.pad(b_r, (0, LANES - b_r.shape[0])).reshape(1, LANES)
    x1, xp, logits = _mix(
        o_n, p, xn, w_a_proj[0].astype(BF16), w_b_proj[0].astype(BF16), w_o[0].astype(BF16),
        b_gate[0].astype(F32), row(ln1_g[0]), row(ln1_b[0]), w_r2, b_r, tm3, sub3)
    route, route_t, counts = _route(logits, tr)

    n_assign = n_tok * TOP_K
    n_blocks = -(-n_assign // MOE_BLOCK) + N_EXPERTS
    cnt = counts[0, :N_EXPERTS].astype(jnp.int32)
    padded = ((cnt + MOE_BLOCK - 1) // MOE_BLOCK) * MOE_BLOCK
    pad_end = jnp.cumsum(padded)
    pad_start = pad_end - padded
    n_active = (pad_end[-1:] // MOE_BLOCK).astype(jnp.int32)
    blk_row0 = jnp.arange(n_blocks, dtype=jnp.int32) * MOE_BLOCK
    block_expert = jnp.minimum(
        jnp.sum((pad_end[None, :] <= blk_row0[:, None]).astype(jnp.int32), axis=1),
        N_EXPERTS - 1).astype(jnp.int32)
    experts = route_t[0:TOP_K].astype(jnp.int32)
    ranks = route_t[4:4 + TOP_K].astype(jnp.int32)
    is_e = experts[None] == jnp.arange(N_EXPERTS, dtype=jnp.int32)[:, None, None]
    dest = jnp.sum(jnp.where(is_e, pad_start[:, None, None], 0), axis=0) + ranks

    def per_tile(t):
        tiles = jnp.transpose(dest.reshape(TOP_K, n_tok // t, t), (1, 0, 2))
        return tiles.reshape(n_tok // t, 1, TOP_K * t).astype(jnp.int32)

    last_blk = jnp.maximum(pad_end // MOE_BLOCK - 1, 0)
    idle_blk = jnp.minimum(n_active[0] + jnp.arange(N_EXPERTS), n_blocks - 1)
    n_zero = N_EXPERTS + n_blocks - n_active
    zero_blocks = jnp.concatenate([last_blk, idle_blk, n_zero]).astype(jnp.int32)
    n_rows = n_blocks * MOE_BLOCK
    xs = _dispatch(xp.reshape(n_tok, SLAB_ROWS, LANES), per_tile(td), zero_blocks, n_rows, td)
    e_ids = jnp.arange(N_EXPERTS, dtype=jnp.int32)
    later = (padded > 0)[None, :] & (e_ids[None, :] > e_ids[:, None])
    next_expert = jnp.min(jnp.where(later, e_ids[None, :], N_EXPERTS), axis=1)
    next_expert = jnp.where(next_expert < N_EXPERTS, next_expert, -1).astype(jnp.int32)
    ys = _expert_ffn(xs.reshape(n_rows * SLAB_ROWS, LANES), block_expert, n_active, next_expert,
                     w_gate_e[0].astype(F32), w_up_e[0].astype(F32), w_down_e[0].astype(F32))
    out = _combine(ys.reshape(n_rows, SLAB_ROWS, LANES), per_tile(tc), x1, route,
                   row(ln2_g[0]), row(ln2_b[0]), tc)
    return out.reshape(bsz, seq, d)
```

```python
import functools
import math

import numpy as np
import jax
import jax.numpy as jnp
from jax import lax
from jax.experimental import pallas as pl
from jax.experimental.pallas import tpu as pltpu

F32 = jnp.float32
BF16 = jnp.bfloat16

D_MODEL = 1024
N_HEADS = 4
HEAD_DIM = 128
D_HEAD_V = 2 * HEAD_DIM
CHUNK = 64
N_BUCKETS = 32
MAX_DISTANCE = 128
N_GROUPS = 4
EXPERTS_PER_GROUP = 8
N_EXPERTS = N_GROUPS * EXPERTS_PER_GROUP
TOP_K = 2
D_EXPERT = 512
MOE_BLOCK = 512
LN_EPS = 1e-5
RMS_EPS = 1e-6
DEPTH = 1
DEEPNORM_ALPHA = (2.0 * DEPTH) ** 0.25
N_IN = 8 * D_MODEL
LANES = 128
NEG_BIG = -1e30
LOG2_E = math.log2(math.e)

VMEM_LIMIT = 56 * 1024 * 1024


def _layer_norm(x, g, b):
    mu = jnp.mean(x, axis=-1, keepdims=True)
    xc = x - mu
    var = jnp.mean(xc * xc, axis=-1, keepdims=True)
    return xc * lax.rsqrt(var + LN_EPS) * g + b


def _const_spec(shape):
    return pl.BlockSpec(shape, lambda *_: (0,) * len(shape), pipeline_mode=pl.Buffered(1))


HALO = 16
P_Q, P_V, P_CONV, P_GATE_A, P_GATE_B = range(5)
N_P = 5 * D_MODEL


def _ln_proj_kernel(x_ref, g_ref, b_ref, w_ref, wkt_ref, cw_ref, xn_ref, p_ref, kt_ref, zs_ref,
                    *, q_scale, tm, sub, tiles_per_seq):
    r = pl.program_id(0)

    @pl.when((r % tiles_per_seq) == 0)
    def _():
        zs_ref[0:HALO, :] = jnp.zeros((HALO, D_MODEL), F32)

    cw = cw_ref[...]
    for lo in range(0, tm, sub):
        rows = slice(lo, lo + sub)
        xn = _layer_norm(x_ref[rows, :], g_ref[...], b_ref[...])
        xn_ref[rows, :] = xn
        xb = xn.astype(BF16)

        def proj(c, xb=xb):
            return jnp.dot(xb, w_ref[:, c * D_MODEL:(c + 1) * D_MODEL],
                           preferred_element_type=F32)

        def put(block, val, rows=rows):
            p_ref[rows, block * D_MODEL:(block + 1) * D_MODEL] = val.astype(BF16)

        put(P_Q, proj(0) * q_scale)
        put(P_V, proj(2))
        kt = lax.dot_general(wkt_ref[...], xb, (((1,), (1,)), ((), ())),
                             preferred_element_type=F32)
        kt_ref[:, rows] = kt.astype(BF16)
        z0 = HALO + lo
        zs_ref[z0:z0 + sub, :] = proj(4) * proj(5)
        zc = (cw[0:1] * zs_ref[z0 - 2:z0 - 2 + sub, :] + cw[1:2] * zs_ref[z0 - 1:z0 - 1 + sub, :]
              + cw[2:3] * zs_ref[z0:z0 + sub, :])
        put(P_CONV, proj(3) * zc)
        put(P_GATE_A, proj(6))
        put(P_GATE_B, proj(7))

    zs_ref[0:HALO, :] = zs_ref[tm:tm + HALO, :]


def _ln_proj(x2, g, b, w_bf, wkt, conv_w, seq, tm):
    n_tok = x2.shape[0]
    kern = functools.partial(_ln_proj_kernel, q_scale=HEAD_DIM ** -0.5 * LOG2_E, tm=tm,
                             sub=_row_tile(tm, 256), tiles_per_seq=seq // tm)
    return pl.pallas_call(
        kern,
        grid=(n_tok // tm,),
        in_specs=[
            pl.BlockSpec((tm, D_MODEL), lambda r: (r, 0)),
            _const_spec((1, D_MODEL)),
            _const_spec((1, D_MODEL)),
            _const_spec((D_MODEL, N_IN)),
            _const_spec((D_MODEL, D_MODEL)),
            _const_spec((3, D_MODEL)),
        ],
        out_specs=[
            pl.BlockSpec((tm, D_MODEL), lambda r: (r, 0)),
            pl.BlockSpec((tm, N_P), lambda r: (r, 0)),
            pl.BlockSpec((D_MODEL, tm), lambda r: (0, r)),
        ],
        out_shape=[
            jax.ShapeDtypeStruct((n_tok, D_MODEL), F32),
            jax.ShapeDtypeStruct((n_tok, N_P), BF16),
            jax.ShapeDtypeStruct((D_MODEL, n_tok), BF16),
        ],
        scratch_shapes=[pltpu.VMEM((HALO + tm, D_MODEL), F32)],
        compiler_params=pltpu.CompilerParams(
            dimension_semantics=("arbitrary",), vmem_limit_bytes=VMEM_LIMIT),
        name="ln_proj",
    )(x2, g, b, w_bf, wkt, conv_w)


def _t5_bucket_np(rel):
    nb = N_BUCKETS // 2
    max_exact = nb // 2
    n = np.abs(rel)
    large = np.full(n.shape, max_exact, np.int64)
    for d in range(max_exact, MAX_DISTANCE + 1):
        val = max_exact + int(math.log(d / max_exact) / math.log(MAX_DISTANCE / max_exact)
                              * (nb - max_exact))
        large = np.where(n >= d, min(val, nb - 1), large)
    return np.where(rel > 0, nb, 0) + np.where(n < max_exact, n, large)


def _band_bias(rel_bias, tq):
    qi = np.arange(tq)[:, None]
    kj = np.arange(2 * tq)[None, :] - tq
    allowed = (kj // CHUNK) <= (qi // CHUNK)
    far_bucket = int(_t5_bucket_np(np.array([-(tq + 1)]))[0])
    rb = rel_bias.astype(F32)
    rb = ((rb - rb[far_bucket][None, :]) * LOG2_E).T
    n_rel = 3 * tq
    bucket = _t5_bucket_np(np.arange(n_rel) - (2 * tq - 1))
    onehot = jnp.asarray(bucket[:, None] == np.arange(N_BUCKETS)[None, :])
    per_rel = jnp.sum(jnp.where(onehot[None], rb[:, None, :], 0.0), axis=-1)
    skew = jnp.tile(per_rel, (1, tq))[:, :tq * (n_rel - 1)].reshape(N_HEADS, tq, n_rel - 1)
    tile = skew[:, :, tq - 1:3 * tq - 1]
    return jnp.where(jnp.asarray(allowed)[None], tile, NEG_BIG)


QK_BLOCKS = 2
PV_BLOCKS = 2


def _diff_attn_kernel(q_ref, kt_ref, v_ref, bias_ref, lq_ref, lk_ref, sg_ref, o_ref, s_ref,
                      *, tq, nq, qpb, lam_init):
    dots = jnp.sum(lq_ref[...] * lk_ref[...], axis=-1, keepdims=True)
    lam = jnp.exp(dots[0:1]) - jnp.exp(dots[1:2]) + lam_init
    gain = sg_ref[...] * (1.0 - lam_init)

    def lane_tiles(s):
        return [s[:, c * LANES:(c + 1) * LANES] for c in range(s.shape[1] // LANES)]

    def logits_pass(h, a, i):
        rows = slice(a * tq, (a + 1) * tq)
        row_max = []
        for m in range(2):
            c = h * D_HEAD_V + m * HEAD_DIM
            q = q_ref[rows, c:c + HEAD_DIM]
            mx = None
            for j0 in range(0, i + 1, QK_BLOCKS):
                nb = min(QK_BLOCKS, i + 1 - j0)
                wide = jnp.dot(q, kt_ref[c:c + HEAD_DIM, j0 * tq:(j0 + nb) * tq],
                               preferred_element_type=F32)
                for j in range(j0, j0 + nb):
                    s = wide[:, (j - j0) * tq:(j - j0 + 1) * tq]
                    if j == i:
                        s = s + bias_ref[h, :, tq:2 * tq]
                    elif j == i - 1:
                        s = s + bias_ref[h, :, 0:tq]
                    s_ref[h % s_ref.shape[0], a, m, j] = s
                    t = functools.reduce(jnp.maximum, lane_tiles(s))
                    mx = t if mx is None else jnp.maximum(mx, t)
            row_max.append(jnp.broadcast_to(jnp.max(mx, axis=-1, keepdims=True), (tq, LANES)))
        return row_max

    def values_pass(h, a, i, row_max):
        c0 = h * D_HEAD_V
        acc = [None, None]
        lsum = [None, None]
        for j0 in range(0, i + 1, PV_BLOCKS):
            nb = min(PV_BLOCKS, i + 1 - j0)
            v_blk = v_ref[j0 * tq:(j0 + nb) * tq, c0:c0 + D_HEAD_V]
            for m in range(2):
                tiles = []
                for j in range(j0, j0 + nb):
                    tiles += [jnp.exp2(x - row_max[m]) for x in lane_tiles(s_ref[h % s_ref.shape[0], a, m, j])]
                tsum = functools.reduce(jnp.add, tiles)
                lsum[m] = tsum if lsum[m] is None else lsum[m] + tsum
                p = jnp.concatenate(tiles, axis=1).astype(BF16)
                part = jnp.dot(p, v_blk, preferred_element_type=F32)
                acc[m] = part if acc[m] is None else acc[m] + part
        r1 = 1.0 / jnp.sum(lsum[0], axis=-1, keepdims=True)
        r2 = lam / jnp.sum(lsum[1], axis=-1, keepdims=True)
        o = acc[0] * r1 - acc[1] * r2
        o = o * lax.rsqrt(jnp.mean(o * o, axis=-1, keepdims=True) + RMS_EPS)
        o_ref[a * tq:(a + 1) * tq, c0:c0 + D_HEAD_V] = (o * gain).astype(BF16)

    def query_tile(step):
        for h in range(N_HEADS):
            maxima = [logits_pass(h, a, step * qpb + a) for a in range(qpb)]
            for a in range(qpb):
                values_pass(h, a, step * qpb + a, maxima[a])

    lax.switch(pl.program_id(1), [functools.partial(query_tile, s) for s in range(nq // qpb)])


def _diff_attn(p, kt, band, lam_q, lam_k, subln_g, bsz, seq, tq, qpb, lam_init):
    n_tok = bsz * seq
    nq = seq // tq
    steps = nq // qpb
    s_heads = N_HEADS if qpb == 1 else 1
    kern = functools.partial(_diff_attn_kernel, tq=tq, nq=nq, qpb=qpb, lam_init=lam_init)
    return pl.pallas_call(
        kern,
        grid=(bsz, steps),
        in_specs=[
            pl.BlockSpec((qpb * tq, D_MODEL), lambda b, i: (b * steps + i, P_Q)),
            pl.BlockSpec((D_MODEL, seq), lambda b, i: (0, b)),
            pl.BlockSpec((seq, D_MODEL), lambda b, i: (b, P_V)),
            _const_spec((N_HEADS, tq, 2 * tq)),
            _const_spec((2, HEAD_DIM)),
            _const_spec((2, HEAD_DIM)),
            _const_spec((1, D_HEAD_V)),
        ],
        out_specs=pl.BlockSpec((qpb * tq, D_MODEL), lambda b, i: (b * steps + i, 0)),
        out_shape=jax.ShapeDtypeStruct((n_tok, D_MODEL), BF16),
        scratch_shapes=[
            pltpu.VMEM((s_heads, qpb, 2, nq, tq, tq), F32),
        ],
        compiler_params=pltpu.CompilerParams(
            dimension_semantics=("arbitrary", "arbitrary"), vmem_limit_bytes=VMEM_LIMIT),
        name="diff_attn",
    )(p, kt, p, band, lam_q, lam_k, subln_g)


SLAB_ROWS = D_MODEL // (2 * LANES)


def _store_slabs(ref, row0, x_bf):
    n = x_bf.shape[0]
    bits = pltpu.bitcast(x_bf.astype(F32), jnp.uint32)
    for c in range(SLAB_ROWS):
        lo = bits[:, 2 * c * LANES:(2 * c + 1) * LANES]
        hi = bits[:, (2 * c + 1) * LANES:(2 * c + 2) * LANES]
        ref[pl.ds(SLAB_ROWS * row0 + c, n, stride=SLAB_ROWS), :] = (lo >> 16) | hi


def _load_slabs(ref, row0, n, every=1):
    parts = []
    for c in range(SLAB_ROWS):
        words = ref[pl.ds(SLAB_ROWS * row0 + c, n, stride=SLAB_ROWS * every), :]
        parts.append(pltpu.bitcast(words << 16, F32))
        parts.append(pltpu.bitcast(words & jnp.uint32(0xFFFF0000), F32))
    return jnp.concatenate(parts, axis=1)


def _mix_kernel(on_ref, yb_ref, ga_ref, gb_ref, xn_ref, wa_ref, wb_ref, wo_ref, bg_ref,
                g1_ref, b1_ref, wr_ref, br_ref, x1_ref, xp_ref, lg_ref, *, tm, sub):
    bg = bg_ref[...]
    for lo in range(0, tm, sub):
        rows = slice(lo, lo + sub)
        y_b = jnp.dot(yb_ref[rows, :], wb_ref[...], preferred_element_type=F32)
        y_a = jnp.dot(on_ref[rows, :], wa_ref[...], preferred_element_type=F32)
        g_a = jax.nn.sigmoid(ga_ref[rows, :].astype(F32) + bg[0:1])
        g_b = jax.nn.sigmoid(gb_ref[rows, :].astype(F32) + bg[1:2])
        merged = (g_a * y_a + g_b * y_b).astype(BF16)
        mix = jnp.dot(merged, wo_ref[...], preferred_element_type=F32)
        x1 = _layer_norm(DEEPNORM_ALPHA * xn_ref[rows, :] + mix, g1_ref[...], b1_ref[...])
        x1_ref[rows, :] = x1
        x_hi = x1.astype(BF16)
        x_lo = (x1 - x_hi.astype(F32)).astype(BF16)
        part = (jnp.dot(x_hi, wr_ref[...], preferred_element_type=F32)
                + jnp.dot(x_lo, wr_ref[...], preferred_element_type=F32))
        lg_ref[rows, :] = part[:, 0:LANES] + part[:, LANES:2 * LANES] + br_ref[...]
        _store_slabs(xp_ref, lo, x_hi)


def _mix(o_n, p, xn, wa, wb, wo, b_gate, g1, b1, wr2, br, tm, sub):
    n_tok = xn.shape[0]
    kern = functools.partial(_mix_kernel, tm=tm, sub=sub)

    def col(c):
        return pl.BlockSpec((tm, D_MODEL), lambda r, c=c: (r, c))

    return pl.pallas_call(
        kern,
        grid=(n_tok // tm,),
        in_specs=[
            pl.BlockSpec((tm, D_MODEL), lambda r: (r, 0)),
            col(P_CONV), col(P_GATE_A), col(P_GATE_B),
            pl.BlockSpec((tm, D_MODEL), lambda r: (r, 0)),
            _const_spec((D_MODEL, D_MODEL)), _const_spec((D_MODEL, D_MODEL)),
            _const_spec((D_MODEL, D_MODEL)),
            _const_spec((2, D_MODEL)), _const_spec((1, D_MODEL)), _const_spec((1, D_MODEL)),
            _const_spec((D_MODEL, 2 * LANES)), _const_spec((1, LANES)),
        ],
        out_specs=[
            pl.BlockSpec((tm, D_MODEL), lambda r: (r, 0)),
            pl.BlockSpec((tm * SLAB_ROWS, LANES), lambda r: (r, 0)),
            pl.BlockSpec((tm, LANES), lambda r: (r, 0)),
        ],
        out_shape=[
            jax.ShapeDtypeStruct((n_tok, D_MODEL), F32),
            jax.ShapeDtypeStruct((n_tok * SLAB_ROWS, LANES), jnp.uint32),
            jax.ShapeDtypeStruct((n_tok, LANES), F32),
        ],
        compiler_params=pltpu.CompilerParams(
            dimension_semantics=("arbitrary",), vmem_limit_bytes=VMEM_LIMIT),
        name="mix",
    )(o_n, p, p, p, xn, wa, wb, wo, b_gate, g1, b1, wr2, br)


ROUTE_COLS = 8
EXPERT_LANE0 = 8


def _route_kernel(lg_ref, tri_ref, route_ref, route_t_ref, cnt_ref, run_ref, *, tr):
    @pl.when(pl.program_id(0) == 0)
    def _():
        run_ref[...] = jnp.zeros_like(run_ref)

    lt = jnp.transpose(lg_ref[...])
    row = lax.broadcasted_iota(jnp.int32, (EXPERTS_PER_GROUP, tr), 0).astype(F32)

    def first_argmax(vals, vmax):
        return jnp.min(jnp.where(vals == vmax, row, float(EXPERTS_PER_GROUP)),
                       axis=0, keepdims=True)

    gl = jnp.where(row < N_GROUPS, lt[0:EXPERTS_PER_GROUP], -jnp.inf)
    gmax = jnp.max(gl, axis=0, keepdims=True)
    gsum = jnp.sum(jnp.exp(gl - gmax), axis=0, keepdims=True)
    g_p = 1.0 / gsum
    g_idx = first_argmax(gl, gmax)

    def group_rows(g):
        lo = EXPERT_LANE0 + EXPERTS_PER_GROUP * g
        return lt[lo:lo + EXPERTS_PER_GROUP]

    sl = group_rows(N_GROUPS - 1)
    for g in reversed(range(N_GROUPS - 1)):
        sl = jnp.where(g_idx == g, group_rows(g), sl)
    s1 = jnp.max(sl, axis=0, keepdims=True)
    i1 = first_argmax(sl, s1)
    sl2 = jnp.where(row == i1, -jnp.inf, sl)
    s2 = jnp.max(sl2, axis=0, keepdims=True)
    i2 = first_argmax(sl2, s2)
    t = jnp.exp(s2 - s1)
    w1 = g_p / (1.0 + t)
    w2 = g_p * t / (1.0 + t)
    e1 = EXPERTS_PER_GROUP * g_idx + i1
    e2 = EXPERTS_PER_GROUP * g_idx + i2

    e_row = lax.broadcasted_iota(jnp.int32, (N_EXPERTS, tr), 0).astype(F32)
    oh1 = e_row == e1
    oh2 = e_row == e2
    onehot_t = jnp.concatenate([jnp.where(oh1 | oh2, 1.0, 0.0),
                                jnp.zeros((LANES - N_EXPERTS, tr), F32)], axis=0)
    onehot = jnp.transpose(onehot_t)
    before = jnp.dot(tri_ref[...], onehot.astype(BF16), preferred_element_type=F32) + run_ref[...]
    before_t = jnp.transpose(before)[0:N_EXPERTS]
    rank1 = jnp.sum(jnp.where(oh1, before_t, 0.0), axis=0, keepdims=True)
    rank2 = jnp.sum(jnp.where(oh2, before_t, 0.0), axis=0, keepdims=True)
    run_ref[...] = run_ref[...] + jnp.sum(onehot, axis=0, keepdims=True)
    cnt_ref[...] = jnp.broadcast_to(run_ref[...], cnt_ref.shape)

    zero = jnp.zeros((1, tr), F32)
    route_t = jnp.concatenate([e1, e2, w1, w2, rank1, rank2, zero, zero], axis=0)
    route_t_ref[...] = route_t
    padded = jnp.concatenate([route_t, jnp.zeros((LANES - ROUTE_COLS, tr), F32)], axis=0)
    route_ref[...] = jnp.transpose(padded)[:, 0:ROUTE_COLS]


def _route(logits, tr):
    n_tok = logits.shape[0]
    tri = jnp.asarray(np.tril(np.ones((tr, tr), np.float32), k=-1), BF16)
    return pl.pallas_call(
        functools.partial(_route_kernel, tr=tr),
        grid=(n_tok // tr,),
        in_specs=[pl.BlockSpec((tr, LANES), lambda r: (r, 0)), _const_spec((tr, tr))],
        out_specs=[
            pl.BlockSpec((tr, ROUTE_COLS), lambda r: (r, 0)),
            pl.BlockSpec((ROUTE_COLS, tr), lambda r: (0, r)),
            pl.BlockSpec((8, LANES), lambda r: (0, 0)),
        ],
        out_shape=[
            jax.ShapeDtypeStruct((n_tok, ROUTE_COLS), F32),
            jax.ShapeDtypeStruct((ROUTE_COLS, n_tok), F32),
            jax.ShapeDtypeStruct((8, LANES), F32),
        ],
        scratch_shapes=[pltpu.VMEM((1, LANES), F32)],
        compiler_params=pltpu.CompilerParams(
            dimension_semantics=("arbitrary",), vmem_limit_bytes=VMEM_LIMIT),
        name="route",
    )(logits, tri)


def _slab_copy(src, src_row, dst, dst_row, sem):
    return pltpu.make_async_copy(src.at[src_row], dst.at[dst_row], sem)


N_ZERO_BLOCKS = 2 * N_EXPERTS


def _dispatch_kernel(zb_ref, dest_ref, x_ref, out_ref, zeros, sem, zsem, *, td):
    @pl.when(pl.program_id(0) == 0)
    def _():
        zeros[...] = jnp.zeros_like(zeros)

        def block_copy(n):
            row0 = pl.multiple_of(zb_ref[n] * MOE_BLOCK, MOE_BLOCK)
            return pltpu.make_async_copy(zeros, out_ref.at[pl.ds(row0, MOE_BLOCK)], zsem)

        n_zero = zb_ref[N_ZERO_BLOCKS]

        def start(n, carry):
            block_copy(n).start()
            return carry

        def wait(n, carry):
            block_copy(n).wait()
            return carry

        lax.fori_loop(0, n_zero, start, 0)
        lax.fori_loop(0, n_zero, wait, 0)

    def issue(t, carry):
        for k in range(TOP_K):
            _slab_copy(x_ref, t, out_ref, dest_ref[0, 0, k * td + t], sem).start(priority=k)
        return carry

    lax.fori_loop(0, td, issue, 0, unroll=8)
    for k in range(TOP_K):
        pltpu.make_async_copy(x_ref, out_ref.at[pl.ds(0, td)], sem).wait()


def _dispatch(xp, dest3, zero_blocks, n_rows, td):
    n_tok = xp.shape[0]
    grid_spec = pltpu.PrefetchScalarGridSpec(
        num_scalar_prefetch=1,
        grid=(n_tok // td,),
        in_specs=[
            pl.BlockSpec((1, 1, TOP_K * td), lambda s, zb: (s, 0, 0), memory_space=pltpu.SMEM),
            pl.BlockSpec((td, SLAB_ROWS, LANES), lambda s, zb: (s, 0, 0)),
        ],
        out_specs=pl.BlockSpec(memory_space=pl.ANY),
        scratch_shapes=[
            pltpu.VMEM((MOE_BLOCK, SLAB_ROWS, LANES), jnp.uint32),
            pltpu.SemaphoreType.DMA(()),
            pltpu.SemaphoreType.DMA(()),
        ],
    )
    return pl.pallas_call(
        functools.partial(_dispatch_kernel, td=td),
        grid_spec=grid_spec,
        out_shape=jax.ShapeDtypeStruct((n_rows, SLAB_ROWS, LANES), jnp.uint32),
        compiler_params=pltpu.CompilerParams(dimension_semantics=("arbitrary",)),
        name="dispatch",
    )(zero_blocks, dest3, xp)


FFN_SUB = 512


def _expert_ffn_kernel(be_ref, na_ref, nx_ref, x_ref, wg_hbm, wu_hbm, wd_hbm, y_ref,
                       wg_f32, wu_f32, wd_f32, wg_bf, wu_bf, wd_bf, sem):
    i = pl.program_id(0)
    active = i < na_ref[0]
    expert = be_ref[i]
    new_expert = (i == 0) | (expert != be_ref[jnp.maximum(i - 1, 0)])

    def fetch(e):
        return (pltpu.make_async_copy(wg_hbm.at[e], wg_f32, sem.at[0]),
                pltpu.make_async_copy(wu_hbm.at[e], wu_f32, sem.at[1]),
                pltpu.make_async_copy(wd_hbm.at[e], wd_f32, sem.at[2]))

    @pl.when(active & (i == 0))
    def _():
        for cp in fetch(expert):
            cp.start()

    @pl.when(active & new_expert)
    def _():
        for cp in fetch(expert):
            cp.wait()
        wg_bf[...] = wg_f32[...].astype(BF16)
        wu_bf[...] = wu_f32[...].astype(BF16)
        wd_bf[...] = wd_f32[...].astype(BF16)
        nxt = nx_ref[expert]

        @pl.when(nxt >= 0)
        def _():
            for cp in fetch(nxt):
                cp.start()

    @pl.when(active)
    def _():
        for lo in range(0, MOE_BLOCK, FFN_SUB):
            x = _load_slabs(x_ref, lo, FFN_SUB).astype(BF16)
            g = jnp.dot(x, wg_bf[...], preferred_element_type=F32)
            u = jnp.dot(x, wu_bf[...], preferred_element_type=F32)
            hid = (jax.nn.silu(g) * u).astype(BF16)
            y = jnp.dot(hid, wd_bf[...], preferred_element_type=F32)
            _store_slabs(y_ref, lo, y.astype(BF16))

    @pl.when(jnp.logical_not(active))
    def _():
        y_ref[...] = jnp.zeros_like(y_ref)


def _expert_ffn(xs, block_expert, n_active, next_expert, wg, wu, wd):
    n_rows = xs.shape[0] // SLAB_ROWS
    n_blocks = n_rows // MOE_BLOCK

    def blk(i, be, na, nx):
        return (jnp.maximum(jnp.minimum(i, na[0] - 1), 0), 0)

    grid_spec = pltpu.PrefetchScalarGridSpec(
        num_scalar_prefetch=3,
        grid=(n_blocks,),
        in_specs=[
            pl.BlockSpec((MOE_BLOCK * SLAB_ROWS, LANES), blk),
            pl.BlockSpec(memory_space=pl.ANY),
            pl.BlockSpec(memory_space=pl.ANY),
            pl.BlockSpec(memory_space=pl.ANY),
        ],
        out_specs=pl.BlockSpec((MOE_BLOCK * SLAB_ROWS, LANES), lambda i, be, na, nx: (i, 0)),
        scratch_shapes=[
            pltpu.VMEM((D_MODEL, D_EXPERT), F32),
            pltpu.VMEM((D_MODEL, D_EXPERT), F32),
            pltpu.VMEM((D_EXPERT, D_MODEL), F32),
            pltpu.VMEM((D_MODEL, D_EXPERT), BF16),
            pltpu.VMEM((D_MODEL, D_EXPERT), BF16),
            pltpu.VMEM((D_EXPERT, D_MODEL), BF16),
            pltpu.SemaphoreType.DMA((3,)),
        ],
    )
    return pl.pallas_call(
        _expert_ffn_kernel,
        grid_spec=grid_spec,
        out_shape=jax.ShapeDtypeStruct((n_rows * SLAB_ROWS, LANES), jnp.uint32),
        compiler_params=pltpu.CompilerParams(
            dimension_semantics=("arbitrary",), vmem_limit_bytes=VMEM_LIMIT),
        name="expert_ffn",
    )(block_expert, n_active, next_expert, xs, wg, wu, wd)


def _combine_kernel(dcur_ref, dnxt_ref, x_ref, route_ref, g_ref, b_ref, y_ref, o_ref,
                    buf, sem, *, td):
    s = pl.program_id(0)
    n = pl.num_programs(0)
    slot = s % 2

    def base(slot_):
        return pl.multiple_of(slot_ * TOP_K * td, TOP_K * td)

    def issue(d_ref, to_slot):
        def body(t, carry):
            for k in range(TOP_K):
                _slab_copy(y_ref, d_ref[0, 0, k * td + t], buf, base(to_slot) + TOP_K * t + k,
                           sem.at[to_slot]).start(priority=k)
            return carry
        lax.fori_loop(0, td, body, 0, unroll=8)

    @pl.when(s == 0)
    def _():
        issue(dcur_ref, 0)

    @pl.when(s + 1 < n)
    def _():
        issue(dnxt_ref, 1 - slot)

    pltpu.make_async_copy(y_ref.at[pl.ds(0, TOP_K * td)], buf.at[pl.ds(base(slot), TOP_K * td)],
                          sem.at[slot]).wait()

    route = route_ref[...]
    buf2 = buf.reshape(2 * TOP_K * td * SLAB_ROWS, LANES)
    ffn = (route[:, 2:3] * _load_slabs(buf2, base(slot), td, every=TOP_K)
           + route[:, 3:4] * _load_slabs(buf2, base(slot) + 1, td, every=TOP_K))
    o_ref[...] = _layer_norm(DEEPNORM_ALPHA * x_ref[...] + ffn, g_ref[...], b_ref[...])


def _combine(ys, dest3, x1, route, g2, b2, td):
    n_tok = x1.shape[0]
    n_steps = n_tok // td
    return pl.pallas_call(
        functools.partial(_combine_kernel, td=td),
        grid=(n_steps,),
        in_specs=[
            pl.BlockSpec((1, 1, TOP_K * td), lambda s: (s, 0, 0), memory_space=pltpu.SMEM),
            pl.BlockSpec((1, 1, TOP_K * td), lambda s: (jnp.minimum(s + 1, n_steps - 1), 0, 0),
                         memory_space=pltpu.SMEM),
            pl.BlockSpec((td, D_MODEL), lambda s: (s, 0)),
            pl.BlockSpec((td, ROUTE_COLS), lambda s: (s, 0)),
            pl.BlockSpec((1, D_MODEL), lambda s: (0, 0)),
            pl.BlockSpec((1, D_MODEL), lambda s: (0, 0)),
            pl.BlockSpec(memory_space=pl.ANY),
        ],
        out_specs=pl.BlockSpec((td, D_MODEL), lambda s: (s, 0)),
        out_shape=jax.ShapeDtypeStruct((n_tok, D_MODEL), F32),
        scratch_shapes=[
            pltpu.VMEM((2 * TOP_K * td, SLAB_ROWS, LANES), jnp.uint32),
            pltpu.SemaphoreType.DMA((2,)),
        ],
        compiler_params=pltpu.CompilerParams(
            dimension_semantics=("arbitrary",), vmem_limit_bytes=VMEM_LIMIT),
        name="combine",
    )(dest3, dest3, x1, route, g2, b2, ys)


def _row_tile(n, want):
    t = min(want, n)
    while n % t:
        t //= 2
    return t


def kernel(x, ln_in_g, ln_in_b, w_in, b_gate, lambda_q, lambda_k, subln_g, rel_bias, conv_w,
           w_a_proj, w_b_proj, w_o, ln1_g, ln1_b, w_group, b_group, w_sub, b_sub,
           w_gate_e, w_up_e, w_down_e, ln2_g, ln2_b):
    bsz, seq, d = x.shape
    assert DEPTH == 1 and d == D_MODEL and w_in.shape == (DEPTH, D_MODEL, N_IN)
    n_tok = bsz * seq
    tq = _row_tile(seq, 256)
    assert tq % LANES == 0 and tq % CHUNK == 0
    qpb = _row_tile(seq // tq, 2)
    tm1 = _row_tile(seq, 512)
    tm3 = _row_tile(n_tok, 1024)
    sub3 = _row_tile(tm3, 256)
    tr = _row_tile(n_tok, 1024)
    td = _row_tile(n_tok, 2048)
    tc = _row_tile(n_tok, 512)
    row = lambda v: v.reshape(1, -1).astype(F32)
    lam_init = 0.8 - 0.6 * math.exp(-0.3 * 0)

    wkt = jnp.transpose(lax.optimization_barrier(w_in[0, :, D_MODEL:2 * D_MODEL])).astype(BF16)
    xn, p, kt = _ln_proj(x.reshape(n_tok, d), row(ln_in_g), row(ln_in_b), w_in[0].astype(BF16),
                         wkt, conv_w[0].reshape(3, d).astype(F32), seq, tm1)

    band = _band_bias(rel_bias, tq)
    o_n = _diff_attn(p, kt, band, lambda_q[0].astype(F32), lambda_k[0].astype(F32),
                     row(subln_g[0]), bsz, seq, tq, qpb, lam_init)

    gap = EXPERT_LANE0 - N_GROUPS
    w_r = jnp.concatenate(
        [w_group[0].astype(F32), jnp.zeros((d, gap), F32),
         jnp.transpose(w_sub[0].astype(F32), (1, 0, 2)).reshape(d, N_EXPERTS)], axis=1)
    w_r = jnp.pad(w_r, ((0, 0), (0, LANES - w_r.shape[1])))
    w_hi = w_r.astype(BF16)
    w_lo = (w_r - w_hi.astype(F32)).astype(BF16)
    w_r2 = jnp.concatenate([w_hi, w_lo], axis=1)
    b_r = jnp.concatenate([b_group[0].astype(F32), jnp.zeros((gap,), F32),
                           b_sub[0].astype(F32).reshape(-1)])
    b_r = jnp.pad(b_r, (0, LANES - b_r.shape[0])).reshape(1, LANES)
    x1, xp, logits = _mix(
        o_n, p, xn, w_a_proj[0].astype(BF16), w_b_proj[0].astype(BF16), w_o[0].astype(BF16),
        b_gate[0].astype(F32), row(ln1_g[0]), row(ln1_b[0]), w_r2, b_r, tm3, sub3)
    route, route_t, counts = _route(logits, tr)

    n_assign = n_tok * TOP_K
    n_blocks = -(-n_assign // MOE_BLOCK) + N_EXPERTS
    cnt = counts[0, :N_EXPERTS].astype(jnp.int32)
    padded = ((cnt + MOE_BLOCK - 1) // MOE_BLOCK) * MOE_BLOCK
    pad_end = jnp.cumsum(padded)
    pad_start = pad_end - padded
    n_active = (pad_end[-1:] // MOE_BLOCK).astype(jnp.int32)
    blk_row0 = jnp.arange(n_blocks, dtype=jnp.int32) * MOE_BLOCK
    block_expert = jnp.minimum(
        jnp.sum((pad_end[None, :] <= blk_row0[:, None]).astype(jnp.int32), axis=1),
        N_EXPERTS - 1).astype(jnp.int32)
    experts = route_t[0:TOP_K].astype(jnp.int32)
    ranks = route_t[4:4 + TOP_K].astype(jnp.int32)
    is_e = experts[None] == jnp.arange(N_EXPERTS, dtype=jnp.int32)[:, None, None]
    dest = jnp.sum(jnp.where(is_e, pad_start[:, None, None], 0), axis=0) + ranks

    def per_tile(t):
        tiles = jnp.transpose(dest.reshape(TOP_K, n_tok // t, t), (1, 0, 2))
        return tiles.reshape(n_tok // t, 1, TOP_K * t).astype(jnp.int32)

    last_blk = jnp.maximum(pad_end // MOE_BLOCK - 1, 0)
    idle_blk = jnp.minimum(n_active[0] + jnp.arange(N_EXPERTS), n_blocks - 1)
    n_zero = N_EXPERTS + n_blocks - n_active
    zero_blocks = jnp.concatenate([last_blk, idle_blk, n_zero]).astype(jnp.int32)
    n_rows = n_blocks * MOE_BLOCK
    xs = _dispatch(xp.reshape(n_tok, SLAB_ROWS, LANES), per_tile(td), zero_blocks, n_rows, td)
    e_ids = jnp.arange(N_EXPERTS, dtype=jnp.int32)
    later = (padded > 0)[None, :] & (e_ids[None, :] > e_ids[:, None])
    next_expert = jnp.min(jnp.where(later, e_ids[None, :], N_EXPERTS), axis=1)
    next_expert = jnp.where(next_expert < N_EXPERTS, next_expert, -1).astype(jnp.int32)
    ys = _expert_ffn(xs.reshape(n_rows * SLAB_ROWS, LANES), block_expert, n_active, next_expert,
                     w_gate_e[0].astype(F32), w_up_e[0].astype(F32), w_down_e[0].astype(F32))
    out = _combine(ys.reshape(n_rows, SLAB_ROWS, LANES), per_tile(tc), x1, route,
                   row(ln2_g[0]), row(ln2_b[0]), tc)
    return out.reshape(bsz, seq, d)
```

```python
import functools
import math

import numpy as np
import jax
import jax.numpy as jnp
from jax import lax
from jax.experimental import pallas as pl
from jax.experimental.pallas import tpu as pltpu

F32 = jnp.float32
BF16 = jnp.bfloat16

D_MODEL = 1024
N_HEADS = 4
HEAD_DIM = 128
D_HEAD_V = 2 * HEAD_DIM
CHUNK = 64
N_BUCKETS = 32
MAX_DISTANCE = 128
N_GROUPS = 4
EXPERTS_PER_GROUP = 8
N_EXPERTS = N_GROUPS * EXPERTS_PER_GROUP
TOP_K = 2
D_EXPERT = 512
MOE_BLOCK = 512
LN_EPS = 1e-5
RMS_EPS = 1e-6
DEPTH = 1
DEEPNORM_ALPHA = (2.0 * DEPTH) ** 0.25
N_IN = 8 * D_MODEL
LANES = 128
NEG_BIG = -1e30
LOG2_E = math.log2(math.e)

VMEM_LIMIT = 56 * 1024 * 1024


def _layer_norm(x, g, b):
    mu = jnp.mean(x, axis=-1, keepdims=True)
    xc = x - mu
    var = jnp.mean(xc * xc, axis=-1, keepdims=True)
    return xc * lax.rsqrt(var + LN_EPS) * g + b


def _const_spec(shape):
    return pl.BlockSpec(shape, lambda *_: (0,) * len(shape), pipeline_mode=pl.Buffered(1))


HALO = 16
P_Q, P_V, P_CONV, P_GATE_A, P_GATE_B = range(5)
N_P = 5 * D_MODEL


def _ln_proj_kernel(x_ref, g_ref, b_ref, w_ref, wkt_ref, cw_ref, xn_ref, p_ref, kt_ref, zs_ref,
                    *, q_scale, tm, sub, tiles_per_seq):
    r = pl.program_id(0)

    @pl.when((r % tiles_per_seq) == 0)
    def _():
        zs_ref[0:HALO, :] = jnp.zeros((HALO, D_MODEL), F32)

    cw = cw_ref[...]
    for lo in range(0, tm, sub):
        rows = slice(lo, lo + sub)
        xn = _layer_norm(x_ref[rows, :], g_ref[...], b_ref[...])
        xn_ref[rows, :] = xn
        xb = xn.astype(BF16)

        def proj(c, xb=xb):
            return jnp.dot(xb, w_ref[:, c * D_MODEL:(c + 1) * D_MODEL],
                           preferred_element_type=F32)

        def put(block, val, rows=rows):
            p_ref[rows, block * D_MODEL:(block + 1) * D_MODEL] = val.astype(BF16)

        put(P_Q, proj(0) * q_scale)
        put(P_V, proj(2))
        kt = lax.dot_general(wkt_ref[...], xb, (((1,), (1,)), ((), ())),
                             preferred_element_type=F32)
        kt_ref[:, rows] = kt.astype(BF16)
        z0 = HALO + lo
        zs_ref[z0:z0 + sub, :] = proj(4) * proj(5)
        zc = (cw[0:1] * zs_ref[z0 - 2:z0 - 2 + sub, :] + cw[1:2] * zs_ref[z0 - 1:z0 - 1 + sub, :]
              + cw[2:3] * zs_ref[z0:z0 + sub, :])
        put(P_CONV, proj(3) * zc)
        put(P_GATE_A, proj(6))
        put(P_GATE_B, proj(7))

    zs_ref[0:HALO, :] = zs_ref[tm:tm + HALO, :]


def _ln_proj(x2, g, b, w_bf, wkt, conv_w, seq, tm):
    n_tok = x2.shape[0]
    kern = functools.partial(_ln_proj_kernel, q_scale=HEAD_DIM ** -0.5 * LOG2_E, tm=tm,
                             sub=_row_tile(tm, 256), tiles_per_seq=seq // tm)
    return pl.pallas_call(
        kern,
        grid=(n_tok // tm,),
        in_specs=[
            pl.BlockSpec((tm, D_MODEL), lambda r: (r, 0)),
            _const_spec((1, D_MODEL)),
            _const_spec((1, D_MODEL)),
            _const_spec((D_MODEL, N_IN)),
            _const_spec((D_MODEL, D_MODEL)),
            _const_spec((3, D_MODEL)),
        ],
        out_specs=[
            pl.BlockSpec((tm, D_MODEL), lambda r: (r, 0)),
            pl.BlockSpec((tm, N_P), lambda r: (r, 0)),
            pl.BlockSpec((D_MODEL, tm), lambda r: (0, r)),
        ],
        out_shape=[
            jax.ShapeDtypeStruct((n_tok, D_MODEL), F32),
            jax.ShapeDtypeStruct((n_tok, N_P), BF16),
            jax.ShapeDtypeStruct((D_MODEL, n_tok), BF16),
        ],
        scratch_shapes=[pltpu.VMEM((HALO + tm, D_MODEL), F32)],
        compiler_params=pltpu.CompilerParams(
            dimension_semantics=("arbitrary",), vmem_limit_bytes=VMEM_LIMIT),
        name="ln_proj",
    )(x2, g, b, w_bf, wkt, conv_w)


def _t5_bucket_np(rel):
    nb = N_BUCKETS // 2
    max_exact = nb // 2
    n = np.abs(rel)
    large = np.full(n.shape, max_exact, np.int64)
    for d in range(max_exact, MAX_DISTANCE + 1):
        val = max_exact + int(math.log(d / max_exact) / math.log(MAX_DISTANCE / max_exact)
                              * (nb - max_exact))
        large = np.where(n >= d, min(val, nb - 1), large)
    return np.where(rel > 0, nb, 0) + np.where(n < max_exact, n, large)


def _band_bias(rel_bias, tq):
    qi = np.arange(tq)[:, None]
    kj = np.arange(2 * tq)[None, :] - tq
    allowed = (kj // CHUNK) <= (qi // CHUNK)
    far_bucket = int(_t5_bucket_np(np.array([-(tq + 1)]))[0])
    rb = rel_bias.astype(F32)
    rb = ((rb - rb[far_bucket][None, :]) * LOG2_E).T
    n_rel = 3 * tq
    bucket = _t5_bucket_np(np.arange(n_rel) - (2 * tq - 1))
    onehot = jnp.asarray(bucket[:, None] == np.arange(N_BUCKETS)[None, :])
    per_rel = jnp.sum(jnp.where(onehot[None], rb[:, None, :], 0.0), axis=-1)
    skew = jnp.tile(per_rel, (1, tq))[:, :tq * (n_rel - 1)].reshape(N_HEADS, tq, n_rel - 1)
    tile = skew[:, :, tq - 1:3 * tq - 1]
    return jnp.where(jnp.asarray(allowed)[None], tile, NEG_BIG)


QK_BLOCKS = 2
PV_BLOCKS = 2


def _diff_attn_kernel(q_ref, kt_ref, v_ref, bias_ref, lq_ref, lk_ref, sg_ref, o_ref, s_ref,
                      *, tq, nq, qpb, lam_init):
    dots = jnp.sum(lq_ref[...] * lk_ref[...], axis=-1, keepdims=True)
    lam = jnp.exp(dots[0:1]) - jnp.exp(dots[1:2]) + lam_init
    gain = sg_ref[...] * (1.0 - lam_init)

    def lane_tiles(s):
        return [s[:, c * LANES:(c + 1) * LANES] for c in range(s.shape[1] // LANES)]

    def logits_pass(h, a, i):
        rows = slice(a * tq, (a + 1) * tq)
        row_max = []
        for m in range(2):
            c = h * D_HEAD_V + m * HEAD_DIM
            q = q_ref[rows, c:c + HEAD_DIM]
            mx = None
            for j0 in range(0, i + 1, QK_BLOCKS):
                nb = min(QK_BLOCKS, i + 1 - j0)
                wide = jnp.dot(q, kt_ref[c:c + HEAD_DIM, j0 * tq:(j0 + nb) * tq],
                               preferred_element_type=F32)
                for j in range(j0, j0 + nb):
                    s = wide[:, (j - j0) * tq:(j - j0 + 1) * tq]
                    if j == i:
                        s = s + bias_ref[h, :, tq:2 * tq]
                    elif j == i - 1:
                        s = s + bias_ref[h, :, 0:tq]
                    s_ref[h % s_ref.shape[0], a, m, j] = s
                    t = functools.reduce(jnp.maximum, lane_tiles(s))
                    mx = t if mx is None else jnp.maximum(mx, t)
            row_max.append(jnp.broadcast_to(jnp.max(mx, axis=-1, keepdims=True), (tq, LANES)))
        return row_max

    def values_pass(h, a, i, row_max):
        c0 = h * D_HEAD_V
        acc = [None, None]
        lsum = [None, None]
        for j0 in range(0, i + 1, PV_BLOCKS):
            nb = min(PV_BLOCKS, i + 1 - j0)
            v_blk = v_ref[j0 * tq:(j0 + nb) * tq, c0:c0 + D_HEAD_V]
            for m in range(2):
                tiles = []
                for j in range(j0, j0 + nb):
                    tiles += [jnp.exp2(x - row_max[m]) for x in lane_tiles(s_ref[h % s_ref.shape[0], a, m, j])]
                tsum = functools.reduce(jnp.add, tiles)
                lsum[m] = tsum if lsum[m] is None else lsum[m] + tsum
                p = jnp.concatenate(tiles, axis=1).astype(BF16)
                part = jnp.dot(p, v_blk, preferred_element_type=F32)
                acc[m] = part if acc[m] is None else acc[m] + part
        r1 = 1.0 / jnp.sum(lsum[0], axis=-1, keepdims=True)
        r2 = lam / jnp.sum(lsum[1], axis=-1, keepdims=True)
        o = acc[0] * r1 - acc[1] * r2
        o = o * lax.rsqrt(jnp.mean(o * o, axis=-1, keepdims=True) + RMS_EPS)
        o_ref[a * tq:(a + 1) * tq, c0:c0 + D_HEAD_V] = (o * gain).astype(BF16)

    def query_tile(step):
        for h in range(N_HEADS):
            maxima = [logits_pass(h, a, step * qpb + a) for a in range(qpb)]
            for a in range(qpb):
                values_pass(h, a, step * qpb + a, maxima[a])

    lax.switch(pl.program_id(1), [functools.partial(query_tile, s) for s in range(nq // qpb)])


def _diff_attn(p, kt, band, lam_q, lam_k, subln_g, bsz, seq, tq, qpb, lam_init):
    n_tok = bsz * seq
    nq = seq // tq
    steps = nq // qpb
    s_heads = N_HEADS if qpb == 1 else 1
    kern = functools.partial(_diff_attn_kernel, tq=tq, nq=nq, qpb=qpb, lam_init=lam_init)
    return pl.pallas_call(
        kern,
        grid=(bsz, steps),
        in_specs=[
            pl.BlockSpec((qpb * tq, D_MODEL), lambda b, i: (b * steps + i, P_Q)),
            pl.BlockSpec((D_MODEL, seq), lambda b, i: (0, b)),
            pl.BlockSpec((seq, D_MODEL), lambda b, i: (b, P_V)),
            _const_spec((N_HEADS, tq, 2 * tq)),
            _const_spec((2, HEAD_DIM)),
            _const_spec((2, HEAD_DIM)),
            _const_spec((1, D_HEAD_V)),
        ],
        out_specs=pl.BlockSpec((qpb * tq, D_MODEL), lambda b, i: (b * steps + i, 0)),
        out_shape=jax.ShapeDtypeStruct((n_tok, D_MODEL), BF16),
        scratch_shapes=[
            pltpu.VMEM((s_heads, qpb, 2, nq, tq, tq), F32),
        ],
        compiler_params=pltpu.CompilerParams(
            dimension_semantics=("arbitrary", "arbitrary"), vmem_limit_bytes=VMEM_LIMIT),
        name="diff_attn",
    )(p, kt, p, band, lam_q, lam_k, subln_g)


SLAB_ROWS = D_MODEL // (2 * LANES)


def _store_slabs(ref, row0, x_bf):
    n = x_bf.shape[0]
    bits = pltpu.bitcast(x_bf.astype(F32), jnp.uint32)
    for c in range(SLAB_ROWS):
        lo = bits[:, 2 * c * LANES:(2 * c + 1) * LANES]
        hi = bits[:, (2 * c + 1) * LANES:(2 * c + 2) * LANES]
        ref[pl.ds(SLAB_ROWS * row0 + c, n, stride=SLAB_ROWS), :] = (lo >> 16) | hi


def _load_slabs(ref, row0, n, every=1):
    parts = []
    for c in range(SLAB_ROWS):
        words = ref[pl.ds(SLAB_ROWS * row0 + c, n, stride=SLAB_ROWS * every), :]
        parts.append(pltpu.bitcast(words << 16, F32))
        parts.append(pltpu.bitcast(words & jnp.uint32(0xFFFF0000), F32))
    return jnp.concatenate(parts, axis=1)


def _mix_kernel(on_ref, yb_ref, ga_ref, gb_ref, xn_ref, wa_ref, wb_ref, wo_ref, bg_ref,
                g1_ref, b1_ref, wr_ref, br_ref, x1_ref, xp_ref, lg_ref, *, tm, sub):
    bg = bg_ref[...]
    for lo in range(0, tm, sub):
        rows = slice(lo, lo + sub)
        y_b = jnp.dot(yb_ref[rows, :], wb_ref[...], preferred_element_type=F32)
        y_a = jnp.dot(on_ref[rows, :], wa_ref[...], preferred_element_type=F32)
        g_a = jax.nn.sigmoid(ga_ref[rows, :].astype(F32) + bg[0:1])
        g_b = jax.nn.sigmoid(gb_ref[rows, :].astype(F32) + bg[1:2])
        merged = (g_a * y_a + g_b * y_b).astype(BF16)
        mix = jnp.dot(merged, wo_ref[...], preferred_element_type=F32)
        x1 = _layer_norm(DEEPNORM_ALPHA * xn_ref[rows, :] + mix, g1_ref[...], b1_ref[...])
        x1_ref[rows, :] = x1
        x_hi = x1.astype(BF16)
        x_lo = (x1 - x_hi.astype(F32)).astype(BF16)
        part = (jnp.dot(x_hi, wr_ref[...], preferred_element_type=F32)
                + jnp.dot(x_lo, wr_ref[...], preferred_element_type=F32))
        lg_ref[rows, :] = part[:, 0:LANES] + part[:, LANES:2 * LANES] + br_ref[...]
        _store_slabs(xp_ref, lo, x_hi)


def _mix(o_n, p, xn, wa, wb, wo, b_gate, g1, b1, wr2, br, tm, sub):
    n_tok = xn.shape[0]
    kern = functools.partial(_mix_kernel, tm=tm, sub=sub)

    def col(c):
        return pl.BlockSpec((tm, D_MODEL), lambda r, c=c: (r, c))

    return pl.pallas_call(
        kern,
        grid=(n_tok // tm,),
        in_specs=[
            pl.BlockSpec((tm, D_MODEL), lambda r: (r, 0)),
            col(P_CONV), col(P_GATE_A), col(P_GATE_B),
            pl.BlockSpec((tm, D_MODEL), lambda r: (r, 0)),
            _const_spec((D_MODEL, D_MODEL)), _const_spec((D_MODEL, D_MODEL)),
            _const_spec((D_MODEL, D_MODEL)),
            _const_spec((2, D_MODEL)), _const_spec((1, D_MODEL)), _const_spec((1, D_MODEL)),
            _const_spec((D_MODEL, 2 * LANES)), _const_spec((1, LANES)),
        ],
        out_specs=[
            pl.BlockSpec((tm, D_MODEL), lambda r: (r, 0)),
            pl.BlockSpec((tm * SLAB_ROWS, LANES), lambda r: (r, 0)),
            pl.BlockSpec((tm, LANES), lambda r: (r, 0)),
        ],
        out_shape=[
            jax.ShapeDtypeStruct((n_tok, D_MODEL), F32),
            jax.ShapeDtypeStruct((n_tok * SLAB_ROWS, LANES), jnp.uint32),
            jax.ShapeDtypeStruct((n_tok, LANES), F32),
        ],
        compiler_params=pltpu.CompilerParams(
            dimension_semantics=("arbitrary",), vmem_limit_bytes=VMEM_LIMIT),
        name="mix",
    )(o_n, p, p, p, xn, wa, wb, wo, b_gate, g1, b1, wr2, br)


ROUTE_COLS = 8
EXPERT_LANE0 = 8
RANK_ROWS = 256


def _route_kernel(lg_ref, tri_ref, route_ref, route_t_ref, cnt_ref, run_ref, *, tr):
    @pl.when(pl.program_id(0) == 0)
    def _():
        run_ref[...] = jnp.zeros_like(run_ref)

    lt = jnp.transpose(lg_ref[...])
    row = lax.broadcasted_iota(jnp.int32, (EXPERTS_PER_GROUP, tr), 0).astype(F32)

    def first_argmax(vals, vmax):
        return jnp.min(jnp.where(vals == vmax, row, float(EXPERTS_PER_GROUP)),
                       axis=0, keepdims=True)

    gl = jnp.where(row < N_GROUPS, lt[0:EXPERTS_PER_GROUP], -jnp.inf)
    gmax = jnp.max(gl, axis=0, keepdims=True)
    gsum = jnp.sum(jnp.exp(gl - gmax), axis=0, keepdims=True)
    g_p = 1.0 / gsum
    g_idx = first_argmax(gl, gmax)

    def group_rows(g):
        lo = EXPERT_LANE0 + EXPERTS_PER_GROUP * g
        return lt[lo:lo + EXPERTS_PER_GROUP]

    sl = group_rows(N_GROUPS - 1)
    for g in reversed(range(N_GROUPS - 1)):
        sl = jnp.where(g_idx == g, group_rows(g), sl)
    s1 = jnp.max(sl, axis=0, keepdims=True)
    i1 = first_argmax(sl, s1)
    sl2 = jnp.where(row == i1, -jnp.inf, sl)
    s2 = jnp.max(sl2, axis=0, keepdims=True)
    i2 = first_argmax(sl2, s2)
    t = jnp.exp(s2 - s1)
    w1 = g_p / (1.0 + t)
    w2 = g_p * t / (1.0 + t)
    e1 = EXPERTS_PER_GROUP * g_idx + i1
    e2 = EXPERTS_PER_GROUP * g_idx + i2

    e_row = lax.broadcasted_iota(jnp.int32, (N_EXPERTS, tr), 0).astype(F32)
    oh1 = e_row == e1
    oh2 = e_row == e2
    onehot_t = jnp.concatenate([jnp.where(oh1 | oh2, 1.0, 0.0),
                                jnp.zeros((LANES - N_EXPERTS, tr), F32)], axis=0)
    onehot = jnp.transpose(onehot_t)
    running = run_ref[...]
    groups = []
    for r0 in range(0, tr, RANK_ROWS):
        part = onehot[r0:r0 + RANK_ROWS]
        groups.append(jnp.dot(tri_ref[...], part.astype(BF16), preferred_element_type=F32)
                      + running)
        running = running + jnp.sum(part, axis=0, keepdims=True)
    before_t = jnp.transpose(jnp.concatenate(groups, axis=0))[0:N_EXPERTS]
    rank1 = jnp.sum(jnp.where(oh1, before_t, 0.0), axis=0, keepdims=True)
    rank2 = jnp.sum(jnp.where(oh2, before_t, 0.0), axis=0, keepdims=True)
    run_ref[...] = running
    cnt_ref[...] = jnp.broadcast_to(running, cnt_ref.shape)

    zero = jnp.zeros((1, tr), F32)
    route_t = jnp.concatenate([e1, e2, w1, w2, rank1, rank2, zero, zero], axis=0)
    route_t_ref[...] = route_t
    padded = jnp.concatenate([route_t, jnp.zeros((LANES - ROUTE_COLS, tr), F32)], axis=0)
    route_ref[...] = jnp.transpose(padded)[:, 0:ROUTE_COLS]


def _route(logits, tr):
    n_tok = logits.shape[0]
    assert tr % RANK_ROWS == 0
    tri = jnp.asarray(np.tril(np.ones((RANK_ROWS, RANK_ROWS), np.float32), k=-1), BF16)
    return pl.pallas_call(
        functools.partial(_route_kernel, tr=tr),
        grid=(n_tok // tr,),
        in_specs=[pl.BlockSpec((tr, LANES), lambda r: (r, 0)),
                  _const_spec((RANK_ROWS, RANK_ROWS))],
        out_specs=[
            pl.BlockSpec((tr, ROUTE_COLS), lambda r: (r, 0)),
            pl.BlockSpec((ROUTE_COLS, tr), lambda r: (0, r)),
            pl.BlockSpec((8, LANES), lambda r: (0, 0)),
        ],
        out_shape=[
            jax.ShapeDtypeStruct((n_tok, ROUTE_COLS), F32),
            jax.ShapeDtypeStruct((ROUTE_COLS, n_tok), F32),
            jax.ShapeDtypeStruct((8, LANES), F32),
        ],
        scratch_shapes=[pltpu.VMEM((1, LANES), F32)],
        compiler_params=pltpu.CompilerParams(
            dimension_semantics=("arbitrary",), vmem_limit_bytes=VMEM_LIMIT),
        name="route",
    )(logits, tri)


def _slab_copy(src, src_row, dst, dst_row, sem):
    return pltpu.make_async_copy(src.at[src_row], dst.at[dst_row], sem)


N_ZERO_BLOCKS = 2 * N_EXPERTS


def _dispatch_kernel(zb_ref, dest_ref, x_ref, out_ref, zeros, sem, zsem, *, td):
    @pl.when(pl.program_id(0) == 0)
    def _():
        zeros[...] = jnp.zeros_like(zeros)

        def block_copy(n):
            row0 = pl.multiple_of(zb_ref[n] * MOE_BLOCK, MOE_BLOCK)
            return pltpu.make_async_copy(zeros, out_ref.at[pl.ds(row0, MOE_BLOCK)], zsem)

        n_zero = zb_ref[N_ZERO_BLOCKS]

        def start(n, carry):
            block_copy(n).start()
            return carry

        def wait(n, carry):
            block_copy(n).wait()
            return carry

        lax.fori_loop(0, n_zero, start, 0)
        lax.fori_loop(0, n_zero, wait, 0)

    def issue(t, carry):
        for k in range(TOP_K):
            _slab_copy(x_ref, t, out_ref, dest_ref[0, 0, k * td + t], sem).start(priority=k)
        return carry

    lax.fori_loop(0, td, issue, 0, unroll=8)
    for k in range(TOP_K):
        pltpu.make_async_copy(x_ref, out_ref.at[pl.ds(0, td)], sem).wait()


def _dispatch(xp, dest3, zero_blocks, n_rows, td):
    n_tok = xp.shape[0]
    grid_spec = pltpu.PrefetchScalarGridSpec(
        num_scalar_prefetch=1,
        grid=(n_tok // td,),
        in_specs=[
            pl.BlockSpec((1, 1, TOP_K * td), lambda s, zb: (s, 0, 0), memory_space=pltpu.SMEM),
            pl.BlockSpec((td, SLAB_ROWS, LANES), lambda s, zb: (s, 0, 0)),
        ],
        out_specs=pl.BlockSpec(memory_space=pl.ANY),
        scratch_shapes=[
            pltpu.VMEM((MOE_BLOCK, SLAB_ROWS, LANES), jnp.uint32),
            pltpu.SemaphoreType.DMA(()),
            pltpu.SemaphoreType.DMA(()),
        ],
    )
    return pl.pallas_call(
        functools.partial(_dispatch_kernel, td=td),
        grid_spec=grid_spec,
        out_shape=jax.ShapeDtypeStruct((n_rows, SLAB_ROWS, LANES), jnp.uint32),
        compiler_params=pltpu.CompilerParams(dimension_semantics=("arbitrary",)),
        name="dispatch",
    )(zero_blocks, dest3, xp)


FFN_SUB = 512


def _expert_ffn_kernel(be_ref, na_ref, nx_ref, x_ref, wg_hbm, wu_hbm, wd_hbm, y_ref,
                       wg_f32, wu_f32, wd_f32, wg_bf, wu_bf, wd_bf, sem):
    i = pl.program_id(0)
    active = i < na_ref[0]
    expert = be_ref[i]
    new_expert = (i == 0) | (expert != be_ref[jnp.maximum(i - 1, 0)])

    def fetch(e):
        return (pltpu.make_async_copy(wg_hbm.at[e], wg_f32, sem.at[0]),
                pltpu.make_async_copy(wu_hbm.at[e], wu_f32, sem.at[1]),
                pltpu.make_async_copy(wd_hbm.at[e], wd_f32, sem.at[2]))

    @pl.when(active & (i == 0))
    def _():
        for cp in fetch(expert):
            cp.start()

    @pl.when(active & new_expert)
    def _():
        for cp in fetch(expert):
            cp.wait()
        wg_bf[...] = wg_f32[...].astype(BF16)
        wu_bf[...] = wu_f32[...].astype(BF16)
        wd_bf[...] = wd_f32[...].astype(BF16)
        nxt = nx_ref[expert]

        @pl.when(nxt >= 0)
        def _():
            for cp in fetch(nxt):
                cp.start()

    @pl.when(active)
    def _():
        for lo in range(0, MOE_BLOCK, FFN_SUB):
            x = _load_slabs(x_ref, lo, FFN_SUB).astype(BF16)
            g = jnp.dot(x, wg_bf[...], preferred_element_type=F32)
            u = jnp.dot(x, wu_bf[...], preferred_element_type=F32)
            hid = (jax.nn.silu(g) * u).astype(BF16)
            y = jnp.dot(hid, wd_bf[...], preferred_element_type=F32)
            _store_slabs(y_ref, lo, y.astype(BF16))

    @pl.when(jnp.logical_not(active))
    def _():
        y_ref[...] = jnp.zeros_like(y_ref)


def _expert_ffn(xs, block_expert, n_active, next_expert, wg, wu, wd):
    n_rows = xs.shape[0] // SLAB_ROWS
    n_blocks = n_rows // MOE_BLOCK

    def blk(i, be, na, nx):
        return (jnp.maximum(jnp.minimum(i, na[0] - 1), 0), 0)

    grid_spec = pltpu.PrefetchScalarGridSpec(
        num_scalar_prefetch=3,
        grid=(n_blocks,),
        in_specs=[
            pl.BlockSpec((MOE_BLOCK * SLAB_ROWS, LANES), blk),
            pl.BlockSpec(memory_space=pl.ANY),
            pl.BlockSpec(memory_space=pl.ANY),
            pl.BlockSpec(memory_space=pl.ANY),
        ],
        out_specs=pl.BlockSpec((MOE_BLOCK * SLAB_ROWS, LANES), lambda i, be, na, nx: (i, 0)),
        scratch_shapes=[
            pltpu.VMEM((D_MODEL, D_EXPERT), F32),
            pltpu.VMEM((D_MODEL, D_EXPERT), F32),
            pltpu.VMEM((D_EXPERT, D_MODEL), F32),
            pltpu.VMEM((D_MODEL, D_EXPERT), BF16),
            pltpu.VMEM((D_MODEL, D_EXPERT), BF16),
            pltpu.VMEM((D_EXPERT, D_MODEL), BF16),
            pltpu.SemaphoreType.DMA((3,)),
        ],
    )
    return pl.pallas_call(
        _expert_ffn_kernel,
        grid_spec=grid_spec,
        out_shape=jax.ShapeDtypeStruct((n_rows * SLAB_ROWS, LANES), jnp.uint32),
        compiler_params=pltpu.CompilerParams(
            dimension_semantics=("arbitrary",), vmem_limit_bytes=VMEM_LIMIT),
        name="expert_ffn",
    )(block_expert, n_active, next_expert, xs, wg, wu, wd)


def _combine_kernel(dcur_ref, dnxt_ref, x_ref, route_ref, g_ref, b_ref, y_ref, o_ref,
                    buf, sem, *, td):
    s = pl.program_id(0)
    n = pl.num_programs(0)
    slot = s % 2

    def base(slot_):
        return pl.multiple_of(slot_ * TOP_K * td, TOP_K * td)

    def issue(d_ref, to_slot):
        def body(t, carry):
            for k in range(TOP_K):
                _slab_copy(y_ref, d_ref[0, 0, k * td + t], buf, base(to_slot) + TOP_K * t + k,
                           sem.at[to_slot]).start(priority=k)
            return carry
        lax.fori_loop(0, td, body, 0, unroll=8)

    @pl.when(s == 0)
    def _():
        issue(dcur_ref, 0)

    @pl.when(s + 1 < n)
    def _():
        issue(dnxt_ref, 1 - slot)

    pltpu.make_async_copy(y_ref.at[pl.ds(0, TOP_K * td)], buf.at[pl.ds(base(slot), TOP_K * td)],
                          sem.at[slot]).wait()

    route = route_ref[...]
    buf2 = buf.reshape(2 * TOP_K * td * SLAB_ROWS, LANES)
    ffn = (route[:, 2:3] * _load_slabs(buf2, base(slot), td, every=TOP_K)
           + route[:, 3:4] * _load_slabs(buf2, base(slot) + 1, td, every=TOP_K))
    o_ref[...] = _layer_norm(DEEPNORM_ALPHA * x_ref[...] + ffn, g_ref[...], b_ref[...])


def _combine(ys, dest3, x1, route, g2, b2, td):
    n_tok = x1.shape[0]
    n_steps = n_tok // td
    return pl.pallas_call(
        functools.partial(_combine_kernel, td=td),
        grid=(n_steps,),
        in_specs=[
            pl.BlockSpec((1, 1, TOP_K * td), lambda s: (s, 0, 0), memory_space=pltpu.SMEM),
            pl.BlockSpec((1, 1, TOP_K * td), lambda s: (jnp.minimum(s + 1, n_steps - 1), 0, 0),
                         memory_space=pltpu.SMEM),
            pl.BlockSpec((td, D_MODEL), lambda s: (s, 0)),
            pl.BlockSpec((td, ROUTE_COLS), lambda s: (s, 0)),
            pl.BlockSpec((1, D_MODEL), lambda s: (0, 0)),
            pl.BlockSpec((1, D_MODEL), lambda s: (0, 0)),
            pl.BlockSpec(memory_space=pl.ANY),
        ],
        out_specs=pl.BlockSpec((td, D_MODEL), lambda s: (s, 0)),
        out_shape=jax.ShapeDtypeStruct((n_tok, D_MODEL), F32),
        scratch_shapes=[
            pltpu.VMEM((2 * TOP_K * td, SLAB_ROWS, LANES), jnp.uint32),
            pltpu.SemaphoreType.DMA((2,)),
        ],
        compiler_params=pltpu.CompilerParams(
            dimension_semantics=("arbitrary",), vmem_limit_bytes=VMEM_LIMIT),
        name="combine",
    )(dest3, dest3, x1, route, g2, b2, ys)


def _row_tile(n, want):
    t = min(want, n)
    while n % t:
        t //= 2
    return t


def kernel(x, ln_in_g, ln_in_b, w_in, b_gate, lambda_q, lambda_k, subln_g, rel_bias, conv_w,
           w_a_proj, w_b_proj, w_o, ln1_g, ln1_b, w_group, b_group, w_sub, b_sub,
           w_gate_e, w_up_e, w_down_e, ln2_g, ln2_b):
    bsz, seq, d = x.shape
    assert DEPTH == 1 and d == D_MODEL and w_in.shape == (DEPTH, D_MODEL, N_IN)
    n_tok = bsz * seq
    tq = _row_tile(seq, 256)
    assert tq % LANES == 0 and tq % CHUNK == 0
    qpb = _row_tile(seq // tq, 2)
    tm1 = _row_tile(seq, 512)
    tm3 = _row_tile(n_tok, 1024)
    sub3 = _row_tile(tm3, 256)
    tr = _row_tile(n_tok, 1024)
    td = _row_tile(n_tok, 2048)
    tc = _row_tile(n_tok, 512)
    row = lambda v: v.reshape(1, -1).astype(F32)
    lam_init = 0.8 - 0.6 * math.exp(-0.3 * 0)

    wkt = jnp.transpose(lax.optimization_barrier(w_in[0, :, D_MODEL:2 * D_MODEL])).astype(BF16)
    xn, p, kt = _ln_proj(x.reshape(n_tok, d), row(ln_in_g), row(ln_in_b), w_in[0].astype(BF16),
                         wkt, conv_w[0].reshape(3, d).astype(F32), seq, tm1)

    band = _band_bias(rel_bias, tq)
    o_n = _diff_attn(p, kt, band, lambda_q[0].astype(F32), lambda_k[0].astype(F32),
                     row(subln_g[0]), bsz, seq, tq, qpb, lam_init)

    gap = EXPERT_LANE0 - N_GROUPS
    w_r = jnp.concatenate(
        [w_group[0].astype(F32), jnp.zeros((d, gap), F32),
         jnp.transpose(w_sub[0].astype(F32), (1, 0, 2)).reshape(d, N_EXPERTS)], axis=1)
    w_r = jnp.pad(w_r, ((0, 0), (0, LANES - w_r.shape[1])))
    w_hi = w_r.astype(BF16)
    w_lo = (w_r - w_hi.astype(F32)).astype(BF16)
    w_r2 = jnp.concatenate([w_hi, w_lo], axis=1)
    b_r = jnp.concatenate([b_group[0].astype(F32), jnp.zeros((gap,), F32),
                           b_sub[0].astype(F32).reshape(-1)])
    b_r = jnp.pad(b_r, (0, LANES - b_r.shape[0])).reshape(1, LANES)
    x1, xp, logits = _mix(
        o_n, p, xn, w_a_proj[0].astype(BF16), w_b_proj[0].astype(BF16), w_o[0].astype(BF16),
        b_gate[0].astype(F32), row(ln1_g[0]), row(ln1_b[0]), w_r2, b_r, tm3, sub3)
    route, route_t, counts = _route(logits, tr)

    n_assign = n_tok * TOP_K
    n_blocks = -(-n_assign // MOE_BLOCK) + N_EXPERTS
    cnt = counts[0, :N_EXPERTS].astype(jnp.int32)
    padded = ((cnt + MOE_BLOCK - 1) // MOE_BLOCK) * MOE_BLOCK
    pad_end = jnp.cumsum(padded)
    pad_start = pad_end - padded
    n_active = (pad_end[-1:] // MOE_BLOCK).astype(jnp.int32)
    blk_row0 = jnp.arange(n_blocks, dtype=jnp.int32) * MOE_BLOCK
    block_expert = jnp.minimum(
        jnp.sum((pad_end[None, :] <= blk_row0[:, None]).astype(jnp.int32), axis=1),
        N_EXPERTS - 1).astype(jnp.int32)
    experts = route_t[0:TOP_K].astype(jnp.int32)
    ranks = route_t[4:4 + TOP_K].astype(jnp.int32)
    is_e = experts[None] == jnp.arange(N_EXPERTS, dtype=jnp.int32)[:, None, None]
    dest = jnp.sum(jnp.where(is_e, pad_start[:, None, None], 0), axis=0) + ranks

    def per_tile(t):
        tiles = jnp.transpose(dest.reshape(TOP_K, n_tok // t, t), (1, 0, 2))
        return tiles.reshape(n_tok // t, 1, TOP_K * t).astype(jnp.int32)

    last_blk = jnp.maximum(pad_end // MOE_BLOCK - 1, 0)
    idle_blk = jnp.minimum(n_active[0] + jnp.arange(N_EXPERTS), n_blocks - 1)
    n_zero = N_EXPERTS + n_blocks - n_active
    zero_blocks = jnp.concatenate([last_blk, idle_blk, n_zero]).astype(jnp.int32)
    n_rows = n_blocks * MOE_BLOCK
    xs = _dispatch(xp.reshape(n_tok, SLAB_ROWS, LANES), per_tile(td), zero_blocks, n_rows, td)
    e_ids = jnp.arange(N_EXPERTS, dtype=jnp.int32)
    later = (padded > 0)[None, :] & (e_ids[None, :] > e_ids[:, None])
    next_expert = jnp.min(jnp.where(later, e_ids[None, :], N_EXPERTS), axis=1)
    next_expert = jnp.where(next_expert < N_EXPERTS, next_expert, -1).astype(jnp.int32)
    ys = _expert_ffn(xs.reshape(n_rows * SLAB_ROWS, LANES), block_expert, n_active, next_expert,
                     w_gate_e[0].astype(F32), w_up_e[0].astype(F32), w_down_e[0].astype(F32))
    out = _combine(ys.reshape(n_rows, SLAB_ROWS, LANES), per_tile(tc), x1, route,
                   row(ln2_g[0]), row(ln2_b[0]), tc)
    return out.reshape(bsz, seq, d)
```

```python
import functools
import math

import numpy as np
import jax
import jax.numpy as jnp
from jax import lax
from jax.experimental import pallas as pl
from jax.experimental.pallas import tpu as pltpu

F32 = jnp.float32
BF16 = jnp.bfloat16

D_MODEL = 1024
N_HEADS = 4
HEAD_DIM = 128
D_HEAD_V = 2 * HEAD_DIM
CHUNK = 64
N_BUCKETS = 32
MAX_DISTANCE = 128
N_GROUPS = 4
EXPERTS_PER_GROUP = 8
N_EXPERTS = N_GROUPS * EXPERTS_PER_GROUP
TOP_K = 2
D_EXPERT = 512
MOE_BLOCK = 512
LN_EPS = 1e-5
RMS_EPS = 1e-6
DEPTH = 1
DEEPNORM_ALPHA = (2.0 * DEPTH) ** 0.25
N_IN = 8 * D_MODEL
LANES = 128
NEG_BIG = -1e30
LOG2_E = math.log2(math.e)

VMEM_LIMIT = 56 * 1024 * 1024


def _layer_norm(x, g, b):
    mu = jnp.mean(x, axis=-1, keepdims=True)
    xc = x - mu
    var = jnp.mean(xc * xc, axis=-1, keepdims=True)
    return xc * lax.rsqrt(var + LN_EPS) * g + b


def _const_spec(shape):
    return pl.BlockSpec(shape, lambda *_: (0,) * len(shape), pipeline_mode=pl.Buffered(1))


HALO = 16
P_Q, P_V, P_CONV, P_GATE_A, P_GATE_B = range(5)
N_P = 5 * D_MODEL


def _ln_proj_kernel(x_ref, g_ref, b_ref, w_ref, wkt_ref, cw_ref, xn_ref, p_ref, kt_ref, zs_ref,
                    *, q_scale, tm, sub, tiles_per_seq):
    r = pl.program_id(0)

    @pl.when((r % tiles_per_seq) == 0)
    def _():
        zs_ref[0:HALO, :] = jnp.zeros((HALO, D_MODEL), F32)

    cw = cw_ref[...]
    for lo in range(0, tm, sub):
        rows = slice(lo, lo + sub)
        xn = _layer_norm(x_ref[rows, :], g_ref[...], b_ref[...])
        xn_ref[rows, :] = xn
        xb = xn.astype(BF16)

        def proj(c, xb=xb):
            return jnp.dot(xb, w_ref[:, c * D_MODEL:(c + 1) * D_MODEL],
                           preferred_element_type=F32)

        def put(block, val, rows=rows):
            p_ref[rows, block * D_MODEL:(block + 1) * D_MODEL] = val.astype(BF16)

        put(P_Q, proj(0) * q_scale)
        put(P_V, proj(2))
        kt = lax.dot_general(wkt_ref[...], xb, (((1,), (1,)), ((), ())),
                             preferred_element_type=F32)
        kt_ref[:, rows] = kt.astype(BF16)
        z0 = HALO + lo
        zs_ref[z0:z0 + sub, :] = proj(4) * proj(5)
        zc = (cw[0:1] * zs_ref[z0 - 2:z0 - 2 + sub, :] + cw[1:2] * zs_ref[z0 - 1:z0 - 1 + sub, :]
              + cw[2:3] * zs_ref[z0:z0 + sub, :])
        put(P_CONV, proj(3) * zc)
        put(P_GATE_A, proj(6))
        put(P_GATE_B, proj(7))

    zs_ref[0:HALO, :] = zs_ref[tm:tm + HALO, :]


def _ln_proj(x2, g, b, w_bf, wkt, conv_w, seq, tm):
    n_tok = x2.shape[0]
    kern = functools.partial(_ln_proj_kernel, q_scale=HEAD_DIM ** -0.5 * LOG2_E, tm=tm,
                             sub=_row_tile(tm, 256), tiles_per_seq=seq // tm)
    return pl.pallas_call(
        kern,
        grid=(n_tok // tm,),
        in_specs=[
            pl.BlockSpec((tm, D_MODEL), lambda r: (r, 0)),
            _const_spec((1, D_MODEL)),
            _const_spec((1, D_MODEL)),
            _const_spec((D_MODEL, N_IN)),
            _const_spec((D_MODEL, D_MODEL)),
            _const_spec((3, D_MODEL)),
        ],
        out_specs=[
            pl.BlockSpec((tm, D_MODEL), lambda r: (r, 0)),
            pl.BlockSpec((tm, N_P), lambda r: (r, 0)),
            pl.BlockSpec((D_MODEL, tm), lambda r: (0, r)),
        ],
        out_shape=[
            jax.ShapeDtypeStruct((n_tok, D_MODEL), F32),
            jax.ShapeDtypeStruct((n_tok, N_P), BF16),
            jax.ShapeDtypeStruct((D_MODEL, n_tok), BF16),
        ],
        scratch_shapes=[pltpu.VMEM((HALO + tm, D_MODEL), F32)],
        compiler_params=pltpu.CompilerParams(
            dimension_semantics=("arbitrary",), vmem_limit_bytes=VMEM_LIMIT),
        name="ln_proj",
    )(x2, g, b, w_bf, wkt, conv_w)


def _t5_bucket_np(rel):
    nb = N_BUCKETS // 2
    max_exact = nb // 2
    n = np.abs(rel)
    large = np.full(n.shape, max_exact, np.int64)
    for d in range(max_exact, MAX_DISTANCE + 1):
        val = max_exact + int(math.log(d / max_exact) / math.log(MAX_DISTANCE / max_exact)
                              * (nb - max_exact))
        large = np.where(n >= d, min(val, nb - 1), large)
    return np.where(rel > 0, nb, 0) + np.where(n < max_exact, n, large)


def _band_bias(rel_bias, tq):
    qi = np.arange(tq)[:, None]
    kj = np.arange(2 * tq)[None, :] - tq
    allowed = (kj // CHUNK) <= (qi // CHUNK)
    far_bucket = int(_t5_bucket_np(np.array([-(tq + 1)]))[0])
    rb = rel_bias.astype(F32)
    rb = ((rb - rb[far_bucket][None, :]) * LOG2_E).T
    n_rel = 3 * tq
    bucket = _t5_bucket_np(np.arange(n_rel) - (2 * tq - 1))
    onehot = jnp.asarray(bucket[:, None] == np.arange(N_BUCKETS)[None, :])
    per_rel = jnp.sum(jnp.where(onehot[None], rb[:, None, :], 0.0), axis=-1)
    skew = jnp.tile(per_rel, (1, tq))[:, :tq * (n_rel - 1)].reshape(N_HEADS, tq, n_rel - 1)
    tile = skew[:, :, tq - 1:3 * tq - 1]
    return jnp.where(jnp.asarray(allowed)[None], tile, NEG_BIG)


QK_BLOCKS = 2
PV_BLOCKS = 2


def _diff_attn_kernel(q_ref, kt_ref, v_ref, bias_ref, lq_ref, lk_ref, sg_ref, o_ref, s_ref,
                      *, tq, nq, qpb, lam_init):
    dots = jnp.sum(lq_ref[...] * lk_ref[...], axis=-1, keepdims=True)
    lam = jnp.exp(dots[0:1]) - jnp.exp(dots[1:2]) + lam_init
    gain = sg_ref[...] * (1.0 - lam_init)

    def lane_tiles(s):
        return [s[:, c * LANES:(c + 1) * LANES] for c in range(s.shape[1] // LANES)]

    def logits_pass(h, a, i):
        rows = slice(a * tq, (a + 1) * tq)
        row_max = []
        for m in range(2):
            c = h * D_HEAD_V + m * HEAD_DIM
            q = q_ref[rows, c:c + HEAD_DIM]
            mx = None
            for j0 in range(0, i + 1, QK_BLOCKS):
                nb = min(QK_BLOCKS, i + 1 - j0)
                wide = jnp.dot(q, kt_ref[c:c + HEAD_DIM, j0 * tq:(j0 + nb) * tq],
                               preferred_element_type=F32)
                for j in range(j0, j0 + nb):
                    s = wide[:, (j - j0) * tq:(j - j0 + 1) * tq]
                    if j == i:
                        s = s + bias_ref[h, :, tq:2 * tq]
                    elif j == i - 1:
                        s = s + bias_ref[h, :, 0:tq]
                    s_ref[h % s_ref.shape[0], a, m, j] = s
                    t = functools.reduce(jnp.maximum, lane_tiles(s))
                    mx = t if mx is None else jnp.maximum(mx, t)
            row_max.append(jnp.broadcast_to(jnp.max(mx, axis=-1, keepdims=True), (tq, LANES)))
        return row_max

    def values_pass(h, a, i, row_max):
        c0 = h * D_HEAD_V
        acc = [None, None]
        lsum = [None, None]
        for j0 in range(0, i + 1, PV_BLOCKS):
            nb = min(PV_BLOCKS, i + 1 - j0)
            v_blk = v_ref[j0 * tq:(j0 + nb) * tq, c0:c0 + D_HEAD_V]
            for m in range(2):
                tiles = []
                for j in range(j0, j0 + nb):
                    tiles += [jnp.exp2(x - row_max[m]) for x in lane_tiles(s_ref[h % s_ref.shape[0], a, m, j])]
                tsum = functools.reduce(jnp.add, tiles)
                lsum[m] = tsum if lsum[m] is None else lsum[m] + tsum
                p = jnp.concatenate(tiles, axis=1).astype(BF16)
                part = jnp.dot(p, v_blk, preferred_element_type=F32)
                acc[m] = part if acc[m] is None else acc[m] + part
        r1 = 1.0 / jnp.sum(lsum[0], axis=-1, keepdims=True)
        r2 = lam / jnp.sum(lsum[1], axis=-1, keepdims=True)
        o = acc[0] * r1 - acc[1] * r2
        o = o * lax.rsqrt(jnp.mean(o * o, axis=-1, keepdims=True) + RMS_EPS)
        o_ref[a * tq:(a + 1) * tq, c0:c0 + D_HEAD_V] = (o * gain).astype(BF16)

    def query_tile(step):
        for h in range(N_HEADS):
            maxima = [logits_pass(h, a, step * qpb + a) for a in range(qpb)]
            for a in range(qpb):
                values_pass(h, a, step * qpb + a, maxima[a])

    lax.switch(pl.program_id(1), [functools.partial(query_tile, s) for s in range(nq // qpb)])


def _diff_attn(p, kt, band, lam_q, lam_k, subln_g, bsz, seq, tq, qpb, lam_init):
    n_tok = bsz * seq
    nq = seq // tq
    steps = nq // qpb
    s_heads = N_HEADS if qpb == 1 else 1
    kern = functools.partial(_diff_attn_kernel, tq=tq, nq=nq, qpb=qpb, lam_init=lam_init)
    return pl.pallas_call(
        kern,
        grid=(bsz, steps),
        in_specs=[
            pl.BlockSpec((qpb * tq, D_MODEL), lambda b, i: (b * steps + i, P_Q)),
            pl.BlockSpec((D_MODEL, seq), lambda b, i: (0, b)),
            pl.BlockSpec((seq, D_MODEL), lambda b, i: (b, P_V)),
            _const_spec((N_HEADS, tq, 2 * tq)),
            _const_spec((2, HEAD_DIM)),
            _const_spec((2, HEAD_DIM)),
            _const_spec((1, D_HEAD_V)),
        ],
        out_specs=pl.BlockSpec((qpb * tq, D_MODEL), lambda b, i: (b * steps + i, 0)),
        out_shape=jax.ShapeDtypeStruct((n_tok, D_MODEL), BF16),
        scratch_shapes=[
            pltpu.VMEM((s_heads, qpb, 2, nq, tq, tq), F32),
        ],
        compiler_params=pltpu.CompilerParams(
            dimension_semantics=("arbitrary", "arbitrary"), vmem_limit_bytes=VMEM_LIMIT),
        name="diff_attn",
    )(p, kt, p, band, lam_q, lam_k, subln_g)


SLAB_ROWS = D_MODEL // (2 * LANES)


def _store_slabs(ref, row0, x_bf):
    n = x_bf.shape[0]
    bits = pltpu.bitcast(x_bf.astype(F32), jnp.uint32)
    for c in range(SLAB_ROWS):
        lo = bits[:, 2 * c * LANES:(2 * c + 1) * LANES]
        hi = bits[:, (2 * c + 1) * LANES:(2 * c + 2) * LANES]
        ref[pl.ds(SLAB_ROWS * row0 + c, n, stride=SLAB_ROWS), :] = (lo >> 16) | hi


def _load_slabs(ref, row0, n, every=1):
    parts = []
    for c in range(SLAB_ROWS):
        words = ref[pl.ds(SLAB_ROWS * row0 + c, n, stride=SLAB_ROWS * every), :]
        parts.append(pltpu.bitcast(words << 16, F32))
        parts.append(pltpu.bitcast(words & jnp.uint32(0xFFFF0000), F32))
    return jnp.concatenate(parts, axis=1)


def _mix_kernel(on_ref, yb_ref, ga_ref, gb_ref, xn_ref, wa_ref, wb_ref, wo_ref, bg_ref,
                g1_ref, b1_ref, wr_ref, br_ref, x1_ref, xp_ref, lg_ref, *, tm, sub):
    bg = bg_ref[...]
    for lo in range(0, tm, sub):
        rows = slice(lo, lo + sub)
        y_b = jnp.dot(yb_ref[rows, :], wb_ref[...], preferred_element_type=F32)
        y_a = jnp.dot(on_ref[rows, :], wa_ref[...], preferred_element_type=F32)
        g_a = jax.nn.sigmoid(ga_ref[rows, :].astype(F32) + bg[0:1])
        g_b = jax.nn.sigmoid(gb_ref[rows, :].astype(F32) + bg[1:2])
        merged = (g_a * y_a + g_b * y_b).astype(BF16)
        mix = jnp.dot(merged, wo_ref[...], preferred_element_type=F32)
        x1 = _layer_norm(DEEPNORM_ALPHA * xn_ref[rows, :] + mix, g1_ref[...], b1_ref[...])
        x1_ref[rows, :] = x1
        x_hi = x1.astype(BF16)
        x_lo = (x1 - x_hi.astype(F32)).astype(BF16)
        part = (jnp.dot(x_hi, wr_ref[...], preferred_element_type=F32)
                + jnp.dot(x_lo, wr_ref[...], preferred_element_type=F32))
        lg_ref[rows, :] = part[:, 0:LANES] + part[:, LANES:2 * LANES] + br_ref[...]
        _store_slabs(xp_ref, lo, x_hi)


def _mix(o_n, p, xn, wa, wb, wo, b_gate, g1, b1, wr2, br, tm, sub):
    n_tok = xn.shape[0]
    kern = functools.partial(_mix_kernel, tm=tm, sub=sub)

    def col(c):
        return pl.BlockSpec((tm, D_MODEL), lambda r, c=c: (r, c))

    return pl.pallas_call(
        kern,
        grid=(n_tok // tm,),
        in_specs=[
            pl.BlockSpec((tm, D_MODEL), lambda r: (r, 0)),
            col(P_CONV), col(P_GATE_A), col(P_GATE_B),
            pl.BlockSpec((tm, D_MODEL), lambda r: (r, 0)),
            _const_spec((D_MODEL, D_MODEL)), _const_spec((D_MODEL, D_MODEL)),
            _const_spec((D_MODEL, D_MODEL)),
            _const_spec((2, D_MODEL)), _const_spec((1, D_MODEL)), _const_spec((1, D_MODEL)),
            _const_spec((D_MODEL, 2 * LANES)), _const_spec((1, LANES)),
        ],
        out_specs=[
            pl.BlockSpec((tm, D_MODEL), lambda r: (r, 0)),
            pl.BlockSpec((tm * SLAB_ROWS, LANES), lambda r: (r, 0)),
            pl.BlockSpec((tm, LANES), lambda r: (r, 0)),
        ],
        out_shape=[
            jax.ShapeDtypeStruct((n_tok, D_MODEL), F32),
            jax.ShapeDtypeStruct((n_tok * SLAB_ROWS, LANES), jnp.uint32),
            jax.ShapeDtypeStruct((n_tok, LANES), F32),
        ],
        compiler_params=pltpu.CompilerParams(
            dimension_semantics=("arbitrary",), vmem_limit_bytes=VMEM_LIMIT),
        name="mix",
    )(o_n, p, p, p, xn, wa, wb, wo, b_gate, g1, b1, wr2, br)


ROUTE_COLS = 8
EXPERT_LANE0 = 8
RANK_ROWS = 256


def _route_kernel(lg_ref, tri_ref, route_ref, route_t_ref, cnt_ref, run_ref, *, tr):
    @pl.when(pl.program_id(0) == 0)
    def _():
        run_ref[...] = jnp.zeros_like(run_ref)

    lt = jnp.transpose(lg_ref[...])
    row = lax.broadcasted_iota(jnp.int32, (EXPERTS_PER_GROUP, tr), 0).astype(F32)

    def first_argmax(vals, vmax):
        return jnp.min(jnp.where(vals == vmax, row, float(EXPERTS_PER_GROUP)),
                       axis=0, keepdims=True)

    gl = jnp.where(row < N_GROUPS, lt[0:EXPERTS_PER_GROUP], -jnp.inf)
    gmax = jnp.max(gl, axis=0, keepdims=True)
    gsum = jnp.sum(jnp.exp(gl - gmax), axis=0, keepdims=True)
    g_p = 1.0 / gsum
    g_idx = first_argmax(gl, gmax)

    def group_rows(g):
        lo = EXPERT_LANE0 + EXPERTS_PER_GROUP * g
        return lt[lo:lo + EXPERTS_PER_GROUP]

    sl = group_rows(N_GROUPS - 1)
    for g in reversed(range(N_GROUPS - 1)):
        sl = jnp.where(g_idx == g, group_rows(g), sl)
    s1 = jnp.max(sl, axis=0, keepdims=True)
    i1 = first_argmax(sl, s1)
    sl2 = jnp.where(row == i1, -jnp.inf, sl)
    s2 = jnp.max(sl2, axis=0, keepdims=True)
    i2 = first_argmax(sl2, s2)
    t = jnp.exp(s2 - s1)
    w1 = g_p / (1.0 + t)
    w2 = g_p * t / (1.0 + t)
    e1 = EXPERTS_PER_GROUP * g_idx + i1
    e2 = EXPERTS_PER_GROUP * g_idx + i2

    e_row = lax.broadcasted_iota(jnp.int32, (N_EXPERTS, tr), 0).astype(F32)
    oh1 = e_row == e1
    oh2 = e_row == e2
    onehot_t = jnp.concatenate([jnp.where(oh1 | oh2, 1.0, 0.0),
                                jnp.zeros((LANES - N_EXPERTS, tr), F32)], axis=0)
    onehot = jnp.transpose(onehot_t)
    running = run_ref[...]
    groups = []
    for r0 in range(0, tr, RANK_ROWS):
        part = onehot[r0:r0 + RANK_ROWS]
        groups.append(jnp.dot(tri_ref[...], part.astype(BF16), preferred_element_type=F32)
                      + running)
        running = running + jnp.sum(part, axis=0, keepdims=True)
    before_t = jnp.transpose(jnp.concatenate(groups, axis=0))[0:N_EXPERTS]
    rank1 = jnp.sum(jnp.where(oh1, before_t, 0.0), axis=0, keepdims=True)
    rank2 = jnp.sum(jnp.where(oh2, before_t, 0.0), axis=0, keepdims=True)
    run_ref[...] = running
    cnt_ref[...] = jnp.broadcast_to(running, cnt_ref.shape)

    zero = jnp.zeros((1, tr), F32)
    route_t = jnp.concatenate([e1, e2, w1, w2, rank1, rank2, zero, zero], axis=0)
    route_t_ref[...] = route_t
    padded = jnp.concatenate([route_t, jnp.zeros((LANES - ROUTE_COLS, tr), F32)], axis=0)
    route_ref[...] = jnp.transpose(padded)[:, 0:ROUTE_COLS]


def _route(logits, tr):
    n_tok = logits.shape[0]
    assert tr % RANK_ROWS == 0
    tri = jnp.asarray(np.tril(np.ones((RANK_ROWS, RANK_ROWS), np.float32), k=-1), BF16)
    return pl.pallas_call(
        functools.partial(_route_kernel, tr=tr),
        grid=(n_tok // tr,),
        in_specs=[pl.BlockSpec((tr, LANES), lambda r: (r, 0)),
                  _const_spec((RANK_ROWS, RANK_ROWS))],
        out_specs=[
            pl.BlockSpec((tr, ROUTE_COLS), lambda r: (r, 0)),
            pl.BlockSpec((ROUTE_COLS, tr), lambda r: (0, r)),
            pl.BlockSpec((8, LANES), lambda r: (0, 0)),
        ],
        out_shape=[
            jax.ShapeDtypeStruct((n_tok, ROUTE_COLS), F32),
            jax.ShapeDtypeStruct((ROUTE_COLS, n_tok), F32),
            jax.ShapeDtypeStruct((8, LANES), F32),
        ],
        scratch_shapes=[pltpu.VMEM((1, LANES), F32)],
        compiler_params=pltpu.CompilerParams(
            dimension_semantics=("arbitrary",), vmem_limit_bytes=VMEM_LIMIT),
        name="route",
    )(logits, tri)


def _slab_copy(src, src_row, dst, dst_row, sem):
    return pltpu.make_async_copy(src.at[src_row], dst.at[dst_row], sem)


N_ZERO_BLOCKS = 2 * N_EXPERTS


def _dispatch_kernel(zb_ref, dest_ref, x_ref, out_ref, zeros, sem, zsem, *, td):
    @pl.when(pl.program_id(0) == 0)
    def _():
        zeros[...] = jnp.zeros_like(zeros)

        def block_copy(n):
            row0 = pl.multiple_of(zb_ref[n] * MOE_BLOCK, MOE_BLOCK)
            return pltpu.make_async_copy(zeros, out_ref.at[pl.ds(row0, MOE_BLOCK)], zsem)

        n_zero = zb_ref[N_ZERO_BLOCKS]

        def start(n, carry):
            block_copy(n).start()
            return carry

        def wait(n, carry):
            block_copy(n).wait()
            return carry

        lax.fori_loop(0, n_zero, start, 0)
        lax.fori_loop(0, n_zero, wait, 0)

    def issue(t, carry):
        for k in range(TOP_K):
            _slab_copy(x_ref, t, out_ref, dest_ref[0, 0, k * td + t], sem).start(priority=k)
        return carry

    lax.fori_loop(0, td, issue, 0, unroll=8)
    for k in range(TOP_K):
        pltpu.make_async_copy(x_ref, out_ref.at[pl.ds(0, td)], sem).wait()


def _dispatch(xp, dest3, zero_blocks, n_rows, td):
    n_tok = xp.shape[0]
    grid_spec = pltpu.PrefetchScalarGridSpec(
        num_scalar_prefetch=1,
        grid=(n_tok // td,),
        in_specs=[
            pl.BlockSpec((1, 1, TOP_K * td), lambda s, zb: (s, 0, 0), memory_space=pltpu.SMEM),
            pl.BlockSpec((td, SLAB_ROWS, LANES), lambda s, zb: (s, 0, 0)),
        ],
        out_specs=pl.BlockSpec(memory_space=pl.ANY),
        scratch_shapes=[
            pltpu.VMEM((MOE_BLOCK, SLAB_ROWS, LANES), jnp.uint32),
            pltpu.SemaphoreType.DMA(()),
            pltpu.SemaphoreType.DMA(()),
        ],
    )
    return pl.pallas_call(
        functools.partial(_dispatch_kernel, td=td),
        grid_spec=grid_spec,
        out_shape=jax.ShapeDtypeStruct((n_rows, SLAB_ROWS, LANES), jnp.uint32),
        compiler_params=pltpu.CompilerParams(dimension_semantics=("arbitrary",)),
        name="dispatch",
    )(zero_blocks, dest3, xp)


FFN_SUB = 512


def _expert_ffn_kernel(be_ref, na_ref, nx_ref, x_ref, wg_hbm, wu_hbm, wd_hbm, y_ref,
                       wg_f32, wu_f32, wd_f32, wg_bf, wu_bf, wd_bf, sem):
    i = pl.program_id(0)
    active = i < na_ref[0]
    expert = be_ref[i]
    new_expert = (i == 0) | (expert != be_ref[jnp.maximum(i - 1, 0)])

    def fetch(e):
        return (pltpu.make_async_copy(wg_hbm.at[e], wg_f32, sem.at[0]),
                pltpu.make_async_copy(wu_hbm.at[e], wu_f32, sem.at[1]),
                pltpu.make_async_copy(wd_hbm.at[e], wd_f32, sem.at[2]))

    @pl.when(active & (i == 0))
    def _():
        for cp in fetch(expert):
            cp.start()

    @pl.when(active & new_expert)
    def _():
        for cp in fetch(expert):
            cp.wait()
        wg_bf[...] = wg_f32[...].astype(BF16)
        wu_bf[...] = wu_f32[...].astype(BF16)
        wd_bf[...] = wd_f32[...].astype(BF16)
        nxt = nx_ref[expert]

        @pl.when(nxt >= 0)
        def _():
            for cp in fetch(nxt):
                cp.start()

    @pl.when(active)
    def _():
        for lo in range(0, MOE_BLOCK, FFN_SUB):
            x = _load_slabs(x_ref, lo, FFN_SUB).astype(BF16)
            g = jnp.dot(x, wg_bf[...], preferred_element_type=F32)
            u = jnp.dot(x, wu_bf[...], preferred_element_type=F32)
            hid = (jax.nn.silu(g) * u).astype(BF16)
            y = jnp.dot(hid, wd_bf[...], preferred_element_type=F32)
            _store_slabs(y_ref, lo, y.astype(BF16))

    @pl.when(jnp.logical_not(active))
    def _():
        y_ref[...] = jnp.zeros_like(y_ref)


def _expert_ffn(xs, block_expert, n_active, next_expert, wg, wu, wd):
    n_rows = xs.shape[0] // SLAB_ROWS
    n_blocks = n_rows // MOE_BLOCK

    def blk(i, be, na, nx):
        return (jnp.maximum(jnp.minimum(i, na[0] - 1), 0), 0)

    grid_spec = pltpu.PrefetchScalarGridSpec(
        num_scalar_prefetch=3,
        grid=(n_blocks,),
        in_specs=[
            pl.BlockSpec((MOE_BLOCK * SLAB_ROWS, LANES), blk),
            pl.BlockSpec(memory_space=pl.ANY),
            pl.BlockSpec(memory_space=pl.ANY),
            pl.BlockSpec(memory_space=pl.ANY),
        ],
        out_specs=pl.BlockSpec((MOE_BLOCK * SLAB_ROWS, LANES), lambda i, be, na, nx: (i, 0)),
        scratch_shapes=[
            pltpu.VMEM((D_MODEL, D_EXPERT), F32),
            pltpu.VMEM((D_MODEL, D_EXPERT), F32),
            pltpu.VMEM((D_EXPERT, D_MODEL), F32),
            pltpu.VMEM((D_MODEL, D_EXPERT), BF16),
            pltpu.VMEM((D_MODEL, D_EXPERT), BF16),
            pltpu.VMEM((D_EXPERT, D_MODEL), BF16),
            pltpu.SemaphoreType.DMA((3,)),
        ],
    )
    return pl.pallas_call(
        _expert_ffn_kernel,
        grid_spec=grid_spec,
        out_shape=jax.ShapeDtypeStruct((n_rows * SLAB_ROWS, LANES), jnp.uint32),
        compiler_params=pltpu.CompilerParams(
            dimension_semantics=("arbitrary",), vmem_limit_bytes=VMEM_LIMIT),
        name="expert_ffn",
    )(block_expert, n_active, next_expert, xs, wg, wu, wd)


def _combine_kernel(dcur_ref, dnxt_ref, x_ref, route_ref, g_ref, b_ref, y_ref, o_ref,
                    buf, sem, *, td):
    s = pl.program_id(0)
    n = pl.num_programs(0)
    slot = s % 2

    def base(slot_):
        return pl.multiple_of(slot_ * TOP_K * td, TOP_K * td)

    def issue(d_ref, to_slot):
        def body(t, carry):
            for k in range(TOP_K):
                _slab_copy(y_ref, d_ref[0, 0, k * td + t], buf, base(to_slot) + TOP_K * t + k,
                           sem.at[to_slot]).start(priority=k)
            return carry
        lax.fori_loop(0, td, body, 0, unroll=8)

    @pl.when(s == 0)
    def _():
        issue(dcur_ref, 0)

    @pl.when(s + 1 < n)
    def _():
        issue(dnxt_ref, 1 - slot)

    pltpu.make_async_copy(y_ref.at[pl.ds(0, TOP_K * td)], buf.at[pl.ds(base(slot), TOP_K * td)],
                          sem.at[slot]).wait()

    route = route_ref[...]
    buf2 = buf.reshape(2 * TOP_K * td * SLAB_ROWS, LANES)
    ffn = (route[:, 2:3] * _load_slabs(buf2, base(slot), td, every=TOP_K)
           + route[:, 3:4] * _load_slabs(buf2, base(slot) + 1, td, every=TOP_K))
    o_ref[...] = _layer_norm(DEEPNORM_ALPHA * x_ref[...] + ffn, g_ref[...], b_ref[...])


def _combine(ys, dest3, x1, route, g2, b2, td):
    n_tok = x1.shape[0]
    n_steps = n_tok // td
    return pl.pallas_call(
        functools.partial(_combine_kernel, td=td),
        grid=(n_steps,),
        in_specs=[
            pl.BlockSpec((1, 1, TOP_K * td), lambda s: (s, 0, 0), memory_space=pltpu.SMEM),
            pl.BlockSpec((1, 1, TOP_K * td), lambda s: (jnp.minimum(s + 1, n_steps - 1), 0, 0),
                         memory_space=pltpu.SMEM),
            pl.BlockSpec((td, D_MODEL), lambda s: (s, 0)),
            pl.BlockSpec((td, ROUTE_COLS), lambda s: (s, 0)),
            pl.BlockSpec((1, D_MODEL), lambda s: (0, 0)),
            pl.BlockSpec((1, D_MODEL), lambda s: (0, 0)),
            pl.BlockSpec(memory_space=pl.ANY),
        ],
        out_specs=pl.BlockSpec((td, D_MODEL), lambda s: (s, 0)),
        out_shape=jax.ShapeDtypeStruct((n_tok, D_MODEL), F32),
        scratch_shapes=[
            pltpu.VMEM((2 * TOP_K * td, SLAB_ROWS, LANES), jnp.uint32),
            pltpu.SemaphoreType.DMA((2,)),
        ],
        compiler_params=pltpu.CompilerParams(
            dimension_semantics=("arbitrary",), vmem_limit_bytes=VMEM_LIMIT),
        name="combine",
    )(dest3, dest3, x1, route, g2, b2, ys)


def _row_tile(n, want):
    t = min(want, n)
    while n % t:
        t //= 2
    return t


def kernel(x, ln_in_g, ln_in_b, w_in, b_gate, lambda_q, lambda_k, subln_g, rel_bias, conv_w,
           w_a_proj, w_b_proj, w_o, ln1_g, ln1_b, w_group, b_group, w_sub, b_sub,
           w_gate_e, w_up_e, w_down_e, ln2_g, ln2_b):
    bsz, seq, d = x.shape
    assert DEPTH == 1 and d == D_MODEL and w_in.shape == (DEPTH, D_MODEL, N_IN)
    n_tok = bsz * seq
    tq = _row_tile(seq, 256)
    assert tq % LANES == 0 and tq % CHUNK == 0
    qpb = _row_tile(seq // tq, 2)
    tm1 = _row_tile(seq, 512)
    tm3 = _row_tile(n_tok, 1024)
    sub3 = _row_tile(tm3, 256)
    tr = _row_tile(n_tok, 1024)
    td = _row_tile(n_tok, 2048)
    tc = _row_tile(n_tok, 256)
    row = lambda v: v.reshape(1, -1).astype(F32)
    lam_init = 0.8 - 0.6 * math.exp(-0.3 * 0)

    wkt = jnp.transpose(lax.optimization_barrier(w_in[0, :, D_MODEL:2 * D_MODEL])).astype(BF16)
    xn, p, kt = _ln_proj(x.reshape(n_tok, d), row(ln_in_g), row(ln_in_b), w_in[0].astype(BF16),
                         wkt, conv_w[0].reshape(3, d).astype(F32), seq, tm1)

    band = _band_bias(rel_bias, tq)
    o_n = _diff_attn(p, kt, band, lambda_q[0].astype(F32), lambda_k[0].astype(F32),
                     row(subln_g[0]), bsz, seq, tq, qpb, lam_init)

    gap = EXPERT_LANE0 - N_GROUPS
    w_r = jnp.concatenate(
        [w_group[0].astype(F32), jnp.zeros((d, gap), F32),
         jnp.transpose(w_sub[0].astype(F32), (1, 0, 2)).reshape(d, N_EXPERTS)], axis=1)
    w_r = jnp.pad(w_r, ((0, 0), (0, LANES - w_r.shape[1])))
    w_hi = w_r.astype(BF16)
    w_lo = (w_r - w_hi.astype(F32)).astype(BF16)
    w_r2 = jnp.concatenate([w_hi, w_lo], axis=1)
    b_r = jnp.concatenate([b_group[0].astype(F32), jnp.zeros((gap,), F32),
                           b_sub[0].astype(F32).reshape(-1)])
    b_r = jnp.pad(b_r, (0, LANES - b_r.shape[0])).reshape(1, LANES)
    x1, xp, logits = _mix(
        o_n, p, xn, w_a_proj[0].astype(BF16), w_b_proj[0].astype(BF16), w_o[0].astype(BF16),
        b_gate[0].astype(F32), row(ln1_g[0]), row(ln1_b[0]), w_r2, b_r, tm3, sub3)
    route, route_t, counts = _route(logits, tr)

    n_assign = n_tok * TOP_K
    n_blocks = -(-n_assign // MOE_BLOCK) + N_EXPERTS
    cnt = counts[0, :N_EXPERTS].astype(jnp.int32)
    padded = ((cnt + MOE_BLOCK - 1) // MOE_BLOCK) * MOE_BLOCK
    pad_end = jnp.cumsum(padded)
    pad_start = pad_end - padded
    n_active = (pad_end[-1:] // MOE_BLOCK).astype(jnp.int32)
    blk_row0 = jnp.arange(n_blocks, dtype=jnp.int32) * MOE_BLOCK
    block_expert = jnp.minimum(
        jnp.sum((pad_end[None, :] <= blk_row0[:, None]).astype(jnp.int32), axis=1),
        N_EXPERTS - 1).astype(jnp.int32)
    experts = route_t[0:TOP_K].astype(jnp.int32)
    ranks = route_t[4:4 + TOP_K].astype(jnp.int32)
    is_e = experts[None] == jnp.arange(N_EXPERTS, dtype=jnp.int32)[:, None, None]
    dest = jnp.sum(jnp.where(is_e, pad_start[:, None, None], 0), axis=0) + ranks

    def per_tile(t):
        tiles = jnp.transpose(dest.reshape(TOP_K, n_tok // t, t), (1, 0, 2))
        return tiles.reshape(n_tok // t, 1, TOP_K * t).astype(jnp.int32)

    last_blk = jnp.maximum(pad_end // MOE_BLOCK - 1, 0)
    idle_blk = jnp.minimum(n_active[0] + jnp.arange(N_EXPERTS), n_blocks - 1)
    n_zero = N_EXPERTS + n_blocks - n_active
    zero_blocks = jnp.concatenate([last_blk, idle_blk, n_zero]).astype(jnp.int32)
    n_rows = n_blocks * MOE_BLOCK
    xs = _dispatch(xp.reshape(n_tok, SLAB_ROWS, LANES), per_tile(td), zero_blocks, n_rows, td)
    e_ids = jnp.arange(N_EXPERTS, dtype=jnp.int32)
    later = (padded > 0)[None, :] & (e_ids[None, :] > e_ids[:, None])
    next_expert = jnp.min(jnp.where(later, e_ids[None, :], N_EXPERTS), axis=1)
    next_expert = jnp.where(next_expert < N_EXPERTS, next_expert, -1).astype(jnp.int32)
    ys = _expert_ffn(xs.reshape(n_rows * SLAB_ROWS, LANES), block_expert, n_active, next_expert,
                     w_gate_e[0].astype(F32), w_up_e[0].astype(F32), w_down_e[0].astype(F32))
    out = _combine(ys.reshape(n_rows, SLAB_ROWS, LANES), per_tile(tc), x1, route,
                   row(ln2_g[0]), row(ln2_b[0]), tc)
    return out.reshape(bsz, seq, d)
```

```python
import functools
import math

import numpy as np
import jax
import jax.numpy as jnp
from jax import lax
from jax.experimental import pallas as pl
from jax.experimental.pallas import tpu as pltpu

F32 = jnp.float32
BF16 = jnp.bfloat16

D_MODEL = 1024
N_HEADS = 4
HEAD_DIM = 128
D_HEAD_V = 2 * HEAD_DIM
CHUNK = 64
N_BUCKETS = 32
MAX_DISTANCE = 128
N_GROUPS = 4
EXPERTS_PER_GROUP = 8
N_EXPERTS = N_GROUPS * EXPERTS_PER_GROUP
TOP_K = 2
D_EXPERT = 512
MOE_BLOCK = 512
LN_EPS = 1e-5
RMS_EPS = 1e-6
DEPTH = 1
DEEPNORM_ALPHA = (2.0 * DEPTH) ** 0.25
N_IN = 8 * D_MODEL
LANES = 128
NEG_BIG = -1e30
LOG2_E = math.log2(math.e)

VMEM_LIMIT = 56 * 1024 * 1024


def _layer_norm(x, g, b):
    mu = jnp.mean(x, axis=-1, keepdims=True)
    xc = x - mu
    var = jnp.mean(xc * xc, axis=-1, keepdims=True)
    return xc * lax.rsqrt(var + LN_EPS) * g + b


def _const_spec(shape):
    return pl.BlockSpec(shape, lambda *_: (0,) * len(shape), pipeline_mode=pl.Buffered(1))


HALO = 16
P_Q, P_V, P_CONV, P_GATE_A, P_GATE_B = range(5)
N_P = 5 * D_MODEL


def _ln_proj_kernel(x_ref, g_ref, b_ref, w_ref, wkt_ref, cw_ref, xn_ref, p_ref, kt_ref, zs_ref,
                    *, q_scale, tm, sub, tiles_per_seq):
    r = pl.program_id(0)

    @pl.when((r % tiles_per_seq) == 0)
    def _():
        zs_ref[0:HALO, :] = jnp.zeros((HALO, D_MODEL), F32)

    cw = cw_ref[...]
    for lo in range(0, tm, sub):
        rows = slice(lo, lo + sub)
        xn = _layer_norm(x_ref[rows, :], g_ref[...], b_ref[...])
        xn_ref[rows, :] = xn
        xb = xn.astype(BF16)

        def proj(c, xb=xb):
            return jnp.dot(xb, w_ref[:, c * D_MODEL:(c + 1) * D_MODEL],
                           preferred_element_type=F32)

        def put(block, val, rows=rows):
            p_ref[rows, block * D_MODEL:(block + 1) * D_MODEL] = val.astype(BF16)

        put(P_Q, proj(0) * q_scale)
        put(P_V, proj(2))
        kt = lax.dot_general(wkt_ref[...], xb, (((1,), (1,)), ((), ())),
                             preferred_element_type=F32)
        kt_ref[:, rows] = kt.astype(BF16)
        z0 = HALO + lo
        zs_ref[z0:z0 + sub, :] = proj(4) * proj(5)
        zc = (cw[0:1] * zs_ref[z0 - 2:z0 - 2 + sub, :] + cw[1:2] * zs_ref[z0 - 1:z0 - 1 + sub, :]
              + cw[2:3] * zs_ref[z0:z0 + sub, :])
        put(P_CONV, proj(3) * zc)
        put(P_GATE_A, proj(6))
        put(P_GATE_B, proj(7))

    zs_ref[0:HALO, :] = zs_ref[tm:tm + HALO, :]


def _ln_proj(x2, g, b, w_bf, wkt, conv_w, seq, tm):
    n_tok = x2.shape[0]
    kern = functools.partial(_ln_proj_kernel, q_scale=HEAD_DIM ** -0.5 * LOG2_E, tm=tm,
                             sub=_row_tile(tm, 256), tiles_per_seq=seq // tm)
    return pl.pallas_call(
        kern,
        grid=(n_tok // tm,),
        in_specs=[
            pl.BlockSpec((tm, D_MODEL), lambda r: (r, 0)),
            _const_spec((1, D_MODEL)),
            _const_spec((1, D_MODEL)),
            _const_spec((D_MODEL, N_IN)),
            _const_spec((D_MODEL, D_MODEL)),
            _const_spec((3, D_MODEL)),
        ],
        out_specs=[
            pl.BlockSpec((tm, D_MODEL), lambda r: (r, 0)),
            pl.BlockSpec((tm, N_P), lambda r: (r, 0)),
            pl.BlockSpec((D_MODEL, tm), lambda r: (0, r)),
        ],
        out_shape=[
            jax.ShapeDtypeStruct((n_tok, D_MODEL), F32),
            jax.ShapeDtypeStruct((n_tok, N_P), BF16),
            jax.ShapeDtypeStruct((D_MODEL, n_tok), BF16),
        ],
        scratch_shapes=[pltpu.VMEM((HALO + tm, D_MODEL), F32)],
        compiler_params=pltpu.CompilerParams(
            dimension_semantics=("arbitrary",), vmem_limit_bytes=VMEM_LIMIT),
        name="ln_proj",
    )(x2, g, b, w_bf, wkt, conv_w)


def _t5_bucket_np(rel):
    nb = N_BUCKETS // 2
    max_exact = nb // 2
    n = np.abs(rel)
    large = np.full(n.shape, max_exact, np.int64)
    for d in range(max_exact, MAX_DISTANCE + 1):
        val = max_exact + int(math.log(d / max_exact) / math.log(MAX_DISTANCE / max_exact)
                              * (nb - max_exact))
        large = np.where(n >= d, min(val, nb - 1), large)
    return np.where(rel > 0, nb, 0) + np.where(n < max_exact, n, large)


def _band_bias(rel_bias, tq):
    qi = np.arange(tq)[:, None]
    kj = np.arange(2 * tq)[None, :] - tq
    allowed = (kj // CHUNK) <= (qi // CHUNK)
    far_bucket = int(_t5_bucket_np(np.array([-(tq + 1)]))[0])
    rb = rel_bias.astype(F32)
    rb = ((rb - rb[far_bucket][None, :]) * LOG2_E).T
    n_rel = 3 * tq
    bucket = _t5_bucket_np(np.arange(n_rel) - (2 * tq - 1))
    onehot = jnp.asarray(bucket[:, None] == np.arange(N_BUCKETS)[None, :])
    per_rel = jnp.sum(jnp.where(onehot[None], rb[:, None, :], 0.0), axis=-1)
    skew = jnp.tile(per_rel, (1, tq))[:, :tq * (n_rel - 1)].reshape(N_HEADS, tq, n_rel - 1)
    tile = skew[:, :, tq - 1:3 * tq - 1]
    return jnp.where(jnp.asarray(allowed)[None], tile, NEG_BIG)


QK_BLOCKS = 2
PV_BLOCKS = 2


def _diff_attn_kernel(q_ref, kt_ref, v_ref, bias_ref, lq_ref, lk_ref, sg_ref, o_ref, s_ref,
                      *, tq, nq, qpb, lam_init):
    dots = jnp.sum(lq_ref[...] * lk_ref[...], axis=-1, keepdims=True)
    lam = jnp.exp(dots[0:1]) - jnp.exp(dots[1:2]) + lam_init
    gain = sg_ref[...] * (1.0 - lam_init)

    def lane_tiles(s):
        return [s[:, c * LANES:(c + 1) * LANES] for c in range(s.shape[1] // LANES)]

    def logits_pass(h, a, i):
        rows = slice(a * tq, (a + 1) * tq)
        row_max = []
        for m in range(2):
            c = h * D_HEAD_V + m * HEAD_DIM
            q = q_ref[rows, c:c + HEAD_DIM]
            mx = None
            for j0 in range(0, i + 1, QK_BLOCKS):
                nb = min(QK_BLOCKS, i + 1 - j0)
                wide = jnp.dot(q, kt_ref[c:c + HEAD_DIM, j0 * tq:(j0 + nb) * tq],
                               preferred_element_type=F32)
                for j in range(j0, j0 + nb):
                    s = wide[:, (j - j0) * tq:(j - j0 + 1) * tq]
                    if j == i:
                        s = s + bias_ref[h, :, tq:2 * tq]
                    elif j == i - 1:
                        s = s + bias_ref[h, :, 0:tq]
                    s_ref[h % s_ref.shape[0], a, m, j] = s
                    t = functools.reduce(jnp.maximum, lane_tiles(s))
                    mx = t if mx is None else jnp.maximum(mx, t)
            row_max.append(jnp.broadcast_to(jnp.max(mx, axis=-1, keepdims=True), (tq, LANES)))
        return row_max

    def values_pass(h, a, i, row_max):
        c0 = h * D_HEAD_V
        acc = [None, None]
        lsum = [None, None]
        for j0 in range(0, i + 1, PV_BLOCKS):
            nb = min(PV_BLOCKS, i + 1 - j0)
            v_blk = v_ref[j0 * tq:(j0 + nb) * tq, c0:c0 + D_HEAD_V]
            for m in range(2):
                tiles = []
                for j in range(j0, j0 + nb):
                    tiles += [jnp.exp2(x - row_max[m]) for x in lane_tiles(s_ref[h % s_ref.shape[0], a, m, j])]
                tsum = functools.reduce(jnp.add, tiles)
                lsum[m] = tsum if lsum[m] is None else lsum[m] + tsum
                p = jnp.concatenate(tiles, axis=1).astype(BF16)
                part = jnp.dot(p, v_blk, preferred_element_type=F32)
                acc[m] = part if acc[m] is None else acc[m] + part
        r1 = 1.0 / jnp.sum(lsum[0], axis=-1, keepdims=True)
        r2 = lam / jnp.sum(lsum[1], axis=-1, keepdims=True)
        o = acc[0] * r1 - acc[1] * r2
        o = o * lax.rsqrt(jnp.mean(o * o, axis=-1, keepdims=True) + RMS_EPS)
        o_ref[a * tq:(a + 1) * tq, c0:c0 + D_HEAD_V] = (o * gain).astype(BF16)

    def query_tile(step):
        for h in range(N_HEADS):
            maxima = [logits_pass(h, a, step * qpb + a) for a in range(qpb)]
            for a in range(qpb):
                values_pass(h, a, step * qpb + a, maxima[a])

    lax.switch(pl.program_id(1), [functools.partial(query_tile, s) for s in range(nq // qpb)])


def _diff_attn(p, kt, band, lam_q, lam_k, subln_g, bsz, seq, tq, qpb, lam_init):
    n_tok = bsz * seq
    nq = seq // tq
    steps = nq // qpb
    s_heads = N_HEADS if qpb == 1 else 1
    kern = functools.partial(_diff_attn_kernel, tq=tq, nq=nq, qpb=qpb, lam_init=lam_init)
    return pl.pallas_call(
        kern,
        grid=(bsz, steps),
        in_specs=[
            pl.BlockSpec((qpb * tq, D_MODEL), lambda b, i: (b * steps + i, P_Q)),
            pl.BlockSpec((D_MODEL, seq), lambda b, i: (0, b)),
            pl.BlockSpec((seq, D_MODEL), lambda b, i: (b, P_V)),
            _const_spec((N_HEADS, tq, 2 * tq)),
            _const_spec((2, HEAD_DIM)),
            _const_spec((2, HEAD_DIM)),
            _const_spec((1, D_HEAD_V)),
        ],
        out_specs=pl.BlockSpec((qpb * tq, D_MODEL), lambda b, i: (b * steps + i, 0)),
        out_shape=jax.ShapeDtypeStruct((n_tok, D_MODEL), BF16),
        scratch_shapes=[
            pltpu.VMEM((s_heads, qpb, 2, nq, tq, tq), F32),
        ],
        compiler_params=pltpu.CompilerParams(
            dimension_semantics=("arbitrary", "arbitrary"), vmem_limit_bytes=VMEM_LIMIT),
        name="diff_attn",
    )(p, kt, p, band, lam_q, lam_k, subln_g)


SLAB_ROWS = D_MODEL // (2 * LANES)


def _store_slabs(ref, row0, x_bf):
    n = x_bf.shape[0]
    bits = pltpu.bitcast(x_bf.astype(F32), jnp.uint32)
    for c in range(SLAB_ROWS):
        lo = bits[:, 2 * c * LANES:(2 * c + 1) * LANES]
        hi = bits[:, (2 * c + 1) * LANES:(2 * c + 2) * LANES]
        ref[pl.ds(SLAB_ROWS * row0 + c, n, stride=SLAB_ROWS), :] = (lo >> 16) | hi


def _load_slabs(ref, row0, n, every=1):
    parts = []
    for c in range(SLAB_ROWS):
        words = ref[pl.ds(SLAB_ROWS * row0 + c, n, stride=SLAB_ROWS * every), :]
        parts.append(pltpu.bitcast(words << 16, F32))
        parts.append(pltpu.bitcast(words & jnp.uint32(0xFFFF0000), F32))
    return jnp.concatenate(parts, axis=1)


def _mix_kernel(on_ref, yb_ref, ga_ref, gb_ref, xn_ref, wa_ref, wb_ref, wo_ref, bg_ref,
                g1_ref, b1_ref, wr_ref, br_ref, x1_ref, xp_ref, lg_ref, *, tm, sub):
    bg = bg_ref[...]
    for lo in range(0, tm, sub):
        rows = slice(lo, lo + sub)
        y_b = jnp.dot(yb_ref[rows, :], wb_ref[...], preferred_element_type=F32)
        y_a = jnp.dot(on_ref[rows, :], wa_ref[...], preferred_element_type=F32)
        g_a = jax.nn.sigmoid(ga_ref[rows, :].astype(F32) + bg[0:1])
        g_b = jax.nn.sigmoid(gb_ref[rows, :].astype(F32) + bg[1:2])
        merged = (g_a * y_a + g_b * y_b).astype(BF16)
        mix = jnp.dot(merged, wo_ref[...], preferred_element_type=F32)
        x1 = _layer_norm(DEEPNORM_ALPHA * xn_ref[rows, :] + mix, g1_ref[...], b1_ref[...])
        x1_ref[rows, :] = x1
        x_hi = x1.astype(BF16)
        x_lo = (x1 - x_hi.astype(F32)).astype(BF16)
        part = (jnp.dot(x_hi, wr_ref[...], preferred_element_type=F32)
                + jnp.dot(x_lo, wr_ref[...], preferred_element_type=F32))
        lg_ref[rows, :] = part[:, 0:LANES] + part[:, LANES:2 * LANES] + br_ref[...]
        _store_slabs(xp_ref, lo, x_hi)


def _mix(o_n, p, xn, wa, wb, wo, b_gate, g1, b1, wr2, br, tm, sub):
    n_tok = xn.shape[0]
    kern = functools.partial(_mix_kernel, tm=tm, sub=sub)

    def col(c):
        return pl.BlockSpec((tm, D_MODEL), lambda r, c=c: (r, c))

    return pl.pallas_call(
        kern,
        grid=(n_tok // tm,),
        in_specs=[
            pl.BlockSpec((tm, D_MODEL), lambda r: (r, 0)),
            col(P_CONV), col(P_GATE_A), col(P_GATE_B),
            pl.BlockSpec((tm, D_MODEL), lambda r: (r, 0)),
            _const_spec((D_MODEL, D_MODEL)), _const_spec((D_MODEL, D_MODEL)),
            _const_spec((D_MODEL, D_MODEL)),
            _const_spec((2, D_MODEL)), _const_spec((1, D_MODEL)), _const_spec((1, D_MODEL)),
            _const_spec((D_MODEL, 2 * LANES)), _const_spec((1, LANES)),
        ],
        out_specs=[
            pl.BlockSpec((tm, D_MODEL), lambda r: (r, 0)),
            pl.BlockSpec((tm * SLAB_ROWS, LANES), lambda r: (r, 0)),
            pl.BlockSpec((tm, LANES), lambda r: (r, 0)),
        ],
        out_shape=[
            jax.ShapeDtypeStruct((n_tok, D_MODEL), F32),
            jax.ShapeDtypeStruct((n_tok * SLAB_ROWS, LANES), jnp.uint32),
            jax.ShapeDtypeStruct((n_tok, LANES), F32),
        ],
        compiler_params=pltpu.CompilerParams(
            dimension_semantics=("arbitrary",), vmem_limit_bytes=VMEM_LIMIT),
        name="mix",
    )(o_n, p, p, p, xn, wa, wb, wo, b_gate, g1, b1, wr2, br)


ROUTE_COLS = 8
EXPERT_LANE0 = 8
RANK_ROWS = 256


def _route_kernel(lg_ref, tri_ref, route_ref, route_t_ref, cnt_ref, run_ref, *, tr):
    @pl.when(pl.program_id(0) == 0)
    def _():
        run_ref[...] = jnp.zeros_like(run_ref)

    lt = jnp.transpose(lg_ref[...])
    row = lax.broadcasted_iota(jnp.int32, (EXPERTS_PER_GROUP, tr), 0).astype(F32)

    def first_argmax(vals, vmax):
        return jnp.min(jnp.where(vals == vmax, row, float(EXPERTS_PER_GROUP)),
                       axis=0, keepdims=True)

    gl = jnp.where(row < N_GROUPS, lt[0:EXPERTS_PER_GROUP], -jnp.inf)
    gmax = jnp.max(gl, axis=0, keepdims=True)
    gsum = jnp.sum(jnp.exp(gl - gmax), axis=0, keepdims=True)
    g_p = 1.0 / gsum
    g_idx = first_argmax(gl, gmax)

    def group_rows(g):
        lo = EXPERT_LANE0 + EXPERTS_PER_GROUP * g
        return lt[lo:lo + EXPERTS_PER_GROUP]

    sl = group_rows(N_GROUPS - 1)
    for g in reversed(range(N_GROUPS - 1)):
        sl = jnp.where(g_idx == g, group_rows(g), sl)
    s1 = jnp.max(sl, axis=0, keepdims=True)
    i1 = first_argmax(sl, s1)
    sl2 = jnp.where(row == i1, -jnp.inf, sl)
    s2 = jnp.max(sl2, axis=0, keepdims=True)
    i2 = first_argmax(sl2, s2)
    t = jnp.exp(s2 - s1)
    w1 = g_p / (1.0 + t)
    w2 = g_p * t / (1.0 + t)
    e1 = EXPERTS_PER_GROUP * g_idx + i1
    e2 = EXPERTS_PER_GROUP * g_idx + i2

    e_row = lax.broadcasted_iota(jnp.int32, (N_EXPERTS, tr), 0).astype(F32)
    oh1 = e_row == e1
    oh2 = e_row == e2
    onehot_t = jnp.concatenate([jnp.where(oh1 | oh2, 1.0, 0.0),
                                jnp.zeros((LANES - N_EXPERTS, tr), F32)], axis=0)
    onehot = jnp.transpose(onehot_t)
    running = run_ref[...]
    groups = []
    for r0 in range(0, tr, RANK_ROWS):
        part = onehot[r0:r0 + RANK_ROWS]
        groups.append(jnp.dot(tri_ref[...], part.astype(BF16), preferred_element_type=F32)
                      + running)
        running = running + jnp.sum(part, axis=0, keepdims=True)
    before_t = jnp.transpose(jnp.concatenate(groups, axis=0))[0:N_EXPERTS]
    rank1 = jnp.sum(jnp.where(oh1, before_t, 0.0), axis=0, keepdims=True)
    rank2 = jnp.sum(jnp.where(oh2, before_t, 0.0), axis=0, keepdims=True)
    run_ref[...] = running
    cnt_ref[...] = jnp.broadcast_to(running, cnt_ref.shape)

    zero = jnp.zeros((1, tr), F32)
    route_t = jnp.concatenate([e1, e2, w1, w2, rank1, rank2, zero, zero], axis=0)
    route_t_ref[...] = route_t
    padded = jnp.concatenate([route_t, jnp.zeros((LANES - ROUTE_COLS, tr), F32)], axis=0)
    route_ref[...] = jnp.transpose(padded)[:, 0:ROUTE_COLS]


def _route(logits, tr):
    n_tok = logits.shape[0]
    assert tr % RANK_ROWS == 0
    tri = jnp.asarray(np.tril(np.ones((RANK_ROWS, RANK_ROWS), np.float32), k=-1), BF16)
    return pl.pallas_call(
        functools.partial(_route_kernel, tr=tr),
        grid=(n_tok // tr,),
        in_specs=[pl.BlockSpec((tr, LANES), lambda r: (r, 0)),
                  _const_spec((RANK_ROWS, RANK_ROWS))],
        out_specs=[
            pl.BlockSpec((tr, ROUTE_COLS), lambda r: (r, 0)),
            pl.BlockSpec((ROUTE_COLS, tr), lambda r: (0, r)),
            pl.BlockSpec((8, LANES), lambda r: (0, 0)),
        ],
        out_shape=[
            jax.ShapeDtypeStruct((n_tok, ROUTE_COLS), F32),
            jax.ShapeDtypeStruct((ROUTE_COLS, n_tok), F32),
            jax.ShapeDtypeStruct((8, LANES), F32),
        ],
        scratch_shapes=[pltpu.VMEM((1, LANES), F32)],
        compiler_params=pltpu.CompilerParams(
            dimension_semantics=("arbitrary",), vmem_limit_bytes=VMEM_LIMIT),
        name="route",
    )(logits, tri)


def _slab_copy(src, src_row, dst, dst_row, sem):
    return pltpu.make_async_copy(src.at[src_row], dst.at[dst_row], sem)


N_ZERO_BLOCKS = 2 * N_EXPERTS


def _dispatch_kernel(zb_ref, dest_ref, x_ref, out_ref, zeros, sem, zsem, *, td):
    @pl.when(pl.program_id(0) == 0)
    def _():
        zeros[...] = jnp.zeros_like(zeros)

        def block_copy(n):
            row0 = pl.multiple_of(zb_ref[n] * MOE_BLOCK, MOE_BLOCK)
            return pltpu.make_async_copy(zeros, out_ref.at[pl.ds(row0, MOE_BLOCK)], zsem)

        n_zero = zb_ref[N_ZERO_BLOCKS]

        def start(n, carry):
            block_copy(n).start()
            return carry

        def wait(n, carry):
            block_copy(n).wait()
            return carry

        lax.fori_loop(0, n_zero, start, 0)
        lax.fori_loop(0, n_zero, wait, 0)

    def issue(t, carry):
        for k in range(TOP_K):
            _slab_copy(x_ref, t, out_ref, dest_ref[0, 0, k * td + t], sem).start(priority=k)
        return carry

    lax.fori_loop(0, td, issue, 0, unroll=8)
    for k in range(TOP_K):
        pltpu.make_async_copy(x_ref, out_ref.at[pl.ds(0, td)], sem).wait()


def _dispatch(xp, dest3, zero_blocks, n_rows, td):
    n_tok = xp.shape[0]
    grid_spec = pltpu.PrefetchScalarGridSpec(
        num_scalar_prefetch=1,
        grid=(n_tok // td,),
        in_specs=[
            pl.BlockSpec((1, 1, TOP_K * td), lambda s, zb: (s, 0, 0), memory_space=pltpu.SMEM),
            pl.BlockSpec((td, SLAB_ROWS, LANES), lambda s, zb: (s, 0, 0)),
        ],
        out_specs=pl.BlockSpec(memory_space=pl.ANY),
        scratch_shapes=[
            pltpu.VMEM((MOE_BLOCK, SLAB_ROWS, LANES), jnp.uint32),
            pltpu.SemaphoreType.DMA(()),
            pltpu.SemaphoreType.DMA(()),
        ],
    )
    return pl.pallas_call(
        functools.partial(_dispatch_kernel, td=td),
        grid_spec=grid_spec,
        out_shape=jax.ShapeDtypeStruct((n_rows, SLAB_ROWS, LANES), jnp.uint32),
        compiler_params=pltpu.CompilerParams(dimension_semantics=("arbitrary",)),
        name="dispatch",
    )(zero_blocks, dest3, xp)


FFN_SUB = 512


def _expert_ffn_kernel(be_ref, na_ref, nx_ref, x_ref, wg_hbm, wu_hbm, wd_hbm, y_ref,
                       wg_f32, wu_f32, wd_f32, wg_bf, wu_bf, wd_bf, sem):
    i = pl.program_id(0)
    active = i < na_ref[0]
    expert = be_ref[i]
    new_expert = (i == 0) | (expert != be_ref[jnp.maximum(i - 1, 0)])

    def fetch(e):
        return (pltpu.make_async_copy(wg_hbm.at[e], wg_f32, sem.at[0]),
                pltpu.make_async_copy(wu_hbm.at[e], wu_f32, sem.at[1]),
                pltpu.make_async_copy(wd_hbm.at[e], wd_f32, sem.at[2]))

    @pl.when(active & (i == 0))
    def _():
        for cp in fetch(expert):
            cp.start()

    @pl.when(active & new_expert)
    def _():
        for cp in fetch(expert):
            cp.wait()
        wg_bf[...] = wg_f32[...].astype(BF16)
        wu_bf[...] = wu_f32[...].astype(BF16)
        wd_bf[...] = wd_f32[...].astype(BF16)
        nxt = nx_ref[expert]

        @pl.when(nxt >= 0)
        def _():
            for cp in fetch(nxt):
                cp.start()

    @pl.when(active)
    def _():
        for lo in range(0, MOE_BLOCK, FFN_SUB):
            x = _load_slabs(x_ref, lo, FFN_SUB).astype(BF16)
            g = jnp.dot(x, wg_bf[...], preferred_element_type=F32)
            u = jnp.dot(x, wu_bf[...], preferred_element_type=F32)
            hid = (jax.nn.silu(g) * u).astype(BF16)
            y = jnp.dot(hid, wd_bf[...], preferred_element_type=F32)
            _store_slabs(y_ref, lo, y.astype(BF16))

    @pl.when(jnp.logical_not(active))
    def _():
        y_ref[...] = jnp.zeros_like(y_ref)


def _expert_ffn(xs, block_expert, n_active, next_expert, wg, wu, wd):
    n_rows = xs.shape[0] // SLAB_ROWS
    n_blocks = n_rows // MOE_BLOCK

    def blk(i, be, na, nx):
        return (jnp.maximum(jnp.minimum(i, na[0] - 1), 0), 0)

    grid_spec = pltpu.PrefetchScalarGridSpec(
        num_scalar_prefetch=3,
        grid=(n_blocks,),
        in_specs=[
            pl.BlockSpec((MOE_BLOCK * SLAB_ROWS, LANES), blk),
            pl.BlockSpec(memory_space=pl.ANY),
            pl.BlockSpec(memory_space=pl.ANY),
            pl.BlockSpec(memory_space=pl.ANY),
        ],
        out_specs=pl.BlockSpec((MOE_BLOCK * SLAB_ROWS, LANES), lambda i, be, na, nx: (i, 0)),
        scratch_shapes=[
            pltpu.VMEM((D_MODEL, D_EXPERT), F32),
            pltpu.VMEM((D_MODEL, D_EXPERT), F32),
            pltpu.VMEM((D_EXPERT, D_MODEL), F32),
            pltpu.VMEM((D_MODEL, D_EXPERT), BF16),
            pltpu.VMEM((D_MODEL, D_EXPERT), BF16),
            pltpu.VMEM((D_EXPERT, D_MODEL), BF16),
            pltpu.SemaphoreType.DMA((3,)),
        ],
    )
    return pl.pallas_call(
        _expert_ffn_kernel,
        grid_spec=grid_spec,
        out_shape=jax.ShapeDtypeStruct((n_rows * SLAB_ROWS, LANES), jnp.uint32),
        compiler_params=pltpu.CompilerParams(
            dimension_semantics=("arbitrary",), vmem_limit_bytes=VMEM_LIMIT),
        name="expert_ffn",
    )(block_expert, n_active, next_expert, xs, wg, wu, wd)


def _combine_kernel(dcur_ref, dnxt_ref, x_ref, route_ref, g_ref, b_ref, y_ref, o_ref,
                    buf, sem, *, td):
    s = pl.program_id(0)
    n = pl.num_programs(0)
    slot = s % 2

    def base(slot_):
        return pl.multiple_of(slot_ * TOP_K * td, TOP_K * td)

    def issue(d_ref, to_slot):
        def body(t, carry):
            for k in range(TOP_K):
                _slab_copy(y_ref, d_ref[0, 0, k * td + t], buf, base(to_slot) + TOP_K * t + k,
                           sem.at[to_slot]).start(priority=k)
            return carry
        lax.fori_loop(0, td, body, 0, unroll=8)

    @pl.when(s == 0)
    def _():
        issue(dcur_ref, 0)

    @pl.when(s + 1 < n)
    def _():
        issue(dnxt_ref, 1 - slot)

    pltpu.make_async_copy(y_ref.at[pl.ds(0, TOP_K * td)], buf.at[pl.ds(base(slot), TOP_K * td)],
                          sem.at[slot]).wait()

    route = route_ref[...]
    buf2 = buf.reshape(2 * TOP_K * td * SLAB_ROWS, LANES)
    ffn = (route[:, 2:3] * _load_slabs(buf2, base(slot), td, every=TOP_K)
           + route[:, 3:4] * _load_slabs(buf2, base(slot) + 1, td, every=TOP_K))
    o_ref[...] = _layer_norm(DEEPNORM_ALPHA * x_ref[...] + ffn, g_ref[...], b_ref[...])


def _combine(ys, dest3, x1, route, g2, b2, td):
    n_tok = x1.shape[0]
    n_steps = n_tok // td
    return pl.pallas_call(
        functools.partial(_combine_kernel, td=td),
        grid=(n_steps,),
        in_specs=[
            pl.BlockSpec((1, 1, TOP_K * td), lambda s: (s, 0, 0), memory_space=pltpu.SMEM),
            pl.BlockSpec((1, 1, TOP_K * td), lambda s: (jnp.minimum(s + 1, n_steps - 1), 0, 0),
                         memory_space=pltpu.SMEM),
            pl.BlockSpec((td, D_MODEL), lambda s: (s, 0)),
            pl.BlockSpec((td, ROUTE_COLS), lambda s: (s, 0)),
            pl.BlockSpec((1, D_MODEL), lambda s: (0, 0)),
            pl.BlockSpec((1, D_MODEL), lambda s: (0, 0)),
            pl.BlockSpec(memory_space=pl.ANY),
        ],
        out_specs=pl.BlockSpec((td, D_MODEL), lambda s: (s, 0)),
        out_shape=jax.ShapeDtypeStruct((n_tok, D_MODEL), F32),
        scratch_shapes=[
            pltpu.VMEM((2 * TOP_K * td, SLAB_ROWS, LANES), jnp.uint32),
            pltpu.SemaphoreType.DMA((2,)),
        ],
        compiler_params=pltpu.CompilerParams(
            dimension_semantics=("arbitrary",), vmem_limit_bytes=VMEM_LIMIT),
        name="combine",
    )(dest3, dest3, x1, route, g2, b2, ys)


def _row_tile(n, want):
    t = min(want, n)
    while n % t:
        t //= 2
    return t


def kernel(x, ln_in_g, ln_in_b, w_in, b_gate, lambda_q, lambda_k, subln_g, rel_bias, conv_w,
           w_a_proj, w_b_proj, w_o, ln1_g, ln1_b, w_group, b_group, w_sub, b_sub,
           w_gate_e, w_up_e, w_down_e, ln2_g, ln2_b):
    bsz, seq, d = x.shape
    assert DEPTH == 1 and d == D_MODEL and w_in.shape == (DEPTH, D_MODEL, N_IN)
    n_tok = bsz * seq
    tq = _row_tile(seq, 256)
    assert tq % LANES == 0 and tq % CHUNK == 0
    qpb = _row_tile(seq // tq, 2)
    tm1 = _row_tile(seq, 512)
    tm3 = _row_tile(n_tok, 1024)
    sub3 = _row_tile(tm3, 256)
    tr = _row_tile(n_tok, 2048)
    td = _row_tile(n_tok, 4096)
    tc = _row_tile(n_tok, 512)
    row = lambda v: v.reshape(1, -1).astype(F32)
    lam_init = 0.8 - 0.6 * math.exp(-0.3 * 0)

    wkt = jnp.transpose(lax.optimization_barrier(w_in[0, :, D_MODEL:2 * D_MODEL])).astype(BF16)
    xn, p, kt = _ln_proj(x.reshape(n_tok, d), row(ln_in_g), row(ln_in_b), w_in[0].astype(BF16),
                         wkt, conv_w[0].reshape(3, d).astype(F32), seq, tm1)

    band = _band_bias(rel_bias, tq)
    o_n = _diff_attn(p, kt, band, lambda_q[0].astype(F32), lambda_k[0].astype(F32),
                     row(subln_g[0]), bsz, seq, tq, qpb, lam_init)

    gap = EXPERT_LANE0 - N_GROUPS
    w_r = jnp.concatenate(
        [w_group[0].astype(F32), jnp.zeros((d, gap), F32),
         jnp.transpose(w_sub[0].astype(F32), (1, 0, 2)).reshape(d, N_EXPERTS)], axis=1)
    w_r = jnp.pad(w_r, ((0, 0), (0, LANES - w_r.shape[1])))
    w_hi = w_r.astype(BF16)
    w_lo = (w_r - w_hi.astype(F32)).astype(BF16)
    w_r2 = jnp.concatenate([w_hi, w_lo], axis=1)
    b_r = jnp.concatenate([b_group[0].astype(F32), jnp.zeros((gap,), F32),
                           b_sub[0].astype(F32).reshape(-1)])
    b_r = jnp.pad(b_r, (0, LANES - b_r.shape[0])).reshape(1, LANES)
    x1, xp, logits = _mix(
        o_n, p, xn, w_a_proj[0].astype(BF16), w_b_proj[0].astype(BF16), w_o[0].astype(BF16),
        b_gate[0].astype(F32), row(ln1_g[0]), row(ln1_b[0]), w_r2, b_r, tm3, sub3)
    route, route_t, counts = _route(logits, tr)

    n_assign = n_tok * TOP_K
    n_blocks = -(-n_assign // MOE_BLOCK) + N_EXPERTS
    cnt = counts[0, :N_EXPERTS].astype(jnp.int32)
    padded = ((cnt + MOE_BLOCK - 1) // MOE_BLOCK) * MOE_BLOCK
    pad_end = jnp.cumsum(padded)
    pad_start = pad_end - padded
    n_active = (pad_end[-1:] // MOE_BLOCK).astype(jnp.int32)
    blk_row0 = jnp.arange(n_blocks, dtype=jnp.int32) * MOE_BLOCK
    block_expert = jnp.minimum(
        jnp.sum((pad_end[None, :] <= blk_row0[:, None]).astype(jnp.int32), axis=1),
        N_EXPERTS - 1).astype(jnp.int32)
    experts = route_t[0:TOP_K].astype(jnp.int32)
    ranks = route_t[4:4 + TOP_K].astype(jnp.int32)
    is_e = experts[None] == jnp.arange(N_EXPERTS, dtype=jnp.int32)[:, None, None]
    dest = jnp.sum(jnp.where(is_e, pad_start[:, None, None], 0), axis=0) + ranks

    def per_tile(t):
        tiles = jnp.transpose(dest.reshape(TOP_K, n_tok // t, t), (1, 0, 2))
        return tiles.reshape(n_tok // t, 1, TOP_K * t).astype(jnp.int32)

    last_blk = jnp.maximum(pad_end // MOE_BLOCK - 1, 0)
    idle_blk = jnp.minimum(n_active[0] + jnp.arange(N_EXPERTS), n_blocks - 1)
    n_zero = N_EXPERTS + n_blocks - n_active
    zero_blocks = jnp.concatenate([last_blk, idle_blk, n_zero]).astype(jnp.int32)
    n_rows = n_blocks * MOE_BLOCK
    xs = _dispatch(xp.reshape(n_tok, SLAB_ROWS, LANES), per_tile(td), zero_blocks, n_rows, td)
    e_ids = jnp.arange(N_EXPERTS, dtype=jnp.int32)
    later = (padded > 0)[None, :] & (e_ids[None, :] > e_ids[:, None])
    next_expert = jnp.min(jnp.where(later, e_ids[None, :], N_EXPERTS), axis=1)
    next_expert = jnp.where(next_expert < N_EXPERTS, next_expert, -1).astype(jnp.int32)
    ys = _expert_ffn(xs.reshape(n_rows * SLAB_ROWS, LANES), block_expert, n_active, next_expert,
                     w_gate_e[0].astype(F32), w_up_e[0].astype(F32), w_down_e[0].astype(F32))
    out = _combine(ys.reshape(n_rows, SLAB_ROWS, LANES), per_tile(tc), x1, route,
                   row(ln2_g[0]), row(ln2_b[0]), tc)
    return out.reshape(bsz, seq, d)
```

```python
import functools
import math

import numpy as np
import jax
import jax.numpy as jnp
from jax import lax
from jax.experimental import pallas as pl
from jax.experimental.pallas import tpu as pltpu

F32 = jnp.float32
BF16 = jnp.bfloat16

D_MODEL = 1024
N_HEADS = 4
HEAD_DIM = 128
D_HEAD_V = 2 * HEAD_DIM
CHUNK = 64
N_BUCKETS = 32
MAX_DISTANCE = 128
N_GROUPS = 4
EXPERTS_PER_GROUP = 8
N_EXPERTS = N_GROUPS * EXPERTS_PER_GROUP
TOP_K = 2
D_EXPERT = 512
MOE_BLOCK = 512
LN_EPS = 1e-5
RMS_EPS = 1e-6
DEPTH = 1
DEEPNORM_ALPHA = (2.0 * DEPTH) ** 0.25
N_IN = 8 * D_MODEL
LANES = 128
NEG_BIG = -1e30
LOG2_E = math.log2(math.e)

VMEM_LIMIT = 56 * 1024 * 1024


def _layer_norm(x, g, b):
    mu = jnp.mean(x, axis=-1, keepdims=True)
    xc = x - mu
    var = jnp.mean(xc * xc, axis=-1, keepdims=True)
    return xc * lax.rsqrt(var + LN_EPS) * g + b


def _const_spec(shape):
    return pl.BlockSpec(shape, lambda *_: (0,) * len(shape), pipeline_mode=pl.Buffered(1))


HALO = 16
P_Q, P_V, P_CONV, P_GATE_A, P_GATE_B = range(5)
N_P = 5 * D_MODEL


def _ln_proj_kernel(x_ref, g_ref, b_ref, w_ref, wkt_ref, cw_ref, xn_ref, p_ref, kt_ref, zs_ref,
                    *, q_scale, tm, sub, tiles_per_seq):
    r = pl.program_id(0)

    @pl.when((r % tiles_per_seq) == 0)
    def _():
        zs_ref[0:HALO, :] = jnp.zeros((HALO, D_MODEL), F32)

    cw = cw_ref[...]
    for lo in range(0, tm, sub):
        rows = slice(lo, lo + sub)
        xn = _layer_norm(x_ref[rows, :], g_ref[...], b_ref[...])
        xn_ref[rows, :] = xn
        xb = xn.astype(BF16)

        def proj(c, xb=xb):
            return jnp.dot(xb, w_ref[:, c * D_MODEL:(c + 1) * D_MODEL],
                           preferred_element_type=F32)

        def put(block, val, rows=rows):
            p_ref[rows, block * D_MODEL:(block + 1) * D_MODEL] = val.astype(BF16)

        put(P_Q, proj(0) * q_scale)
        put(P_V, proj(2))
        kt = lax.dot_general(wkt_ref[...], xb, (((1,), (1,)), ((), ())),
                             preferred_element_type=F32)
        kt_ref[:, rows] = kt.astype(BF16)
        z0 = HALO + lo
        zs_ref[z0:z0 + sub, :] = proj(4) * proj(5)
        zc = (cw[0:1] * zs_ref[z0 - 2:z0 - 2 + sub, :] + cw[1:2] * zs_ref[z0 - 1:z0 - 1 + sub, :]
              + cw[2:3] * zs_ref[z0:z0 + sub, :])
        put(P_CONV, proj(3) * zc)
        put(P_GATE_A, proj(6))
        put(P_GATE_B, proj(7))

    zs_ref[0:HALO, :] = zs_ref[tm:tm + HALO, :]


def _ln_proj(x2, g, b, w_bf, wkt, conv_w, seq, tm):
    n_tok = x2.shape[0]
    kern = functools.partial(_ln_proj_kernel, q_scale=HEAD_DIM ** -0.5 * LOG2_E, tm=tm,
                             sub=_row_tile(tm, 256), tiles_per_seq=seq // tm)
    return pl.pallas_call(
        kern,
        grid=(n_tok // tm,),
        in_specs=[
            pl.BlockSpec((tm, D_MODEL), lambda r: (r, 0)),
            _const_spec((1, D_MODEL)),
            _const_spec((1, D_MODEL)),
            _const_spec((D_MODEL, N_IN)),
            _const_spec((D_MODEL, D_MODEL)),
            _const_spec((3, D_MODEL)),
        ],
        out_specs=[
            pl.BlockSpec((tm, D_MODEL), lambda r: (r, 0)),
            pl.BlockSpec((tm, N_P), lambda r: (r, 0)),
            pl.BlockSpec((D_MODEL, tm), lambda r: (0, r)),
        ],
        out_shape=[
            jax.ShapeDtypeStruct((n_tok, D_MODEL), F32),
            jax.ShapeDtypeStruct((n_tok, N_P), BF16),
            jax.ShapeDtypeStruct((D_MODEL, n_tok), BF16),
        ],
        scratch_shapes=[pltpu.VMEM((HALO + tm, D_MODEL), F32)],
        compiler_params=pltpu.CompilerParams(
            dimension_semantics=("arbitrary",), vmem_limit_bytes=VMEM_LIMIT),
        name="ln_proj",
    )(x2, g, b, w_bf, wkt, conv_w)


def _t5_bucket_np(rel):
    nb = N_BUCKETS // 2
    max_exact = nb // 2
    n = np.abs(rel)
    large = np.full(n.shape, max_exact, np.int64)
    for d in range(max_exact, MAX_DISTANCE + 1):
        val = max_exact + int(math.log(d / max_exact) / math.log(MAX_DISTANCE / max_exact)
                              * (nb - max_exact))
        large = np.where(n >= d, min(val, nb - 1), large)
    return np.where(rel > 0, nb, 0) + np.where(n < max_exact, n, large)


def _band_bias(rel_bias, tq):
    qi = np.arange(tq)[:, None]
    kj = np.arange(2 * tq)[None, :] - tq
    allowed = (kj // CHUNK) <= (qi // CHUNK)
    far_bucket = int(_t5_bucket_np(np.array([-(tq + 1)]))[0])
    rb = rel_bias.astype(F32)
    rb = ((rb - rb[far_bucket][None, :]) * LOG2_E).T
    n_rel = 3 * tq
    bucket = _t5_bucket_np(np.arange(n_rel) - (2 * tq - 1))
    onehot = jnp.asarray(bucket[:, None] == np.arange(N_BUCKETS)[None, :])
    per_rel = jnp.sum(jnp.where(onehot[None], rb[:, None, :], 0.0), axis=-1)
    skew = jnp.tile(per_rel, (1, tq))[:, :tq * (n_rel - 1)].reshape(N_HEADS, tq, n_rel - 1)
    tile = skew[:, :, tq - 1:3 * tq - 1]
    return jnp.where(jnp.asarray(allowed)[None], tile, NEG_BIG)


QK_BLOCKS = 2
PV_BLOCKS = 2


def _diff_attn_kernel(q_ref, kt_ref, v_ref, bias_ref, lq_ref, lk_ref, sg_ref, o_ref, s_ref,
                      *, tq, nq, qpb, lam_init):
    dots = jnp.sum(lq_ref[...] * lk_ref[...], axis=-1, keepdims=True)
    lam = jnp.exp(dots[0:1]) - jnp.exp(dots[1:2]) + lam_init
    gain = sg_ref[...] * (1.0 - lam_init)

    def lane_tiles(s):
        return [s[:, c * LANES:(c + 1) * LANES] for c in range(s.shape[1] // LANES)]

    def logits_pass(h, a, i):
        rows = slice(a * tq, (a + 1) * tq)
        row_max = []
        for m in range(2):
            c = h * D_HEAD_V + m * HEAD_DIM
            q = q_ref[rows, c:c + HEAD_DIM]
            mx = None
            for j0 in range(0, i + 1, QK_BLOCKS):
                nb = min(QK_BLOCKS, i + 1 - j0)
                wide = jnp.dot(q, kt_ref[c:c + HEAD_DIM, j0 * tq:(j0 + nb) * tq],
                               preferred_element_type=F32)
                for j in range(j0, j0 + nb):
                    s = wide[:, (j - j0) * tq:(j - j0 + 1) * tq]
                    if j == i:
                        s = s + bias_ref[h, :, tq:2 * tq]
                    elif j == i - 1:
                        s = s + bias_ref[h, :, 0:tq]
                    s_ref[h % s_ref.shape[0], a, m, j] = s
                    t = functools.reduce(jnp.maximum, lane_tiles(s))
                    mx = t if mx is None else jnp.maximum(mx, t)
            row_max.append(jnp.broadcast_to(jnp.max(mx, axis=-1, keepdims=True), (tq, LANES)))
        return row_max

    def values_pass(h, a, i, row_max):
        c0 = h * D_HEAD_V
        acc = [None, None]
        lsum = [None, None]
        for j0 in range(0, i + 1, PV_BLOCKS):
            nb = min(PV_BLOCKS, i + 1 - j0)
            v_blk = v_ref[j0 * tq:(j0 + nb) * tq, c0:c0 + D_HEAD_V]
            for m in range(2):
                tiles = []
                for j in range(j0, j0 + nb):
                    tiles += [jnp.exp2(x - row_max[m]) for x in lane_tiles(s_ref[h % s_ref.shape[0], a, m, j])]
                tsum = functools.reduce(jnp.add, tiles)
                lsum[m] = tsum if lsum[m] is None else lsum[m] + tsum
                p = jnp.concatenate(tiles, axis=1).astype(BF16)
                part = jnp.dot(p, v_blk, preferred_element_type=F32)
                acc[m] = part if acc[m] is None else acc[m] + part
        r1 = 1.0 / jnp.sum(lsum[0], axis=-1, keepdims=True)
        r2 = lam / jnp.sum(lsum[1], axis=-1, keepdims=True)
        o = acc[0] * r1 - acc[1] * r2
        o = o * lax.rsqrt(jnp.mean(o * o, axis=-1, keepdims=True) + RMS_EPS)
        o_ref[a * tq:(a + 1) * tq, c0:c0 + D_HEAD_V] = (o * gain).astype(BF16)

    def query_tile(step):
        for h in range(N_HEADS):
            maxima = [logits_pass(h, a, step * qpb + a) for a in range(qpb)]
            for a in range(qpb):
                values_pass(h, a, step * qpb + a, maxima[a])

    lax.switch(pl.program_id(1), [functools.partial(query_tile, s) for s in range(nq // qpb)])


def _diff_attn(p, kt, band, lam_q, lam_k, subln_g, bsz, seq, tq, qpb, lam_init):
    n_tok = bsz * seq
    nq = seq // tq
    steps = nq // qpb
    s_heads = N_HEADS if qpb == 1 else 1
    kern = functools.partial(_diff_attn_kernel, tq=tq, nq=nq, qpb=qpb, lam_init=lam_init)
    return pl.pallas_call(
        kern,
        grid=(bsz, steps),
        in_specs=[
            pl.BlockSpec((qpb * tq, D_MODEL), lambda b, i: (b * steps + i, P_Q)),
            pl.BlockSpec((D_MODEL, seq), lambda b, i: (0, b)),
            pl.BlockSpec((seq, D_MODEL), lambda b, i: (b, P_V)),
            _const_spec((N_HEADS, tq, 2 * tq)),
            _const_spec((2, HEAD_DIM)),
            _const_spec((2, HEAD_DIM)),
            _const_spec((1, D_HEAD_V)),
        ],
        out_specs=pl.BlockSpec((qpb * tq, D_MODEL), lambda b, i: (b * steps + i, 0)),
        out_shape=jax.ShapeDtypeStruct((n_tok, D_MODEL), BF16),
        scratch_shapes=[
            pltpu.VMEM((s_heads, qpb, 2, nq, tq, tq), F32),
        ],
        compiler_params=pltpu.CompilerParams(
            dimension_semantics=("arbitrary", "arbitrary"), vmem_limit_bytes=VMEM_LIMIT),
        name="diff_attn",
    )(p, kt, p, band, lam_q, lam_k, subln_g)


SLAB_ROWS = D_MODEL // (2 * LANES)


def _store_slabs(ref, row0, x_bf):
    n = x_bf.shape[0]
    bits = pltpu.bitcast(x_bf.astype(F32), jnp.uint32)
    for c in range(SLAB_ROWS):
        lo = bits[:, 2 * c * LANES:(2 * c + 1) * LANES]
        hi = bits[:, (2 * c + 1) * LANES:(2 * c + 2) * LANES]
        ref[pl.ds(SLAB_ROWS * row0 + c, n, stride=SLAB_ROWS), :] = (lo >> 16) | hi


def _load_slabs(ref, row0, n, every=1):
    parts = []
    for c in range(SLAB_ROWS):
        words = ref[pl.ds(SLAB_ROWS * row0 + c, n, stride=SLAB_ROWS * every), :]
        parts.append(pltpu.bitcast(words << 16, F32))
        parts.append(pltpu.bitcast(words & jnp.uint32(0xFFFF0000), F32))
    return jnp.concatenate(parts, axis=1)


def _mix_kernel(on_ref, yb_ref, ga_ref, gb_ref, xn_ref, wa_ref, wb_ref, wo_ref, bg_ref,
                g1_ref, b1_ref, wr_ref, br_ref, x1_ref, xp_ref, lg_ref, *, tm, sub):
    bg = bg_ref[...]
    for lo in range(0, tm, sub):
        rows = slice(lo, lo + sub)
        y_b = jnp.dot(yb_ref[rows, :], wb_ref[...], preferred_element_type=F32)
        y_a = jnp.dot(on_ref[rows, :], wa_ref[...], preferred_element_type=F32)
        g_a = jax.nn.sigmoid(ga_ref[rows, :].astype(F32) + bg[0:1])
        g_b = jax.nn.sigmoid(gb_ref[rows, :].astype(F32) + bg[1:2])
        merged = (g_a * y_a + g_b * y_b).astype(BF16)
        mix = jnp.dot(merged, wo_ref[...], preferred_element_type=F32)
        x1 = _layer_norm(DEEPNORM_ALPHA * xn_ref[rows, :] + mix, g1_ref[...], b1_ref[...])
        x1_ref[rows, :] = x1
        x_hi = x1.astype(BF16)
        x_lo = (x1 - x_hi.astype(F32)).astype(BF16)
        part = (jnp.dot(x_hi, wr_ref[...], preferred_element_type=F32)
                + jnp.dot(x_lo, wr_ref[...], preferred_element_type=F32))
        lg_ref[rows, :] = part[:, 0:LANES] + part[:, LANES:2 * LANES] + br_ref[...]
        _store_slabs(xp_ref, lo, x_hi)


def _mix(o_n, p, xn, wa, wb, wo, b_gate, g1, b1, wr2, br, tm, sub):
    n_tok = xn.shape[0]
    kern = functools.partial(_mix_kernel, tm=tm, sub=sub)

    def col(c):
        return pl.BlockSpec((tm, D_MODEL), lambda r, c=c: (r, c))

    return pl.pallas_call(
        kern,
        grid=(n_tok // tm,),
        in_specs=[
            pl.BlockSpec((tm, D_MODEL), lambda r: (r, 0)),
            col(P_CONV), col(P_GATE_A), col(P_GATE_B),
            pl.BlockSpec((tm, D_MODEL), lambda r: (r, 0)),
            _const_spec((D_MODEL, D_MODEL)), _const_spec((D_MODEL, D_MODEL)),
            _const_spec((D_MODEL, D_MODEL)),
            _const_spec((2, D_MODEL)), _const_spec((1, D_MODEL)), _const_spec((1, D_MODEL)),
            _const_spec((D_MODEL, 2 * LANES)), _const_spec((1, LANES)),
        ],
        out_specs=[
            pl.BlockSpec((tm, D_MODEL), lambda r: (r, 0)),
            pl.BlockSpec((tm * SLAB_ROWS, LANES), lambda r: (r, 0)),
            pl.BlockSpec((tm, LANES), lambda r: (r, 0)),
        ],
        out_shape=[
            jax.ShapeDtypeStruct((n_tok, D_MODEL), F32),
            jax.ShapeDtypeStruct((n_tok * SLAB_ROWS, LANES), jnp.uint32),
            jax.ShapeDtypeStruct((n_tok, LANES), F32),
        ],
        compiler_params=pltpu.CompilerParams(
            dimension_semantics=("arbitrary",), vmem_limit_bytes=VMEM_LIMIT),
        name="mix",
    )(o_n, p, p, p, xn, wa, wb, wo, b_gate, g1, b1, wr2, br)


ROUTE_COLS = 8
EXPERT_LANE0 = 8
RANK_ROWS = 256


def _route_kernel(lg_ref, tri_ref, route_ref, route_t_ref, cnt_ref, run_ref, *, tr):
    @pl.when(pl.program_id(0) == 0)
    def _():
        run_ref[...] = jnp.zeros_like(run_ref)

    lt = jnp.transpose(lg_ref[...])
    row = lax.broadcasted_iota(jnp.int32, (EXPERTS_PER_GROUP, tr), 0).astype(F32)

    def first_argmax(vals, vmax):
        return jnp.min(jnp.where(vals == vmax, row, float(EXPERTS_PER_GROUP)),
                       axis=0, keepdims=True)

    gl = jnp.where(row < N_GROUPS, lt[0:EXPERTS_PER_GROUP], -jnp.inf)
    gmax = jnp.max(gl, axis=0, keepdims=True)
    gsum = jnp.sum(jnp.exp(gl - gmax), axis=0, keepdims=True)
    g_p = 1.0 / gsum
    g_idx = first_argmax(gl, gmax)

    def group_rows(g):
        lo = EXPERT_LANE0 + EXPERTS_PER_GROUP * g
        return lt[lo:lo + EXPERTS_PER_GROUP]

    sl = group_rows(N_GROUPS - 1)
    for g in reversed(range(N_GROUPS - 1)):
        sl = jnp.where(g_idx == g, group_rows(g), sl)
    s1 = jnp.max(sl, axis=0, keepdims=True)
    i1 = first_argmax(sl, s1)
    sl2 = jnp.where(row == i1, -jnp.inf, sl)
    s2 = jnp.max(sl2, axis=0, keepdims=True)
    i2 = first_argmax(sl2, s2)
    t = jnp.exp(s2 - s1)
    w1 = g_p / (1.0 + t)
    w2 = g_p * t / (1.0 + t)
    e1 = EXPERTS_PER_GROUP * g_idx + i1
    e2 = EXPERTS_PER_GROUP * g_idx + i2

    e_row = lax.broadcasted_iota(jnp.int32, (N_EXPERTS, tr), 0).astype(F32)
    oh1 = e_row == e1
    oh2 = e_row == e2
    onehot_t = jnp.concatenate([jnp.where(oh1 | oh2, 1.0, 0.0),
                                jnp.zeros((LANES - N_EXPERTS, tr), F32)], axis=0)
    onehot = jnp.transpose(onehot_t)
    running = run_ref[...]
    groups = []
    for r0 in range(0, tr, RANK_ROWS):
        part = onehot[r0:r0 + RANK_ROWS]
        groups.append(jnp.dot(tri_ref[...], part.astype(BF16), preferred_element_type=F32)
                      + running)
        running = running + jnp.sum(part, axis=0, keepdims=True)
    before_t = jnp.transpose(jnp.concatenate(groups, axis=0))[0:N_EXPERTS]
    rank1 = jnp.sum(jnp.where(oh1, before_t, 0.0), axis=0, keepdims=True)
    rank2 = jnp.sum(jnp.where(oh2, before_t, 0.0), axis=0, keepdims=True)
    run_ref[...] = running
    cnt_ref[...] = jnp.broadcast_to(running, cnt_ref.shape)

    zero = jnp.zeros((1, tr), F32)
    route_t = jnp.concatenate([e1, e2, w1, w2, rank1, rank2, zero, zero], axis=0)
    route_t_ref[...] = route_t
    padded = jnp.concatenate([route_t, jnp.zeros((LANES - ROUTE_COLS, tr), F32)], axis=0)
    route_ref[...] = jnp.transpose(padded)[:, 0:ROUTE_COLS]


def _route(logits, tr):
    n_tok = logits.shape[0]
    assert tr % RANK_ROWS == 0
    tri = jnp.asarray(np.tril(np.ones((RANK_ROWS, RANK_ROWS), np.float32), k=-1), BF16)
    return pl.pallas_call(
        functools.partial(_route_kernel, tr=tr),
        grid=(n_tok // tr,),
        in_specs=[pl.BlockSpec((tr, LANES), lambda r: (r, 0)),
                  _const_spec((RANK_ROWS, RANK_ROWS))],
        out_specs=[
            pl.BlockSpec((tr, ROUTE_COLS), lambda r: (r, 0)),
            pl.BlockSpec((ROUTE_COLS, tr), lambda r: (0, r)),
            pl.BlockSpec((8, LANES), lambda r: (0, 0)),
        ],
        out_shape=[
            jax.ShapeDtypeStruct((n_tok, ROUTE_COLS), F32),
            jax.ShapeDtypeStruct((ROUTE_COLS, n_tok), F32),
            jax.ShapeDtypeStruct((8, LANES), F32),
        ],
        scratch_shapes=[pltpu.VMEM((1, LANES), F32)],
        compiler_params=pltpu.CompilerParams(
            dimension_semantics=("arbitrary",), vmem_limit_bytes=VMEM_LIMIT),
        name="route",
    )(logits, tri)


def _slab_copy(src, src_row, dst, dst_row, sem):
    return pltpu.make_async_copy(src.at[src_row], dst.at[dst_row], sem)


N_ZERO_BLOCKS = 2 * N_EXPERTS


def _dispatch_kernel(zb_ref, dest_ref, x_ref, out_ref, zeros, sem, zsem, *, td):
    @pl.when(pl.program_id(0) == 0)
    def _():
        zeros[...] = jnp.zeros_like(zeros)

        def block_copy(n):
            row0 = pl.multiple_of(zb_ref[n] * MOE_BLOCK, MOE_BLOCK)
            return pltpu.make_async_copy(zeros, out_ref.at[pl.ds(row0, MOE_BLOCK)], zsem)

        n_zero = zb_ref[N_ZERO_BLOCKS]

        def start(n, carry):
            block_copy(n).start()
            return carry

        def wait(n, carry):
            block_copy(n).wait()
            return carry

        lax.fori_loop(0, n_zero, start, 0)
        lax.fori_loop(0, n_zero, wait, 0)

    def issue(t, carry):
        for k in range(TOP_K):
            _slab_copy(x_ref, t, out_ref, dest_ref[0, 0, k * td + t], sem).start(priority=k)
        return carry

    lax.fori_loop(0, td, issue, 0, unroll=8)
    for k in range(TOP_K):
        pltpu.make_async_copy(x_ref, out_ref.at[pl.ds(0, td)], sem).wait()


def _dispatch(xp, dest3, zero_blocks, n_rows, td):
    n_tok = xp.shape[0]
    grid_spec = pltpu.PrefetchScalarGridSpec(
        num_scalar_prefetch=1,
        grid=(n_tok // td,),
        in_specs=[
            pl.BlockSpec((1, 1, TOP_K * td), lambda s, zb: (s, 0, 0), memory_space=pltpu.SMEM),
            pl.BlockSpec((td, SLAB_ROWS, LANES), lambda s, zb: (s, 0, 0)),
        ],
        out_specs=pl.BlockSpec(memory_space=pl.ANY),
        scratch_shapes=[
            pltpu.VMEM((MOE_BLOCK, SLAB_ROWS, LANES), jnp.uint32),
            pltpu.SemaphoreType.DMA(()),
            pltpu.SemaphoreType.DMA(()),
        ],
    )
    return pl.pallas_call(
        functools.partial(_dispatch_kernel, td=td),
        grid_spec=grid_spec,
        out_shape=jax.ShapeDtypeStruct((n_rows, SLAB_ROWS, LANES), jnp.uint32),
        compiler_params=pltpu.CompilerParams(dimension_semantics=("arbitrary",)),
        name="dispatch",
    )(zero_blocks, dest3, xp)


FFN_SUB = 512


def _expert_ffn_kernel(be_ref, na_ref, nx_ref, x_ref, wg_hbm, wu_hbm, wd_hbm, y_ref,
                       wg_f32, wu_f32, wd_f32, wg_bf, wu_bf, wd_bf, sem):
    i = pl.program_id(0)
    active = i < na_ref[0]
    expert = be_ref[i]
    new_expert = (i == 0) | (expert != be_ref[jnp.maximum(i - 1, 0)])

    def fetch(e):
        return (pltpu.make_async_copy(wg_hbm.at[e], wg_f32, sem.at[0]),
                pltpu.make_async_copy(wu_hbm.at[e], wu_f32, sem.at[1]),
                pltpu.make_async_copy(wd_hbm.at[e], wd_f32, sem.at[2]))

    @pl.when(active & (i == 0))
    def _():
        for cp in fetch(expert):
            cp.start()

    @pl.when(active & new_expert)
    def _():
        for cp in fetch(expert):
            cp.wait()
        wg_bf[...] = wg_f32[...].astype(BF16)
        wu_bf[...] = wu_f32[...].astype(BF16)
        wd_bf[...] = wd_f32[...].astype(BF16)
        nxt = nx_ref[expert]

        @pl.when(nxt >= 0)
        def _():
            for cp in fetch(nxt):
                cp.start()

    @pl.when(active)
    def _():
        for lo in range(0, MOE_BLOCK, FFN_SUB):
            x = _load_slabs(x_ref, lo, FFN_SUB).astype(BF16)
            g = jnp.dot(x, wg_bf[...], preferred_element_type=F32)
            u = jnp.dot(x, wu_bf[...], preferred_element_type=F32)
            hid = (jax.nn.silu(g) * u).astype(BF16)
            y = jnp.dot(hid, wd_bf[...], preferred_element_type=F32)
            _store_slabs(y_ref, lo, y.astype(BF16))

    @pl.when(jnp.logical_not(active))
    def _():
        y_ref[...] = jnp.zeros_like(y_ref)


def _expert_ffn(xs, block_expert, n_active, next_expert, wg, wu, wd):
    n_rows = xs.shape[0] // SLAB_ROWS
    n_blocks = n_rows // MOE_BLOCK

    def blk(i, be, na, nx):
        return (jnp.maximum(jnp.minimum(i, na[0] - 1), 0), 0)

    grid_spec = pltpu.PrefetchScalarGridSpec(
        num_scalar_prefetch=3,
        grid=(n_blocks,),
        in_specs=[
            pl.BlockSpec((MOE_BLOCK * SLAB_ROWS, LANES), blk),
            pl.BlockSpec(memory_space=pl.ANY),
            pl.BlockSpec(memory_space=pl.ANY),
            pl.BlockSpec(memory_space=pl.ANY),
        ],
        out_specs=pl.BlockSpec((MOE_BLOCK * SLAB_ROWS, LANES), lambda i, be, na, nx: (i, 0)),
        scratch_shapes=[
            pltpu.VMEM((D_MODEL, D_EXPERT), F32),
            pltpu.VMEM((D_MODEL, D_EXPERT), F32),
            pltpu.VMEM((D_EXPERT, D_MODEL), F32),
            pltpu.VMEM((D_MODEL, D_EXPERT), BF16),
            pltpu.VMEM((D_MODEL, D_EXPERT), BF16),
            pltpu.VMEM((D_EXPERT, D_MODEL), BF16),
            pltpu.SemaphoreType.DMA((3,)),
        ],
    )
    return pl.pallas_call(
        _expert_ffn_kernel,
        grid_spec=grid_spec,
        out_shape=jax.ShapeDtypeStruct((n_rows * SLAB_ROWS, LANES), jnp.uint32),
        compiler_params=pltpu.CompilerParams(
            dimension_semantics=("arbitrary",), vmem_limit_bytes=VMEM_LIMIT),
        name="expert_ffn",
    )(block_expert, n_active, next_expert, xs, wg, wu, wd)


def _combine_kernel(dcur_ref, dnxt_ref, x_ref, route_ref, g_ref, b_ref, y_ref, o_ref,
                    buf, sem, *, td):
    s = pl.program_id(0)
    n = pl.num_programs(0)
    slot = s % 2

    def base(slot_):
        return pl.multiple_of(slot_ * TOP_K * td, TOP_K * td)

    def issue(d_ref, to_slot):
        def body(t, carry):
            for k in range(TOP_K):
                _slab_copy(y_ref, d_ref[0, 0, k * td + t], buf, base(to_slot) + TOP_K * t + k,
                           sem.at[to_slot]).start(priority=k)
            return carry
        lax.fori_loop(0, td, body, 0, unroll=8)

    @pl.when(s == 0)
    def _():
        issue(dcur_ref, 0)

    @pl.when(s + 1 < n)
    def _():
        issue(dnxt_ref, 1 - slot)

    pltpu.make_async_copy(y_ref.at[pl.ds(0, TOP_K * td)], buf.at[pl.ds(base(slot), TOP_K * td)],
                          sem.at[slot]).wait()

    route = route_ref[...]
    buf2 = buf.reshape(2 * TOP_K * td * SLAB_ROWS, LANES)
    ffn = (route[:, 2:3] * _load_slabs(buf2, base(slot), td, every=TOP_K)
           + route[:, 3:4] * _load_slabs(buf2, base(slot) + 1, td, every=TOP_K))
    o_ref[...] = _layer_norm(DEEPNORM_ALPHA * x_ref[...] + ffn, g_ref[...], b_ref[...])


def _combine(ys, dest3, x1, route, g2, b2, td):
    n_tok = x1.shape[0]
    n_steps = n_tok // td
    return pl.pallas_call(
        functools.partial(_combine_kernel, td=td),
        grid=(n_steps,),
        in_specs=[
            pl.BlockSpec((1, 1, TOP_K * td), lambda s: (s, 0, 0), memory_space=pltpu.SMEM),
            pl.BlockSpec((1, 1, TOP_K * td), lambda s: (jnp.minimum(s + 1, n_steps - 1), 0, 0),
                         memory_space=pltpu.SMEM),
            pl.BlockSpec((td, D_MODEL), lambda s: (s, 0)),
            pl.BlockSpec((td, ROUTE_COLS), lambda s: (s, 0)),
            pl.BlockSpec((1, D_MODEL), lambda s: (0, 0)),
            pl.BlockSpec((1, D_MODEL), lambda s: (0, 0)),
            pl.BlockSpec(memory_space=pl.ANY),
        ],
        out_specs=pl.BlockSpec((td, D_MODEL), lambda s: (s, 0)),
        out_shape=jax.ShapeDtypeStruct((n_tok, D_MODEL), F32),
        scratch_shapes=[
            pltpu.VMEM((2 * TOP_K * td, SLAB_ROWS, LANES), jnp.uint32),
            pltpu.SemaphoreType.DMA((2,)),
        ],
        compiler_params=pltpu.CompilerParams(
            dimension_semantics=("arbitrary",), vmem_limit_bytes=VMEM_LIMIT),
        name="combine",
    )(dest3, dest3, x1, route, g2, b2, ys)


def _row_tile(n, want):
    t = min(want, n)
    while n % t:
        t //= 2
    return t


def kernel(x, ln_in_g, ln_in_b, w_in, b_gate, lambda_q, lambda_k, subln_g, rel_bias, conv_w,
           w_a_proj, w_b_proj, w_o, ln1_g, ln1_b, w_group, b_group, w_sub, b_sub,
           w_gate_e, w_up_e, w_down_e, ln2_g, ln2_b):
    bsz, seq, d = x.shape
    assert DEPTH == 1 and d == D_MODEL and w_in.shape == (DEPTH, D_MODEL, N_IN)
    n_tok = bsz * seq
    tq = _row_tile(seq, 256)
    assert tq % LANES == 0 and tq % CHUNK == 0
    qpb = _row_tile(seq // tq, 2)
    tm1 = _row_tile(seq, 512)
    tm3 = _row_tile(n_tok, 1024)
    sub3 = _row_tile(tm3, 256)
    tr = _row_tile(n_tok, 4096)
    td = _row_tile(n_tok, 8192)
    tc = _row_tile(n_tok, 512)
    row = lambda v: v.reshape(1, -1).astype(F32)
    lam_init = 0.8 - 0.6 * math.exp(-0.3 * 0)

    wkt = jnp.transpose(lax.optimization_barrier(w_in[0, :, D_MODEL:2 * D_MODEL])).astype(BF16)
    xn, p, kt = _ln_proj(x.reshape(n_tok, d), row(ln_in_g), row(ln_in_b), w_in[0].astype(BF16),
                         wkt, conv_w[0].reshape(3, d).astype(F32), seq, tm1)

    band = _band_bias(rel_bias, tq)
    o_n = _diff_attn(p, kt, band, lambda_q[0].astype(F32), lambda_k[0].astype(F32),
                     row(subln_g[0]), bsz, seq, tq, qpb, lam_init)

    gap = EXPERT_LANE0 - N_GROUPS
    w_r = jnp.concatenate(
        [w_group[0].astype(F32), jnp.zeros((d, gap), F32),
         jnp.transpose(w_sub[0].astype(F32), (1, 0, 2)).reshape(d, N_EXPERTS)], axis=1)
    w_r = jnp.pad(w_r, ((0, 0), (0, LANES - w_r.shape[1])))
    w_hi = w_r.astype(BF16)
    w_lo = (w_r - w_hi.astype(F32)).astype(BF16)
    w_r2 = jnp.concatenate([w_hi, w_lo], axis=1)
    b_r = jnp.concatenate([b_group[0].astype(F32), jnp.zeros((gap,), F32),
                           b_sub[0].astype(F32).reshape(-1)])
    b_r = jnp.pad(b_r, (0, LANES - b_r.shape[0])).reshape(1, LANES)
    x1, xp, logits = _mix(
        o_n, p, xn, w_a_proj[0].astype(BF16), w_b_proj[0].astype(BF16), w_o[0].astype(BF16),
        b_gate[0].astype(F32), row(ln1_g[0]), row(ln1_b[0]), w_r2, b_r, tm3, sub3)
    route, route_t, counts = _route(logits, tr)

    n_assign = n_tok * TOP_K
    n_blocks = -(-n_assign // MOE_BLOCK) + N_EXPERTS
    cnt = counts[0, :N_EXPERTS].astype(jnp.int32)
    padded = ((cnt + MOE_BLOCK - 1) // MOE_BLOCK) * MOE_BLOCK
    pad_end = jnp.cumsum(padded)
    pad_start = pad_end - padded
    n_active = (pad_end[-1:] // MOE_BLOCK).astype(jnp.int32)
    blk_row0 = jnp.arange(n_blocks, dtype=jnp.int32) * MOE_BLOCK
    block_expert = jnp.minimum(
        jnp.sum((pad_end[None, :] <= blk_row0[:, None]).astype(jnp.int32), axis=1),
        N_EXPERTS - 1).astype(jnp.int32)
    experts = route_t[0:TOP_K].astype(jnp.int32)
    ranks = route_t[4:4 + TOP_K].astype(jnp.int32)
    is_e = experts[None] == jnp.arange(N_EXPERTS, dtype=jnp.int32)[:, None, None]
    dest = jnp.sum(jnp.where(is_e, pad_start[:, None, None], 0), axis=0) + ranks

    def per_tile(t):
        tiles = jnp.transpose(dest.reshape(TOP_K, n_tok // t, t), (1, 0, 2))
        return tiles.reshape(n_tok // t, 1, TOP_K * t).astype(jnp.int32)

    last_blk = jnp.maximum(pad_end // MOE_BLOCK - 1, 0)
    idle_blk = jnp.minimum(n_active[0] + jnp.arange(N_EXPERTS), n_blocks - 1)
    n_zero = N_EXPERTS + n_blocks - n_active
    zero_blocks = jnp.concatenate([last_blk, idle_blk, n_zero]).astype(jnp.int32)
    n_rows = n_blocks * MOE_BLOCK
    xs = _dispatch(xp.reshape(n_tok, SLAB_ROWS, LANES), per_tile(td), zero_blocks, n_rows, td)
    e_ids = jnp.arange(N_EXPERTS, dtype=jnp.int32)
    later = (padded > 0)[None, :] & (e_ids[None, :] > e_ids[:, None])
    next_expert = jnp.min(jnp.where(later, e_ids[None, :], N_EXPERTS), axis=1)
    next_expert = jnp.where(next_expert < N_EXPERTS, next_expert, -1).astype(jnp.int32)
    ys = _expert_ffn(xs.reshape(n_rows * SLAB_ROWS, LANES), block_expert, n_active, next_expert,
                     w_gate_e[0].astype(F32), w_up_e[0].astype(F32), w_down_e[0].astype(F32))
    out = _combine(ys.reshape(n_rows, SLAB_ROWS, LANES), per_tile(tc), x1, route,
                   row(ln2_g[0]), row(ln2_b[0]), tc)
    return out.reshape(bsz, seq, d)
```
